```python
import jax
import jax.numpy as jnp
from jax import lax
import numpy as np

D_MODEL = 1024
BATCH = 8
SEQ = 2048
DEPTH = 1
DEC_BATCH = 128
DEC_SEQ = 4
PAST_LEN = 16384
PAGE_SIZE = 128

NORM_EPS = 1e-5
HEAD_DIM = 64
N_Q_HEADS = 8
N_KV_HEADS = 2
Q_PER_KV = N_Q_HEADS // N_KV_HEADS
ATTN_WIDTH = N_Q_HEADS * HEAD_DIM
KV_WIDTH = N_KV_HEADS * HEAD_DIM
WINDOW = 128
ATTN_BLOCK = WINDOW
ROT_DIM = HEAD_DIM // 4
ROPE_THETA = 500000.0
ATTN_SCALE = HEAD_DIM ** -0.5
RWKV_HEADS = 8
RWKV_HEAD_DIM = 64
RWKV_WIDTH = RWKV_HEADS * RWKV_HEAD_DIM
DECAY_LORA = 32
AAA_LORA = 32
GATE_LORA = 96
GN_EPS = 64e-5
RWKV_COLS = 3 * RWKV_WIDTH + DECAY_LORA + AAA_LORA + GATE_LORA
RWKV_SPLITS = (RWKV_WIDTH, 2 * RWKV_WIDTH, 3 * RWKV_WIDTH, 3 * RWKV_WIDTH + DECAY_LORA,
               3 * RWKV_WIDTH + DECAY_LORA + AAA_LORA)
OFF_Q = 0
OFF_K = OFF_Q + ATTN_WIDTH
OFF_V = OFF_K + KV_WIDTH
OFF_B = OFF_V + KV_WIDTH
OFF_GATE = OFF_B + RWKV_COLS
IN_WIDTH = OFF_GATE + 2 * D_MODEL
N_KEYS = 128
N_EXPERTS = N_KEYS * N_KEYS
PEER_HEADS = 8
PEER_TOPK = 16
D_KEY = 256
D_KEY_HALF = D_KEY // 2
PEER_BLOCK = 128

kernel_name = "hybrid_swa_rwkv7_peer_step"


def rms_norm(x, g):
    xf = x.astype(jnp.float32)
    y = xf * lax.rsqrt(jnp.mean(xf * xf, axis=-1, keepdims=True) + NORM_EPS)
    return (y * g.astype(jnp.float32)).astype(x.dtype)


def rope_partial(x, pos):
    half = ROT_DIM // 2
    inv_freq = ROPE_THETA ** (-jnp.arange(half, dtype=jnp.float32) / half)
    ang = pos.astype(jnp.float32)[:, None] * inv_freq[None, :]
    cos = jnp.cos(ang)[:, None, :]
    sin = jnp.sin(ang)[:, None, :]
    xf = x.astype(jnp.float32)
    x1, x2, rest = xf[..., :half], xf[..., half:ROT_DIM], xf[..., ROT_DIM:]
    out = jnp.concatenate([x1 * cos - x2 * sin, x2 * cos + x1 * sin, rest], axis=-1)
    return out.astype(x.dtype)


def sink_softmax(s, sinks, mask):
    s = jnp.where(mask, s, -jnp.inf)
    sink_col = jnp.broadcast_to(sinks.astype(jnp.float32)[:, :, None, None], s.shape[:-1] + (1,))
    p = jax.nn.softmax(jnp.concatenate([s, sink_col], axis=-1), axis=-1)
    return p[..., :-1]


def window_attn_prompt(q, k, v, sinks):
    b, t = q.shape[:2]
    nb = t // ATTN_BLOCK
    qb = q.reshape(b, nb, ATTN_BLOCK, N_KV_HEADS, Q_PER_KV, HEAD_DIM)

    def band(z):
        zb = z.reshape(b, nb, ATTN_BLOCK, N_KV_HEADS, HEAD_DIM)
        prev = jnp.pad(zb, ((0, 0), (1, 0), (0, 0), (0, 0), (0, 0)))[:, :-1]
        return jnp.concatenate([prev, zb], axis=2)

    kb, vb = band(k), band(v)
    s = jnp.einsum("bnqkgd,bnskd->bnkgqs", qb, kb).astype(jnp.float32) * ATTN_SCALE
    i = jnp.arange(ATTN_BLOCK)[:, None]
    j = jnp.arange(2 * ATTN_BLOCK)[None, :]
    diff = ATTN_BLOCK + i - j
    blk = jnp.arange(nb)[:, None, None]
    mask = (diff >= 0) & (diff < WINDOW) & ((blk > 0) | (j >= ATTN_BLOCK))
    p = sink_softmax(s, sinks, mask[:, None, None])
    o = jnp.einsum("bnkgqs,bnskd->bnqkgd", p.astype(vb.dtype), vb)
    return o.reshape(b, t, ATTN_WIDTH)


def window_attn_sample(q, k, v, cache_k, cache_v, sinks):
    b, t = q.shape[:2]
    n_past = cache_k.shape[1]
    keys = jnp.concatenate([cache_k.astype(k.dtype), k], axis=1)
    vals = jnp.concatenate([cache_v.astype(v.dtype), v], axis=1)
    s = jnp.einsum("bqkgd,bskd->bkgqs", q, keys).astype(jnp.float32) * ATTN_SCALE
    i = jnp.arange(t)[:, None]
    j = jnp.arange(n_past + t)[None, :]
    diff = n_past + i - j
    mask = (diff >= 0) & (diff < WINDOW)
    p = sink_softmax(s, sinks, mask)
    o = jnp.einsum("bkgqs,bskd->bqkgd", p.astype(vals.dtype), vals).reshape(b, t, ATTN_WIDTH)
    return o, keys[:, t:], vals[:, t:]


def _wkv_step(S, inp):
    r, w, k, v, kk, bb = inp
    sa = jnp.einsum("bhij,bhj->bhi", S, kk)
    S = S * w[:, :, None, :] - sa[..., None] * bb[:, :, None, :] + v[..., None] * k[:, :, None, :]
    return S, jnp.einsum("bhij,bhj->bhi", S, r)


def rwkv7_time_mix(p, wkv0, shift0, lp):
    f32 = jnp.float32
    b, t, _ = p.shape
    prev = jnp.concatenate([shift0[:, None].astype(p.dtype), p[:, :-1]], axis=1)
    xm = p + lp["shift_mu"] * (prev - p)
    r, k, v, wl, al, gl = jnp.split(xm, RWKV_SPLITS, axis=-1)
    w_log = -jax.nn.softplus(-(lp["w0"] + jnp.tanh(wl) @ lp["w2"]).astype(f32)) - 0.5
    a = jax.nn.sigmoid((lp["a0"] + al @ lp["a2"]).astype(f32))
    g = jax.nn.sigmoid(gl) @ lp["g2"]

    def heads(z):
        return z.astype(f32).reshape(z.shape[:-1] + (RWKV_HEADS, RWKV_HEAD_DIM))

    r, k, v, a = heads(r), heads(k), heads(v), heads(a)
    decay = heads(jnp.exp(-jnp.exp(w_log)))
    kk = k * heads(lp["k_k"])
    kk = kk * lax.rsqrt(jnp.maximum(jnp.sum(kk * kk, axis=-1, keepdims=True), 1e-24))
    k = k * (1.0 + (a - 1.0) * heads(lp["k_a"]))
    xs = tuple(jnp.moveaxis(z, 1, 0) for z in (r, decay, k, v, kk, kk * a))
    wkv_t, y = lax.scan(_wkv_step, wkv0.astype(f32), xs)
    y = jnp.moveaxis(y, 0, 1)
    mean = jnp.mean(y, axis=-1, keepdims=True)
    var = jnp.mean(jnp.square(y - mean), axis=-1, keepdims=True)
    y = (y - mean) * lax.rsqrt(var + GN_EPS) * heads(lp["ln_x_w"]) + heads(lp["ln_x_b"])
    y = y + jnp.sum(r * k * lp["r_k"].astype(f32), axis=-1, keepdims=True) * v
    y = y.reshape(b, t, RWKV_WIDTH) * g.astype(f32)
    return y.astype(p.dtype), wkv_t, p[:, -1]


def peer_ffn(x, w_query, sub_keys, expert_u, expert_v):
    m = x.shape[0]
    pad = (-m) % PEER_BLOCK
    xb = jnp.pad(x, ((0, pad), (0, 0))).reshape(-1, PEER_BLOCK, D_MODEL)
    n_cand = PEER_TOPK * PEER_TOPK

    def block(xt):
        q = (xt @ w_query).reshape(PEER_BLOCK, PEER_HEADS, 2, D_KEY_HALF)
        s = jnp.einsum("thcd,hcnd->thcn", q, sub_keys).astype(jnp.float32)
        sv, si = lax.top_k(s, PEER_TOPK)
        cand_s = (sv[:, :, 0, :, None] + sv[:, :, 1, None, :]).reshape(PEER_BLOCK, PEER_HEADS, n_cand)
        cand_i = (si[:, :, 0, :, None] * N_KEYS + si[:, :, 1, None, :]).reshape(PEER_BLOCK, PEER_HEADS, n_cand)
        best_s, best_pos = lax.top_k(cand_s, PEER_TOPK)
        idx = jnp.take_along_axis(cand_i, best_pos, axis=-1)
        gate = jax.nn.softmax(best_s, axis=-1)
        u = expert_u[idx]
        v = expert_v[idx]
        act = jax.nn.gelu(jnp.einsum("thkd,td->thk", u, xt).astype(jnp.float32), approximate=False)
        return jnp.einsum("thk,thkd->td", (gate * act).astype(v.dtype), v)

    y = lax.map(block, xb)
    return y.reshape(-1, D_MODEL)[:m].astype(x.dtype)


def hybrid_layer(x, pos, lp, win_k, win_v, wkv0, shift0):
    b, t, _ = x.shape
    h = rms_norm(x, lp["norm1_g"])
    proj = h @ lp["w_in"] + lp["b_in"]
    q = rope_partial(proj[..., OFF_Q:OFF_K].reshape(b, t, N_Q_HEADS, HEAD_DIM), pos)
    k = rope_partial(proj[..., OFF_K:OFF_V].reshape(b, t, N_KV_HEADS, HEAD_DIM), pos)
    v = proj[..., OFF_V:OFF_B].reshape(b, t, N_KV_HEADS, HEAD_DIM)
    q = q.reshape(b, t, N_KV_HEADS, Q_PER_KV, HEAD_DIM)
    sinks = lp["attn_sinks"].reshape(N_KV_HEADS, Q_PER_KV)
    if win_k is None:
        att = window_attn_prompt(q, k, v, sinks)
        keep = min(WINDOW, t)
        new_k, new_v = k[:, t - keep:], v[:, t - keep:]
    else:
        att, new_k, new_v = window_attn_sample(q, k, v, win_k, win_v, sinks)
    rw, wkv_t, shift_t = rwkv7_time_mix(proj[..., OFF_B:OFF_GATE], wkv0, shift0, lp)
    gate_a = jax.nn.sigmoid(proj[..., OFF_GATE:OFF_GATE + D_MODEL])
    gate_b = jax.nn.sigmoid(proj[..., OFF_GATE + D_MODEL:])
    merged = gate_a * (att @ lp["w_up_a"]) + gate_b * (rw @ lp["w_up_b"])
    x = x + merged @ lp["w_o"]
    h2 = rms_norm(x, lp["norm2_g"]).reshape(b * t, D_MODEL)
    x = x + peer_ffn(h2, lp["w_query"], lp["sub_keys"], lp["expert_u"], lp["expert_v"]).reshape(b, t, D_MODEL)
    return x, new_k.astype(x.dtype), new_v.astype(x.dtype), wkv_t.astype(x.dtype), shift_t


def setup_inputs(seed: int = 0) -> dict:
    key = jax.random.key(seed)
    ks = iter(jax.random.split(key, 32))
    f32 = jnp.float32

    def nrm(shape, scale):
        return jax.random.normal(next(ks), shape, f32) * scale

    L = DEPTH
    win = min(WINDOW, PAST_LEN)
    return {
        "x_prompt": nrm((BATCH, SEQ, D_MODEL), 1.0),
        "x_sample": nrm((DEC_BATCH, DEC_SEQ, D_MODEL), 1.0),
        "cache_win_k": nrm((L, DEC_BATCH, win, N_KV_HEADS, HEAD_DIM), 1.0),
        "cache_win_v": nrm((L, DEC_BATCH, win, N_KV_HEADS, HEAD_DIM), 1.0),
        "state_wkv": nrm((L, DEC_BATCH, RWKV_HEADS, RWKV_HEAD_DIM, RWKV_HEAD_DIM), 0.3),
        "state_shift": nrm((L, DEC_BATCH, RWKV_COLS), 1.0),
        "norm1_g": 1.0 + nrm((L, D_MODEL), 0.05),
        "w_in": nrm((L, D_MODEL, IN_WIDTH), D_MODEL ** -0.5),
        "b_in": nrm((L, IN_WIDTH), 0.02),
        "attn_sinks": nrm((L, N_Q_HEADS), 0.5),
        "shift_mu": jax.random.uniform(next(ks), (L, RWKV_COLS), f32),
        "w0": nrm((L, RWKV_WIDTH), 0.5) - 0.5,
        "w2": nrm((L, DECAY_LORA, RWKV_WIDTH), 0.1 * DECAY_LORA ** -0.5),
        "a0": nrm((L, RWKV_WIDTH), 0.3),
        "a2": nrm((L, AAA_LORA, RWKV_WIDTH), 0.1 * AAA_LORA ** -0.5),
        "g2": nrm((L, GATE_LORA, RWKV_WIDTH), GATE_LORA ** -0.5),
        "k_k": 0.85 + nrm((L, RWKV_WIDTH), 0.05),
        "k_a": 1.0 + nrm((L, RWKV_WIDTH), 0.05),
        "r_k": nrm((L, RWKV_HEADS, RWKV_HEAD_DIM), 0.1),
        "ln_x_w": 1.0 + nrm((L, RWKV_WIDTH), 0.05),
        "ln_x_b": nrm((L, RWKV_WIDTH), 0.02),
        "w_up_a": nrm((L, ATTN_WIDTH, D_MODEL), ATTN_WIDTH ** -0.5),
        "w_up_b": nrm((L, RWKV_WIDTH, D_MODEL), RWKV_WIDTH ** -0.5),
        "w_o": nrm((L, D_MODEL, D_MODEL), D_MODEL ** -0.5),
        "norm2_g": 1.0 + nrm((L, D_MODEL), 0.05),
        "w_query": nrm((L, D_MODEL, PEER_HEADS * D_KEY), D_MODEL ** -0.5),
        "sub_keys": nrm((L, PEER_HEADS, 2, N_KEYS, D_KEY_HALF), D_KEY_HALF ** -0.5),
        "expert_u": nrm((L, N_EXPERTS, D_MODEL), D_MODEL ** -0.5),
        "expert_v": nrm((L, N_EXPERTS, D_MODEL), (PEER_HEADS * PEER_TOPK) ** -0.5),
        "final_norm_g": 1.0 + nrm((D_MODEL,), 0.05),
    }


def reference(x_prompt, x_sample, cache_win_k, cache_win_v, state_wkv, state_shift, norm1_g, w_in, b_in,
              attn_sinks, shift_mu, w0, w2, a0, a2, g2, k_k, k_a, r_k, ln_x_w, ln_x_b, w_up_a, w_up_b, w_o,
              norm2_g, w_query, sub_keys, expert_u, expert_v, final_norm_g):
    pos_p = jnp.arange(x_prompt.shape[1])
    pos_s = PAST_LEN + jnp.arange(x_sample.shape[1])
    hp, hs = x_prompt, x_sample
    pk, pv, pw, psh, sk, sv, sw, ssh = [], [], [], [], [], [], [], []
    for l in range(DEPTH):
        lp = dict(norm1_g=norm1_g[l], w_in=w_in[l], b_in=b_in[l], attn_sinks=attn_sinks[l],
                  shift_mu=shift_mu[l], w0=w0[l], w2=w2[l], a0=a0[l], a2=a2[l], g2=g2[l], k_k=k_k[l],
                  k_a=k_a[l], r_k=r_k[l], ln_x_w=ln_x_w[l], ln_x_b=ln_x_b[l], w_up_a=w_up_a[l],
                  w_up_b=w_up_b[l], w_o=w_o[l], norm2_g=norm2_g[l], w_query=w_query[l],
                  sub_keys=sub_keys[l], expert_u=expert_u[l], expert_v=expert_v[l])
        wkv0 = jnp.zeros((x_prompt.shape[0], RWKV_HEADS, RWKV_HEAD_DIM, RWKV_HEAD_DIM), jnp.float32)
        shift0 = jnp.zeros((x_prompt.shape[0], RWKV_COLS), x_prompt.dtype)
        hp, k_, v_, w_, s_ = hybrid_layer(hp, pos_p, lp, None, None, wkv0, shift0)
        pk.append(k_)
        pv.append(v_)
        pw.append(w_)
        psh.append(s_)
        hs, k_, v_, w_, s_ = hybrid_layer(hs, pos_s, lp, cache_win_k[l], cache_win_v[l], state_wkv[l],
                                          state_shift[l])
        sk.append(k_)
        sv.append(v_)
        sw.append(w_)
        ssh.append(s_)
    y_prompt = rms_norm(hp, final_norm_g)
    y_sample = rms_norm(hs, final_norm_g)
    prompt_win_k = jnp.stack(pk)
    prompt_win_v = jnp.stack(pv)
    prompt_wkv = jnp.stack(pw)
    prompt_shift = jnp.stack(psh)
    sample_win_k = jnp.stack(sk)
    sample_win_v = jnp.stack(sv)
    sample_wkv = jnp.stack(sw)
    sample_shift = jnp.stack(ssh)
    return (y_prompt, y_sample, prompt_win_k, prompt_win_v, prompt_wkv, prompt_shift, sample_win_k, sample_win_v, sample_wkv, sample_shift)
```

```python
import functools

import jax
import jax.numpy as jnp
from jax import lax
from jax.experimental import pallas as pl
from jax.experimental.pallas import tpu as pltpu

F32, BF16, I32 = jnp.float32, jnp.bfloat16, jnp.int32
HIGHEST = lax.Precision.HIGHEST

D_MODEL = 1024
NORM_EPS = 1e-5
HEAD_DIM = 64
N_Q_HEADS = 8
N_KV_HEADS = 2
Q_PER_KV = N_Q_HEADS // N_KV_HEADS
ATTN_WIDTH = N_Q_HEADS * HEAD_DIM
KV_WIDTH = N_KV_HEADS * HEAD_DIM
WINDOW = 128
ROT_DIM = HEAD_DIM // 4
ROPE_THETA = 500000.0
ATTN_SCALE = HEAD_DIM ** -0.5
RWKV_HEADS = 8
RWKV_HEAD_DIM = 64
RWKV_WIDTH = RWKV_HEADS * RWKV_HEAD_DIM
DECAY_LORA = 32
AAA_LORA = 32
GATE_LORA = 96
GN_EPS = 64e-5
RWKV_COLS = 3 * RWKV_WIDTH + DECAY_LORA + AAA_LORA + GATE_LORA
LORA_PAD = 256
RWKV_PAD = 3 * RWKV_WIDTH + LORA_PAD
OFF_K = ATTN_WIDTH
OFF_V = OFF_K + KV_WIDTH
OFF_B = OFF_V + KV_WIDTH
OFF_GATE = OFF_B + RWKV_COLS
N_KEYS = 128
N_EXPERTS = N_KEYS * N_KEYS
PEER_HEADS = 8
PEER_TOPK = 16
D_KEY_HALF = 128
N_SLOTS = PEER_HEADS * PEER_TOPK
PAST_LEN = 16384
RWKV_CHUNK = 64
LANES = 128
SUBLANES = 8
SOLVE_COLS = 16
NEG_INF = float("-inf")

NN = ((1,), (0,))
NT = ((1,), (1,))
TN = ((0,), (0,))


def _dot(a, b, dims=NN, precision=None):
    return lax.dot_general(a, b, (dims, ((), ())), precision=precision, preferred_element_type=F32)


def _doth(a, b, dims=NN):
    return _dot(a, b, dims, HIGHEST)


def _split3(x):
    hi = x.astype(BF16)
    r1 = x - hi.astype(F32)
    mid = r1.astype(BF16)
    lo = (r1 - mid.astype(F32)).astype(BF16)
    return hi, mid, lo


def _dot_exact_rhs(x, e):
    hi, mid, lo = _split3(x)
    return _dot(hi, e) + _dot(mid, e) + _dot(lo, e)


def _dot_exact_lhs(e, x):
    hi, mid, lo = _split3(x)
    return _dot(e, hi) + _dot(e, mid) + _dot(e, lo)


def _params(sem, vmem_mb):
    return pltpu.CompilerParams(dimension_semantics=sem, vmem_limit_bytes=vmem_mb * 1024 * 1024)


def _full(a):
    nd = a.ndim
    return pl.BlockSpec(a.shape, lambda *_: (0,) * nd)


SEG_QKV = ATTN_WIDTH + 2 * KV_WIDTH
SEG_P = SEG_QKV + RWKV_PAD
SEG_END = SEG_P + 2 * D_MODEL


def _inproj_body(x_ref, g_ref, w_ref, b_ref, rope_ref, q_ref, k_ref, v_ref, p_ref, ga_ref, gb_ref):
    x = x_ref[...]
    h = (x * lax.rsqrt(jnp.mean(x * x, axis=-1, keepdims=True) + NORM_EPS) * g_ref[...]).astype(BF16)

    def seg(lo, hi):
        return _dot(h, w_ref[:, lo:hi]) + b_ref[:, lo:hi]

    cos = rope_ref[:, 0:LANES]
    sin_up = rope_ref[:, LANES:2 * LANES]
    sin_dn = rope_ref[:, 2 * LANES:3 * LANES]

    def rope(z):
        return z * cos + pltpu.roll(z, ROT_DIM // 2, 1) * sin_up + pltpu.roll(z, LANES - ROT_DIM // 2, 1) * sin_dn

    for c in range(ATTN_WIDTH // LANES):
        q_ref[:, c * LANES:(c + 1) * LANES] = rope(seg(c * LANES, (c + 1) * LANES))
    k_ref[...] = rope(seg(OFF_K, OFF_V))
    v_ref[...] = seg(OFF_V, SEG_QKV)
    p_ref[...] = seg(SEG_QKV, SEG_P)
    ga_ref[...] = jax.nn.sigmoid(seg(SEG_P, SEG_P + D_MODEL))
    gb_ref[...] = jax.nn.sigmoid(seg(SEG_P + D_MODEL, SEG_END))


def _in_proj(x2, g1, w_cat, b_cat, rope_tab, tm):
    m = x2.shape[0]
    nrep = rope_tab.shape[0] // tm
    widths = (ATTN_WIDTH, KV_WIDTH, KV_WIDTH, RWKV_PAD, D_MODEL, D_MODEL)
    return pl.pallas_call(
        _inproj_body,
        grid=(m // tm,),
        in_specs=[
            pl.BlockSpec((tm, D_MODEL), lambda i: (i, 0)),
            _full(g1), _full(w_cat), _full(b_cat),
            pl.BlockSpec((tm, 3 * LANES), lambda i: (i % nrep, 0)),
        ],
        out_specs=[pl.BlockSpec((tm, w), lambda i: (i, 0)) for w in widths],
        out_shape=[jax.ShapeDtypeStruct((m, w), F32) for w in widths],
        compiler_params=_params(("parallel",), 48),
        name="in_proj",
    )(x2, g1, w_cat, b_cat, rope_tab)


def _rope_table(pos, rows):
    t = pos.shape[0]
    half = ROT_DIM // 2
    inv_freq = ROPE_THETA ** (-jnp.arange(half, dtype=F32) / half)
    ang = pos.astype(F32)[:, None] * inv_freq[None, :]
    cos, sin = jnp.cos(ang), jnp.sin(ang)
    rest = HEAD_DIM - ROT_DIM
    z8 = jnp.zeros((t, half), F32)
    cos64 = jnp.concatenate([cos, cos, jnp.ones((t, rest), F32)], axis=1)
    up64 = jnp.concatenate([z8, sin, jnp.zeros((t, rest), F32)], axis=1)
    dn64 = jnp.concatenate([-sin, z8, jnp.zeros((t, rest), F32)], axis=1)
    tab = jnp.concatenate([jnp.tile(a, (1, LANES // HEAD_DIM)) for a in (cos64, up64, dn64)], axis=1)
    if rows > t:
        tab = jnp.tile(tab, (rows // t, 1))
    return tab


def _softmax_av(s, sink, vv):
    m = jnp.maximum(jnp.max(s, axis=-1, keepdims=True), sink)
    p = jnp.exp(s - m)
    den = jnp.sum(p, axis=-1, keepdims=True) + jnp.exp(sink - m)
    return _dot(p.astype(BF16), vv) / den


def _attn_prompt_body(sink_ref, q_ref, kc_ref, kp_ref, vc_ref, vp_ref, o_ref):
    n = pl.program_id(1)
    blk = q_ref.shape[0]
    i = lax.broadcasted_iota(I32, (blk, 2 * blk), 0)
    j = lax.broadcasted_iota(I32, (blk, 2 * blk), 1)
    diff = blk + i - j
    j_lo = jnp.where(n > 0, 0, blk)
    mask = (diff >= 0) & (diff < WINDOW) & (j >= j_lo)
    outs = []
    for g in range(N_KV_HEADS):
        sl = slice(g * HEAD_DIM, (g + 1) * HEAD_DIM)
        kk = jnp.concatenate([kp_ref[:, sl], kc_ref[:, sl]], axis=0).astype(BF16)
        vv = jnp.concatenate([vp_ref[:, sl], vc_ref[:, sl]], axis=0).astype(BF16)
        for hq in range(Q_PER_KV):
            h = g * Q_PER_KV + hq
            qh = q_ref[:, h * HEAD_DIM:(h + 1) * HEAD_DIM].astype(BF16)
            s = jnp.where(mask, _dot(qh, kk, NT) * ATTN_SCALE, NEG_INF)
            outs.append(_softmax_av(s, sink_ref[h], vv))
    o_ref[...] = jnp.concatenate(outs, axis=1).astype(o_ref.dtype)


def _attn_prompt(sinks, q3, k3, v3):
    b, t, _ = q3.shape
    blk = WINDOW
    cur = lambda w: pl.BlockSpec((None, blk, w), lambda bi, n: (bi, n, 0))
    prev = lambda w: pl.BlockSpec((None, blk, w), lambda bi, n: (bi, jnp.maximum(n - 1, 0), 0))
    return pl.pallas_call(
        _attn_prompt_body,
        grid=(b, t // blk),
        in_specs=[pl.BlockSpec(memory_space=pltpu.SMEM), cur(ATTN_WIDTH), cur(KV_WIDTH), prev(KV_WIDTH),
                  cur(KV_WIDTH), prev(KV_WIDTH)],
        out_specs=cur(ATTN_WIDTH),
        out_shape=jax.ShapeDtypeStruct((b, t, ATTN_WIDTH), BF16),
        compiler_params=_params(("parallel", "arbitrary"), 32),
        name="attn_prompt",
    )(sinks, q3, k3, k3, v3, v3)


def _attn_sample_body(sink_ref, q_ref, kn_ref, vn_ref, kc_ref, vc_ref, o_ref, *, t):
    rows = Q_PER_KV * t
    n_past = kc_ref.shape[0]
    ri = lax.broadcasted_iota(I32, (rows, 1), 0)
    qi = ri % t
    jc = lax.broadcasted_iota(I32, (rows, n_past), 1)
    mask_c = (n_past + qi - jc) < WINDOW
    for g in range(N_KV_HEADS):
        sl = slice(g * HEAD_DIM, (g + 1) * HEAD_DIM)
        sink = jnp.zeros((rows, 1), F32)
        for hq in range(Q_PER_KV):
            sink = jnp.where(ri // t == hq, sink_ref[g * Q_PER_KV + hq], sink)
        qg = q_ref[g]
        qb = qg.astype(BF16)
        kn = kn_ref[:, sl]
        vn = vn_ref[:, sl]
        s_c = jnp.where(mask_c, _dot(qb, kc_ref[:, sl].astype(BF16), NT) * ATTN_SCALE, NEG_INF)
        s_n = []
        for jn in range(t):
            sj = jnp.sum(qg * kn[jn:jn + 1, :], axis=-1, keepdims=True) * ATTN_SCALE
            s_n.append(jnp.where(qi >= jn, sj, NEG_INF))
        m = jnp.maximum(jnp.max(s_c, axis=-1, keepdims=True), sink)
        for sj in s_n:
            m = jnp.maximum(m, sj)
        p_c = jnp.exp(s_c - m)
        den = jnp.sum(p_c, axis=-1, keepdims=True) + jnp.exp(sink - m)
        o = _dot(p_c.astype(BF16), vc_ref[:, sl].astype(BF16))
        for jn in range(t):
            p_n = jnp.exp(s_n[jn] - m)
            den = den + p_n
            o = o + p_n * vn[jn:jn + 1, :]
        o_ref[g] = (o / den).astype(o_ref.dtype)


def _attn_sample(sinks, qg, kn, vn, cache_k, cache_v, t):
    b = qg.shape[0]
    rows = Q_PER_KV * t
    n_past = cache_k.shape[1]
    qspec = pl.BlockSpec((None, N_KV_HEADS, rows, HEAD_DIM), lambda bi: (bi, 0, 0, 0))
    nspec = pl.BlockSpec((None, t, KV_WIDTH), lambda bi: (bi, 0, 0))
    cspec = pl.BlockSpec((None, n_past, KV_WIDTH), lambda bi: (bi, 0, 0))
    return pl.pallas_call(
        functools.partial(_attn_sample_body, t=t),
        grid=(b,),
        in_specs=[pl.BlockSpec(memory_space=pltpu.SMEM), qspec, nspec, nspec, cspec, cspec],
        out_specs=qspec,
        out_shape=jax.ShapeDtypeStruct((b, N_KV_HEADS, rows, HEAD_DIM), BF16),
        compiler_params=_params(("parallel",), 32),
        name="attn_sample",
    )(sinks, qg, kn, vn, cache_k, cache_v)


def _rwkv_prep_body(p_ref, pprev_ref, shift_ref, mu_ref, w0_ref, a0_ref, kk_ref, ka_ref, w2_ref, a2_ref, g2_ref,
                    eblk_ref, akb_ref, akk_ref, qrk_ref, qrb_ref, kkt_ref, rt_ref, kdec_ref, bdec_ref, kmod_ref,
                    r_ref, v_ref, g_ref, gamc_ref, *, t_real):
    c = pl.program_id(1)
    ch = p_ref.shape[0]
    pc = p_ref[...]
    row = lax.broadcasted_iota(I32, (ch, 1), 0)
    first = jnp.where(c == 0, shift_ref[...], pprev_ref[SUBLANES - 1:SUBLANES, :])
    prev = jnp.where(row == 0, first, pltpu.roll(pc, 1, 0))
    xm = pc + mu_ref[...] * (prev - pc)
    valid = (c * ch + row) < t_real
    w3 = 3 * RWKV_WIDTH
    r = xm[:, 0:RWKV_WIDTH]
    k = xm[:, RWKV_WIDTH:2 * RWKV_WIDTH]
    v = xm[:, 2 * RWKV_WIDTH:w3]
    lo = xm[:, w3:w3 + LORA_PAD]
    lane = lax.broadcasted_iota(I32, lo.shape, 1)
    act = jnp.where(lane < DECAY_LORA, jnp.tanh(lo),
                    jnp.where(lane < DECAY_LORA + AAA_LORA, lo, jax.nn.sigmoid(lo)))
    z = w0_ref[...] + _doth(act, w2_ref[...])
    w_log = -(jnp.maximum(-z, 0.0) + jnp.log1p(jnp.exp(-jnp.abs(z)))) - 0.5
    lw = jnp.where(valid, -jnp.exp(w_log), 0.0)
    a = jax.nn.sigmoid(a0_ref[...] + _doth(act, a2_ref[...]))
    g = _dot(act.astype(BF16), g2_ref[...].astype(BF16))
    kk = k * kk_ref[...]
    ss = _dot_exact_rhs(kk * kk, eblk_ref[...])
    kk = jnp.where(valid, kk * lax.rsqrt(jnp.maximum(ss, 1e-24)), 0.0)
    kmod = jnp.where(valid, k * (1.0 + (a - 1.0) * ka_ref[...]), 0.0)
    bb = kk * a
    ti = lax.broadcasted_iota(I32, (ch, ch), 0)
    tj = lax.broadcasted_iota(I32, (ch, ch), 1)
    tri = jnp.where(ti >= tj, 1.0, 0.0).astype(BF16)
    cum = _dot_exact_lhs(tri, lw)
    cum_c = cum[ch - 1:ch, :]
    e_neg = jnp.exp(-cum)
    e_rem = jnp.exp(cum_c - cum)
    kkt = kk * jnp.exp(cum - lw)
    rt = r * jnp.exp(cum)
    khat = kmod * e_neg
    bhat = bb * e_neg
    strict = ti > tj
    incl = ti >= tj
    for h in range(RWKV_HEADS):
        sl = slice(h * RWKV_HEAD_DIM, (h + 1) * RWKV_HEAD_DIM)
        akb_ref[h] = jnp.where(strict, _doth(kkt[:, sl], bhat[:, sl], NT), 0.0)
        akk_ref[h] = jnp.where(strict, _doth(kkt[:, sl], khat[:, sl], NT), 0.0)
        qrk_ref[h] = jnp.where(incl, _doth(rt[:, sl], khat[:, sl], NT), 0.0)
        qrb_ref[h] = jnp.where(incl, _doth(rt[:, sl], bhat[:, sl], NT), 0.0)
    kkt_ref[...] = kkt
    rt_ref[...] = rt
    kdec_ref[...] = kmod * e_rem
    bdec_ref[...] = bb * e_rem
    kmod_ref[...] = kmod
    r_ref[...] = r
    v_ref[...] = v
    g_ref[...] = g
    gamc_ref[...] = jnp.exp(cum_c)


def _rwkv_prep(p3, shift0, lp, eblk, ch, t_real):
    b, tp, _ = p3.shape
    nc = tp // ch
    per8 = ch // SUBLANES
    mats = jax.ShapeDtypeStruct((b, nc, RWKV_HEADS, ch, ch), F32)
    rows = jax.ShapeDtypeStruct((b, tp, RWKV_WIDTH), F32)
    mspec = pl.BlockSpec((None, None, RWKV_HEADS, ch, ch), lambda bi, ci: (bi, ci, 0, 0, 0))
    rspec = pl.BlockSpec((None, ch, RWKV_WIDTH), lambda bi, ci: (bi, ci, 0))
    consts = (lp["mu"], lp["w0"], lp["a0"], lp["k_k"], lp["k_a"], lp["w2p"], lp["a2p"], lp["g2p"], eblk)
    return pl.pallas_call(
        functools.partial(_rwkv_prep_body, t_real=t_real),
        grid=(b, nc),
        in_specs=[
            pl.BlockSpec((None, ch, RWKV_PAD), lambda bi, ci: (bi, ci, 0)),
            pl.BlockSpec((None, SUBLANES, RWKV_PAD), lambda bi, ci: (bi, jnp.maximum(ci * per8 - 1, 0), 0)),
            pl.BlockSpec((None, 1, RWKV_PAD), lambda bi, ci: (bi, 0, 0)),
        ] + [_full(a) for a in consts],
        out_specs=[mspec] * 4 + [rspec] * 8 + [pl.BlockSpec((None, None, 1, RWKV_WIDTH), lambda bi, ci: (bi, ci, 0, 0))],
        out_shape=[mats] * 4 + [rows] * 8 + [jax.ShapeDtypeStruct((b, nc, 1, RWKV_WIDTH), F32)],
        compiler_params=_params(("parallel", "parallel"), 48),
        name="rwkv_prep",
    )(p3, p3, shift0, *consts)


def _tri_solve_body(a_ref, t_ref):
    ch = a_ref.shape[0]
    cols = min(SOLVE_COLS, ch)
    t_ref[...] = jnp.zeros(t_ref.shape, F32)
    ci = lax.broadcasted_iota(I32, (cols, SUBLANES, LANES), 0)

    def row_body(s, carry):
        for kc in range(ch // cols):
            c0 = kc * cols

            @pl.when(s >= c0)
            def _():
                acc0 = jnp.where(ci == s - c0, 1.0, 0.0).astype(F32)

                def r_body(r, acc):
                    return acc - a_ref[s, r][None] * t_ref[r, pl.ds(c0, cols)]

                t_ref[s, pl.ds(c0, cols)] = lax.fori_loop(c0, s, r_body, acc0)
        return carry

    lax.fori_loop(0, ch, row_body, 0)


def _tri_solve(akb):
    b, nc, nh, ch, _ = akb.shape
    nsys = b * nc * nh
    slab = SUBLANES * LANES
    nslab = -(-nsys // slab)
    a2 = akb.reshape(nsys, ch * ch)
    a2 = jnp.pad(a2, ((0, nslab * slab - nsys), (0, 0)))
    a5 = a2.T.reshape(ch, ch, nslab, SUBLANES, LANES)
    spec = pl.BlockSpec((ch, ch, None, SUBLANES, LANES), lambda gi: (0, 0, gi, 0, 0),
                        pipeline_mode=pl.Buffered(1))
    t5 = pl.pallas_call(
        _tri_solve_body,
        grid=(nslab,),
        in_specs=[spec],
        out_specs=spec,
        out_shape=jax.ShapeDtypeStruct(a5.shape, F32),
        compiler_params=_params(("parallel",), 48),
        name="rwkv_tri_solve",
    )(a5)
    t2 = t5.reshape(ch * ch, nslab * slab).T[:nsys]
    return t2.reshape(b, nc, nh, ch, ch)


def _rwkv_scan_body(t_ref, akk_ref, qrk_ref, qrb_ref, kkt_ref, rt_ref, kdec_ref, bdec_ref, kmod_ref, r_ref, v_ref,
                    g_ref, gamc_ref, wkv0_ref, eblk_ref, rk_ref, lnw_ref, lnb_ref, rw_ref, wkv_ref, s_scr):
    c = pl.program_id(1)
    nd = RWKV_HEAD_DIM

    @pl.when(c == 0)
    def _():
        s_scr[...] = wkv0_ref[...]

    di = lax.broadcasted_iota(I32, (nd, nd), 0)
    dj = lax.broadcasted_iota(I32, (nd, nd), 1)
    ys = []
    for h in range(RWKV_HEADS):
        sl = slice(h * nd, (h + 1) * nd)
        tm = t_ref[h]
        vh = v_ref[:, sl]
        bdec = bdec_ref[:, sl]
        qrb = qrb_ref[h]
        w1 = _doth(tm, kkt_ref[:, sl])
        w2 = _doth(tm, _doth(akk_ref[h], vh))
        r2 = rt_ref[:, sl] - _doth(qrb, w1)
        y0 = _doth(qrk_ref[h], vh) - _doth(qrb, w2)
        mm = jnp.where(di == dj, gamc_ref[:, sl], 0.0) - _doth(bdec, w1, TN)
        nt = _doth(vh, kdec_ref[:, sl], TN) - _doth(w2, bdec, TN)
        s0 = s_scr[h]
        ys.append(_doth(r2, s0, NT) + y0)
        s_scr[h] = _doth(s0, mm, NT) + nt
    y = jnp.concatenate(ys, axis=1)
    eblk = eblk_ref[...]
    inv_n = 1.0 / nd
    mean = _dot_exact_rhs(y, eblk) * inv_n
    yc = y - mean
    var = _dot_exact_rhs(yc * yc, eblk) * inv_n
    yn = yc * lax.rsqrt(var + GN_EPS) * lnw_ref[...] + lnb_ref[...]
    v = v_ref[...]
    bonus = _dot_exact_rhs(r_ref[...] * kmod_ref[...] * rk_ref[...], eblk) * v
    rw_ref[...] = ((yn + bonus) * g_ref[...]).astype(rw_ref.dtype)

    @pl.when(c == pl.num_programs(1) - 1)
    def _():
        wkv_ref[...] = s_scr[...]


def _rwkv_scan(tmat, prep, wkv0, lp, eblk):
    akk, qrk, qrb, kkt, rt, kdec, bdec, kmod, r, v, g, gamc = prep
    b, nc, nh, ch, _ = tmat.shape
    tp = kkt.shape[1]
    nd = RWKV_HEAD_DIM
    mspec = pl.BlockSpec((None, None, nh, ch, ch), lambda bi, ci: (bi, ci, 0, 0, 0))
    rspec = pl.BlockSpec((None, ch, RWKV_WIDTH), lambda bi, ci: (bi, ci, 0))
    sspec = pl.BlockSpec((None, nh, nd, nd), lambda bi, ci: (bi, 0, 0, 0))
    consts = (eblk, lp["r_k"], lp["ln_w"], lp["ln_b"])
    return pl.pallas_call(
        _rwkv_scan_body,
        grid=(b, nc),
        in_specs=[mspec] * 4 + [rspec] * 8
        + [pl.BlockSpec((None, None, 1, RWKV_WIDTH), lambda bi, ci: (bi, ci, 0, 0)), sspec]
        + [_full(a) for a in consts],
        out_specs=[rspec, sspec],
        out_shape=[jax.ShapeDtypeStruct((b, tp, RWKV_WIDTH), BF16), jax.ShapeDtypeStruct((b, nh, nd, nd), F32)],
        scratch_shapes=[pltpu.VMEM((nh, nd, nd), F32)],
        compiler_params=_params(("parallel", "arbitrary"), 48),
        name="rwkv_scan",
    )(tmat, akk, qrk, qrb, kkt, rt, kdec, bdec, kmod, r, v, g, gamc, wkv0, *consts)


def _merge_body(x_ref, att_ref, rw_ref, ga_ref, gb_ref, wa_ref, wb_ref, wo_ref, g2_ref, wq_ref,
                x1_ref, h2_ref, qp_ref):
    merged = ga_ref[...] * _dot(att_ref[...], wa_ref[...]) + gb_ref[...] * _dot(rw_ref[...], wb_ref[...])
    x1 = x_ref[...] + _dot(merged.astype(BF16), wo_ref[...])
    x1_ref[...] = x1
    h2 = (x1 * lax.rsqrt(jnp.mean(x1 * x1, axis=-1, keepdims=True) + NORM_EPS) * g2_ref[...]).astype(BF16)
    h2_ref[...] = h2
    qp_ref[...] = _dot(h2, wq_ref[...]).astype(qp_ref.dtype)


def _merge(x2, att, rw, ga, gb, lp, tm):
    m = x2.shape[0]
    kq = lp["w_query"].shape[1]
    row = lambda w: pl.BlockSpec((tm, w), lambda i: (i, 0))
    consts = (lp["w_up_a"], lp["w_up_b"], lp["w_o"], lp["norm2_g"], lp["w_query"])
    return pl.pallas_call(
        _merge_body,
        grid=(m // tm,),
        in_specs=[row(D_MODEL), row(ATTN_WIDTH), row(RWKV_WIDTH), row(D_MODEL), row(D_MODEL)]
        + [_full(a) for a in consts],
        out_specs=[row(D_MODEL), row(D_MODEL), row(kq)],
        out_shape=[jax.ShapeDtypeStruct((m, D_MODEL), F32), jax.ShapeDtypeStruct((m, D_MODEL), BF16),
                   jax.ShapeDtypeStruct((m, kq), BF16)],
        compiler_params=_params(("parallel",), 48),
        name="merge",
    )(x2, att, rw, ga, gb, *consts)


def _extract_top(src_ref, n_out, val_ref, idx_ref):
    nrow = src_ref.shape[0]
    rowi = lax.broadcasted_iota(I32, src_ref.shape, 0)

    def body(r, carry):
        s = src_ref[...]
        m = jnp.max(s, axis=0, keepdims=True)
        idx = jnp.min(jnp.where(s == m, rowi, nrow), axis=0, keepdims=True)
        val_ref[pl.ds(r, 1), :] = m
        idx_ref[pl.ds(r, 1), :] = idx
        src_ref[...] = jnp.where(rowi == idx, NEG_INF, s)
        return carry

    lax.fori_loop(0, n_out, body, 0)


def _peer_topk_body(qp_ref, sk_ref, i1_ref, i2_ref, gate_ref, s_scr, cand_scr, sv_scr, si_scr, bs_scr, bp_scr):
    k = PEER_TOPK
    for c in range(2):
        s_scr[...] = _dot(sk_ref[c], qp_ref[:, c * D_KEY_HALF:(c + 1) * D_KEY_HALF], NT)
        _extract_top(s_scr, k, sv_scr.at[c], si_scr.at[c])
    sv0 = sv_scr[0]
    sv1 = sv_scr[1]
    for a in range(k):
        cand_scr[a * k:(a + 1) * k, :] = sv0[a:a + 1, :] + sv1
    _extract_top(cand_scr, k, bs_scr, bp_scr)
    best_s = bs_scr[...]
    best_pos = bp_scr[...]
    a_sel = best_pos >> 4
    b_sel = best_pos & (k - 1)
    si0 = si_scr[0]
    si1 = si_scr[1]
    i1 = jnp.zeros_like(best_pos)
    i2 = jnp.zeros_like(best_pos)
    for a in range(k):
        i1 = jnp.where(a_sel == a, si0[a:a + 1, :], i1)
        i2 = jnp.where(b_sel == a, si1[a:a + 1, :], i2)
    e = jnp.exp(best_s - jnp.max(best_s, axis=0, keepdims=True))
    i1_ref[...] = i1
    i2_ref[...] = i2
    gate_ref[...] = e / jnp.sum(e, axis=0, keepdims=True)


def _peer_topk(qp, sub_keys, tm):
    m = qp.shape[0]
    k = PEER_TOPK
    ospec = pl.BlockSpec((None, k, tm), lambda i, h: (h, 0, i))
    return pl.pallas_call(
        _peer_topk_body,
        grid=(m // tm, PEER_HEADS),
        in_specs=[pl.BlockSpec((tm, 2 * D_KEY_HALF), lambda i, h: (i, h)),
                  pl.BlockSpec((None, 2, N_KEYS, D_KEY_HALF), lambda i, h: (h, 0, 0, 0))],
        out_specs=[ospec, ospec, ospec],
        out_shape=[jax.ShapeDtypeStruct((PEER_HEADS, k, m), I32), jax.ShapeDtypeStruct((PEER_HEADS, k, m), I32),
                   jax.ShapeDtypeStruct((PEER_HEADS, k, m), F32)],
        scratch_shapes=[pltpu.VMEM((N_KEYS, tm), F32), pltpu.VMEM((k * k, tm), F32),
                        pltpu.VMEM((2, k, tm), F32), pltpu.VMEM((2, k, tm), I32),
                        pltpu.VMEM((k, tm), F32), pltpu.VMEM((k, tm), I32)],
        compiler_params=_params(("parallel", "arbitrary"), 32),
        name="peer_topk",
    )(qp, sub_keys)


def _peer_expert_body(x1_ref, h2_ref, i1_ref, i2_ref, gate_ref, u_ref, v_ref, gf_ref, y_ref, w_scr, acc_scr):
    j = pl.program_id(1)
    tm = h2_ref.shape[0]
    te = u_ref.shape[0]

    @pl.when(j == 0)
    def _():
        acc_scr[...] = jnp.zeros(acc_scr.shape, F32)
        sub = lax.broadcasted_iota(I32, (N_KEYS, N_SLOTS), 0)

        def tok(t, carry):
            g = gate_ref[pl.ds(t, 1), :]
            g_hi = g.astype(BF16).astype(F32)
            g_lo = g - g_hi
            hit1 = sub == i1_ref[pl.ds(t, 1), :]
            hit2 = sub == i2_ref[pl.ds(t, 1), :]
            a = jnp.concatenate([jnp.where(hit1, g_hi, 0.0), jnp.where(hit1, g_lo, 0.0)], axis=1).astype(BF16)
            one_hot = jnp.where(hit2, 1.0, 0.0)
            bm = jnp.concatenate([one_hot, one_hot], axis=1).astype(BF16)
            w_scr[pl.ds(t, N_KEYS, stride=tm), :] = _dot(a, bm, NT)
            return carry

        lax.fori_loop(0, tm, tok, 0)

    x = h2_ref[...]
    pair = 2 * N_KEYS
    for q in range(te // pair):
        h = _dot(x, u_ref[q * pair:(q + 1) * pair, :], NT)
        i1 = j * (te // N_KEYS) + 2 * q
        w = jnp.concatenate([w_scr[pl.ds(pl.multiple_of(i1 * tm, tm), tm), :],
                             w_scr[pl.ds(pl.multiple_of((i1 + 1) * tm, tm), tm), :]], axis=1)
        act = 0.5 * h * (1.0 + lax.erf(h * (2.0 ** -0.5)))
        acc_scr[...] += _dot((w * act).astype(BF16), v_ref[q * pair:(q + 1) * pair, :])

    @pl.when(j == pl.num_programs(1) - 1)
    def _():
        x2 = x1_ref[...] + acc_scr[...]
        y_ref[...] = x2 * lax.rsqrt(jnp.mean(x2 * x2, axis=-1, keepdims=True) + NORM_EPS) * gf_ref[...]


def _peer_experts(x1, h2, i1, i2, gate, eu, ev, gf, tm, te):
    m = x1.shape[0]
    row = lambda w: pl.BlockSpec((tm, w), lambda i, j: (i, 0))
    tab = pl.BlockSpec((te, D_MODEL), lambda i, j: (j, 0))
    return pl.pallas_call(
        _peer_expert_body,
        grid=(m // tm, N_EXPERTS // te),
        in_specs=[row(D_MODEL), row(D_MODEL), row(N_SLOTS), row(N_SLOTS), row(N_SLOTS), tab, tab,
                  pl.BlockSpec((1, D_MODEL), lambda i, j: (0, 0))],
        out_specs=row(D_MODEL),
        out_shape=jax.ShapeDtypeStruct((m, D_MODEL), F32),
        scratch_shapes=[pltpu.VMEM((N_KEYS * tm, N_KEYS), F32), pltpu.VMEM((tm, D_MODEL), F32)],
        compiler_params=_params(("parallel", "arbitrary"), 56),
        name="peer_experts",
    )(x1, h2, i1, i2, gate, eu, ev, gf)


def _pad_cols(a, width):
    return jnp.pad(a, [(0, 0)] * (a.ndim - 1) + [(0, width - a.shape[-1])])


def _prep_layer_params(l, norm1_g, w_in, b_in, attn_sinks, shift_mu, w0, w2, a0, a2, g2, k_k, k_a, r_k, ln_x_w,
                       ln_x_b, w_up_a, w_up_b, w_o, norm2_g, w_query, sub_keys, expert_u, expert_v):
    wi, bi = w_in[l], b_in[l][None, :]
    cat = lambda z: jnp.concatenate([z[:, :OFF_B], _pad_cols(z[:, OFF_B:OFF_GATE], RWKV_PAD), z[:, OFF_GATE:]], axis=1)
    row = lambda z: z[l].reshape(1, -1)
    lora = lambda z, off: jnp.pad(z[l], ((off, LORA_PAD - off - z.shape[1]), (0, 0)))
    return dict(
        norm1_g=row(norm1_g), w_cat=cat(wi).astype(BF16), b_cat=cat(bi), sinks=attn_sinks[l],
        mu=_pad_cols(row(shift_mu), RWKV_PAD), w0=row(w0), a0=row(a0), k_k=row(k_k), k_a=row(k_a),
        w2p=lora(w2, 0), a2p=lora(a2, DECAY_LORA), g2p=lora(g2, DECAY_LORA + AAA_LORA),
        r_k=row(r_k), ln_w=row(ln_x_w), ln_b=row(ln_x_b),
        w_up_a=w_up_a[l].astype(BF16), w_up_b=w_up_b[l].astype(BF16), w_o=w_o[l].astype(BF16),
        norm2_g=row(norm2_g), w_query=w_query[l].astype(BF16), sub_keys=sub_keys[l].astype(BF16),
        expert_u=expert_u[l].astype(BF16), expert_v=expert_v[l].astype(BF16),
    )


def _row_tile(m, pref):
    return pref if m % pref == 0 else m


def _hybrid_layer(x, pos, lp, eblk, gf, win_k, win_v, wkv0, shift0):
    b, t, _ = x.shape
    m = b * t
    x2 = x.reshape(m, D_MODEL)
    tm = _row_tile(m, 256)
    rope_tab = _rope_table(pos, max(t, tm))
    q, k, v, p, ga, gb = _in_proj(x2, lp["norm1_g"], lp["w_cat"], lp["b_cat"], rope_tab, tm)

    k3 = k.reshape(b, t, KV_WIDTH)
    v3 = v.reshape(b, t, KV_WIDTH)
    if win_k is None:
        att = _attn_prompt(lp["sinks"], q.reshape(b, t, ATTN_WIDTH), k3, v3).reshape(m, ATTN_WIDTH)
        keep = min(WINDOW, t)
        new_k, new_v = k3[:, t - keep:], v3[:, t - keep:]
    else:
        n_past = win_k.shape[1]
        qg = q.reshape(b, t, N_KV_HEADS, Q_PER_KV, HEAD_DIM).transpose(0, 2, 3, 1, 4)
        qg = qg.reshape(b, N_KV_HEADS, Q_PER_KV * t, HEAD_DIM)
        ck = win_k.reshape(b, n_past, KV_WIDTH)
        cv = win_v.reshape(b, n_past, KV_WIDTH)
        og = _attn_sample(lp["sinks"], qg, k3, v3, ck, cv, t)
        att = og.reshape(b, N_KV_HEADS, Q_PER_KV, t, HEAD_DIM).transpose(0, 3, 1, 2, 4).reshape(m, ATTN_WIDTH)
        new_k = jnp.concatenate([ck, k3], axis=1)[:, t:]
        new_v = jnp.concatenate([cv, v3], axis=1)[:, t:]
    new_k = new_k.reshape(b, -1, N_KV_HEADS, HEAD_DIM)
    new_v = new_v.reshape(b, -1, N_KV_HEADS, HEAD_DIM)

    p3 = p.reshape(b, t, RWKV_PAD)
    if t % RWKV_CHUNK == 0:
        ch, tp = RWKV_CHUNK, t
    else:
        tp = -(-t // SUBLANES) * SUBLANES
        ch = tp
        p3 = jnp.pad(p3, ((0, 0), (0, tp - t), (0, 0)))
    shift_pad = _pad_cols(shift0, RWKV_PAD)[:, None, :]
    akb, *prep = _rwkv_prep(p3, shift_pad, lp, eblk, ch, t)
    tmat = _tri_solve(akb)
    rw, wkv_t = _rwkv_scan(tmat, prep, wkv0, lp, eblk)
    rw = rw[:, :t].reshape(m, RWKV_WIDTH)
    shift_t = p3[:, t - 1, :RWKV_COLS]

    x1, h2, qp = _merge(x2, att, rw, ga, gb, lp, tm)
    i1, i2, gate = _peer_topk(qp, lp["sub_keys"], tm)
    slots = lambda z: z.reshape(N_SLOTS, m).T
    y = _peer_experts(x1, h2, slots(i1), slots(i2), slots(gate), lp["expert_u"], lp["expert_v"], gf, tm, 1024)
    return y.reshape(b, t, D_MODEL), new_k, new_v, wkv_t, shift_t


def kernel(x_prompt, x_sample, cache_win_k, cache_win_v, state_wkv, state_shift, norm1_g, w_in, b_in, attn_sinks,
           shift_mu, w0, w2, a0, a2, g2, k_k, k_a, r_k, ln_x_w, ln_x_b, w_up_a, w_up_b, w_o, norm2_g, w_query,
           sub_keys, expert_u, expert_v, final_norm_g):
    depth = w_in.shape[0]
    assert depth == 1, "the final RMSNorm is fused into the last layer's expert kernel"
    pos_p = jnp.arange(x_prompt.shape[1])
    pos_s = PAST_LEN + jnp.arange(x_sample.shape[1])
    li = lax.broadcasted_iota(I32, (RWKV_WIDTH, RWKV_WIDTH), 0) // RWKV_HEAD_DIM
    lj = lax.broadcasted_iota(I32, (RWKV_WIDTH, RWKV_WIDTH), 1) // RWKV_HEAD_DIM
    eblk = (li == lj).astype(BF16)
    gf = final_norm_g.reshape(1, D_MODEL)
    lp = _prep_layer_params(0, norm1_g, w_in, b_in, attn_sinks, shift_mu, w0, w2, a0, a2, g2, k_k, k_a, r_k, ln_x_w,
                            ln_x_b, w_up_a, w_up_b, w_o, norm2_g, w_query, sub_keys, expert_u, expert_v)
    bp = x_prompt.shape[0]
    wkv0 = jnp.zeros((bp, RWKV_HEADS, RWKV_HEAD_DIM, RWKV_HEAD_DIM), F32)
    shift0 = jnp.zeros((bp, RWKV_COLS), x_prompt.dtype)
    yp, pk, pv, pw, psh = _hybrid_layer(x_prompt, pos_p, lp, eblk, gf, None, None, wkv0, shift0)
    ys, sk, sv, sw, ssh = _hybrid_layer(x_sample, pos_s, lp, eblk, gf, cache_win_k[0], cache_win_v[0],
                                        state_wkv[0], state_shift[0])
    st = lambda z: z[None]
    return (yp, ys, st(pk), st(pv), st(pw), st(psh), st(sk), st(sv), st(sw), st(ssh))
```

```python
import functools

import jax
import jax.numpy as jnp
from jax import lax
from jax.experimental import pallas as pl
from jax.experimental.pallas import tpu as pltpu

F32, BF16, I32 = jnp.float32, jnp.bfloat16, jnp.int32
HIGHEST = lax.Precision.HIGHEST

D_MODEL = 1024
NORM_EPS = 1e-5
HEAD_DIM = 64
N_Q_HEADS = 8
N_KV_HEADS = 2
Q_PER_KV = N_Q_HEADS // N_KV_HEADS
ATTN_WIDTH = N_Q_HEADS * HEAD_DIM
KV_WIDTH = N_KV_HEADS * HEAD_DIM
WINDOW = 128
ROT_DIM = HEAD_DIM // 4
ROPE_THETA = 500000.0
ATTN_SCALE = HEAD_DIM ** -0.5
RWKV_HEADS = 8
RWKV_HEAD_DIM = 64
RWKV_WIDTH = RWKV_HEADS * RWKV_HEAD_DIM
DECAY_LORA = 32
AAA_LORA = 32
GATE_LORA = 96
GN_EPS = 64e-5
RWKV_COLS = 3 * RWKV_WIDTH + DECAY_LORA + AAA_LORA + GATE_LORA
LORA_PAD = 256
RWKV_PAD = 3 * RWKV_WIDTH + LORA_PAD
OFF_K = ATTN_WIDTH
OFF_V = OFF_K + KV_WIDTH
OFF_B = OFF_V + KV_WIDTH
OFF_GATE = OFF_B + RWKV_COLS
N_KEYS = 128
N_EXPERTS = N_KEYS * N_KEYS
PEER_HEADS = 8
PEER_TOPK = 16
D_KEY_HALF = 128
N_SLOTS = PEER_HEADS * PEER_TOPK
PAST_LEN = 16384
RWKV_CHUNK = 64
LANES = 128
SUBLANES = 8
SOLVE_COLS = 16
TOKEN_UNROLL = 8
W_PITCH_PAD = 8
NEG_INF = float("-inf")

NN = ((1,), (0,))
NT = ((1,), (1,))
TN = ((0,), (0,))


def _dot(a, b, dims=NN, precision=None):
    return lax.dot_general(a, b, (dims, ((), ())), precision=precision, preferred_element_type=F32)


def _split2(x):
    hi = x.astype(BF16)
    return hi, (x - hi.astype(F32)).astype(BF16)


def _doth(a, b, dims=NN):
    a_hi, a_lo = _split2(a)
    b_hi, b_lo = _split2(b)
    return _dot(a_hi, b_hi, dims) + _dot(a_hi, b_lo, dims) + _dot(a_lo, b_hi, dims)


def _split3(x):
    hi = x.astype(BF16)
    r1 = x - hi.astype(F32)
    mid = r1.astype(BF16)
    lo = (r1 - mid.astype(F32)).astype(BF16)
    return hi, mid, lo


def _dot_exact_rhs(x, e):
    hi, mid, lo = _split3(x)
    return _dot(hi, e) + _dot(mid, e) + _dot(lo, e)


def _dot_exact_lhs(e, x):
    hi, mid, lo = _split3(x)
    return _dot(e, hi) + _dot(e, mid) + _dot(e, lo)


def _params(sem, vmem_mb):
    return pltpu.CompilerParams(dimension_semantics=sem, vmem_limit_bytes=vmem_mb * 1024 * 1024)


def _full(a):
    nd = a.ndim
    return pl.BlockSpec(a.shape, lambda *_: (0,) * nd)


SEG_QKV = ATTN_WIDTH + 2 * KV_WIDTH
SEG_P = SEG_QKV + RWKV_PAD
SEG_END = SEG_P + 2 * D_MODEL


def _inproj_body(x_ref, g_ref, w_ref, b_ref, rope_ref, q_ref, k_ref, v_ref, p_ref, ga_ref, gb_ref):
    x = x_ref[...]
    h = (x * lax.rsqrt(jnp.mean(x * x, axis=-1, keepdims=True) + NORM_EPS) * g_ref[...]).astype(BF16)

    def seg(lo, hi):
        return _dot(h, w_ref[:, lo:hi]) + b_ref[:, lo:hi]

    cos = rope_ref[:, 0:LANES]
    sin_up = rope_ref[:, LANES:2 * LANES]
    sin_dn = rope_ref[:, 2 * LANES:3 * LANES]

    def rope(z):
        return z * cos + pltpu.roll(z, ROT_DIM // 2, 1) * sin_up + pltpu.roll(z, LANES - ROT_DIM // 2, 1) * sin_dn

    for c in range(ATTN_WIDTH // LANES):
        q_ref[:, c * LANES:(c + 1) * LANES] = rope(seg(c * LANES, (c + 1) * LANES))
    k_ref[...] = rope(seg(OFF_K, OFF_V))
    v_ref[...] = seg(OFF_V, SEG_QKV)
    p_ref[...] = seg(SEG_QKV, SEG_P)
    ga_ref[...] = jax.nn.sigmoid(seg(SEG_P, SEG_P + D_MODEL))
    gb_ref[...] = jax.nn.sigmoid(seg(SEG_P + D_MODEL, SEG_END))


def _in_proj(x2, g1, w_cat, b_cat, rope_tab, tm):
    m = x2.shape[0]
    nrep = rope_tab.shape[0] // tm
    widths = (ATTN_WIDTH, KV_WIDTH, KV_WIDTH, RWKV_PAD, D_MODEL, D_MODEL)
    return pl.pallas_call(
        _inproj_body,
        grid=(m // tm,),
        in_specs=[
            pl.BlockSpec((tm, D_MODEL), lambda i: (i, 0)),
            _full(g1), _full(w_cat), _full(b_cat),
            pl.BlockSpec((tm, 3 * LANES), lambda i: (i % nrep, 0)),
        ],
        out_specs=[pl.BlockSpec((tm, w), lambda i: (i, 0)) for w in widths],
        out_shape=[jax.ShapeDtypeStruct((m, w), F32) for w in widths],
        compiler_params=_params(("parallel",), 48),
        name="in_proj",
    )(x2, g1, w_cat, b_cat, rope_tab)


def _rope_table(pos, rows):
    t = pos.shape[0]
    half = ROT_DIM // 2
    inv_freq = ROPE_THETA ** (-jnp.arange(half, dtype=F32) / half)
    ang = pos.astype(F32)[:, None] * inv_freq[None, :]
    cos, sin = jnp.cos(ang), jnp.sin(ang)
    rest = HEAD_DIM - ROT_DIM
    z8 = jnp.zeros((t, half), F32)
    cos64 = jnp.concatenate([cos, cos, jnp.ones((t, rest), F32)], axis=1)
    up64 = jnp.concatenate([z8, sin, jnp.zeros((t, rest), F32)], axis=1)
    dn64 = jnp.concatenate([-sin, z8, jnp.zeros((t, rest), F32)], axis=1)
    tab = jnp.concatenate([jnp.tile(a, (1, LANES // HEAD_DIM)) for a in (cos64, up64, dn64)], axis=1)
    if rows > t:
        tab = jnp.tile(tab, (rows // t, 1))
    return tab


def _softmax_av(s, sink, vv):
    m = jnp.maximum(jnp.max(s, axis=-1, keepdims=True), sink)
    p = jnp.exp(s - m)
    den = jnp.sum(p, axis=-1, keepdims=True) + jnp.exp(sink - m)
    return _dot(p.astype(BF16), vv) / den


def _attn_prompt_body(sink_ref, q_ref, kc_ref, kp_ref, vc_ref, vp_ref, o_ref):
    n = pl.program_id(1)
    blk = q_ref.shape[0]
    i = lax.broadcasted_iota(I32, (blk, 2 * blk), 0)
    j = lax.broadcasted_iota(I32, (blk, 2 * blk), 1)
    diff = blk + i - j
    j_lo = jnp.where(n > 0, 0, blk)
    mask = (diff >= 0) & (diff < WINDOW) & (j >= j_lo)
    outs = []
    for g in range(N_KV_HEADS):
        sl = slice(g * HEAD_DIM, (g + 1) * HEAD_DIM)
        kk = jnp.concatenate([kp_ref[:, sl], kc_ref[:, sl]], axis=0).astype(BF16)
        vv = jnp.concatenate([vp_ref[:, sl], vc_ref[:, sl]], axis=0).astype(BF16)
        for hq in range(Q_PER_KV):
            h = g * Q_PER_KV + hq
            qh = q_ref[:, h * HEAD_DIM:(h + 1) * HEAD_DIM].astype(BF16)
            s = jnp.where(mask, _dot(qh, kk, NT) * ATTN_SCALE, NEG_INF)
            outs.append(_softmax_av(s, sink_ref[h], vv))
    o_ref[...] = jnp.concatenate(outs, axis=1).astype(o_ref.dtype)


def _attn_prompt(sinks, q3, k3, v3):
    b, t, _ = q3.shape
    blk = WINDOW
    cur = lambda w: pl.BlockSpec((None, blk, w), lambda bi, n: (bi, n, 0))
    prev = lambda w: pl.BlockSpec((None, blk, w), lambda bi, n: (bi, jnp.maximum(n - 1, 0), 0))
    return pl.pallas_call(
        _attn_prompt_body,
        grid=(b, t // blk),
        in_specs=[pl.BlockSpec(memory_space=pltpu.SMEM), cur(ATTN_WIDTH), cur(KV_WIDTH), prev(KV_WIDTH),
                  cur(KV_WIDTH), prev(KV_WIDTH)],
        out_specs=cur(ATTN_WIDTH),
        out_shape=jax.ShapeDtypeStruct((b, t, ATTN_WIDTH), BF16),
        compiler_params=_params(("parallel", "arbitrary"), 32),
        name="attn_prompt",
    )(sinks, q3, k3, k3, v3, v3)


def _attn_sample_body(sink_ref, q_ref, kn_ref, vn_ref, kc_ref, vc_ref, o_ref, *, t):
    rows = Q_PER_KV * t
    n_past = kc_ref.shape[0]
    ri = lax.broadcasted_iota(I32, (rows, 1), 0)
    qi = ri % t
    jc = lax.broadcasted_iota(I32, (rows, n_past), 1)
    mask_c = (n_past + qi - jc) < WINDOW
    for g in range(N_KV_HEADS):
        sl = slice(g * HEAD_DIM, (g + 1) * HEAD_DIM)
        sink = jnp.zeros((rows, 1), F32)
        for hq in range(Q_PER_KV):
            sink = jnp.where(ri // t == hq, sink_ref[g * Q_PER_KV + hq], sink)
        qg = q_ref[g]
        qb = qg.astype(BF16)
        kn = kn_ref[:, sl]
        vn = vn_ref[:, sl]
        s_c = jnp.where(mask_c, _dot(qb, kc_ref[:, sl].astype(BF16), NT) * ATTN_SCALE, NEG_INF)
        s_n = []
        for jn in range(t):
            sj = jnp.sum(qg * kn[jn:jn + 1, :], axis=-1, keepdims=True) * ATTN_SCALE
            s_n.append(jnp.where(qi >= jn, sj, NEG_INF))
        m = jnp.maximum(jnp.max(s_c, axis=-1, keepdims=True), sink)
        for sj in s_n:
            m = jnp.maximum(m, sj)
        p_c = jnp.exp(s_c - m)
        den = jnp.sum(p_c, axis=-1, keepdims=True) + jnp.exp(sink - m)
        o = _dot(p_c.astype(BF16), vc_ref[:, sl].astype(BF16))
        for jn in range(t):
            p_n = jnp.exp(s_n[jn] - m)
            den = den + p_n
            o = o + p_n * vn[jn:jn + 1, :]
        o_ref[g] = (o / den).astype(o_ref.dtype)


def _attn_sample(sinks, qg, kn, vn, cache_k, cache_v, t):
    b = qg.shape[0]
    rows = Q_PER_KV * t
    n_past = cache_k.shape[1]
    qspec = pl.BlockSpec((None, N_KV_HEADS, rows, HEAD_DIM), lambda bi: (bi, 0, 0, 0))
    nspec = pl.BlockSpec((None, t, KV_WIDTH), lambda bi: (bi, 0, 0))
    cspec = pl.BlockSpec((None, n_past, KV_WIDTH), lambda bi: (bi, 0, 0))
    return pl.pallas_call(
        functools.partial(_attn_sample_body, t=t),
        grid=(b,),
        in_specs=[pl.BlockSpec(memory_space=pltpu.SMEM), qspec, nspec, nspec, cspec, cspec],
        out_specs=qspec,
        out_shape=jax.ShapeDtypeStruct((b, N_KV_HEADS, rows, HEAD_DIM), BF16),
        compiler_params=_params(("parallel",), 32),
        name="attn_sample",
    )(sinks, qg, kn, vn, cache_k, cache_v)


def _rwkv_prep_body(p_ref, pprev_ref, shift_ref, mu_ref, w0_ref, a0_ref, kk_ref, ka_ref, w2_ref, a2_ref, g2_ref,
                    eblk_ref, akb_ref, akk_ref, qrk_ref, qrb_ref, kkt_ref, rt_ref, kdec_ref, bdec_ref, kmod_ref,
                    r_ref, v_ref, g_ref, gamc_ref, *, t_real):
    c = pl.program_id(1)
    ch = p_ref.shape[0]
    pc = p_ref[...]
    row = lax.broadcasted_iota(I32, (ch, 1), 0)
    first = jnp.where(c == 0, shift_ref[...], pprev_ref[SUBLANES - 1:SUBLANES, :])
    prev = jnp.where(row == 0, first, pltpu.roll(pc, 1, 0))
    xm = pc + mu_ref[...] * (prev - pc)
    valid = (c * ch + row) < t_real
    w3 = 3 * RWKV_WIDTH
    r = xm[:, 0:RWKV_WIDTH]
    k = xm[:, RWKV_WIDTH:2 * RWKV_WIDTH]
    v = xm[:, 2 * RWKV_WIDTH:w3]
    lo = xm[:, w3:w3 + LORA_PAD]
    lane = lax.broadcasted_iota(I32, lo.shape, 1)
    act = jnp.where(lane < DECAY_LORA, jnp.tanh(lo),
                    jnp.where(lane < DECAY_LORA + AAA_LORA, lo, jax.nn.sigmoid(lo)))
    z = w0_ref[...] + _doth(act, w2_ref[...])
    w_log = -(jnp.maximum(-z, 0.0) + jnp.log1p(jnp.exp(-jnp.abs(z)))) - 0.5
    lw = jnp.where(valid, -jnp.exp(w_log), 0.0)
    a = jax.nn.sigmoid(a0_ref[...] + _doth(act, a2_ref[...]))
    g = _dot(act.astype(BF16), g2_ref[...].astype(BF16))
    kk = k * kk_ref[...]
    ss = _dot_exact_rhs(kk * kk, eblk_ref[...])
    kk = jnp.where(valid, kk * lax.rsqrt(jnp.maximum(ss, 1e-24)), 0.0)
    kmod = jnp.where(valid, k * (1.0 + (a - 1.0) * ka_ref[...]), 0.0)
    bb = kk * a
    ti = lax.broadcasted_iota(I32, (ch, ch), 0)
    tj = lax.broadcasted_iota(I32, (ch, ch), 1)
    tri = jnp.where(ti >= tj, 1.0, 0.0).astype(BF16)
    cum = _dot_exact_lhs(tri, lw)
    cum_c = cum[ch - 1:ch, :]
    e_neg = jnp.exp(-cum)
    e_rem = jnp.exp(cum_c - cum)
    kkt = kk * jnp.exp(cum - lw)
    rt = r * jnp.exp(cum)
    khat = kmod * e_neg
    bhat = bb * e_neg
    strict = ti > tj
    incl = ti >= tj
    hs = range(RWKV_HEADS)
    sl = [slice(h * RWKV_HEAD_DIM, (h + 1) * RWKV_HEAD_DIM) for h in hs]
    split = lambda z: [_split2(z[:, s]) for s in sl]
    kkt_s, rt_s, khat_s, bhat_s = split(kkt), split(rt), split(khat), split(bhat)

    def dot3(a, b):
        return _dot(a[0], b[0], NT) + _dot(a[0], b[1], NT) + _dot(a[1], b[0], NT)

    akb = [dot3(kkt_s[h], bhat_s[h]) for h in hs]
    akk = [dot3(kkt_s[h], khat_s[h]) for h in hs]
    qrk = [dot3(rt_s[h], khat_s[h]) for h in hs]
    qrb = [dot3(rt_s[h], bhat_s[h]) for h in hs]
    for h in hs:
        akb_ref[h] = jnp.where(strict, akb[h], 0.0)
        akk_ref[h] = jnp.where(strict, akk[h], 0.0)
        qrk_ref[h] = jnp.where(incl, qrk[h], 0.0)
        qrb_ref[h] = jnp.where(incl, qrb[h], 0.0)
    kkt_ref[...] = kkt
    rt_ref[...] = rt
    kdec_ref[...] = kmod * e_rem
    bdec_ref[...] = bb * e_rem
    kmod_ref[...] = kmod
    r_ref[...] = r
    v_ref[...] = v
    g_ref[...] = g
    gamc_ref[...] = jnp.exp(cum_c)


def _rwkv_prep(p3, shift0, lp, eblk, ch, t_real):
    b, tp, _ = p3.shape
    nc = tp // ch
    per8 = ch // SUBLANES
    mats = jax.ShapeDtypeStruct((b, nc, RWKV_HEADS, ch, ch), F32)
    rows = jax.ShapeDtypeStruct((b, tp, RWKV_WIDTH), F32)
    mspec = pl.BlockSpec((None, None, RWKV_HEADS, ch, ch), lambda bi, ci: (bi, ci, 0, 0, 0))
    rspec = pl.BlockSpec((None, ch, RWKV_WIDTH), lambda bi, ci: (bi, ci, 0))
    consts = (lp["mu"], lp["w0"], lp["a0"], lp["k_k"], lp["k_a"], lp["w2p"], lp["a2p"], lp["g2p"], eblk)
    return pl.pallas_call(
        functools.partial(_rwkv_prep_body, t_real=t_real),
        grid=(b, nc),
        in_specs=[
            pl.BlockSpec((None, ch, RWKV_PAD), lambda bi, ci: (bi, ci, 0)),
            pl.BlockSpec((None, SUBLANES, RWKV_PAD), lambda bi, ci: (bi, jnp.maximum(ci * per8 - 1, 0), 0)),
            pl.BlockSpec((None, 1, RWKV_PAD), lambda bi, ci: (bi, 0, 0)),
        ] + [_full(a) for a in consts],
        out_specs=[mspec] * 4 + [rspec] * 8 + [pl.BlockSpec((None, None, 1, RWKV_WIDTH), lambda bi, ci: (bi, ci, 0, 0))],
        out_shape=[mats] * 4 + [rows] * 8 + [jax.ShapeDtypeStruct((b, nc, 1, RWKV_WIDTH), F32)],
        compiler_params=_params(("parallel", "parallel"), 48),
        name="rwkv_prep",
    )(p3, p3, shift0, *consts)


def _tri_solve_body(a_ref, t_ref):
    ch = a_ref.shape[0]
    cols = min(SOLVE_COLS, ch)
    t_ref[...] = jnp.zeros(t_ref.shape, F32)
    ci = lax.broadcasted_iota(I32, (cols, SUBLANES, LANES), 0)

    def row_body(s, carry):
        for kc in range(ch // cols):
            c0 = kc * cols

            @pl.when(s >= c0)
            def _():
                acc0 = jnp.where(ci == s - c0, 1.0, 0.0).astype(F32)

                def r_body(r, acc):
                    return acc - a_ref[s, r][None] * t_ref[r, pl.ds(c0, cols)]

                t_ref[s, pl.ds(c0, cols)] = lax.fori_loop(c0, s, r_body, acc0)
        return carry

    lax.fori_loop(0, ch, row_body, 0)


def _tri_solve(akb):
    b, nc, nh, ch, _ = akb.shape
    nsys = b * nc * nh
    slab = SUBLANES * LANES
    nslab = -(-nsys // slab)
    a2 = akb.reshape(nsys, ch * ch)
    a2 = jnp.pad(a2, ((0, nslab * slab - nsys), (0, 0)))
    a5 = a2.T.reshape(ch, ch, nslab, SUBLANES, LANES)
    spec = pl.BlockSpec((ch, ch, None, SUBLANES, LANES), lambda gi: (0, 0, gi, 0, 0),
                        pipeline_mode=pl.Buffered(1))
    t5 = pl.pallas_call(
        _tri_solve_body,
        grid=(nslab,),
        in_specs=[spec],
        out_specs=spec,
        out_shape=jax.ShapeDtypeStruct(a5.shape, F32),
        compiler_params=_params(("parallel",), 48),
        name="rwkv_tri_solve",
    )(a5)
    t2 = t5.reshape(ch * ch, nslab * slab).T[:nsys]
    return t2.reshape(b, nc, nh, ch, ch)


def _rwkv_scan_body(t_ref, akk_ref, qrk_ref, qrb_ref, kkt_ref, rt_ref, kdec_ref, bdec_ref, kmod_ref, r_ref, v_ref,
                    g_ref, gamc_ref, wkv0_ref, eblk_ref, rk_ref, lnw_ref, lnb_ref, rw_ref, wkv_ref, s_scr):
    c = pl.program_id(1)
    nd = RWKV_HEAD_DIM

    @pl.when(c == 0)
    def _():
        s_scr[...] = wkv0_ref[...]

    di = lax.broadcasted_iota(I32, (nd, nd), 0)
    dj = lax.broadcasted_iota(I32, (nd, nd), 1)
    hs = range(RWKV_HEADS)
    sl = [slice(h * nd, (h + 1) * nd) for h in hs]
    tmat = [t_ref[h] for h in hs]
    vh = [v_ref[:, sl[h]] for h in hs]
    bdec = [bdec_ref[:, sl[h]] for h in hs]
    qrb = [qrb_ref[h] for h in hs]
    w1 = [_doth(tmat[h], kkt_ref[:, sl[h]]) for h in hs]
    av = [_doth(akk_ref[h], vh[h]) for h in hs]
    qv = [_doth(qrk_ref[h], vh[h]) for h in hs]
    vk = [_doth(vh[h], kdec_ref[:, sl[h]], TN) for h in hs]
    w2 = [_doth(tmat[h], av[h]) for h in hs]
    qw1 = [_doth(qrb[h], w1[h]) for h in hs]
    bw1 = [_doth(bdec[h], w1[h], TN) for h in hs]
    qw2 = [_doth(qrb[h], w2[h]) for h in hs]
    wb = [_doth(w2[h], bdec[h], TN) for h in hs]
    s0 = [s_scr[h] for h in hs]
    ys = [_doth(rt_ref[:, sl[h]] - qw1[h], s0[h], NT) + (qv[h] - qw2[h]) for h in hs]
    for h in hs:
        mm = jnp.where(di == dj, gamc_ref[:, sl[h]], 0.0) - bw1[h]
        s_scr[h] = _doth(s0[h], mm, NT) + (vk[h] - wb[h])
    y = jnp.concatenate(ys, axis=1)
    eblk = eblk_ref[...]
    inv_n = 1.0 / nd
    mean = _dot_exact_rhs(y, eblk) * inv_n
    yc = y - mean
    var = _dot_exact_rhs(yc * yc, eblk) * inv_n
    yn = yc * lax.rsqrt(var + GN_EPS) * lnw_ref[...] + lnb_ref[...]
    v = v_ref[...]
    bonus = _dot_exact_rhs(r_ref[...] * kmod_ref[...] * rk_ref[...], eblk) * v
    rw_ref[...] = ((yn + bonus) * g_ref[...]).astype(rw_ref.dtype)

    @pl.when(c == pl.num_programs(1) - 1)
    def _():
        wkv_ref[...] = s_scr[...]


def _rwkv_scan(tmat, prep, wkv0, lp, eblk):
    akk, qrk, qrb, kkt, rt, kdec, bdec, kmod, r, v, g, gamc = prep
    b, nc, nh, ch, _ = tmat.shape
    tp = kkt.shape[1]
    nd = RWKV_HEAD_DIM
    mspec = pl.BlockSpec((None, None, nh, ch, ch), lambda bi, ci: (bi, ci, 0, 0, 0))
    rspec = pl.BlockSpec((None, ch, RWKV_WIDTH), lambda bi, ci: (bi, ci, 0))
    sspec = pl.BlockSpec((None, nh, nd, nd), lambda bi, ci: (bi, 0, 0, 0))
    consts = (eblk, lp["r_k"], lp["ln_w"], lp["ln_b"])
    return pl.pallas_call(
        _rwkv_scan_body,
        grid=(b, nc),
        in_specs=[mspec] * 4 + [rspec] * 8
        + [pl.BlockSpec((None, None, 1, RWKV_WIDTH), lambda bi, ci: (bi, ci, 0, 0)), sspec]
        + [_full(a) for a in consts],
        out_specs=[rspec, sspec],
        out_shape=[jax.ShapeDtypeStruct((b, tp, RWKV_WIDTH), BF16), jax.ShapeDtypeStruct((b, nh, nd, nd), F32)],
        scratch_shapes=[pltpu.VMEM((nh, nd, nd), F32)],
        compiler_params=_params(("parallel", "arbitrary"), 48),
        name="rwkv_scan",
    )(tmat, akk, qrk, qrb, kkt, rt, kdec, bdec, kmod, r, v, g, gamc, wkv0, *consts)


def _merge_body(x_ref, att_ref, rw_ref, ga_ref, gb_ref, wa_ref, wb_ref, wo_ref, g2_ref, wq_ref,
                x1_ref, h2_ref, qp_ref):
    merged = ga_ref[...] * _dot(att_ref[...], wa_ref[...]) + gb_ref[...] * _dot(rw_ref[...], wb_ref[...])
    x1 = x_ref[...] + _dot(merged.astype(BF16), wo_ref[...])
    x1_ref[...] = x1
    h2 = (x1 * lax.rsqrt(jnp.mean(x1 * x1, axis=-1, keepdims=True) + NORM_EPS) * g2_ref[...]).astype(BF16)
    h2_ref[...] = h2
    qp_ref[...] = _dot(h2, wq_ref[...]).astype(qp_ref.dtype)


def _merge(x2, att, rw, ga, gb, lp, tm):
    m = x2.shape[0]
    kq = lp["w_query"].shape[1]
    row = lambda w: pl.BlockSpec((tm, w), lambda i: (i, 0))
    consts = (lp["w_up_a"], lp["w_up_b"], lp["w_o"], lp["norm2_g"], lp["w_query"])
    return pl.pallas_call(
        _merge_body,
        grid=(m // tm,),
        in_specs=[row(D_MODEL), row(ATTN_WIDTH), row(RWKV_WIDTH), row(D_MODEL), row(D_MODEL)]
        + [_full(a) for a in consts],
        out_specs=[row(D_MODEL), row(D_MODEL), row(kq)],
        out_shape=[jax.ShapeDtypeStruct((m, D_MODEL), F32), jax.ShapeDtypeStruct((m, D_MODEL), BF16),
                   jax.ShapeDtypeStruct((m, kq), BF16)],
        compiler_params=_params(("parallel",), 48),
        name="merge",
    )(x2, att, rw, ga, gb, *consts)


def _extract_top(jobs, n_out):
    def body(r, carry):
        for src_ref, val_ref, idx_ref in jobs:
            nrow = src_ref.shape[0]
            rowi = lax.broadcasted_iota(I32, src_ref.shape, 0)
            s = src_ref[...]
            m = jnp.max(s, axis=0, keepdims=True)
            idx = jnp.min(jnp.where(s == m, rowi, nrow), axis=0, keepdims=True)
            val_ref[pl.ds(r, 1), :] = m
            idx_ref[pl.ds(r, 1), :] = idx
            src_ref[...] = jnp.where(rowi == idx, NEG_INF, s)
        return carry

    lax.fori_loop(0, n_out, body, 0)


def _cand_groups(k):
    groups = []
    a = 0
    while a < k and k // (a + 1) > 1:
        nb = k // (a + 1)
        groups.append((a, -(-nb // SUBLANES) * SUBLANES, nb))
        a += 1
    return groups, a


def _n_cand_rows(k):
    groups, a_tail = _cand_groups(k)
    return sum(rows for _, rows, _ in groups) + k - a_tail


def _peer_topk_body(qp_ref, sk_ref, i1_ref, i2_ref, gate_ref, s_scr, cand_scr, sv_scr, si_scr, bs_scr, bp_scr):
    k = PEER_TOPK
    for c in range(2):
        s_scr[c] = _dot(sk_ref[c], qp_ref[:, c * D_KEY_HALF:(c + 1) * D_KEY_HALF], NT)
    _extract_top([(s_scr.at[c], sv_scr.at[c], si_scr.at[c]) for c in range(2)], k)
    sv0 = sv_scr[0]
    sv1 = sv_scr[1]
    groups, a_tail = _cand_groups(k)
    off = 0
    for a, rows, nb in groups:
        blk = sv0[a:a + 1, :] + sv1[0:rows, :]
        if nb < rows:
            blk = jnp.where(lax.broadcasted_iota(I32, blk.shape, 0) < nb, blk, NEG_INF)
        cand_scr[off:off + rows, :] = blk
        off += rows
    cand_scr[off:off + k - a_tail, :] = sv0[a_tail:k, :] + sv1[0:1, :]
    _extract_top([(cand_scr, bs_scr, bp_scr)], k)
    best_s = bs_scr[...]
    pos = bp_scr[...]
    a_sel = jnp.zeros_like(pos)
    b_sel = pos
    off = 0
    for a, rows, nb in groups:
        if a > 0:
            a_sel = jnp.where(pos >= off, a, a_sel)
            b_sel = jnp.where(pos >= off, pos - off, b_sel)
        off += rows
    a_sel = jnp.where(pos >= off, a_tail + pos - off, a_sel)
    b_sel = jnp.where(pos >= off, 0, b_sel)
    si0 = si_scr[0]
    si1 = si_scr[1]
    i1 = jnp.zeros_like(pos)
    i2 = jnp.zeros_like(pos)
    for a in range(k):
        i1 = jnp.where(a_sel == a, si0[a:a + 1, :], i1)
        i2 = jnp.where(b_sel == a, si1[a:a + 1, :], i2)
    e = jnp.exp(best_s - jnp.max(best_s, axis=0, keepdims=True))
    i1_ref[...] = i1
    i2_ref[...] = i2
    gate_ref[...] = e / jnp.sum(e, axis=0, keepdims=True)


def _peer_topk(qp, sub_keys, tm):
    m = qp.shape[0]
    k = PEER_TOPK
    ospec = pl.BlockSpec((None, k, tm), lambda i, h: (h, 0, i))
    return pl.pallas_call(
        _peer_topk_body,
        grid=(m // tm, PEER_HEADS),
        in_specs=[pl.BlockSpec((tm, 2 * D_KEY_HALF), lambda i, h: (i, h)),
                  pl.BlockSpec((None, 2, N_KEYS, D_KEY_HALF), lambda i, h: (h, 0, 0, 0))],
        out_specs=[ospec, ospec, ospec],
        out_shape=[jax.ShapeDtypeStruct((PEER_HEADS, k, m), I32), jax.ShapeDtypeStruct((PEER_HEADS, k, m), I32),
                   jax.ShapeDtypeStruct((PEER_HEADS, k, m), F32)],
        scratch_shapes=[pltpu.VMEM((2, N_KEYS, tm), F32), pltpu.VMEM((_n_cand_rows(k), tm), F32),
                        pltpu.VMEM((2, k, tm), F32), pltpu.VMEM((2, k, tm), I32),
                        pltpu.VMEM((k, tm), F32), pltpu.VMEM((k, tm), I32)],
        compiler_params=_params(("parallel", "arbitrary"), 32),
        name="peer_topk",
    )(qp, sub_keys)


def _peer_expert_body(x1_ref, h2_ref, i1_ref, i2_ref, gate_ref, u_ref, v_ref, gf_ref, y_ref, w_scr, a_scr):
    j = pl.program_id(1)
    tm = h2_ref.shape[0]
    te = u_ref.shape[0]
    pitch = tm + W_PITCH_PAD

    @pl.when(j == 0)
    def _():
        y_ref[...] = x1_ref[...]
        sub = lax.broadcasted_iota(I32, (N_KEYS, N_SLOTS), 0)

        def tok(t, carry):
            g = gate_ref[pl.ds(t, 1), :]
            g_hi = g.astype(BF16).astype(F32)
            g_lo = g - g_hi
            hit1 = sub == i1_ref[pl.ds(t, 1), :]
            hit2 = sub == i2_ref[pl.ds(t, 1), :]
            a = jnp.concatenate([jnp.where(hit1, g_hi, 0.0), jnp.where(hit1, g_lo, 0.0)], axis=1).astype(BF16)
            one_hot = jnp.where(hit2, 1.0, 0.0)
            bm = jnp.concatenate([one_hot, one_hot], axis=1).astype(BF16)
            w_scr[pl.ds(t, N_KEYS, stride=pitch), :] = _dot(a, bm, NT)
            return carry

        lax.fori_loop(0, tm, tok, 0, unroll=TOKEN_UNROLL)

    x = h2_ref[...]
    pair = 2 * N_KEYS
    for q in range(te // pair):
        h = _dot(x, u_ref[q * pair:(q + 1) * pair, :], NT)
        i1 = j * (te // N_KEYS) + 2 * q
        w = jnp.concatenate([w_scr[pl.ds(pl.multiple_of(i1 * pitch, SUBLANES), tm), :],
                             w_scr[pl.ds(pl.multiple_of((i1 + 1) * pitch, SUBLANES), tm), :]], axis=1)
        act = 0.5 * h * (1.0 + lax.erf(h * (2.0 ** -0.5)))
        a_scr[:, q * pair:(q + 1) * pair] = (w * act).astype(BF16)
    y_ref[...] += _dot(a_scr[...], v_ref[...])

    @pl.when(j == pl.num_programs(1) - 1)
    def _():
        x2 = y_ref[...]
        y_ref[...] = x2 * lax.rsqrt(jnp.mean(x2 * x2, axis=-1, keepdims=True) + NORM_EPS) * gf_ref[...]


def _peer_experts(x1, h2, i1, i2, gate, eu, ev, gf, tm, te):
    m = x1.shape[0]
    row = lambda w: pl.BlockSpec((tm, w), lambda i, j: (i, 0))
    tab = pl.BlockSpec((te, D_MODEL), lambda i, j: (j, 0))
    return pl.pallas_call(
        _peer_expert_body,
        grid=(m // tm, N_EXPERTS // te),
        in_specs=[row(D_MODEL), row(D_MODEL), row(N_SLOTS), row(N_SLOTS), row(N_SLOTS), tab, tab,
                  pl.BlockSpec((1, D_MODEL), lambda i, j: (0, 0))],
        out_specs=row(D_MODEL),
        out_shape=jax.ShapeDtypeStruct((m, D_MODEL), F32),
        scratch_shapes=[pltpu.VMEM((N_KEYS * (tm + W_PITCH_PAD), N_KEYS), F32), pltpu.VMEM((tm, te), BF16)],
        compiler_params=_params(("parallel", "arbitrary"), 56),
        name="peer_experts",
    )(x1, h2, i1, i2, gate, eu, ev, gf)


def _pad_cols(a, width):
    return jnp.pad(a, [(0, 0)] * (a.ndim - 1) + [(0, width - a.shape[-1])])


def _prep_layer_params(l, norm1_g, w_in, b_in, attn_sinks, shift_mu, w0, w2, a0, a2, g2, k_k, k_a, r_k, ln_x_w,
                       ln_x_b, w_up_a, w_up_b, w_o, norm2_g, w_query, sub_keys, expert_u, expert_v):
    wi, bi = w_in[l], b_in[l][None, :]
    cat = lambda z: jnp.concatenate([z[:, :OFF_B], _pad_cols(z[:, OFF_B:OFF_GATE], RWKV_PAD), z[:, OFF_GATE:]], axis=1)
    row = lambda z: z[l].reshape(1, -1)
    lora = lambda z, off: jnp.pad(z[l], ((off, LORA_PAD - off - z.shape[1]), (0, 0)))
    return dict(
        norm1_g=row(norm1_g), w_cat=cat(wi).astype(BF16), b_cat=cat(bi), sinks=attn_sinks[l],
        mu=_pad_cols(row(shift_mu), RWKV_PAD), w0=row(w0), a0=row(a0), k_k=row(k_k), k_a=row(k_a),
        w2p=lora(w2, 0), a2p=lora(a2, DECAY_LORA), g2p=lora(g2, DECAY_LORA + AAA_LORA),
        r_k=row(r_k), ln_w=row(ln_x_w), ln_b=row(ln_x_b),
        w_up_a=w_up_a[l].astype(BF16), w_up_b=w_up_b[l].astype(BF16), w_o=w_o[l].astype(BF16),
        norm2_g=row(norm2_g), w_query=w_query[l].astype(BF16), sub_keys=sub_keys[l].astype(BF16),
        expert_u=expert_u[l].astype(BF16), expert_v=expert_v[l].astype(BF16),
    )


def _row_tile(m, pref):
    return pref if m % pref == 0 else m


def _hybrid_layer(x, pos, lp, eblk, gf, win_k, win_v, wkv0, shift0):
    b, t, _ = x.shape
    m = b * t
    x2 = x.reshape(m, D_MODEL)
    tm = _row_tile(m, 256)
    rope_tab = _rope_table(pos, max(t, tm))
    q, k, v, p, ga, gb = _in_proj(x2, lp["norm1_g"], lp["w_cat"], lp["b_cat"], rope_tab, tm)

    k3 = k.reshape(b, t, KV_WIDTH)
    v3 = v.reshape(b, t, KV_WIDTH)
    if win_k is None:
        att = _attn_prompt(lp["sinks"], q.reshape(b, t, ATTN_WIDTH), k3, v3).reshape(m, ATTN_WIDTH)
        keep = min(WINDOW, t)
        new_k, new_v = k3[:, t - keep:], v3[:, t - keep:]
    else:
        n_past = win_k.shape[1]
        qg = q.reshape(b, t, N_KV_HEADS, Q_PER_KV, HEAD_DIM).transpose(0, 2, 3, 1, 4)
        qg = qg.reshape(b, N_KV_HEADS, Q_PER_KV * t, HEAD_DIM)
        ck = win_k.reshape(b, n_past, KV_WIDTH)
        cv = win_v.reshape(b, n_past, KV_WIDTH)
        og = _attn_sample(lp["sinks"], qg, k3, v3, ck, cv, t)
        att = og.reshape(b, N_KV_HEADS, Q_PER_KV, t, HEAD_DIM).transpose(0, 3, 1, 2, 4).reshape(m, ATTN_WIDTH)
        new_k = jnp.concatenate([ck, k3], axis=1)[:, t:]
        new_v = jnp.concatenate([cv, v3], axis=1)[:, t:]
    new_k = new_k.reshape(b, -1, N_KV_HEADS, HEAD_DIM)
    new_v = new_v.reshape(b, -1, N_KV_HEADS, HEAD_DIM)

    p3 = p.reshape(b, t, RWKV_PAD)
    if t % RWKV_CHUNK == 0:
        ch, tp = RWKV_CHUNK, t
    else:
        tp = -(-t // SUBLANES) * SUBLANES
        ch = tp
        p3 = jnp.pad(p3, ((0, 0), (0, tp - t), (0, 0)))
    shift_pad = _pad_cols(shift0, RWKV_PAD)[:, None, :]
    akb, *prep = _rwkv_prep(p3, shift_pad, lp, eblk, ch, t)
    tmat = _tri_solve(akb)
    rw, wkv_t = _rwkv_scan(tmat, prep, wkv0, lp, eblk)
    rw = rw[:, :t].reshape(m, RWKV_WIDTH)
    shift_t = p3[:, t - 1, :RWKV_COLS]

    x1, h2, qp = _merge(x2, att, rw, ga, gb, lp, tm)
    i1, i2, gate = _peer_topk(qp, lp["sub_keys"], tm)
    slots = lambda z: z.reshape(N_SLOTS, m).T
    y = _peer_experts(x1, h2, slots(i1), slots(i2), slots(gate), lp["expert_u"], lp["expert_v"], gf, tm, 1024)
    return y.reshape(b, t, D_MODEL), new_k, new_v, wkv_t, shift_t


def kernel(x_prompt, x_sample, cache_win_k, cache_win_v, state_wkv, state_shift, norm1_g, w_in, b_in, attn_sinks,
           shift_mu, w0, w2, a0, a2, g2, k_k, k_a, r_k, ln_x_w, ln_x_b, w_up_a, w_up_b, w_o, norm2_g, w_query,
           sub_keys, expert_u, expert_v, final_norm_g):
    depth = w_in.shape[0]
    assert depth == 1, "the final RMSNorm is fused into the last layer's expert kernel"
    pos_p = jnp.arange(x_prompt.shape[1])
    pos_s = PAST_LEN + jnp.arange(x_sample.shape[1])
    li = lax.broadcasted_iota(I32, (RWKV_WIDTH, RWKV_WIDTH), 0) // RWKV_HEAD_DIM
    lj = lax.broadcasted_iota(I32, (RWKV_WIDTH, RWKV_WIDTH), 1) // RWKV_HEAD_DIM
    eblk = (li == lj).astype(BF16)
    gf = final_norm_g.reshape(1, D_MODEL)
    lp = _prep_layer_params(0, norm1_g, w_in, b_in, attn_sinks, shift_mu, w0, w2, a0, a2, g2, k_k, k_a, r_k, ln_x_w,
                            ln_x_b, w_up_a, w_up_b, w_o, norm2_g, w_query, sub_keys, expert_u, expert_v)
    bp = x_prompt.shape[0]
    wkv0 = jnp.zeros((bp, RWKV_HEADS, RWKV_HEAD_DIM, RWKV_HEAD_DIM), F32)
    shift0 = jnp.zeros((bp, RWKV_COLS), x_prompt.dtype)
    yp, pk, pv, pw, psh = _hybrid_layer(x_prompt, pos_p, lp, eblk, gf, None, None, wkv0, shift0)
    ys, sk, sv, sw, ssh = _hybrid_layer(x_sample, pos_s, lp, eblk, gf, cache_win_k[0], cache_win_v[0],
                                        state_wkv[0], state_shift[0])
    st = lambda z: z[None]
    return (yp, ys, st(pk), st(pv), st(pw), st(psh), st(sk), st(sv), st(sw), st(ssh))
```

```python
import functools

import jax
import jax.numpy as jnp
from jax import lax
from jax.experimental import pallas as pl
from jax.experimental.pallas import tpu as pltpu

F32, BF16, I32 = jnp.float32, jnp.bfloat16, jnp.int32
HIGHEST = lax.Precision.HIGHEST

D_MODEL = 1024
NORM_EPS = 1e-5
HEAD_DIM = 64
N_Q_HEADS = 8
N_KV_HEADS = 2
Q_PER_KV = N_Q_HEADS // N_KV_HEADS
ATTN_WIDTH = N_Q_HEADS * HEAD_DIM
KV_WIDTH = N_KV_HEADS * HEAD_DIM
WINDOW = 128
ROT_DIM = HEAD_DIM // 4
ROPE_THETA = 500000.0
ATTN_SCALE = HEAD_DIM ** -0.5
RWKV_HEADS = 8
RWKV_HEAD_DIM = 64
RWKV_WIDTH = RWKV_HEADS * RWKV_HEAD_DIM
DECAY_LORA = 32
AAA_LORA = 32
GATE_LORA = 96
GN_EPS = 64e-5
RWKV_COLS = 3 * RWKV_WIDTH + DECAY_LORA + AAA_LORA + GATE_LORA
LORA_PAD = 256
RWKV_PAD = 3 * RWKV_WIDTH + LORA_PAD
OFF_K = ATTN_WIDTH
OFF_V = OFF_K + KV_WIDTH
OFF_B = OFF_V + KV_WIDTH
OFF_GATE = OFF_B + RWKV_COLS
N_KEYS = 128
N_EXPERTS = N_KEYS * N_KEYS
PEER_HEADS = 8
PEER_TOPK = 16
D_KEY_HALF = 128
N_SLOTS = PEER_HEADS * PEER_TOPK
PAST_LEN = 16384
RWKV_CHUNK = 64
LANES = 128
SUBLANES = 8
SOLVE_COLS = 16
TOKEN_UNROLL = 32
RWKV_BATCH_BLOCK = 8
EXPERT_TOKEN_TILE = 512
EXPERT_TILE = 1024
EXPERT_ROW_BLOCK = 512
W_PITCH_PAD = 8
NEG_INF = float("-inf")

NN = ((1,), (0,))
NT = ((1,), (1,))
TN = ((0,), (0,))


def _dot(a, b, dims=NN, precision=None):
    return lax.dot_general(a, b, (dims, ((), ())), precision=precision, preferred_element_type=F32)


def _split2(x):
    hi = x.astype(BF16)
    return hi, (x - hi.astype(F32)).astype(BF16)


def _doth(a, b, dims=NN):
    a_hi, a_lo = _split2(a)
    b_hi, b_lo = _split2(b)
    return _dot(a_hi, b_hi, dims) + _dot(a_hi, b_lo, dims) + _dot(a_lo, b_hi, dims)


def _split3(x):
    hi = x.astype(BF16)
    r1 = x - hi.astype(F32)
    mid = r1.astype(BF16)
    lo = (r1 - mid.astype(F32)).astype(BF16)
    return hi, mid, lo


def _dot_exact_rhs(x, e):
    hi, mid, lo = _split3(x)
    return _dot(hi, e) + _dot(mid, e) + _dot(lo, e)


def _dot_exact_lhs(e, x):
    hi, mid, lo = _split3(x)
    return _dot(e, hi) + _dot(e, mid) + _dot(e, lo)


def _params(sem, vmem_mb):
    return pltpu.CompilerParams(dimension_semantics=sem, vmem_limit_bytes=vmem_mb * 1024 * 1024)


def _full(a):
    nd = a.ndim
    return pl.BlockSpec(a.shape, lambda *_: (0,) * nd)


SEG_QKV = ATTN_WIDTH + 2 * KV_WIDTH
SEG_P = SEG_QKV + RWKV_PAD
SEG_END = SEG_P + 2 * D_MODEL


def _inproj_body(x_ref, g_ref, w_ref, b_ref, rope_ref, q_ref, k_ref, v_ref, p_ref, ga_ref, gb_ref):
    x = x_ref[...]
    h = (x * lax.rsqrt(jnp.mean(x * x, axis=-1, keepdims=True) + NORM_EPS) * g_ref[...]).astype(BF16)

    def seg(lo, hi):
        return _dot(h, w_ref[:, lo:hi]) + b_ref[:, lo:hi]

    cos = rope_ref[:, 0:LANES]
    sin_up = rope_ref[:, LANES:2 * LANES]
    sin_dn = rope_ref[:, 2 * LANES:3 * LANES]

    def rope(z):
        return z * cos + pltpu.roll(z, ROT_DIM // 2, 1) * sin_up + pltpu.roll(z, LANES - ROT_DIM // 2, 1) * sin_dn

    for c in range(ATTN_WIDTH // LANES):
        q_ref[:, c * LANES:(c + 1) * LANES] = rope(seg(c * LANES, (c + 1) * LANES))
    k_ref[...] = rope(seg(OFF_K, OFF_V))
    v_ref[...] = seg(OFF_V, SEG_QKV)
    p_ref[...] = seg(SEG_QKV, SEG_P)
    ga_ref[...] = jax.nn.sigmoid(seg(SEG_P, SEG_P + D_MODEL))
    gb_ref[...] = jax.nn.sigmoid(seg(SEG_P + D_MODEL, SEG_END))


def _in_proj(x2, g1, w_cat, b_cat, rope_tab, tm):
    m = x2.shape[0]
    nrep = rope_tab.shape[0] // tm
    widths = (ATTN_WIDTH, KV_WIDTH, KV_WIDTH, RWKV_PAD, D_MODEL, D_MODEL)
    return pl.pallas_call(
        _inproj_body,
        grid=(m // tm,),
        in_specs=[
            pl.BlockSpec((tm, D_MODEL), lambda i: (i, 0)),
            _full(g1), _full(w_cat), _full(b_cat),
            pl.BlockSpec((tm, 3 * LANES), lambda i: (i % nrep, 0)),
        ],
        out_specs=[pl.BlockSpec((tm, w), lambda i: (i, 0)) for w in widths],
        out_shape=[jax.ShapeDtypeStruct((m, w), F32) for w in widths],
        compiler_params=_params(("parallel",), 48),
        name="in_proj",
    )(x2, g1, w_cat, b_cat, rope_tab)


def _rope_table(pos, rows):
    t = pos.shape[0]
    half = ROT_DIM // 2
    inv_freq = ROPE_THETA ** (-jnp.arange(half, dtype=F32) / half)
    ang = pos.astype(F32)[:, None] * inv_freq[None, :]
    cos, sin = jnp.cos(ang), jnp.sin(ang)
    rest = HEAD_DIM - ROT_DIM
    z8 = jnp.zeros((t, half), F32)
    cos64 = jnp.concatenate([cos, cos, jnp.ones((t, rest), F32)], axis=1)
    up64 = jnp.concatenate([z8, sin, jnp.zeros((t, rest), F32)], axis=1)
    dn64 = jnp.concatenate([-sin, z8, jnp.zeros((t, rest), F32)], axis=1)
    tab = jnp.concatenate([jnp.tile(a, (1, LANES // HEAD_DIM)) for a in (cos64, up64, dn64)], axis=1)
    if rows > t:
        tab = jnp.tile(tab, (rows // t, 1))
    return tab


def _softmax_av(s, sink, vv):
    m = jnp.maximum(jnp.max(s, axis=-1, keepdims=True), sink)
    p = jnp.exp(s - m)
    den = jnp.sum(p, axis=-1, keepdims=True) + jnp.exp(sink - m)
    return _dot(p.astype(BF16), vv) / den


def _attn_prompt_body(sink_ref, q_ref, kc_ref, kp_ref, vc_ref, vp_ref, o_ref):
    n = pl.program_id(1)
    blk = q_ref.shape[0]
    i = lax.broadcasted_iota(I32, (blk, 2 * blk), 0)
    j = lax.broadcasted_iota(I32, (blk, 2 * blk), 1)
    diff = blk + i - j
    j_lo = jnp.where(n > 0, 0, blk)
    mask = (diff >= 0) & (diff < WINDOW) & (j >= j_lo)
    outs = []
    for g in range(N_KV_HEADS):
        sl = slice(g * HEAD_DIM, (g + 1) * HEAD_DIM)
        kk = jnp.concatenate([kp_ref[:, sl], kc_ref[:, sl]], axis=0).astype(BF16)
        vv = jnp.concatenate([vp_ref[:, sl], vc_ref[:, sl]], axis=0).astype(BF16)
        for hq in range(Q_PER_KV):
            h = g * Q_PER_KV + hq
            qh = q_ref[:, h * HEAD_DIM:(h + 1) * HEAD_DIM].astype(BF16)
            s = jnp.where(mask, _dot(qh, kk, NT) * ATTN_SCALE, NEG_INF)
            outs.append(_softmax_av(s, sink_ref[h], vv))
    o_ref[...] = jnp.concatenate(outs, axis=1).astype(o_ref.dtype)


def _attn_prompt(sinks, q3, k3, v3):
    b, t, _ = q3.shape
    blk = WINDOW
    cur = lambda w: pl.BlockSpec((None, blk, w), lambda bi, n: (bi, n, 0))
    prev = lambda w: pl.BlockSpec((None, blk, w), lambda bi, n: (bi, jnp.maximum(n - 1, 0), 0))
    return pl.pallas_call(
        _attn_prompt_body,
        grid=(b, t // blk),
        in_specs=[pl.BlockSpec(memory_space=pltpu.SMEM), cur(ATTN_WIDTH), cur(KV_WIDTH), prev(KV_WIDTH),
                  cur(KV_WIDTH), prev(KV_WIDTH)],
        out_specs=cur(ATTN_WIDTH),
        out_shape=jax.ShapeDtypeStruct((b, t, ATTN_WIDTH), BF16),
        compiler_params=_params(("parallel", "arbitrary"), 32),
        name="attn_prompt",
    )(sinks, q3, k3, k3, v3, v3)


def _attn_sample_body(sink_ref, q_ref, kn_ref, vn_ref, kc_ref, vc_ref, o_ref, *, t):
    rows = Q_PER_KV * t
    n_past = kc_ref.shape[0]
    ri = lax.broadcasted_iota(I32, (rows, 1), 0)
    qi = ri % t
    jc = lax.broadcasted_iota(I32, (rows, n_past), 1)
    mask_c = (n_past + qi - jc) < WINDOW
    for g in range(N_KV_HEADS):
        sl = slice(g * HEAD_DIM, (g + 1) * HEAD_DIM)
        sink = jnp.zeros((rows, 1), F32)
        for hq in range(Q_PER_KV):
            sink = jnp.where(ri // t == hq, sink_ref[g * Q_PER_KV + hq], sink)
        qg = q_ref[g]
        qb = qg.astype(BF16)
        kn = kn_ref[:, sl]
        vn = vn_ref[:, sl]
        s_c = jnp.where(mask_c, _dot(qb, kc_ref[:, sl].astype(BF16), NT) * ATTN_SCALE, NEG_INF)
        s_n = []
        for jn in range(t):
            sj = jnp.sum(qg * kn[jn:jn + 1, :], axis=-1, keepdims=True) * ATTN_SCALE
            s_n.append(jnp.where(qi >= jn, sj, NEG_INF))
        m = jnp.maximum(jnp.max(s_c, axis=-1, keepdims=True), sink)
        for sj in s_n:
            m = jnp.maximum(m, sj)
        p_c = jnp.exp(s_c - m)
        den = jnp.sum(p_c, axis=-1, keepdims=True) + jnp.exp(sink - m)
        o = _dot(p_c.astype(BF16), vc_ref[:, sl].astype(BF16))
        for jn in range(t):
            p_n = jnp.exp(s_n[jn] - m)
            den = den + p_n
            o = o + p_n * vn[jn:jn + 1, :]
        o_ref[g] = (o / den).astype(o_ref.dtype)


def _attn_sample(sinks, qg, kn, vn, cache_k, cache_v, t):
    b = qg.shape[0]
    rows = Q_PER_KV * t
    n_past = cache_k.shape[1]
    qspec = pl.BlockSpec((None, N_KV_HEADS, rows, HEAD_DIM), lambda bi: (bi, 0, 0, 0))
    nspec = pl.BlockSpec((None, t, KV_WIDTH), lambda bi: (bi, 0, 0))
    cspec = pl.BlockSpec((None, n_past, KV_WIDTH), lambda bi: (bi, 0, 0))
    return pl.pallas_call(
        functools.partial(_attn_sample_body, t=t),
        grid=(b,),
        in_specs=[pl.BlockSpec(memory_space=pltpu.SMEM), qspec, nspec, nspec, cspec, cspec],
        out_specs=qspec,
        out_shape=jax.ShapeDtypeStruct((b, N_KV_HEADS, rows, HEAD_DIM), BF16),
        compiler_params=_params(("parallel",), 32),
        name="attn_sample",
    )(sinks, qg, kn, vn, cache_k, cache_v)


N_PREP_CONSTS = 9


def _rwkv_prep_body(*refs, t_real):
    seq, consts, outs = refs[:3], refs[3:3 + N_PREP_CONSTS], refs[3 + N_PREP_CONSTS:]
    for bi in range(seq[0].shape[0]):
        _rwkv_prep_one(*[r.at[bi] for r in seq], *consts, *[r.at[bi] for r in outs], t_real=t_real)


def _rwkv_prep_one(p_ref, pprev_ref, shift_ref, mu_ref, w0_ref, a0_ref, kk_ref, ka_ref, w2_ref, a2_ref, g2_ref,
                   eblk_ref, akb_ref, akk_ref, qrk_ref, qrb_ref, kkt_ref, rt_ref, kdec_ref, bdec_ref, kmod_ref,
                   r_ref, v_ref, g_ref, gamc_ref, *, t_real):
    c = pl.program_id(1)
    ch = p_ref.shape[0]
    pc = p_ref[...]
    row = lax.broadcasted_iota(I32, (ch, 1), 0)
    first = jnp.where(c == 0, shift_ref[...], pprev_ref[SUBLANES - 1:SUBLANES, :])
    prev = jnp.where(row == 0, first, pltpu.roll(pc, 1, 0))
    xm = pc + mu_ref[...] * (prev - pc)
    valid = (c * ch + row) < t_real
    w3 = 3 * RWKV_WIDTH
    r = xm[:, 0:RWKV_WIDTH]
    k = xm[:, RWKV_WIDTH:2 * RWKV_WIDTH]
    v = xm[:, 2 * RWKV_WIDTH:w3]
    lo = xm[:, w3:w3 + LORA_PAD]
    lane = lax.broadcasted_iota(I32, lo.shape, 1)
    act = jnp.where(lane < DECAY_LORA, jnp.tanh(lo),
                    jnp.where(lane < DECAY_LORA + AAA_LORA, lo, jax.nn.sigmoid(lo)))
    z = w0_ref[...] + _doth(act, w2_ref[...])
    w_log = -(jnp.maximum(-z, 0.0) + jnp.log1p(jnp.exp(-jnp.abs(z)))) - 0.5
    lw = jnp.where(valid, -jnp.exp(w_log), 0.0)
    a = jax.nn.sigmoid(a0_ref[...] + _doth(act, a2_ref[...]))
    g = _dot(act.astype(BF16), g2_ref[...].astype(BF16))
    kk = k * kk_ref[...]
    ss = _dot_exact_rhs(kk * kk, eblk_ref[...])
    kk = jnp.where(valid, kk * lax.rsqrt(jnp.maximum(ss, 1e-24)), 0.0)
    kmod = jnp.where(valid, k * (1.0 + (a - 1.0) * ka_ref[...]), 0.0)
    bb = kk * a
    ti = lax.broadcasted_iota(I32, (ch, ch), 0)
    tj = lax.broadcasted_iota(I32, (ch, ch), 1)
    tri = jnp.where(ti >= tj, 1.0, 0.0).astype(BF16)
    cum = _dot_exact_lhs(tri, lw)
    cum_c = cum[ch - 1:ch, :]
    e_neg = jnp.exp(-cum)
    e_rem = jnp.exp(cum_c - cum)
    kkt = kk * jnp.exp(cum - lw)
    rt = r * jnp.exp(cum)
    khat = kmod * e_neg
    bhat = bb * e_neg
    strict = ti > tj
    incl = ti >= tj
    hs = range(RWKV_HEADS)
    sl = [slice(h * RWKV_HEAD_DIM, (h + 1) * RWKV_HEAD_DIM) for h in hs]
    split = lambda z: [_split2(z[:, s]) for s in sl]
    kkt_s, rt_s, khat_s, bhat_s = split(kkt), split(rt), split(khat), split(bhat)

    def dot3(a, b):
        return _dot(a[0], b[0], NT) + _dot(a[0], b[1], NT) + _dot(a[1], b[0], NT)

    akb = [dot3(kkt_s[h], bhat_s[h]) for h in hs]
    akk = [dot3(kkt_s[h], khat_s[h]) for h in hs]
    qrk = [dot3(rt_s[h], khat_s[h]) for h in hs]
    qrb = [dot3(rt_s[h], bhat_s[h]) for h in hs]
    for h in hs:
        akb_ref[h] = jnp.where(strict, akb[h], 0.0)
        akk_ref[h] = jnp.where(strict, akk[h], 0.0)
        qrk_ref[h] = jnp.where(incl, qrk[h], 0.0)
        qrb_ref[h] = jnp.where(incl, qrb[h], 0.0)
    kkt_ref[...] = kkt
    rt_ref[...] = rt
    kdec_ref[...] = kmod * e_rem
    bdec_ref[...] = bb * e_rem
    kmod_ref[...] = kmod
    r_ref[...] = r
    v_ref[...] = v
    g_ref[...] = g
    gamc_ref[...] = jnp.exp(cum_c)


def _rwkv_prep(p3, shift0, lp, eblk, ch, t_real, bb):
    b, tp, _ = p3.shape
    nc = tp // ch
    per8 = ch // SUBLANES
    mats = jax.ShapeDtypeStruct((b, nc, RWKV_HEADS, ch, ch), F32)
    rows = jax.ShapeDtypeStruct((b, tp, RWKV_WIDTH), F32)
    mspec = pl.BlockSpec((bb, None, RWKV_HEADS, ch, ch), lambda bi, ci: (bi, ci, 0, 0, 0))
    rspec = pl.BlockSpec((bb, ch, RWKV_WIDTH), lambda bi, ci: (bi, ci, 0))
    consts = (lp["mu"], lp["w0"], lp["a0"], lp["k_k"], lp["k_a"], lp["w2p"], lp["a2p"], lp["g2p"], eblk)
    assert len(consts) == N_PREP_CONSTS
    return pl.pallas_call(
        functools.partial(_rwkv_prep_body, t_real=t_real),
        grid=(b // bb, nc),
        in_specs=[
            pl.BlockSpec((bb, ch, RWKV_PAD), lambda bi, ci: (bi, ci, 0)),
            pl.BlockSpec((bb, SUBLANES, RWKV_PAD), lambda bi, ci: (bi, jnp.maximum(ci * per8 - 1, 0), 0)),
            pl.BlockSpec((bb, 1, RWKV_PAD), lambda bi, ci: (bi, 0, 0)),
        ] + [_full(a) for a in consts],
        out_specs=[mspec] * 4 + [rspec] * 8 + [pl.BlockSpec((bb, None, 1, RWKV_WIDTH), lambda bi, ci: (bi, ci, 0, 0))],
        out_shape=[mats] * 4 + [rows] * 8 + [jax.ShapeDtypeStruct((b, nc, 1, RWKV_WIDTH), F32)],
        compiler_params=_params(("parallel", "parallel"), 48),
        name="rwkv_prep",
    )(p3, p3, shift0, *consts)


def _tri_solve_body(a_ref, t_ref):
    ch = a_ref.shape[0]
    cols = min(SOLVE_COLS, ch)
    t_ref[...] = jnp.zeros(t_ref.shape, F32)
    ci = lax.broadcasted_iota(I32, (cols, SUBLANES, LANES), 0)

    def row_body(s, carry):
        for kc in range(ch // cols):
            c0 = kc * cols

            @pl.when(s >= c0)
            def _():
                acc0 = jnp.where(ci == s - c0, 1.0, 0.0).astype(F32)

                def r_body(r, acc):
                    return acc - a_ref[s, r][None] * t_ref[r, pl.ds(c0, cols)]

                t_ref[s, pl.ds(c0, cols)] = lax.fori_loop(c0, s, r_body, acc0)
        return carry

    lax.fori_loop(0, ch, row_body, 0)


def _tri_solve(akb):
    b, nc, nh, ch, _ = akb.shape
    nsys = b * nc * nh
    slab = SUBLANES * LANES
    nslab = -(-nsys // slab)
    a2 = akb.reshape(nsys, ch * ch)
    a2 = jnp.pad(a2, ((0, nslab * slab - nsys), (0, 0)))
    a5 = a2.T.reshape(ch, ch, nslab, SUBLANES, LANES)
    spec = pl.BlockSpec((ch, ch, None, SUBLANES, LANES), lambda gi: (0, 0, gi, 0, 0),
                        pipeline_mode=pl.Buffered(1))
    t5 = pl.pallas_call(
        _tri_solve_body,
        grid=(nslab,),
        in_specs=[spec],
        out_specs=spec,
        out_shape=jax.ShapeDtypeStruct(a5.shape, F32),
        compiler_params=_params(("parallel",), 48),
        name="rwkv_tri_solve",
    )(a5)
    t2 = t5.reshape(ch * ch, nslab * slab).T[:nsys]
    return t2.reshape(b, nc, nh, ch, ch)


N_SCAN_SEQ = 14
N_SCAN_CONSTS = 4


def _rwkv_scan_body(*refs):
    seq, consts, rest = refs[:N_SCAN_SEQ], refs[N_SCAN_SEQ:N_SCAN_SEQ + N_SCAN_CONSTS], refs[N_SCAN_SEQ + N_SCAN_CONSTS:]
    for bi in range(seq[0].shape[0]):
        _rwkv_scan_one(*[r.at[bi] for r in seq], *consts, *[r.at[bi] for r in rest])


def _rwkv_scan_one(t_ref, akk_ref, qrk_ref, qrb_ref, kkt_ref, rt_ref, kdec_ref, bdec_ref, kmod_ref, r_ref, v_ref,
                   g_ref, gamc_ref, wkv0_ref, eblk_ref, rk_ref, lnw_ref, lnb_ref, rw_ref, wkv_ref, s_scr):
    c = pl.program_id(1)
    nd = RWKV_HEAD_DIM

    @pl.when(c == 0)
    def _():
        s_scr[...] = wkv0_ref[...]

    di = lax.broadcasted_iota(I32, (nd, nd), 0)
    dj = lax.broadcasted_iota(I32, (nd, nd), 1)
    hs = range(RWKV_HEADS)
    sl = [slice(h * nd, (h + 1) * nd) for h in hs]
    tmat = [t_ref[h] for h in hs]
    vh = [v_ref[:, sl[h]] for h in hs]
    bdec = [bdec_ref[:, sl[h]] for h in hs]
    qrb = [qrb_ref[h] for h in hs]
    w1 = [_doth(tmat[h], kkt_ref[:, sl[h]]) for h in hs]
    av = [_doth(akk_ref[h], vh[h]) for h in hs]
    qv = [_doth(qrk_ref[h], vh[h]) for h in hs]
    vk = [_doth(vh[h], kdec_ref[:, sl[h]], TN) for h in hs]
    w2 = [_doth(tmat[h], av[h]) for h in hs]
    qw1 = [_doth(qrb[h], w1[h]) for h in hs]
    bw1 = [_doth(bdec[h], w1[h], TN) for h in hs]
    qw2 = [_doth(qrb[h], w2[h]) for h in hs]
    wb = [_doth(w2[h], bdec[h], TN) for h in hs]
    s0 = [s_scr[h] for h in hs]
    ys = [_doth(rt_ref[:, sl[h]] - qw1[h], s0[h], NT) + (qv[h] - qw2[h]) for h in hs]
    for h in hs:
        mm = jnp.where(di == dj, gamc_ref[:, sl[h]], 0.0) - bw1[h]
        s_scr[h] = _doth(s0[h], mm, NT) + (vk[h] - wb[h])
    y = jnp.concatenate(ys, axis=1)
    eblk = eblk_ref[...]
    inv_n = 1.0 / nd
    mean = _dot_exact_rhs(y, eblk) * inv_n
    yc = y - mean
    var = _dot_exact_rhs(yc * yc, eblk) * inv_n
    yn = yc * lax.rsqrt(var + GN_EPS) * lnw_ref[...] + lnb_ref[...]
    v = v_ref[...]
    bonus = _dot_exact_rhs(r_ref[...] * kmod_ref[...] * rk_ref[...], eblk) * v
    rw_ref[...] = ((yn + bonus) * g_ref[...]).astype(rw_ref.dtype)

    @pl.when(c == pl.num_programs(1) - 1)
    def _():
        wkv_ref[...] = s_scr[...]


def _rwkv_scan(tmat, prep, wkv0, lp, eblk, bb):
    akk, qrk, qrb, kkt, rt, kdec, bdec, kmod, r, v, g, gamc = prep
    b, nc, nh, ch, _ = tmat.shape
    tp = kkt.shape[1]
    nd = RWKV_HEAD_DIM
    mspec = pl.BlockSpec((bb, None, nh, ch, ch), lambda bi, ci: (bi, ci, 0, 0, 0))
    rspec = pl.BlockSpec((bb, ch, RWKV_WIDTH), lambda bi, ci: (bi, ci, 0))
    sspec = pl.BlockSpec((bb, nh, nd, nd), lambda bi, ci: (bi, 0, 0, 0))
    consts = (eblk, lp["r_k"], lp["ln_w"], lp["ln_b"])
    assert len(consts) == N_SCAN_CONSTS
    return pl.pallas_call(
        _rwkv_scan_body,
        grid=(b // bb, nc),
        in_specs=[mspec] * 4 + [rspec] * 8
        + [pl.BlockSpec((bb, None, 1, RWKV_WIDTH), lambda bi, ci: (bi, ci, 0, 0)), sspec]
        + [_full(a) for a in consts],
        out_specs=[rspec, sspec],
        out_shape=[jax.ShapeDtypeStruct((b, tp, RWKV_WIDTH), BF16), jax.ShapeDtypeStruct((b, nh, nd, nd), F32)],
        scratch_shapes=[pltpu.VMEM((bb, nh, nd, nd), F32)],
        compiler_params=_params(("parallel", "arbitrary"), 48),
        name="rwkv_scan",
    )(tmat, akk, qrk, qrb, kkt, rt, kdec, bdec, kmod, r, v, g, gamc, wkv0, *consts)


def _merge_body(x_ref, att_ref, rw_ref, ga_ref, gb_ref, wa_ref, wb_ref, wo_ref, g2_ref, wq_ref,
                x1_ref, h2_ref, qp_ref):
    merged = ga_ref[...] * _dot(att_ref[...], wa_ref[...]) + gb_ref[...] * _dot(rw_ref[...], wb_ref[...])
    x1 = x_ref[...] + _dot(merged.astype(BF16), wo_ref[...])
    x1_ref[...] = x1
    h2 = (x1 * lax.rsqrt(jnp.mean(x1 * x1, axis=-1, keepdims=True) + NORM_EPS) * g2_ref[...]).astype(BF16)
    h2_ref[...] = h2
    qp_ref[...] = _dot(h2, wq_ref[...]).astype(qp_ref.dtype)


def _merge(x2, att, rw, ga, gb, lp, tm):
    m = x2.shape[0]
    kq = lp["w_query"].shape[1]
    row = lambda w: pl.BlockSpec((tm, w), lambda i: (i, 0))
    consts = (lp["w_up_a"], lp["w_up_b"], lp["w_o"], lp["norm2_g"], lp["w_query"])
    return pl.pallas_call(
        _merge_body,
        grid=(m // tm,),
        in_specs=[row(D_MODEL), row(ATTN_WIDTH), row(RWKV_WIDTH), row(D_MODEL), row(D_MODEL)]
        + [_full(a) for a in consts],
        out_specs=[row(D_MODEL), row(D_MODEL), row(kq)],
        out_shape=[jax.ShapeDtypeStruct((m, D_MODEL), F32), jax.ShapeDtypeStruct((m, D_MODEL), BF16),
                   jax.ShapeDtypeStruct((m, kq), BF16)],
        compiler_params=_params(("parallel",), 48),
        name="merge",
    )(x2, att, rw, ga, gb, *consts)


def _extract_top(jobs, n_out):
    def body(r, carry):
        for src_ref, val_ref, idx_ref in jobs:
            nrow = src_ref.shape[0]
            rowi = lax.broadcasted_iota(I32, src_ref.shape, 0)
            s = src_ref[...]
            m = jnp.max(s, axis=0, keepdims=True)
            idx = jnp.min(jnp.where(s == m, rowi, nrow), axis=0, keepdims=True)
            val_ref[pl.ds(r, 1), :] = m
            idx_ref[pl.ds(r, 1), :] = idx
            src_ref[...] = jnp.where(rowi == idx, NEG_INF, s)
        return carry

    lax.fori_loop(0, n_out, body, 0)


def _cand_groups(k):
    groups = []
    a = 0
    while a < k and k // (a + 1) > 1:
        nb = k // (a + 1)
        groups.append((a, -(-nb // SUBLANES) * SUBLANES, nb))
        a += 1
    return groups, a


def _n_cand_rows(k):
    groups, a_tail = _cand_groups(k)
    return sum(rows for _, rows, _ in groups) + k - a_tail


def _peer_topk_body(qp_ref, sk_ref, i1_ref, i2_ref, gate_ref, s_scr, cand_scr, sv_scr, si_scr, bs_scr, bp_scr):
    k = PEER_TOPK
    for c in range(2):
        s_scr[c] = _dot(sk_ref[c], qp_ref[:, c * D_KEY_HALF:(c + 1) * D_KEY_HALF], NT)
    _extract_top([(s_scr.at[c], sv_scr.at[c], si_scr.at[c]) for c in range(2)], k)
    sv0 = sv_scr[0]
    sv1 = sv_scr[1]
    groups, a_tail = _cand_groups(k)
    off = 0
    for a, rows, nb in groups:
        blk = sv0[a:a + 1, :] + sv1[0:rows, :]
        if nb < rows:
            blk = jnp.where(lax.broadcasted_iota(I32, blk.shape, 0) < nb, blk, NEG_INF)
        cand_scr[off:off + rows, :] = blk
        off += rows
    cand_scr[off:off + k - a_tail, :] = sv0[a_tail:k, :] + sv1[0:1, :]
    _extract_top([(cand_scr, bs_scr, bp_scr)], k)
    best_s = bs_scr[...]
    pos = bp_scr[...]
    a_sel = jnp.zeros_like(pos)
    b_sel = pos
    off = 0
    for a, rows, nb in groups:
        if a > 0:
            a_sel = jnp.where(pos >= off, a, a_sel)
            b_sel = jnp.where(pos >= off, pos - off, b_sel)
        off += rows
    a_sel = jnp.where(pos >= off, a_tail + pos - off, a_sel)
    b_sel = jnp.where(pos >= off, 0, b_sel)
    si0 = si_scr[0]
    si1 = si_scr[1]
    i1 = jnp.zeros_like(pos)
    i2 = jnp.zeros_like(pos)
    for a in range(k):
        i1 = jnp.where(a_sel == a, si0[a:a + 1, :], i1)
        i2 = jnp.where(b_sel == a, si1[a:a + 1, :], i2)
    e = jnp.exp(best_s - jnp.max(best_s, axis=0, keepdims=True))
    i1_ref[...] = i1
    i2_ref[...] = i2
    gate_ref[...] = e / jnp.sum(e, axis=0, keepdims=True)


def _peer_topk(qp, sub_keys, tm):
    m = qp.shape[0]
    k = PEER_TOPK
    ospec = pl.BlockSpec((None, k, tm), lambda i, h: (h, 0, i))
    return pl.pallas_call(
        _peer_topk_body,
        grid=(m // tm, PEER_HEADS),
        in_specs=[pl.BlockSpec((tm, 2 * D_KEY_HALF), lambda i, h: (i, h)),
                  pl.BlockSpec((None, 2, N_KEYS, D_KEY_HALF), lambda i, h: (h, 0, 0, 0))],
        out_specs=[ospec, ospec, ospec],
        out_shape=[jax.ShapeDtypeStruct((PEER_HEADS, k, m), I32), jax.ShapeDtypeStruct((PEER_HEADS, k, m), I32),
                   jax.ShapeDtypeStruct((PEER_HEADS, k, m), F32)],
        scratch_shapes=[pltpu.VMEM((2, N_KEYS, tm), F32), pltpu.VMEM((_n_cand_rows(k), tm), F32),
                        pltpu.VMEM((2, k, tm), F32), pltpu.VMEM((2, k, tm), I32),
                        pltpu.VMEM((k, tm), F32), pltpu.VMEM((k, tm), I32)],
        compiler_params=_params(("parallel", "arbitrary"), 32),
        name="peer_topk",
    )(qp, sub_keys)


def _peer_expert_body(x1_ref, h2_ref, i1_ref, i2_ref, gate_ref, u_ref, v_ref, gf_ref, y_ref, w_scr):
    j = pl.program_id(1)
    tm = h2_ref.shape[0]
    te = u_ref.shape[0]
    pitch = tm + W_PITCH_PAD

    @pl.when(j == 0)
    def _():
        y_ref[...] = x1_ref[...]
        sub = lax.broadcasted_iota(I32, (N_KEYS, N_SLOTS), 0)

        def tok(t, carry):
            hit1 = sub == i1_ref[pl.ds(t, 1), :]
            hit2 = sub == i2_ref[pl.ds(t, 1), :]
            a = jnp.where(hit1, gate_ref[pl.ds(t, 1), :], 0.0).astype(BF16)
            bm = jnp.where(hit2, 1.0, 0.0).astype(BF16)
            w_scr[pl.ds(t, N_KEYS, stride=pitch), :] = _dot(a, bm, NT)
            return carry

        lax.fori_loop(0, tm, tok, 0, unroll=TOKEN_UNROLL)

    rb = min(tm, EXPERT_ROW_BLOCK)
    pair = 2 * N_KEYS
    for r in range(tm // rb):
        x = h2_ref[r * rb:(r + 1) * rb, :]
        parts = []
        for q in range(te // pair):
            h = _dot(x, u_ref[q * pair:(q + 1) * pair, :], NT)
            i1 = j * (te // N_KEYS) + 2 * q
            w = jnp.concatenate([w_scr[pl.ds(pl.multiple_of(i1 * pitch + r * rb, SUBLANES), rb), :],
                                 w_scr[pl.ds(pl.multiple_of((i1 + 1) * pitch + r * rb, SUBLANES), rb), :]], axis=1)
            act = 0.5 * h * (1.0 + lax.erf(h * (2.0 ** -0.5)))
            parts.append((w * act).astype(BF16))
        y_ref[r * rb:(r + 1) * rb, :] += _dot(jnp.concatenate(parts, axis=1), v_ref[...])

    @pl.when(j == pl.num_programs(1) - 1)
    def _():
        x2 = y_ref[...]
        y_ref[...] = x2 * lax.rsqrt(jnp.mean(x2 * x2, axis=-1, keepdims=True) + NORM_EPS) * gf_ref[...]


def _peer_experts(x1, h2, i1, i2, gate, eu, ev, gf, tm, te):
    m = x1.shape[0]
    row = lambda w: pl.BlockSpec((tm, w), lambda i, j: (i, 0))
    tab = pl.BlockSpec((te, D_MODEL), lambda i, j: (j, 0))
    x1_spec = pl.BlockSpec((tm, D_MODEL), lambda i, j: (i, 0), pipeline_mode=pl.Buffered(1))
    return pl.pallas_call(
        _peer_expert_body,
        grid=(m // tm, N_EXPERTS // te),
        in_specs=[x1_spec, row(D_MODEL), row(N_SLOTS), row(N_SLOTS), row(N_SLOTS), tab, tab,
                  pl.BlockSpec((1, D_MODEL), lambda i, j: (0, 0))],
        out_specs=row(D_MODEL),
        out_shape=jax.ShapeDtypeStruct((m, D_MODEL), F32),
        scratch_shapes=[pltpu.VMEM((N_KEYS * (tm + W_PITCH_PAD), N_KEYS), F32)],
        compiler_params=_params(("parallel", "arbitrary"), 60),
        name="peer_experts",
    )(x1, h2, i1, i2, gate, eu, ev, gf)


def _pad_cols(a, width):
    return jnp.pad(a, [(0, 0)] * (a.ndim - 1) + [(0, width - a.shape[-1])])


def _prep_layer_params(l, norm1_g, w_in, b_in, attn_sinks, shift_mu, w0, w2, a0, a2, g2, k_k, k_a, r_k, ln_x_w,
                       ln_x_b, w_up_a, w_up_b, w_o, norm2_g, w_query, sub_keys, expert_u, expert_v):
    wi, bi = w_in[l], b_in[l][None, :]
    cat = lambda z: jnp.concatenate([z[:, :OFF_B], _pad_cols(z[:, OFF_B:OFF_GATE], RWKV_PAD), z[:, OFF_GATE:]], axis=1)
    row = lambda z: z[l].reshape(1, -1)
    lora = lambda z, off: jnp.pad(z[l], ((off, LORA_PAD - off - z.shape[1]), (0, 0)))
    return dict(
        norm1_g=row(norm1_g), w_cat=cat(wi).astype(BF16), b_cat=cat(bi), sinks=attn_sinks[l],
        mu=_pad_cols(row(shift_mu), RWKV_PAD), w0=row(w0), a0=row(a0), k_k=row(k_k), k_a=row(k_a),
        w2p=lora(w2, 0), a2p=lora(a2, DECAY_LORA), g2p=lora(g2, DECAY_LORA + AAA_LORA),
        r_k=row(r_k), ln_w=row(ln_x_w), ln_b=row(ln_x_b),
        w_up_a=w_up_a[l].astype(BF16), w_up_b=w_up_b[l].astype(BF16), w_o=w_o[l].astype(BF16),
        norm2_g=row(norm2_g), w_query=w_query[l].astype(BF16), sub_keys=sub_keys[l].astype(BF16),
        expert_u=expert_u[l].astype(BF16), expert_v=expert_v[l].astype(BF16),
    )


def _row_tile(m, pref):
    return pref if m % pref == 0 else m


def _hybrid_layer(x, pos, lp, eblk, gf, win_k, win_v, wkv0, shift0):
    b, t, _ = x.shape
    m = b * t
    x2 = x.reshape(m, D_MODEL)
    tm = _row_tile(m, 256)
    rope_tab = _rope_table(pos, max(t, tm))
    q, k, v, p, ga, gb = _in_proj(x2, lp["norm1_g"], lp["w_cat"], lp["b_cat"], rope_tab, tm)

    k3 = k.reshape(b, t, KV_WIDTH)
    v3 = v.reshape(b, t, KV_WIDTH)
    if win_k is None:
        att = _attn_prompt(lp["sinks"], q.reshape(b, t, ATTN_WIDTH), k3, v3).reshape(m, ATTN_WIDTH)
        keep = min(WINDOW, t)
        new_k, new_v = k3[:, t - keep:], v3[:, t - keep:]
    else:
        n_past = win_k.shape[1]
        qg = q.reshape(b, t, N_KV_HEADS, Q_PER_KV, HEAD_DIM).transpose(0, 2, 3, 1, 4)
        qg = qg.reshape(b, N_KV_HEADS, Q_PER_KV * t, HEAD_DIM)
        ck = win_k.reshape(b, n_past, KV_WIDTH)
        cv = win_v.reshape(b, n_past, KV_WIDTH)
        og = _attn_sample(lp["sinks"], qg, k3, v3, ck, cv, t)
        att = og.reshape(b, N_KV_HEADS, Q_PER_KV, t, HEAD_DIM).transpose(0, 3, 1, 2, 4).reshape(m, ATTN_WIDTH)
        new_k = jnp.concatenate([ck, k3], axis=1)[:, t:]
        new_v = jnp.concatenate([cv, v3], axis=1)[:, t:]
    new_k = new_k.reshape(b, -1, N_KV_HEADS, HEAD_DIM)
    new_v = new_v.reshape(b, -1, N_KV_HEADS, HEAD_DIM)

    p3 = p.reshape(b, t, RWKV_PAD)
    if t % RWKV_CHUNK == 0:
        ch, tp = RWKV_CHUNK, t
    else:
        tp = -(-t // SUBLANES) * SUBLANES
        ch = tp
        p3 = jnp.pad(p3, ((0, 0), (0, tp - t), (0, 0)))
    shift_pad = _pad_cols(shift0, RWKV_PAD)[:, None, :]
    bb = RWKV_BATCH_BLOCK if (tp == ch and b % RWKV_BATCH_BLOCK == 0) else 1
    akb, *prep = _rwkv_prep(p3, shift_pad, lp, eblk, ch, t, bb)
    tmat = _tri_solve(akb)
    rw, wkv_t = _rwkv_scan(tmat, prep, wkv0, lp, eblk, bb)
    rw = rw[:, :t].reshape(m, RWKV_WIDTH)
    shift_t = p3[:, t - 1, :RWKV_COLS]

    x1, h2, qp = _merge(x2, att, rw, ga, gb, lp, tm)
    i1, i2, gate = _peer_topk(qp, lp["sub_keys"], tm)
    slots = lambda z: z.reshape(N_SLOTS, m).T
    y = _peer_experts(x1, h2, slots(i1), slots(i2), slots(gate), lp["expert_u"], lp["expert_v"], gf,
                      _row_tile(m, EXPERT_TOKEN_TILE), EXPERT_TILE)
    return y.reshape(b, t, D_MODEL), new_k, new_v, wkv_t, shift_t


def kernel(x_prompt, x_sample, cache_win_k, cache_win_v, state_wkv, state_shift, norm1_g, w_in, b_in, attn_sinks,
           shift_mu, w0, w2, a0, a2, g2, k_k, k_a, r_k, ln_x_w, ln_x_b, w_up_a, w_up_b, w_o, norm2_g, w_query,
           sub_keys, expert_u, expert_v, final_norm_g):
    depth = w_in.shape[0]
    assert depth == 1, "the final RMSNorm is fused into the last layer's expert kernel"
    pos_p = jnp.arange(x_prompt.shape[1])
    pos_s = PAST_LEN + jnp.arange(x_sample.shape[1])
    li = lax.broadcasted_iota(I32, (RWKV_WIDTH, RWKV_WIDTH), 0) // RWKV_HEAD_DIM
    lj = lax.broadcasted_iota(I32, (RWKV_WIDTH, RWKV_WIDTH), 1) // RWKV_HEAD_DIM
    eblk = (li == lj).astype(BF16)
    gf = final_norm_g.reshape(1, D_MODEL)
    lp = _prep_layer_params(0, norm1_g, w_in, b_in, attn_sinks, shift_mu, w0, w2, a0, a2, g2, k_k, k_a, r_k, ln_x_w,
                            ln_x_b, w_up_a, w_up_b, w_o, norm2_g, w_query, sub_keys, expert_u, expert_v)
    bp = x_prompt.shape[0]
    wkv0 = jnp.zeros((bp, RWKV_HEADS, RWKV_HEAD_DIM, RWKV_HEAD_DIM), F32)
    shift0 = jnp.zeros((bp, RWKV_COLS), x_prompt.dtype)
    yp, pk, pv, pw, psh = _hybrid_layer(x_prompt, pos_p, lp, eblk, gf, None, None, wkv0, shift0)
    ys, sk, sv, sw, ssh = _hybrid_layer(x_sample, pos_s, lp, eblk, gf, cache_win_k[0], cache_win_v[0],
                                        state_wkv[0], state_shift[0])
    st = lambda z: z[None]
    return (yp, ys, st(pk), st(pv), st(pw), st(psh), st(sk), st(sv), st(sw), st(ssh))
```

```python
import functools

import jax
import jax.numpy as jnp
from jax import lax
from jax.experimental import pallas as pl
from jax.experimental.pallas import tpu as pltpu

F32, BF16, I32 = jnp.float32, jnp.bfloat16, jnp.int32

D_MODEL = 1024
NORM_EPS = 1e-5
HEAD_DIM = 64
N_Q_HEADS = 8
N_KV_HEADS = 2
Q_PER_KV = N_Q_HEADS // N_KV_HEADS
ATTN_WIDTH = N_Q_HEADS * HEAD_DIM
KV_WIDTH = N_KV_HEADS * HEAD_DIM
WINDOW = 128
ROT_DIM = HEAD_DIM // 4
ROPE_THETA = 500000.0
ATTN_SCALE = HEAD_DIM ** -0.5
RWKV_HEADS = 8
RWKV_HEAD_DIM = 64
RWKV_WIDTH = RWKV_HEADS * RWKV_HEAD_DIM
DECAY_LORA = 32
AAA_LORA = 32
GATE_LORA = 96
GN_EPS = 64e-5
RWKV_COLS = 3 * RWKV_WIDTH + DECAY_LORA + AAA_LORA + GATE_LORA
LORA_PAD = 256
RWKV_PAD = 3 * RWKV_WIDTH + LORA_PAD
OFF_K = ATTN_WIDTH
OFF_V = OFF_K + KV_WIDTH
OFF_B = OFF_V + KV_WIDTH
OFF_GATE = OFF_B + RWKV_COLS
N_KEYS = 128
N_EXPERTS = N_KEYS * N_KEYS
PEER_HEADS = 8
PEER_TOPK = 16
D_KEY_HALF = 128
N_SLOTS = PEER_HEADS * PEER_TOPK
PAST_LEN = 16384
RWKV_CHUNK = 64
LANES = 128
SUBLANES = 8
SOLVE_COLS = 16
TOKEN_UNROLL = 32
RWKV_BATCH_BLOCK = 8
EXPERT_TOKEN_TILE = 512
EXPERT_TILE = 1024
EXPERT_ROW_BLOCK = 512
W_PITCH_PAD = 8
NEG_INF = float("-inf")

NN = ((1,), (0,))
NT = ((1,), (1,))
TN = ((0,), (0,))


def _dot(a, b, dims=NN):
    return lax.dot_general(a, b, (dims, ((), ())), preferred_element_type=F32)


def _split2(x):
    hi = x.astype(BF16)
    return hi, (x - hi.astype(F32)).astype(BF16)


def _doth(a, b, dims=NN):
    a_hi, a_lo = _split2(a)
    b_hi, b_lo = _split2(b)
    return _dot(a_hi, b_hi, dims) + _dot(a_hi, b_lo, dims) + _dot(a_lo, b_hi, dims)


def _dotb(a, b, dims=NN):
    return _dot(a.astype(BF16), b.astype(BF16), dims)


def _split3(x):
    hi = x.astype(BF16)
    r1 = x - hi.astype(F32)
    mid = r1.astype(BF16)
    lo = (r1 - mid.astype(F32)).astype(BF16)
    return hi, mid, lo


def _dot_exact_rhs(x, e):
    hi, mid, lo = _split3(x)
    return _dot(hi, e) + _dot(mid, e) + _dot(lo, e)


def _dot_exact_lhs(e, x):
    hi, mid, lo = _split3(x)
    return _dot(e, hi) + _dot(e, mid) + _dot(e, lo)


def _params(sem, vmem_mb):
    return pltpu.CompilerParams(dimension_semantics=sem, vmem_limit_bytes=vmem_mb * 1024 * 1024)


def _full(a):
    nd = a.ndim
    return pl.BlockSpec(a.shape, lambda *_: (0,) * nd)


SEG_QKV = ATTN_WIDTH + 2 * KV_WIDTH
SEG_P = SEG_QKV + RWKV_PAD
SEG_END = SEG_P + 2 * D_MODEL


def _inproj_body(x_ref, g_ref, w_ref, b_ref, rope_ref, q_ref, k_ref, v_ref, p_ref, ga_ref, gb_ref):
    x = x_ref[...]
    h = (x * lax.rsqrt(jnp.mean(x * x, axis=-1, keepdims=True) + NORM_EPS) * g_ref[...]).astype(BF16)

    def seg(lo, hi):
        return _dot(h, w_ref[:, lo:hi]) + b_ref[:, lo:hi]

    cos = rope_ref[:, 0:LANES]
    sin_up = rope_ref[:, LANES:2 * LANES]
    sin_dn = rope_ref[:, 2 * LANES:3 * LANES]

    def rope(z):
        return z * cos + pltpu.roll(z, ROT_DIM // 2, 1) * sin_up + pltpu.roll(z, LANES - ROT_DIM // 2, 1) * sin_dn

    for c in range(ATTN_WIDTH // LANES):
        q_ref[:, c * LANES:(c + 1) * LANES] = rope(seg(c * LANES, (c + 1) * LANES))
    k_ref[...] = rope(seg(OFF_K, OFF_V))
    v_ref[...] = seg(OFF_V, SEG_QKV)
    p_ref[...] = seg(SEG_QKV, SEG_P)
    ga_ref[...] = jax.nn.sigmoid(seg(SEG_P, SEG_P + D_MODEL))
    gb_ref[...] = jax.nn.sigmoid(seg(SEG_P + D_MODEL, SEG_END))


def _in_proj(x2, g1, w_cat, b_cat, rope_tab, tm):
    m = x2.shape[0]
    nrep = rope_tab.shape[0] // tm
    widths = (ATTN_WIDTH, KV_WIDTH, KV_WIDTH, RWKV_PAD, D_MODEL, D_MODEL)
    return pl.pallas_call(
        _inproj_body,
        grid=(m // tm,),
        in_specs=[
            pl.BlockSpec((tm, D_MODEL), lambda i: (i, 0)),
            _full(g1), _full(w_cat), _full(b_cat),
            pl.BlockSpec((tm, 3 * LANES), lambda i: (i % nrep, 0)),
        ],
        out_specs=[pl.BlockSpec((tm, w), lambda i: (i, 0)) for w in widths],
        out_shape=[jax.ShapeDtypeStruct((m, w), F32) for w in widths],
        compiler_params=_params(("parallel",), 48),
        name="in_proj",
    )(x2, g1, w_cat, b_cat, rope_tab)


def _rope_table(pos, rows):
    t = pos.shape[0]
    half = ROT_DIM // 2
    inv_freq = ROPE_THETA ** (-jnp.arange(half, dtype=F32) / half)
    ang = pos.astype(F32)[:, None] * inv_freq[None, :]
    cos, sin = jnp.cos(ang), jnp.sin(ang)
    rest = HEAD_DIM - ROT_DIM
    z8 = jnp.zeros((t, half), F32)
    cos64 = jnp.concatenate([cos, cos, jnp.ones((t, rest), F32)], axis=1)
    up64 = jnp.concatenate([z8, sin, jnp.zeros((t, rest), F32)], axis=1)
    dn64 = jnp.concatenate([-sin, z8, jnp.zeros((t, rest), F32)], axis=1)
    tab = jnp.concatenate([jnp.tile(a, (1, LANES // HEAD_DIM)) for a in (cos64, up64, dn64)], axis=1)
    if rows > t:
        tab = jnp.tile(tab, (rows // t, 1))
    return tab


def _softmax_av(s, sink, vv):
    m = jnp.maximum(jnp.max(s, axis=-1, keepdims=True), sink)
    p = jnp.exp(s - m)
    den = jnp.sum(p, axis=-1, keepdims=True) + jnp.exp(sink - m)
    return _dot(p.astype(BF16), vv) / den


def _attn_prompt_body(sink_ref, q_ref, kc_ref, kp_ref, vc_ref, vp_ref, o_ref):
    n = pl.program_id(1)
    blk = q_ref.shape[0]
    i = lax.broadcasted_iota(I32, (blk, 2 * blk), 0)
    j = lax.broadcasted_iota(I32, (blk, 2 * blk), 1)
    diff = blk + i - j
    j_lo = jnp.where(n > 0, 0, blk)
    mask = (diff >= 0) & (diff < WINDOW) & (j >= j_lo)
    outs = []
    for g in range(N_KV_HEADS):
        sl = slice(g * HEAD_DIM, (g + 1) * HEAD_DIM)
        kk = jnp.concatenate([kp_ref[:, sl], kc_ref[:, sl]], axis=0).astype(BF16)
        vv = jnp.concatenate([vp_ref[:, sl], vc_ref[:, sl]], axis=0).astype(BF16)
        for hq in range(Q_PER_KV):
            h = g * Q_PER_KV + hq
            qh = q_ref[:, h * HEAD_DIM:(h + 1) * HEAD_DIM].astype(BF16)
            s = jnp.where(mask, _dot(qh, kk, NT) * ATTN_SCALE, NEG_INF)
            outs.append(_softmax_av(s, sink_ref[h], vv))
    o_ref[...] = jnp.concatenate(outs, axis=1).astype(o_ref.dtype)


def _attn_prompt(sinks, q3, k3, v3):
    b, t, _ = q3.shape
    blk = WINDOW
    cur = lambda w: pl.BlockSpec((None, blk, w), lambda bi, n: (bi, n, 0))
    prev = lambda w: pl.BlockSpec((None, blk, w), lambda bi, n: (bi, jnp.maximum(n - 1, 0), 0))
    return pl.pallas_call(
        _attn_prompt_body,
        grid=(b, t // blk),
        in_specs=[pl.BlockSpec(memory_space=pltpu.SMEM), cur(ATTN_WIDTH), cur(KV_WIDTH), prev(KV_WIDTH),
                  cur(KV_WIDTH), prev(KV_WIDTH)],
        out_specs=cur(ATTN_WIDTH),
        out_shape=jax.ShapeDtypeStruct((b, t, ATTN_WIDTH), BF16),
        compiler_params=_params(("parallel", "arbitrary"), 32),
        name="attn_prompt",
    )(sinks, q3, k3, k3, v3, v3)


def _attn_sample_body(sink_ref, q_ref, kn_ref, vn_ref, kc_ref, vc_ref, o_ref, *, t):
    rows = Q_PER_KV * t
    n_past = kc_ref.shape[0]
    ri = lax.broadcasted_iota(I32, (rows, 1), 0)
    qi = ri % t
    jc = lax.broadcasted_iota(I32, (rows, n_past), 1)
    mask_c = (n_past + qi - jc) < WINDOW
    for g in range(N_KV_HEADS):
        sl = slice(g * HEAD_DIM, (g + 1) * HEAD_DIM)
        sink = jnp.zeros((rows, 1), F32)
        for hq in range(Q_PER_KV):
            sink = jnp.where(ri // t == hq, sink_ref[g * Q_PER_KV + hq], sink)
        qg = q_ref[g]
        qb = qg.astype(BF16)
        kn = kn_ref[:, sl]
        vn = vn_ref[:, sl]
        s_c = jnp.where(mask_c, _dot(qb, kc_ref[:, sl].astype(BF16), NT) * ATTN_SCALE, NEG_INF)
        s_n = []
        for jn in range(t):
            sj = jnp.sum(qg * kn[jn:jn + 1, :], axis=-1, keepdims=True) * ATTN_SCALE
            s_n.append(jnp.where(qi >= jn, sj, NEG_INF))
        m = jnp.maximum(jnp.max(s_c, axis=-1, keepdims=True), sink)
        for sj in s_n:
            m = jnp.maximum(m, sj)
        p_c = jnp.exp(s_c - m)
        den = jnp.sum(p_c, axis=-1, keepdims=True) + jnp.exp(sink - m)
        o = _dot(p_c.astype(BF16), vc_ref[:, sl].astype(BF16))
        for jn in range(t):
            p_n = jnp.exp(s_n[jn] - m)
            den = den + p_n
            o = o + p_n * vn[jn:jn + 1, :]
        o_ref[g] = (o / den).astype(o_ref.dtype)


def _attn_sample(sinks, qg, kn, vn, cache_k, cache_v, t):
    b = qg.shape[0]
    rows = Q_PER_KV * t
    n_past = cache_k.shape[1]
    qspec = pl.BlockSpec((None, N_KV_HEADS, rows, HEAD_DIM), lambda bi: (bi, 0, 0, 0))
    nspec = pl.BlockSpec((None, t, KV_WIDTH), lambda bi: (bi, 0, 0))
    cspec = pl.BlockSpec((None, n_past, KV_WIDTH), lambda bi: (bi, 0, 0))
    return pl.pallas_call(
        functools.partial(_attn_sample_body, t=t),
        grid=(b,),
        in_specs=[pl.BlockSpec(memory_space=pltpu.SMEM), qspec, nspec, nspec, cspec, cspec],
        out_specs=qspec,
        out_shape=jax.ShapeDtypeStruct((b, N_KV_HEADS, rows, HEAD_DIM), BF16),
        compiler_params=_params(("parallel",), 32),
        name="attn_sample",
    )(sinks, qg, kn, vn, cache_k, cache_v)


N_PREP_CONSTS = 9


def _rwkv_prep_body(*refs, t_real):
    seq, consts, outs = refs[:3], refs[3:3 + N_PREP_CONSTS], refs[3 + N_PREP_CONSTS:]
    for bi in range(seq[0].shape[0]):
        _rwkv_prep_one(*[r.at[bi] for r in seq], *consts, *[r.at[bi] for r in outs], t_real=t_real)


def _rwkv_prep_one(p_ref, pprev_ref, shift_ref, mu_ref, w0_ref, a0_ref, kk_ref, ka_ref, w2_ref, a2_ref, g2_ref,
                   eblk_ref, akb_ref, akk_ref, qrk_ref, qrb_ref, kkt_ref, rt_ref, kdec_ref, bdec_ref, kmod_ref,
                   r_ref, v_ref, g_ref, gamc_ref, *, t_real):
    c = pl.program_id(1)
    ch = p_ref.shape[0]
    pc = p_ref[...]
    row = lax.broadcasted_iota(I32, (ch, 1), 0)
    first = jnp.where(c == 0, shift_ref[...], pprev_ref[SUBLANES - 1:SUBLANES, :])
    prev = jnp.where(row == 0, first, pltpu.roll(pc, 1, 0))
    xm = pc + mu_ref[...] * (prev - pc)
    valid = (c * ch + row) < t_real
    w3 = 3 * RWKV_WIDTH
    r = xm[:, 0:RWKV_WIDTH]
    k = xm[:, RWKV_WIDTH:2 * RWKV_WIDTH]
    v = xm[:, 2 * RWKV_WIDTH:w3]
    lo = xm[:, w3:w3 + LORA_PAD]
    lane = lax.broadcasted_iota(I32, lo.shape, 1)
    act = jnp.where(lane < DECAY_LORA, jnp.tanh(lo),
                    jnp.where(lane < DECAY_LORA + AAA_LORA, lo, jax.nn.sigmoid(lo)))
    z = w0_ref[...] + _doth(act, w2_ref[...])
    w_log = -(jnp.maximum(-z, 0.0) + jnp.log1p(jnp.exp(-jnp.abs(z)))) - 0.5
    lw = jnp.where(valid, -jnp.exp(w_log), 0.0)
    a = jax.nn.sigmoid(a0_ref[...] + _doth(act, a2_ref[...]))
    g = _dot(act.astype(BF16), g2_ref[...].astype(BF16))
    kk = k * kk_ref[...]
    ss = _dot_exact_rhs(kk * kk, eblk_ref[...])
    kk = jnp.where(valid, kk * lax.rsqrt(jnp.maximum(ss, 1e-24)), 0.0)
    kmod = jnp.where(valid, k * (1.0 + (a - 1.0) * ka_ref[...]), 0.0)
    bb = kk * a
    ti = lax.broadcasted_iota(I32, (ch, ch), 0)
    tj = lax.broadcasted_iota(I32, (ch, ch), 1)
    tri = jnp.where(ti >= tj, 1.0, 0.0).astype(BF16)
    cum = _dot_exact_lhs(tri, lw)
    cum_c = cum[ch - 1:ch, :]
    e_neg = jnp.exp(-cum)
    e_rem = jnp.exp(cum_c - cum)
    kkt = kk * jnp.exp(cum - lw)
    rt = r * jnp.exp(cum)
    khat = kmod * e_neg
    bhat = bb * e_neg
    strict = ti > tj
    incl = ti >= tj
    hs = range(RWKV_HEADS)
    sl = [slice(h * RWKV_HEAD_DIM, (h + 1) * RWKV_HEAD_DIM) for h in hs]
    split = lambda z: [z[:, s].astype(BF16) for s in sl]
    kkt_s, rt_s, khat_s, bhat_s = split(kkt), split(rt), split(khat), split(bhat)
    akb = [_dot(kkt_s[h], bhat_s[h], NT) for h in hs]
    akk = [_dot(kkt_s[h], khat_s[h], NT) for h in hs]
    qrk = [_dot(rt_s[h], khat_s[h], NT) for h in hs]
    qrb = [_dot(rt_s[h], bhat_s[h], NT) for h in hs]
    for h in hs:
        akb_ref[h] = jnp.where(strict, akb[h], 0.0)
        akk_ref[h] = jnp.where(strict, akk[h], 0.0)
        qrk_ref[h] = jnp.where(incl, qrk[h], 0.0)
        qrb_ref[h] = jnp.where(incl, qrb[h], 0.0)
    kkt_ref[...] = kkt
    rt_ref[...] = rt
    kdec_ref[...] = kmod * e_rem
    bdec_ref[...] = bb * e_rem
    kmod_ref[...] = kmod
    r_ref[...] = r
    v_ref[...] = v
    g_ref[...] = g
    gamc_ref[...] = jnp.exp(cum_c)


def _rwkv_prep(p3, shift0, lp, eblk, ch, t_real, bb):
    b, tp, _ = p3.shape
    nc = tp // ch
    per8 = ch // SUBLANES
    mats = jax.ShapeDtypeStruct((b, nc, RWKV_HEADS, ch, ch), F32)
    rows = jax.ShapeDtypeStruct((b, tp, RWKV_WIDTH), F32)
    mspec = pl.BlockSpec((bb, None, RWKV_HEADS, ch, ch), lambda bi, ci: (bi, ci, 0, 0, 0))
    rspec = pl.BlockSpec((bb, ch, RWKV_WIDTH), lambda bi, ci: (bi, ci, 0))
    consts = (lp["mu"], lp["w0"], lp["a0"], lp["k_k"], lp["k_a"], lp["w2p"], lp["a2p"], lp["g2p"], eblk)
    assert len(consts) == N_PREP_CONSTS
    return pl.pallas_call(
        functools.partial(_rwkv_prep_body, t_real=t_real),
        grid=(b // bb, nc),
        in_specs=[
            pl.BlockSpec((bb, ch, RWKV_PAD), lambda bi, ci: (bi, ci, 0)),
            pl.BlockSpec((bb, SUBLANES, RWKV_PAD), lambda bi, ci: (bi, jnp.maximum(ci * per8 - 1, 0), 0)),
            pl.BlockSpec((bb, 1, RWKV_PAD), lambda bi, ci: (bi, 0, 0)),
        ] + [_full(a) for a in consts],
        out_specs=[mspec] * 4 + [rspec] * 8 + [pl.BlockSpec((bb, None, 1, RWKV_WIDTH), lambda bi, ci: (bi, ci, 0, 0))],
        out_shape=[mats] * 4 + [rows] * 8 + [jax.ShapeDtypeStruct((b, nc, 1, RWKV_WIDTH), F32)],
        compiler_params=_params(("parallel", "parallel"), 48),
        name="rwkv_prep",
    )(p3, p3, shift0, *consts)


def _tri_solve_body(a_ref, t_ref):
    ch = a_ref.shape[0]
    cols = min(SOLVE_COLS, ch)
    t_ref[...] = jnp.zeros(t_ref.shape, F32)
    ci = lax.broadcasted_iota(I32, (cols, SUBLANES, LANES), 0)

    def row_body(s, carry):
        for kc in range(ch // cols):
            c0 = kc * cols

            @pl.when(s >= c0)
            def _():
                acc0 = jnp.where(ci == s - c0, 1.0, 0.0).astype(F32)

                def r_body(r, acc):
                    return acc - a_ref[s, r][None] * t_ref[r, pl.ds(c0, cols)]

                t_ref[s, pl.ds(c0, cols)] = lax.fori_loop(c0, s, r_body, acc0)
        return carry

    lax.fori_loop(0, ch, row_body, 0)


def _tri_solve(akb):
    b, nc, nh, ch, _ = akb.shape
    nsys = b * nc * nh
    slab = SUBLANES * LANES
    nslab = -(-nsys // slab)
    a2 = akb.reshape(nsys, ch * ch)
    a2 = jnp.pad(a2, ((0, nslab * slab - nsys), (0, 0)))
    a5 = a2.T.reshape(ch, ch, nslab, SUBLANES, LANES)
    spec = pl.BlockSpec((ch, ch, None, SUBLANES, LANES), lambda gi: (0, 0, gi, 0, 0),
                        pipeline_mode=pl.Buffered(1))
    t5 = pl.pallas_call(
        _tri_solve_body,
        grid=(nslab,),
        in_specs=[spec],
        out_specs=spec,
        out_shape=jax.ShapeDtypeStruct(a5.shape, F32),
        compiler_params=_params(("parallel",), 48),
        name="rwkv_tri_solve",
    )(a5)
    t2 = t5.reshape(ch * ch, nslab * slab).T[:nsys]
    return t2.reshape(b, nc, nh, ch, ch)


N_SCAN_SEQ = 14
N_SCAN_CONSTS = 4


def _rwkv_scan_body(*refs):
    seq, consts, rest = refs[:N_SCAN_SEQ], refs[N_SCAN_SEQ:N_SCAN_SEQ + N_SCAN_CONSTS], refs[N_SCAN_SEQ + N_SCAN_CONSTS:]
    for bi in range(seq[0].shape[0]):
        _rwkv_scan_one(*[r.at[bi] for r in seq], *consts, *[r.at[bi] for r in rest])


def _rwkv_scan_one(t_ref, akk_ref, qrk_ref, qrb_ref, kkt_ref, rt_ref, kdec_ref, bdec_ref, kmod_ref, r_ref, v_ref,
                   g_ref, gamc_ref, wkv0_ref, eblk_ref, rk_ref, lnw_ref, lnb_ref, rw_ref, wkv_ref, s_scr):
    c = pl.program_id(1)
    nd = RWKV_HEAD_DIM

    @pl.when(c == 0)
    def _():
        s_scr[...] = wkv0_ref[...]

    hs = range(RWKV_HEADS)
    sl = [slice(h * nd, (h + 1) * nd) for h in hs]
    tmat = [t_ref[h] for h in hs]
    vh = [v_ref[:, sl[h]] for h in hs]
    bdec = [bdec_ref[:, sl[h]] for h in hs]
    qrb = [qrb_ref[h] for h in hs]
    w1 = [_dotb(tmat[h], kkt_ref[:, sl[h]]) for h in hs]
    av = [_dotb(akk_ref[h], vh[h]) for h in hs]
    qv = [_dotb(qrk_ref[h], vh[h]) for h in hs]
    vk = [_dotb(vh[h], kdec_ref[:, sl[h]], TN) for h in hs]
    w2 = [_dotb(tmat[h], av[h]) for h in hs]
    qw1 = [_dotb(qrb[h], w1[h]) for h in hs]
    bw1 = [_dotb(bdec[h], w1[h], TN) for h in hs]
    qw2 = [_dotb(qrb[h], w2[h]) for h in hs]
    wb = [_dotb(w2[h], bdec[h], TN) for h in hs]
    s0 = [s_scr[h] for h in hs]
    ys = [_dotb(rt_ref[:, sl[h]] - qw1[h], s0[h], NT) + (qv[h] - qw2[h]) for h in hs]
    for h in hs:
        s_scr[h] = s0[h] * gamc_ref[:, sl[h]] - _dotb(s0[h], bw1[h], NT) + (vk[h] - wb[h])
    y = jnp.concatenate(ys, axis=1)
    eblk = eblk_ref[...]
    inv_n = 1.0 / nd
    mean = _dot_exact_rhs(y, eblk) * inv_n
    yc = y - mean
    var = _dot_exact_rhs(yc * yc, eblk) * inv_n
    yn = yc * lax.rsqrt(var + GN_EPS) * lnw_ref[...] + lnb_ref[...]
    v = v_ref[...]
    bonus = _dot_exact_rhs(r_ref[...] * kmod_ref[...] * rk_ref[...], eblk) * v
    rw_ref[...] = ((yn + bonus) * g_ref[...]).astype(rw_ref.dtype)

    @pl.when(c == pl.num_programs(1) - 1)
    def _():
        wkv_ref[...] = s_scr[...]


def _rwkv_scan(tmat, prep, wkv0, lp, eblk, bb):
    akk, qrk, qrb, kkt, rt, kdec, bdec, kmod, r, v, g, gamc = prep
    b, nc, nh, ch, _ = tmat.shape
    tp = kkt.shape[1]
    nd = RWKV_HEAD_DIM
    mspec = pl.BlockSpec((bb, None, nh, ch, ch), lambda bi, ci: (bi, ci, 0, 0, 0))
    rspec = pl.BlockSpec((bb, ch, RWKV_WIDTH), lambda bi, ci: (bi, ci, 0))
    sspec = pl.BlockSpec((bb, nh, nd, nd), lambda bi, ci: (bi, 0, 0, 0))
    consts = (eblk, lp["r_k"], lp["ln_w"], lp["ln_b"])
    assert len(consts) == N_SCAN_CONSTS
    return pl.pallas_call(
        _rwkv_scan_body,
        grid=(b // bb, nc),
        in_specs=[mspec] * 4 + [rspec] * 8
        + [pl.BlockSpec((bb, None, 1, RWKV_WIDTH), lambda bi, ci: (bi, ci, 0, 0)), sspec]
        + [_full(a) for a in consts],
        out_specs=[rspec, sspec],
        out_shape=[jax.ShapeDtypeStruct((b, tp, RWKV_WIDTH), BF16), jax.ShapeDtypeStruct((b, nh, nd, nd), F32)],
        scratch_shapes=[pltpu.VMEM((bb, nh, nd, nd), F32)],
        compiler_params=_params(("parallel", "arbitrary"), 48),
        name="rwkv_scan",
    )(tmat, akk, qrk, qrb, kkt, rt, kdec, bdec, kmod, r, v, g, gamc, wkv0, *consts)


def _merge_body(x_ref, att_ref, rw_ref, ga_ref, gb_ref, wa_ref, wb_ref, wo_ref, g2_ref, wq_ref,
                x1_ref, h2_ref, qp_ref):
    merged = ga_ref[...] * _dot(att_ref[...], wa_ref[...]) + gb_ref[...] * _dot(rw_ref[...], wb_ref[...])
    x1 = x_ref[...] + _dot(merged.astype(BF16), wo_ref[...])
    x1_ref[...] = x1
    h2 = (x1 * lax.rsqrt(jnp.mean(x1 * x1, axis=-1, keepdims=True) + NORM_EPS) * g2_ref[...]).astype(BF16)
    h2_ref[...] = h2
    qp_ref[...] = _dot(h2, wq_ref[...]).astype(qp_ref.dtype)


def _merge(x2, att, rw, ga, gb, lp, tm):
    m = x2.shape[0]
    kq = lp["w_query"].shape[1]
    row = lambda w: pl.BlockSpec((tm, w), lambda i: (i, 0))
    consts = (lp["w_up_a"], lp["w_up_b"], lp["w_o"], lp["norm2_g"], lp["w_query"])
    return pl.pallas_call(
        _merge_body,
        grid=(m // tm,),
        in_specs=[row(D_MODEL), row(ATTN_WIDTH), row(RWKV_WIDTH), row(D_MODEL), row(D_MODEL)]
        + [_full(a) for a in consts],
        out_specs=[row(D_MODEL), row(D_MODEL), row(kq)],
        out_shape=[jax.ShapeDtypeStruct((m, D_MODEL), F32), jax.ShapeDtypeStruct((m, D_MODEL), BF16),
                   jax.ShapeDtypeStruct((m, kq), BF16)],
        compiler_params=_params(("parallel",), 48),
        name="merge",
    )(x2, att, rw, ga, gb, *consts)


def _extract_top(jobs, n_out):
    def body(r, carry):
        for src_ref, val_ref, idx_ref in jobs:
            nrow = src_ref.shape[0]
            rowf = lax.broadcasted_iota(I32, src_ref.shape, 0).astype(F32)
            s = src_ref[...]
            m = jnp.max(s, axis=0, keepdims=True)
            idx = jnp.min(jnp.where(s == m, rowf, float(nrow)), axis=0, keepdims=True)
            val_ref[pl.ds(r, 1), :] = m
            idx_ref[pl.ds(r, 1), :] = idx
            src_ref[...] = jnp.where(rowf == idx, NEG_INF, s)
        return carry

    lax.fori_loop(0, n_out, body, 0)


def _cand_groups(k):
    groups = []
    a = 0
    while a < k and k // (a + 1) > 1:
        nb = k // (a + 1)
        groups.append((a, -(-nb // SUBLANES) * SUBLANES, nb))
        a += 1
    return groups, a


def _n_cand_rows(k):
    groups, a_tail = _cand_groups(k)
    return sum(rows for _, rows, _ in groups) + k - a_tail


def _peer_topk_head(qp_ref, sk_ref, h, i1_scr, i2_scr, gate_scr, s_scr, cand_scr, sv_scr, si_scr, bs_scr, bp_scr):
    k = PEER_TOPK
    for c in range(2):
        col = (2 * h + c) * D_KEY_HALF
        s_scr[c] = _dot(sk_ref[h, c], qp_ref[:, col:col + D_KEY_HALF], NT)
    _extract_top([(s_scr.at[c], sv_scr.at[c], si_scr.at[c]) for c in range(2)], k)
    sv0 = sv_scr[0]
    sv1 = sv_scr[1]
    groups, a_tail = _cand_groups(k)
    off = 0
    for a, rows, nb in groups:
        blk = sv0[a:a + 1, :] + sv1[0:rows, :]
        if nb < rows:
            blk = jnp.where(lax.broadcasted_iota(I32, blk.shape, 0) < nb, blk, NEG_INF)
        cand_scr[off:off + rows, :] = blk
        off += rows
    cand_scr[off:off + k - a_tail, :] = sv0[a_tail:k, :] + sv1[0:1, :]
    _extract_top([(cand_scr, bs_scr, bp_scr)], k)
    best_s = bs_scr[...]
    pos = bp_scr[...]
    a_sel = jnp.zeros_like(pos)
    b_sel = pos
    off = 0
    for a, rows, nb in groups:
        if a > 0:
            a_sel = jnp.where(pos >= off, float(a), a_sel)
            b_sel = jnp.where(pos >= off, pos - off, b_sel)
        off += rows
    a_sel = jnp.where(pos >= off, a_tail + pos - off, a_sel)
    b_sel = jnp.where(pos >= off, 0.0, b_sel)
    si0 = si_scr[0]
    si1 = si_scr[1]
    i1 = jnp.zeros_like(pos)
    i2 = jnp.zeros_like(pos)
    for a in range(k):
        i1 = jnp.where(a_sel == a, si0[a:a + 1, :], i1)
        i2 = jnp.where(b_sel == a, si1[a:a + 1, :], i2)
    e = jnp.exp(best_s - jnp.max(best_s, axis=0, keepdims=True))
    i1_scr[h * k:(h + 1) * k, :] = i1
    i2_scr[h * k:(h + 1) * k, :] = i2
    gate_scr[h * k:(h + 1) * k, :] = e / jnp.sum(e, axis=0, keepdims=True)


def _peer_topk_body(qp_ref, sk_ref, i1_ref, i2_ref, gate_ref, i1_scr, i2_scr, gate_scr, *work):
    for h in range(PEER_HEADS):
        _peer_topk_head(qp_ref, sk_ref, h, i1_scr, i2_scr, gate_scr, *work)
    i1_ref[...] = i1_scr[...].T.astype(I32)
    i2_ref[...] = i2_scr[...].T.astype(I32)
    gate_ref[...] = gate_scr[...].T


def _peer_topk(qp, sub_keys, tm):
    m = qp.shape[0]
    k = PEER_TOPK
    ospec = pl.BlockSpec((tm, N_SLOTS), lambda i: (i, 0))
    slots = pltpu.VMEM((N_SLOTS, tm), F32)
    return pl.pallas_call(
        _peer_topk_body,
        grid=(m // tm,),
        in_specs=[pl.BlockSpec((tm, qp.shape[1]), lambda i: (i, 0)), _full(sub_keys)],
        out_specs=[ospec, ospec, ospec],
        out_shape=[jax.ShapeDtypeStruct((m, N_SLOTS), I32), jax.ShapeDtypeStruct((m, N_SLOTS), I32),
                   jax.ShapeDtypeStruct((m, N_SLOTS), F32)],
        scratch_shapes=[slots, slots, slots,
                        pltpu.VMEM((2, N_KEYS, tm), F32), pltpu.VMEM((_n_cand_rows(k), tm), F32),
                        pltpu.VMEM((2, k, tm), F32), pltpu.VMEM((2, k, tm), F32),
                        pltpu.VMEM((k, tm), F32), pltpu.VMEM((k, tm), F32)],
        compiler_params=_params(("parallel",), 32),
        name="peer_topk",
    )(qp, sub_keys)


def _peer_expert_body(x1_ref, h2_ref, i1_ref, i2_ref, gate_ref, u_ref, v_ref, gf_ref, y_ref, w_scr):
    j = pl.program_id(1)
    tm = h2_ref.shape[0]
    te = u_ref.shape[0]
    pitch = tm + W_PITCH_PAD

    @pl.when(j == 0)
    def _():
        y_ref[...] = x1_ref[...]
        sub = lax.broadcasted_iota(I32, (N_KEYS, N_SLOTS), 0)

        def tok(t, carry):
            hit1 = sub == i1_ref[pl.ds(t, 1), :]
            hit2 = sub == i2_ref[pl.ds(t, 1), :]
            a = jnp.where(hit1, gate_ref[pl.ds(t, 1), :], 0.0).astype(BF16)
            bm = jnp.where(hit2, 1.0, 0.0).astype(BF16)
            w_scr[pl.ds(t, N_KEYS, stride=pitch), :] = _dot(a, bm, NT)
            return carry

        lax.fori_loop(0, tm, tok, 0, unroll=TOKEN_UNROLL)

    rb = min(tm, EXPERT_ROW_BLOCK)
    pair = 2 * N_KEYS
    for r in range(tm // rb):
        x = h2_ref[r * rb:(r + 1) * rb, :]
        parts = []
        for q in range(te // pair):
            h = _dot(x, u_ref[q * pair:(q + 1) * pair, :], NT)
            i1 = j * (te // N_KEYS) + 2 * q
            w = jnp.concatenate([w_scr[pl.ds(pl.multiple_of(i1 * pitch + r * rb, SUBLANES), rb), :],
                                 w_scr[pl.ds(pl.multiple_of((i1 + 1) * pitch + r * rb, SUBLANES), rb), :]], axis=1)
            act = 0.5 * h * (1.0 + lax.erf(h * (2.0 ** -0.5)))
            parts.append((w * act).astype(BF16))
        y_ref[r * rb:(r + 1) * rb, :] += _dot(jnp.concatenate(parts, axis=1), v_ref[...])

    @pl.when(j == pl.num_programs(1) - 1)
    def _():
        x2 = y_ref[...]
        y_ref[...] = x2 * lax.rsqrt(jnp.mean(x2 * x2, axis=-1, keepdims=True) + NORM_EPS) * gf_ref[...]


def _peer_experts(x1, h2, i1, i2, gate, eu, ev, gf, tm, te):
    m = x1.shape[0]
    row = lambda w: pl.BlockSpec((tm, w), lambda i, j: (i, 0))
    tab = pl.BlockSpec((te, D_MODEL), lambda i, j: (j, 0))
    x1_spec = pl.BlockSpec((tm, D_MODEL), lambda i, j: (i, 0), pipeline_mode=pl.Buffered(1))
    return pl.pallas_call(
        _peer_expert_body,
        grid=(m // tm, N_EXPERTS // te),
        in_specs=[x1_spec, row(D_MODEL), row(N_SLOTS), row(N_SLOTS), row(N_SLOTS), tab, tab,
                  pl.BlockSpec((1, D_MODEL), lambda i, j: (0, 0))],
        out_specs=row(D_MODEL),
        out_shape=jax.ShapeDtypeStruct((m, D_MODEL), F32),
        scratch_shapes=[pltpu.VMEM((N_KEYS * (tm + W_PITCH_PAD), N_KEYS), F32)],
        compiler_params=_params(("parallel", "arbitrary"), 60),
        name="peer_experts",
    )(x1, h2, i1, i2, gate, eu, ev, gf)


def _pad_cols(a, width):
    return jnp.pad(a, [(0, 0)] * (a.ndim - 1) + [(0, width - a.shape[-1])])


def _prep_layer_params(l, norm1_g, w_in, b_in, attn_sinks, shift_mu, w0, w2, a0, a2, g2, k_k, k_a, r_k, ln_x_w,
                       ln_x_b, w_up_a, w_up_b, w_o, norm2_g, w_query, sub_keys, expert_u, expert_v):
    wi, bi = w_in[l], b_in[l][None, :]
    cat = lambda z: jnp.concatenate([z[:, :OFF_B], _pad_cols(z[:, OFF_B:OFF_GATE], RWKV_PAD), z[:, OFF_GATE:]], axis=1)
    row = lambda z: z[l].reshape(1, -1)
    lora = lambda z, off: jnp.pad(z[l], ((off, LORA_PAD - off - z.shape[1]), (0, 0)))
    return dict(
        norm1_g=row(norm1_g), w_cat=cat(wi).astype(BF16), b_cat=cat(bi), sinks=attn_sinks[l],
        mu=_pad_cols(row(shift_mu), RWKV_PAD), w0=row(w0), a0=row(a0), k_k=row(k_k), k_a=row(k_a),
        w2p=lora(w2, 0), a2p=lora(a2, DECAY_LORA), g2p=lora(g2, DECAY_LORA + AAA_LORA),
        r_k=row(r_k), ln_w=row(ln_x_w), ln_b=row(ln_x_b),
        w_up_a=w_up_a[l].astype(BF16), w_up_b=w_up_b[l].astype(BF16), w_o=w_o[l].astype(BF16),
        norm2_g=row(norm2_g), w_query=w_query[l].astype(BF16), sub_keys=sub_keys[l].astype(BF16),
        expert_u=expert_u[l].astype(BF16), expert_v=expert_v[l].astype(BF16),
    )


def _row_tile(m, pref):
    return pref if m % pref == 0 else m


def _hybrid_layer(x, pos, lp, eblk, gf, win_k, win_v, wkv0, shift0):
    b, t, _ = x.shape
    m = b * t
    x2 = x.reshape(m, D_MODEL)
    tm = _row_tile(m, 256)
    rope_tab = _rope_table(pos, max(t, tm))
    q, k, v, p, ga, gb = _in_proj(x2, lp["norm1_g"], lp["w_cat"], lp["b_cat"], rope_tab, tm)

    k3 = k.reshape(b, t, KV_WIDTH)
    v3 = v.reshape(b, t, KV_WIDTH)
    if win_k is None:
        att = _attn_prompt(lp["sinks"], q.reshape(b, t, ATTN_WIDTH), k3, v3).reshape(m, ATTN_WIDTH)
        keep = min(WINDOW, t)
        new_k, new_v = k3[:, t - keep:], v3[:, t - keep:]
    else:
        n_past = win_k.shape[1]
        qg = q.reshape(b, t, N_KV_HEADS, Q_PER_KV, HEAD_DIM).transpose(0, 2, 3, 1, 4)
        qg = qg.reshape(b, N_KV_HEADS, Q_PER_KV * t, HEAD_DIM)
        ck = win_k.reshape(b, n_past, KV_WIDTH)
        cv = win_v.reshape(b, n_past, KV_WIDTH)
        og = _attn_sample(lp["sinks"], qg, k3, v3, ck, cv, t)
        att = og.reshape(b, N_KV_HEADS, Q_PER_KV, t, HEAD_DIM).transpose(0, 3, 1, 2, 4).reshape(m, ATTN_WIDTH)
        new_k = jnp.concatenate([ck, k3], axis=1)[:, t:]
        new_v = jnp.concatenate([cv, v3], axis=1)[:, t:]
    new_k = new_k.reshape(b, -1, N_KV_HEADS, HEAD_DIM)
    new_v = new_v.reshape(b, -1, N_KV_HEADS, HEAD_DIM)

    p3 = p.reshape(b, t, RWKV_PAD)
    if t % RWKV_CHUNK == 0:
        ch, tp = RWKV_CHUNK, t
    else:
        tp = -(-t // SUBLANES) * SUBLANES
        ch = tp
        p3 = jnp.pad(p3, ((0, 0), (0, tp - t), (0, 0)))
    shift_pad = _pad_cols(shift0, RWKV_PAD)[:, None, :]
    bb = RWKV_BATCH_BLOCK if (tp == ch and b % RWKV_BATCH_BLOCK == 0) else 1
    akb, *prep = _rwkv_prep(p3, shift_pad, lp, eblk, ch, t, bb)
    tmat = _tri_solve(akb)
    rw, wkv_t = _rwkv_scan(tmat, prep, wkv0, lp, eblk, bb)
    rw = rw[:, :t].reshape(m, RWKV_WIDTH)
    shift_t = p3[:, t - 1, :RWKV_COLS]

    x1, h2, qp = _merge(x2, att, rw, ga, gb, lp, tm)
    i1, i2, gate = _peer_topk(qp, lp["sub_keys"], tm)
    y = _peer_experts(x1, h2, i1, i2, gate, lp["expert_u"], lp["expert_v"], gf,
                      _row_tile(m, EXPERT_TOKEN_TILE), EXPERT_TILE)
    return y.reshape(b, t, D_MODEL), new_k, new_v, wkv_t, shift_t


def kernel(x_prompt, x_sample, cache_win_k, cache_win_v, state_wkv, state_shift, norm1_g, w_in, b_in, attn_sinks,
           shift_mu, w0, w2, a0, a2, g2, k_k, k_a, r_k, ln_x_w, ln_x_b, w_up_a, w_up_b, w_o, norm2_g, w_query,
           sub_keys, expert_u, expert_v, final_norm_g):
    depth = w_in.shape[0]
    assert depth == 1, "the final RMSNorm is fused into the last layer's expert kernel"
    pos_p = jnp.arange(x_prompt.shape[1])
    pos_s = PAST_LEN + jnp.arange(x_sample.shape[1])
    li = lax.broadcasted_iota(I32, (RWKV_WIDTH, RWKV_WIDTH), 0) // RWKV_HEAD_DIM
    lj = lax.broadcasted_iota(I32, (RWKV_WIDTH, RWKV_WIDTH), 1) // RWKV_HEAD_DIM
    eblk = (li == lj).astype(BF16)
    gf = final_norm_g.reshape(1, D_MODEL)
    lp = _prep_layer_params(0, norm1_g, w_in, b_in, attn_sinks, shift_mu, w0, w2, a0, a2, g2, k_k, k_a, r_k, ln_x_w,
                            ln_x_b, w_up_a, w_up_b, w_o, norm2_g, w_query, sub_keys, expert_u, expert_v)
    bp = x_prompt.shape[0]
    wkv0 = jnp.zeros((bp, RWKV_HEADS, RWKV_HEAD_DIM, RWKV_HEAD_DIM), F32)
    shift0 = jnp.zeros((bp, RWKV_COLS), x_prompt.dtype)
    yp, pk, pv, pw, psh = _hybrid_layer(x_prompt, pos_p, lp, eblk, gf, None, None, wkv0, shift0)
    ys, sk, sv, sw, ssh = _hybrid_layer(x_sample, pos_s, lp, eblk, gf, cache_win_k[0], cache_win_v[0],
                                        state_wkv[0], state_shift[0])
    st = lambda z: z[None]
    return (yp, ys, st(pk), st(pv), st(pw), st(psh), st(sk), st(sv), st(sw), st(ssh))
```

```python
import functools

import jax
import jax.numpy as jnp
from jax import lax
from jax.experimental import pallas as pl
from jax.experimental.pallas import tpu as pltpu

F32, BF16, I32 = jnp.float32, jnp.bfloat16, jnp.int32

D_MODEL = 1024
NORM_EPS = 1e-5
HEAD_DIM = 64
N_Q_HEADS = 8
N_KV_HEADS = 2
Q_PER_KV = N_Q_HEADS // N_KV_HEADS
ATTN_WIDTH = N_Q_HEADS * HEAD_DIM
KV_WIDTH = N_KV_HEADS * HEAD_DIM
WINDOW = 128
ROT_DIM = HEAD_DIM // 4
ROPE_THETA = 500000.0
ATTN_SCALE = HEAD_DIM ** -0.5
RWKV_HEADS = 8
RWKV_HEAD_DIM = 64
RWKV_WIDTH = RWKV_HEADS * RWKV_HEAD_DIM
DECAY_LORA = 32
AAA_LORA = 32
GATE_LORA = 96
GN_EPS = 64e-5
RWKV_COLS = 3 * RWKV_WIDTH + DECAY_LORA + AAA_LORA + GATE_LORA
LORA_PAD = 256
RWKV_PAD = 3 * RWKV_WIDTH + LORA_PAD
OFF_K = ATTN_WIDTH
OFF_V = OFF_K + KV_WIDTH
OFF_B = OFF_V + KV_WIDTH
OFF_GATE = OFF_B + RWKV_COLS
N_KEYS = 128
N_EXPERTS = N_KEYS * N_KEYS
PEER_HEADS = 8
PEER_TOPK = 16
D_KEY_HALF = 128
N_SLOTS = PEER_HEADS * PEER_TOPK
PAST_LEN = 16384
RWKV_CHUNK = 64
LANES = 128
SUBLANES = 8
SOLVE_COLS = 16
TOKEN_UNROLL = 32
DECODE_BATCH_BLOCK = 8
DENSE_TOKEN_TILE = 512
EXPERT_TOKEN_TILE = 512
EXPERT_TILE = 1024
EXPERT_ROW_BLOCK = 512
W_PITCH_PAD = 8
NEG_INF = float("-inf")

NN = ((1,), (0,))
NT = ((1,), (1,))
TN = ((0,), (0,))


def _dot(a, b, dims=NN):
    return lax.dot_general(a, b, (dims, ((), ())), preferred_element_type=F32)


def _split2(x):
    hi = x.astype(BF16)
    return hi, (x - hi.astype(F32)).astype(BF16)


def _doth(a, b, dims=NN):
    a_hi, a_lo = _split2(a)
    b_hi, b_lo = _split2(b)
    return _dot(a_hi, b_hi, dims) + _dot(a_hi, b_lo, dims) + _dot(a_lo, b_hi, dims)


def _dotb(a, b, dims=NN):
    return _dot(a.astype(BF16), b.astype(BF16), dims)


def _split3(x):
    hi = x.astype(BF16)
    r1 = x - hi.astype(F32)
    mid = r1.astype(BF16)
    lo = (r1 - mid.astype(F32)).astype(BF16)
    return hi, mid, lo


def _dot_exact_rhs(x, e):
    hi, mid, lo = _split3(x)
    return _dot(hi, e) + _dot(mid, e) + _dot(lo, e)


def _dot_exact_lhs(e, x):
    hi, mid, lo = _split3(x)
    return _dot(e, hi) + _dot(e, mid) + _dot(e, lo)


def _params(sem, vmem_mb):
    return pltpu.CompilerParams(dimension_semantics=sem, vmem_limit_bytes=vmem_mb * 1024 * 1024)


def _full(a):
    nd = a.ndim
    return pl.BlockSpec(a.shape, lambda *_: (0,) * nd)


SEG_QKV = ATTN_WIDTH + 2 * KV_WIDTH
SEG_P = SEG_QKV + RWKV_PAD
SEG_END = SEG_P + 2 * D_MODEL


def _inproj_body(x_ref, g_ref, w_ref, b_ref, rope_ref, q_ref, k_ref, v_ref, p_ref, ga_ref, gb_ref):
    x = x_ref[...]
    h = (x * lax.rsqrt(jnp.mean(x * x, axis=-1, keepdims=True) + NORM_EPS) * g_ref[...]).astype(BF16)

    def seg(lo, hi):
        return _dot(h, w_ref[:, lo:hi]) + b_ref[:, lo:hi]

    cos = rope_ref[:, 0:LANES]
    sin_up = rope_ref[:, LANES:2 * LANES]
    sin_dn = rope_ref[:, 2 * LANES:3 * LANES]

    def rope(z):
        return z * cos + pltpu.roll(z, ROT_DIM // 2, 1) * sin_up + pltpu.roll(z, LANES - ROT_DIM // 2, 1) * sin_dn

    for c in range(ATTN_WIDTH // LANES):
        q_ref[:, c * LANES:(c + 1) * LANES] = rope(seg(c * LANES, (c + 1) * LANES))
    k_ref[...] = rope(seg(OFF_K, OFF_V))
    v_ref[...] = seg(OFF_V, SEG_QKV)
    p_ref[...] = seg(SEG_QKV, SEG_P)
    ga_ref[...] = jax.nn.sigmoid(seg(SEG_P, SEG_P + D_MODEL))
    gb_ref[...] = jax.nn.sigmoid(seg(SEG_P + D_MODEL, SEG_END))


def _in_proj(x2, g1, w_cat, b_cat, rope_tab, tm):
    m = x2.shape[0]
    nrep = rope_tab.shape[0] // tm
    widths = (ATTN_WIDTH, KV_WIDTH, KV_WIDTH, RWKV_PAD, D_MODEL, D_MODEL)
    return pl.pallas_call(
        _inproj_body,
        grid=(m // tm,),
        in_specs=[
            pl.BlockSpec((tm, D_MODEL), lambda i: (i, 0)),
            _full(g1), _full(w_cat), _full(b_cat),
            pl.BlockSpec((tm, 3 * LANES), lambda i: (i % nrep, 0)),
        ],
        out_specs=[pl.BlockSpec((tm, w), lambda i: (i, 0)) for w in widths],
        out_shape=[jax.ShapeDtypeStruct((m, w), F32) for w in widths],
        compiler_params=_params(("parallel",), 56),
        name="in_proj",
    )(x2, g1, w_cat, b_cat, rope_tab)


def _rope_table(pos, rows):
    t = pos.shape[0]
    half = ROT_DIM // 2
    inv_freq = ROPE_THETA ** (-jnp.arange(half, dtype=F32) / half)
    ang = pos.astype(F32)[:, None] * inv_freq[None, :]
    cos, sin = jnp.cos(ang), jnp.sin(ang)
    rest = HEAD_DIM - ROT_DIM
    z8 = jnp.zeros((t, half), F32)
    cos64 = jnp.concatenate([cos, cos, jnp.ones((t, rest), F32)], axis=1)
    up64 = jnp.concatenate([z8, sin, jnp.zeros((t, rest), F32)], axis=1)
    dn64 = jnp.concatenate([-sin, z8, jnp.zeros((t, rest), F32)], axis=1)
    tab = jnp.concatenate([jnp.tile(a, (1, LANES // HEAD_DIM)) for a in (cos64, up64, dn64)], axis=1)
    if rows > t:
        tab = jnp.tile(tab, (rows // t, 1))
    return tab


def _softmax_av(s, sink, vv):
    m = jnp.maximum(jnp.max(s, axis=-1, keepdims=True), sink)
    p = jnp.exp(s - m)
    den = jnp.sum(p, axis=-1, keepdims=True) + jnp.exp(sink - m)
    return _dot(p.astype(BF16), vv) / den


def _attn_prompt_body(sink_ref, q_ref, kc_ref, kp_ref, vc_ref, vp_ref, o_ref):
    n = pl.program_id(1)
    blk = q_ref.shape[0]
    i = lax.broadcasted_iota(I32, (blk, 2 * blk), 0)
    j = lax.broadcasted_iota(I32, (blk, 2 * blk), 1)
    diff = blk + i - j
    j_lo = jnp.where(n > 0, 0, blk)
    mask = (diff >= 0) & (diff < WINDOW) & (j >= j_lo)
    outs = []
    for g in range(N_KV_HEADS):
        sl = slice(g * HEAD_DIM, (g + 1) * HEAD_DIM)
        kk = jnp.concatenate([kp_ref[:, sl], kc_ref[:, sl]], axis=0).astype(BF16)
        vv = jnp.concatenate([vp_ref[:, sl], vc_ref[:, sl]], axis=0).astype(BF16)
        for hq in range(Q_PER_KV):
            h = g * Q_PER_KV + hq
            qh = q_ref[:, h * HEAD_DIM:(h + 1) * HEAD_DIM].astype(BF16)
            s = jnp.where(mask, _dot(qh, kk, NT) * ATTN_SCALE, NEG_INF)
            outs.append(_softmax_av(s, sink_ref[h], vv))
    o_ref[...] = jnp.concatenate(outs, axis=1).astype(o_ref.dtype)


def _attn_prompt(sinks, q3, k3, v3):
    b, t, _ = q3.shape
    blk = WINDOW
    cur = lambda w: pl.BlockSpec((None, blk, w), lambda bi, n: (bi, n, 0))
    prev = lambda w: pl.BlockSpec((None, blk, w), lambda bi, n: (bi, jnp.maximum(n - 1, 0), 0))
    return pl.pallas_call(
        _attn_prompt_body,
        grid=(b, t // blk),
        in_specs=[pl.BlockSpec(memory_space=pltpu.SMEM), cur(ATTN_WIDTH), cur(KV_WIDTH), prev(KV_WIDTH),
                  cur(KV_WIDTH), prev(KV_WIDTH)],
        out_specs=cur(ATTN_WIDTH),
        out_shape=jax.ShapeDtypeStruct((b, t, ATTN_WIDTH), BF16),
        compiler_params=_params(("parallel", "arbitrary"), 32),
        name="attn_prompt",
    )(sinks, q3, k3, k3, v3, v3)


def _attn_sample_body(sink_ref, *refs, t):
    for bi in range(refs[0].shape[0]):
        _attn_sample_one(sink_ref, *[r.at[bi] for r in refs], t=t)


def _attn_sample_one(sink_ref, q_ref, kn_ref, vn_ref, kc_ref, vc_ref, o_ref, *, t):
    rows = Q_PER_KV * t
    n_past = kc_ref.shape[0]
    ri = lax.broadcasted_iota(I32, (rows, 1), 0)
    qi = ri % t
    jc = lax.broadcasted_iota(I32, (rows, n_past), 1)
    mask_c = (n_past + qi - jc) < WINDOW
    for g in range(N_KV_HEADS):
        sl = slice(g * HEAD_DIM, (g + 1) * HEAD_DIM)
        sink = jnp.zeros((rows, 1), F32)
        for hq in range(Q_PER_KV):
            sink = jnp.where(ri // t == hq, sink_ref[g * Q_PER_KV + hq], sink)
        qg = q_ref[g]
        qb = qg.astype(BF16)
        kn = kn_ref[:, sl]
        vn = vn_ref[:, sl]
        s_c = jnp.where(mask_c, _dot(qb, kc_ref[:, sl].astype(BF16), NT) * ATTN_SCALE, NEG_INF)
        s_n = []
        for jn in range(t):
            sj = jnp.sum(qg * kn[jn:jn + 1, :], axis=-1, keepdims=True) * ATTN_SCALE
            s_n.append(jnp.where(qi >= jn, sj, NEG_INF))
        m = jnp.maximum(jnp.max(s_c, axis=-1, keepdims=True), sink)
        for sj in s_n:
            m = jnp.maximum(m, sj)
        p_c = jnp.exp(s_c - m)
        den = jnp.sum(p_c, axis=-1, keepdims=True) + jnp.exp(sink - m)
        o = _dot(p_c.astype(BF16), vc_ref[:, sl].astype(BF16))
        for jn in range(t):
            p_n = jnp.exp(s_n[jn] - m)
            den = den + p_n
            o = o + p_n * vn[jn:jn + 1, :]
        o_ref[g] = (o / den).astype(o_ref.dtype)


def _attn_sample(sinks, qg, kn, vn, cache_k, cache_v, t):
    b = qg.shape[0]
    rows = Q_PER_KV * t
    n_past = cache_k.shape[1]
    bb = DECODE_BATCH_BLOCK if b % DECODE_BATCH_BLOCK == 0 else 1
    qspec = pl.BlockSpec((bb, N_KV_HEADS, rows, HEAD_DIM), lambda bi: (bi, 0, 0, 0))
    nspec = pl.BlockSpec((bb, t, KV_WIDTH), lambda bi: (bi, 0, 0))
    cspec = pl.BlockSpec((bb, n_past, KV_WIDTH), lambda bi: (bi, 0, 0))
    return pl.pallas_call(
        functools.partial(_attn_sample_body, t=t),
        grid=(b // bb,),
        in_specs=[pl.BlockSpec(memory_space=pltpu.SMEM), qspec, nspec, nspec, cspec, cspec],
        out_specs=qspec,
        out_shape=jax.ShapeDtypeStruct((b, N_KV_HEADS, rows, HEAD_DIM), BF16),
        compiler_params=_params(("parallel",), 32),
        name="attn_sample",
    )(sinks, qg, kn, vn, cache_k, cache_v)


N_PREP_CONSTS = 9


def _rwkv_prep_body(*refs, t_real):
    seq, consts, outs = refs[:3], refs[3:3 + N_PREP_CONSTS], refs[3 + N_PREP_CONSTS:]
    for bi in range(seq[0].shape[0]):
        _rwkv_prep_one(*[r.at[bi] for r in seq], *consts, *[r.at[bi] for r in outs], t_real=t_real)


def _rwkv_prep_one(p_ref, pprev_ref, shift_ref, mu_ref, w0_ref, a0_ref, kk_ref, ka_ref, w2_ref, a2_ref, g2_ref,
                   eblk_ref, akb_ref, akk_ref, qrk_ref, qrb_ref, kkt_ref, rt_ref, kdec_ref, bdec_ref, kmod_ref,
                   r_ref, v_ref, g_ref, gamc_ref, *, t_real):
    c = pl.program_id(1)
    ch = p_ref.shape[0]
    pc = p_ref[...]
    row = lax.broadcasted_iota(I32, (ch, 1), 0)
    first = jnp.where(c == 0, shift_ref[...], pprev_ref[SUBLANES - 1:SUBLANES, :])
    prev = jnp.where(row == 0, first, pltpu.roll(pc, 1, 0))
    xm = pc + mu_ref[...] * (prev - pc)
    valid = (c * ch + row) < t_real
    w3 = 3 * RWKV_WIDTH
    r = xm[:, 0:RWKV_WIDTH]
    k = xm[:, RWKV_WIDTH:2 * RWKV_WIDTH]
    v = xm[:, 2 * RWKV_WIDTH:w3]
    lo = xm[:, w3:w3 + LORA_PAD]
    lane = lax.broadcasted_iota(I32, lo.shape, 1)
    act = jnp.where(lane < DECAY_LORA, jnp.tanh(lo),
                    jnp.where(lane < DECAY_LORA + AAA_LORA, lo, jax.nn.sigmoid(lo)))
    z = w0_ref[...] + _doth(act, w2_ref[...])
    w_log = -(jnp.maximum(-z, 0.0) + jnp.log1p(jnp.exp(-jnp.abs(z)))) - 0.5
    lw = jnp.where(valid, -jnp.exp(w_log), 0.0)
    a = jax.nn.sigmoid(a0_ref[...] + _doth(act, a2_ref[...]))
    g = _dot(act.astype(BF16), g2_ref[...].astype(BF16))
    kk = k * kk_ref[...]
    ss = _dot_exact_rhs(kk * kk, eblk_ref[...])
    kk = jnp.where(valid, kk * lax.rsqrt(jnp.maximum(ss, 1e-24)), 0.0)
    kmod = jnp.where(valid, k * (1.0 + (a - 1.0) * ka_ref[...]), 0.0)
    bb = kk * a
    ti = lax.broadcasted_iota(I32, (ch, ch), 0)
    tj = lax.broadcasted_iota(I32, (ch, ch), 1)
    tri = jnp.where(ti >= tj, 1.0, 0.0).astype(BF16)
    cum = _dot_exact_lhs(tri, lw)
    cum_c = cum[ch - 1:ch, :]
    e_neg = jnp.exp(-cum)
    e_rem = jnp.exp(cum_c - cum)
    kkt = kk * jnp.exp(cum - lw)
    rt = r * jnp.exp(cum)
    khat = kmod * e_neg
    bhat = bb * e_neg
    strict = ti > tj
    incl = ti >= tj
    hs = range(RWKV_HEADS)
    sl = [slice(h * RWKV_HEAD_DIM, (h + 1) * RWKV_HEAD_DIM) for h in hs]
    split = lambda z: [z[:, s].astype(BF16) for s in sl]
    kkt_s, rt_s, khat_s, bhat_s = split(kkt), split(rt), split(khat), split(bhat)
    akb = [_dot(kkt_s[h], bhat_s[h], NT) for h in hs]
    akk = [_dot(kkt_s[h], khat_s[h], NT) for h in hs]
    qrk = [_dot(rt_s[h], khat_s[h], NT) for h in hs]
    qrb = [_dot(rt_s[h], bhat_s[h], NT) for h in hs]
    for h in hs:
        akb_ref[h] = jnp.where(strict, akb[h], 0.0)
        akk_ref[h] = jnp.where(strict, akk[h], 0.0)
        qrk_ref[h] = jnp.where(incl, qrk[h], 0.0)
        qrb_ref[h] = jnp.where(incl, qrb[h], 0.0)
    kkt_ref[...] = kkt
    rt_ref[...] = rt
    kdec_ref[...] = kmod * e_rem
    bdec_ref[...] = bb * e_rem
    kmod_ref[...] = kmod
    r_ref[...] = r
    v_ref[...] = v
    g_ref[...] = g
    gamc_ref[...] = jnp.exp(cum_c)


def _rwkv_prep(p3, shift0, lp, eblk, ch, t_real, bb):
    b, tp, _ = p3.shape
    nc = tp // ch
    per8 = ch // SUBLANES
    mats = jax.ShapeDtypeStruct((b, nc, RWKV_HEADS, ch, ch), F32)
    rows = jax.ShapeDtypeStruct((b, tp, RWKV_WIDTH), F32)
    mspec = pl.BlockSpec((bb, None, RWKV_HEADS, ch, ch), lambda bi, ci: (bi, ci, 0, 0, 0))
    rspec = pl.BlockSpec((bb, ch, RWKV_WIDTH), lambda bi, ci: (bi, ci, 0))
    consts = (lp["mu"], lp["w0"], lp["a0"], lp["k_k"], lp["k_a"], lp["w2p"], lp["a2p"], lp["g2p"], eblk)
    assert len(consts) == N_PREP_CONSTS
    return pl.pallas_call(
        functools.partial(_rwkv_prep_body, t_real=t_real),
        grid=(b // bb, nc),
        in_specs=[
            pl.BlockSpec((bb, ch, RWKV_PAD), lambda bi, ci: (bi, ci, 0)),
            pl.BlockSpec((bb, SUBLANES, RWKV_PAD), lambda bi, ci: (bi, jnp.maximum(ci * per8 - 1, 0), 0)),
            pl.BlockSpec((bb, 1, RWKV_PAD), lambda bi, ci: (bi, 0, 0)),
        ] + [_full(a) for a in consts],
        out_specs=[mspec] * 4 + [rspec] * 8 + [pl.BlockSpec((bb, None, 1, RWKV_WIDTH), lambda bi, ci: (bi, ci, 0, 0))],
        out_shape=[mats] * 4 + [rows] * 8 + [jax.ShapeDtypeStruct((b, nc, 1, RWKV_WIDTH), F32)],
        compiler_params=_params(("parallel", "parallel"), 48),
        name="rwkv_prep",
    )(p3, p3, shift0, *consts)


def _tri_solve_body(a_ref, t_ref):
    ch = a_ref.shape[0]
    cols = min(SOLVE_COLS, ch)
    t_ref[...] = jnp.zeros(t_ref.shape, F32)
    ci = lax.broadcasted_iota(I32, (cols, SUBLANES, LANES), 0)

    def row_body(s, carry):
        for kc in range(ch // cols):
            c0 = kc * cols

            @pl.when(s >= c0)
            def _():
                acc0 = jnp.where(ci == s - c0, 1.0, 0.0).astype(F32)

                def r_body(r, acc):
                    return acc - a_ref[s, r][None] * t_ref[r, pl.ds(c0, cols)]

                t_ref[s, pl.ds(c0, cols)] = lax.fori_loop(c0, s, r_body, acc0)
        return carry

    lax.fori_loop(0, ch, row_body, 0)


def _tri_solve(akb):
    b, nc, nh, ch, _ = akb.shape
    nsys = b * nc * nh
    slab = SUBLANES * LANES
    nslab = -(-nsys // slab)
    a2 = akb.reshape(nsys, ch * ch)
    a2 = jnp.pad(a2, ((0, nslab * slab - nsys), (0, 0)))
    a5 = a2.T.reshape(ch, ch, nslab, SUBLANES, LANES)
    spec = pl.BlockSpec((ch, ch, None, SUBLANES, LANES), lambda gi: (0, 0, gi, 0, 0),
                        pipeline_mode=pl.Buffered(1))
    t5 = pl.pallas_call(
        _tri_solve_body,
        grid=(nslab,),
        in_specs=[spec],
        out_specs=spec,
        out_shape=jax.ShapeDtypeStruct(a5.shape, F32),
        compiler_params=_params(("parallel",), 48),
        name="rwkv_tri_solve",
    )(a5)
    t2 = t5.reshape(ch * ch, nslab * slab).T[:nsys]
    return t2.reshape(b, nc, nh, ch, ch)


N_SCAN_SEQ = 14
N_SCAN_CONSTS = 4


def _rwkv_scan_body(*refs):
    seq, consts, rest = refs[:N_SCAN_SEQ], refs[N_SCAN_SEQ:N_SCAN_SEQ + N_SCAN_CONSTS], refs[N_SCAN_SEQ + N_SCAN_CONSTS:]
    for bi in range(seq[0].shape[0]):
        _rwkv_scan_one(*[r.at[bi] for r in seq], *consts, *[r.at[bi] for r in rest])


def _rwkv_scan_one(t_ref, akk_ref, qrk_ref, qrb_ref, kkt_ref, rt_ref, kdec_ref, bdec_ref, kmod_ref, r_ref, v_ref,
                   g_ref, gamc_ref, wkv0_ref, eblk_ref, rk_ref, lnw_ref, lnb_ref, rw_ref, wkv_ref, s_scr):
    c = pl.program_id(1)
    nd = RWKV_HEAD_DIM

    @pl.when(c == 0)
    def _():
        s_scr[...] = wkv0_ref[...]

    hs = range(RWKV_HEADS)
    sl = [slice(h * nd, (h + 1) * nd) for h in hs]
    tmat = [t_ref[h] for h in hs]
    vh = [v_ref[:, sl[h]] for h in hs]
    bdec = [bdec_ref[:, sl[h]] for h in hs]
    qrb = [qrb_ref[h] for h in hs]
    w1 = [_dotb(tmat[h], kkt_ref[:, sl[h]]) for h in hs]
    av = [_dotb(akk_ref[h], vh[h]) for h in hs]
    qv = [_dotb(qrk_ref[h], vh[h]) for h in hs]
    vk = [_dotb(vh[h], kdec_ref[:, sl[h]], TN) for h in hs]
    w2 = [_dotb(tmat[h], av[h]) for h in hs]
    qw1 = [_dotb(qrb[h], w1[h]) for h in hs]
    bw1 = [_dotb(bdec[h], w1[h], TN) for h in hs]
    qw2 = [_dotb(qrb[h], w2[h]) for h in hs]
    wb = [_dotb(w2[h], bdec[h], TN) for h in hs]
    s0 = [s_scr[h] for h in hs]
    ys = [_dotb(rt_ref[:, sl[h]] - qw1[h], s0[h], NT) + (qv[h] - qw2[h]) for h in hs]
    for h in hs:
        s_scr[h] = s0[h] * gamc_ref[:, sl[h]] - _dotb(s0[h], bw1[h], NT) + (vk[h] - wb[h])
    y = jnp.concatenate(ys, axis=1)
    eblk = eblk_ref[...]
    inv_n = 1.0 / nd
    mean = _dot_exact_rhs(y, eblk) * inv_n
    yc = y - mean
    var = _dot_exact_rhs(yc * yc, eblk) * inv_n
    yn = yc * lax.rsqrt(var + GN_EPS) * lnw_ref[...] + lnb_ref[...]
    v = v_ref[...]
    bonus = _dot_exact_rhs(r_ref[...] * kmod_ref[...] * rk_ref[...], eblk) * v
    rw_ref[...] = ((yn + bonus) * g_ref[...]).astype(rw_ref.dtype)

    @pl.when(c == pl.num_programs(1) - 1)
    def _():
        wkv_ref[...] = s_scr[...]


def _rwkv_scan(tmat, prep, wkv0, lp, eblk, bb):
    akk, qrk, qrb, kkt, rt, kdec, bdec, kmod, r, v, g, gamc = prep
    b, nc, nh, ch, _ = tmat.shape
    tp = kkt.shape[1]
    nd = RWKV_HEAD_DIM
    mspec = pl.BlockSpec((bb, None, nh, ch, ch), lambda bi, ci: (bi, ci, 0, 0, 0))
    rspec = pl.BlockSpec((bb, ch, RWKV_WIDTH), lambda bi, ci: (bi, ci, 0))
    sspec = pl.BlockSpec((bb, nh, nd, nd), lambda bi, ci: (bi, 0, 0, 0))
    consts = (eblk, lp["r_k"], lp["ln_w"], lp["ln_b"])
    assert len(consts) == N_SCAN_CONSTS
    return pl.pallas_call(
        _rwkv_scan_body,
        grid=(b // bb, nc),
        in_specs=[mspec] * 4 + [rspec] * 8
        + [pl.BlockSpec((bb, None, 1, RWKV_WIDTH), lambda bi, ci: (bi, ci, 0, 0)), sspec]
        + [_full(a) for a in consts],
        out_specs=[rspec, sspec],
        out_shape=[jax.ShapeDtypeStruct((b, tp, RWKV_WIDTH), BF16), jax.ShapeDtypeStruct((b, nh, nd, nd), F32)],
        scratch_shapes=[pltpu.VMEM((bb, nh, nd, nd), F32)],
        compiler_params=_params(("parallel", "arbitrary"), 48),
        name="rwkv_scan",
    )(tmat, akk, qrk, qrb, kkt, rt, kdec, bdec, kmod, r, v, g, gamc, wkv0, *consts)


def _merge_body(x_ref, att_ref, rw_ref, ga_ref, gb_ref, wa_ref, wb_ref, wo_ref, g2_ref, wq_ref,
                x1_ref, h2_ref, qp_ref):
    merged = ga_ref[...] * _dot(att_ref[...], wa_ref[...]) + gb_ref[...] * _dot(rw_ref[...], wb_ref[...])
    x1 = x_ref[...] + _dot(merged.astype(BF16), wo_ref[...])
    x1_ref[...] = x1
    h2 = (x1 * lax.rsqrt(jnp.mean(x1 * x1, axis=-1, keepdims=True) + NORM_EPS) * g2_ref[...]).astype(BF16)
    h2_ref[...] = h2
    qp_ref[...] = _dot(h2, wq_ref[...]).astype(qp_ref.dtype)


def _merge(x2, att, rw, ga, gb, lp, tm):
    m = x2.shape[0]
    kq = lp["w_query"].shape[1]
    row = lambda w: pl.BlockSpec((tm, w), lambda i: (i, 0))
    consts = (lp["w_up_a"], lp["w_up_b"], lp["w_o"], lp["norm2_g"], lp["w_query"])
    return pl.pallas_call(
        _merge_body,
        grid=(m // tm,),
        in_specs=[row(D_MODEL), row(ATTN_WIDTH), row(RWKV_WIDTH), row(D_MODEL), row(D_MODEL)]
        + [_full(a) for a in consts],
        out_specs=[row(D_MODEL), row(D_MODEL), row(kq)],
        out_shape=[jax.ShapeDtypeStruct((m, D_MODEL), F32), jax.ShapeDtypeStruct((m, D_MODEL), BF16),
                   jax.ShapeDtypeStruct((m, kq), BF16)],
        compiler_params=_params(("parallel",), 56),
        name="merge",
    )(x2, att, rw, ga, gb, *consts)


def _argmax_rows(src_ref):
    nslab = src_ref.shape[0] // SUBLANES
    row0 = lax.broadcasted_iota(I32, (SUBLANES, src_ref.shape[1]), 0).astype(F32)
    vals = [src_ref[i * SUBLANES:(i + 1) * SUBLANES, :] for i in range(nslab)]
    rows = [row0 + float(i * SUBLANES) for i in range(nslab)]
    while len(vals) > 1:
        nv, nr = [], []
        for a in range(0, len(vals) - 1, 2):
            take = vals[a + 1] > vals[a]
            nv.append(jnp.maximum(vals[a], vals[a + 1]))
            nr.append(jnp.where(take, rows[a + 1], rows[a]))
        if len(vals) % 2:
            nv.append(vals[-1])
            nr.append(rows[-1])
        vals, rows = nv, nr
    v, i = vals[0], rows[0]
    for shift in (4, 2, 1):
        v2 = pltpu.roll(v, shift, 0)
        i2 = pltpu.roll(i, shift, 0)
        take = (v2 > v) | ((v2 == v) & (i2 < i))
        v = jnp.where(take, v2, v)
        i = jnp.where(take, i2, i)
    return jnp.max(v, axis=0, keepdims=True), jnp.max(i, axis=0, keepdims=True)


def _extract_top(jobs, n_out):
    def body(r, carry):
        for src_ref, val_ref, idx_ref in jobs:
            rowf = lax.broadcasted_iota(I32, src_ref.shape, 0).astype(F32)
            m, idx = _argmax_rows(src_ref)
            val_ref[pl.ds(r, 1), :] = m
            idx_ref[pl.ds(r, 1), :] = idx
            src_ref[...] = jnp.where(rowf == idx, NEG_INF, src_ref[...])
        return carry

    lax.fori_loop(0, n_out, body, 0)


def _cand_groups(k):
    groups = []
    a = 0
    while a < k and k // (a + 1) > 1:
        nb = k // (a + 1)
        groups.append((a, -(-nb // SUBLANES) * SUBLANES, nb))
        a += 1
    return groups, a


def _n_cand_rows(k):
    groups, a_tail = _cand_groups(k)
    return sum(rows for _, rows, _ in groups) + k - a_tail


def _peer_topk_head(qp_ref, sk_ref, h, i1_scr, i2_scr, gate_scr, s_scr, cand_scr, sv_scr, si_scr, bs_scr, bp_scr):
    k = PEER_TOPK
    for c in range(2):
        col = (2 * h + c) * D_KEY_HALF
        s_scr[c] = _dot(sk_ref[h, c], qp_ref[:, col:col + D_KEY_HALF], NT)
    _extract_top([(s_scr.at[c], sv_scr.at[c], si_scr.at[c]) for c in range(2)], k)
    sv0 = sv_scr[0]
    sv1 = sv_scr[1]
    groups, a_tail = _cand_groups(k)
    off = 0
    for a, rows, nb in groups:
        blk = sv0[a:a + 1, :] + sv1[0:rows, :]
        if nb < rows:
            blk = jnp.where(lax.broadcasted_iota(I32, blk.shape, 0) < nb, blk, NEG_INF)
        cand_scr[off:off + rows, :] = blk
        off += rows
    cand_scr[off:off + k - a_tail, :] = sv0[a_tail:k, :] + sv1[0:1, :]
    _extract_top([(cand_scr, bs_scr, bp_scr)], k)
    best_s = bs_scr[...]
    pos = bp_scr[...]
    a_sel = jnp.zeros_like(pos)
    b_sel = pos
    off = 0
    for a, rows, nb in groups:
        if a > 0:
            a_sel = jnp.where(pos >= off, float(a), a_sel)
            b_sel = jnp.where(pos >= off, pos - off, b_sel)
        off += rows
    a_sel = jnp.where(pos >= off, a_tail + pos - off, a_sel)
    b_sel = jnp.where(pos >= off, 0.0, b_sel)
    si0 = si_scr[0]
    si1 = si_scr[1]
    i1 = jnp.zeros_like(pos)
    i2 = jnp.zeros_like(pos)
    for a in range(k):
        i1 = jnp.where(a_sel == a, si0[a:a + 1, :], i1)
        i2 = jnp.where(b_sel == a, si1[a:a + 1, :], i2)
    e = jnp.exp(best_s - jnp.max(best_s, axis=0, keepdims=True))
    i1_scr[h * k:(h + 1) * k, :] = i1
    i2_scr[h * k:(h + 1) * k, :] = i2
    gate_scr[h * k:(h + 1) * k, :] = e / jnp.sum(e, axis=0, keepdims=True)


def _peer_topk_body(qp_ref, sk_ref, i1_ref, i2_ref, gate_ref, i1_scr, i2_scr, gate_scr, *work):
    for h in range(PEER_HEADS):
        _peer_topk_head(qp_ref, sk_ref, h, i1_scr, i2_scr, gate_scr, *work)
    i1_ref[...] = i1_scr[...].T.astype(I32)
    i2_ref[...] = i2_scr[...].T.astype(I32)
    gate_ref[...] = gate_scr[...].T


def _peer_topk(qp, sub_keys, tm):
    m = qp.shape[0]
    k = PEER_TOPK
    ospec = pl.BlockSpec((tm, N_SLOTS), lambda i: (i, 0))
    slots = pltpu.VMEM((N_SLOTS, tm), F32)
    return pl.pallas_call(
        _peer_topk_body,
        grid=(m // tm,),
        in_specs=[pl.BlockSpec((tm, qp.shape[1]), lambda i: (i, 0)), _full(sub_keys)],
        out_specs=[ospec, ospec, ospec],
        out_shape=[jax.ShapeDtypeStruct((m, N_SLOTS), I32), jax.ShapeDtypeStruct((m, N_SLOTS), I32),
                   jax.ShapeDtypeStruct((m, N_SLOTS), F32)],
        scratch_shapes=[slots, slots, slots,
                        pltpu.VMEM((2, N_KEYS, tm), F32), pltpu.VMEM((_n_cand_rows(k), tm), F32),
                        pltpu.VMEM((2, k, tm), F32), pltpu.VMEM((2, k, tm), F32),
                        pltpu.VMEM((k, tm), F32), pltpu.VMEM((k, tm), F32)],
        compiler_params=_params(("parallel",), 32),
        name="peer_topk",
    )(qp, sub_keys)


def _peer_expert_body(x1_ref, h2_ref, i1_ref, i2_ref, gate_ref, u_ref, v_ref, gf_ref, y_ref, w_scr):
    j = pl.program_id(1)
    tm = h2_ref.shape[0]
    te = v_ref.shape[0]
    pitch = tm + W_PITCH_PAD

    @pl.when(j == 0)
    def _():
        y_ref[...] = x1_ref[...]
        sub = lax.broadcasted_iota(I32, (N_KEYS, N_SLOTS), 0)

        def tok(t, carry):
            hit1 = sub == i1_ref[pl.ds(t, 1), :]
            hit2 = sub == i2_ref[pl.ds(t, 1), :]
            a = jnp.where(hit1, gate_ref[pl.ds(t, 1), :], 0.0).astype(BF16)
            bm = jnp.where(hit2, 1.0, 0.0).astype(BF16)
            w_scr[pl.ds(t, N_KEYS, stride=pitch), :] = _dot(a, bm, NT)
            return carry

        lax.fori_loop(0, tm, tok, 0, unroll=TOKEN_UNROLL)

    rb = min(tm, EXPERT_ROW_BLOCK)
    pair = 2 * N_KEYS
    for r in range(tm // rb):
        x = h2_ref[r * rb:(r + 1) * rb, :]
        parts = []
        for q in range(te // pair):
            h = _dot(x, u_ref[q * pair:(q + 1) * pair, :], NT)
            i1 = j * (te // N_KEYS) + 2 * q
            w = jnp.concatenate([w_scr[pl.ds(pl.multiple_of(i1 * pitch + r * rb, SUBLANES), rb), :],
                                 w_scr[pl.ds(pl.multiple_of((i1 + 1) * pitch + r * rb, SUBLANES), rb), :]], axis=1)
            act = 0.5 * h * (1.0 + lax.erf(h * (2.0 ** -0.5)))
            parts.append((w * act).astype(BF16))
        y_ref[r * rb:(r + 1) * rb, :] += _dot(jnp.concatenate(parts, axis=1), v_ref[...])

    @pl.when(j == pl.num_programs(1) - 1)
    def _():
        x2 = y_ref[...]
        y_ref[...] = x2 * lax.rsqrt(jnp.mean(x2 * x2, axis=-1, keepdims=True) + NORM_EPS) * gf_ref[...]


def _peer_experts(x1, h2, i1, i2, gate, eu, ev, gf, tm, te):
    m = x1.shape[0]
    row = lambda w: pl.BlockSpec((tm, w), lambda i, j: (i, 0))
    tab = pl.BlockSpec((te, D_MODEL), lambda i, j: (j, 0))
    x1_spec = pl.BlockSpec((tm, D_MODEL), lambda i, j: (i, 0), pipeline_mode=pl.Buffered(1))
    return pl.pallas_call(
        _peer_expert_body,
        grid=(m // tm, N_EXPERTS // te),
        in_specs=[x1_spec, row(D_MODEL), row(N_SLOTS), row(N_SLOTS), row(N_SLOTS), tab, tab,
                  pl.BlockSpec((1, D_MODEL), lambda i, j: (0, 0))],
        out_specs=row(D_MODEL),
        out_shape=jax.ShapeDtypeStruct((m, D_MODEL), F32),
        scratch_shapes=[pltpu.VMEM((N_KEYS * (tm + W_PITCH_PAD), N_KEYS), F32)],
        compiler_params=_params(("parallel", "arbitrary"), 60),
        name="peer_experts",
    )(x1, h2, i1, i2, gate, eu, ev, gf)


def _pad_cols(a, width):
    return jnp.pad(a, [(0, 0)] * (a.ndim - 1) + [(0, width - a.shape[-1])])


def _prep_layer_params(l, norm1_g, w_in, b_in, attn_sinks, shift_mu, w0, w2, a0, a2, g2, k_k, k_a, r_k, ln_x_w,
                       ln_x_b, w_up_a, w_up_b, w_o, norm2_g, w_query, sub_keys, expert_u, expert_v):
    wi, bi = w_in[l], b_in[l][None, :]
    cat = lambda z: jnp.concatenate([z[:, :OFF_B], _pad_cols(z[:, OFF_B:OFF_GATE], RWKV_PAD), z[:, OFF_GATE:]], axis=1)
    row = lambda z: z[l].reshape(1, -1)
    lora = lambda z, off: jnp.pad(z[l], ((off, LORA_PAD - off - z.shape[1]), (0, 0)))
    return dict(
        norm1_g=row(norm1_g), w_cat=cat(wi).astype(BF16), b_cat=cat(bi), sinks=attn_sinks[l],
        mu=_pad_cols(row(shift_mu), RWKV_PAD), w0=row(w0), a0=row(a0), k_k=row(k_k), k_a=row(k_a),
        w2p=lora(w2, 0), a2p=lora(a2, DECAY_LORA), g2p=lora(g2, DECAY_LORA + AAA_LORA),
        r_k=row(r_k), ln_w=row(ln_x_w), ln_b=row(ln_x_b),
        w_up_a=w_up_a[l].astype(BF16), w_up_b=w_up_b[l].astype(BF16), w_o=w_o[l].astype(BF16),
        norm2_g=row(norm2_g), w_query=w_query[l].astype(BF16), sub_keys=sub_keys[l].astype(BF16),
        expert_u=expert_u[l].astype(BF16), expert_v=expert_v[l].astype(BF16),
    )


def _row_tile(m, pref):
    return pref if m % pref == 0 else m


def _hybrid_layer(x, pos, lp, eblk, gf, win_k, win_v, wkv0, shift0):
    b, t, _ = x.shape
    m = b * t
    x2 = x.reshape(m, D_MODEL)
    tm = _row_tile(m, 256)
    tmd = _row_tile(m, DENSE_TOKEN_TILE)
    rope_tab = _rope_table(pos, max(t, tmd))
    q, k, v, p, ga, gb = _in_proj(x2, lp["norm1_g"], lp["w_cat"], lp["b_cat"], rope_tab, tmd)

    k3 = k.reshape(b, t, KV_WIDTH)
    v3 = v.reshape(b, t, KV_WIDTH)
    if win_k is None:
        att = _attn_prompt(lp["sinks"], q.reshape(b, t, ATTN_WIDTH), k3, v3).reshape(m, ATTN_WIDTH)
        keep = min(WINDOW, t)
        new_k, new_v = k3[:, t - keep:], v3[:, t - keep:]
    else:
        n_past = win_k.shape[1]
        qg = q.reshape(b, t, N_KV_HEADS, Q_PER_KV, HEAD_DIM).transpose(0, 2, 3, 1, 4)
        qg = qg.reshape(b, N_KV_HEADS, Q_PER_KV * t, HEAD_DIM)
        ck = win_k.reshape(b, n_past, KV_WIDTH)
        cv = win_v.reshape(b, n_past, KV_WIDTH)
        og = _attn_sample(lp["sinks"], qg, k3, v3, ck, cv, t)
        att = og.reshape(b, N_KV_HEADS, Q_PER_KV, t, HEAD_DIM).transpose(0, 3, 1, 2, 4).reshape(m, ATTN_WIDTH)
        new_k = jnp.concatenate([ck, k3], axis=1)[:, t:]
        new_v = jnp.concatenate([cv, v3], axis=1)[:, t:]
    new_k = new_k.reshape(b, -1, N_KV_HEADS, HEAD_DIM)
    new_v = new_v.reshape(b, -1, N_KV_HEADS, HEAD_DIM)

    p3 = p.reshape(b, t, RWKV_PAD)
    if t % RWKV_CHUNK == 0:
        ch, tp = RWKV_CHUNK, t
    else:
        tp = -(-t // SUBLANES) * SUBLANES
        ch = tp
        p3 = jnp.pad(p3, ((0, 0), (0, tp - t), (0, 0)))
    shift_pad = _pad_cols(shift0, RWKV_PAD)[:, None, :]
    bb = DECODE_BATCH_BLOCK if (tp == ch and b % DECODE_BATCH_BLOCK == 0) else 1
    akb, *prep = _rwkv_prep(p3, shift_pad, lp, eblk, ch, t, bb)
    tmat = _tri_solve(akb)
    rw, wkv_t = _rwkv_scan(tmat, prep, wkv0, lp, eblk, bb)
    rw = rw[:, :t].reshape(m, RWKV_WIDTH)
    shift_t = p3[:, t - 1, :RWKV_COLS]

    x1, h2, qp = _merge(x2, att, rw, ga, gb, lp, tmd)
    i1, i2, gate = _peer_topk(qp, lp["sub_keys"], tm)
    y = _peer_experts(x1, h2, i1, i2, gate, lp["expert_u"], lp["expert_v"], gf,
                      _row_tile(m, EXPERT_TOKEN_TILE), EXPERT_TILE)
    return y.reshape(b, t, D_MODEL), new_k, new_v, wkv_t, shift_t


def kernel(x_prompt, x_sample, cache_win_k, cache_win_v, state_wkv, state_shift, norm1_g, w_in, b_in, attn_sinks,
           shift_mu, w0, w2, a0, a2, g2, k_k, k_a, r_k, ln_x_w, ln_x_b, w_up_a, w_up_b, w_o, norm2_g, w_query,
           sub_keys, expert_u, expert_v, final_norm_g):
    depth = w_in.shape[0]
    assert depth == 1, "the final RMSNorm is fused into the last layer's expert kernel"
    pos_p = jnp.arange(x_prompt.shape[1])
    pos_s = PAST_LEN + jnp.arange(x_sample.shape[1])
    li = lax.broadcasted_iota(I32, (RWKV_WIDTH, RWKV_WIDTH), 0) // RWKV_HEAD_DIM
    lj = lax.broadcasted_iota(I32, (RWKV_WIDTH, RWKV_WIDTH), 1) // RWKV_HEAD_DIM
    eblk = (li == lj).astype(BF16)
    gf = final_norm_g.reshape(1, D_MODEL)
    lp = _prep_layer_params(0, norm1_g, w_in, b_in, attn_sinks, shift_mu, w0, w2, a0, a2, g2, k_k, k_a, r_k, ln_x_w,
                            ln_x_b, w_up_a, w_up_b, w_o, norm2_g, w_query, sub_keys, expert_u, expert_v)
    bp = x_prompt.shape[0]
    wkv0 = jnp.zeros((bp, RWKV_HEADS, RWKV_HEAD_DIM, RWKV_HEAD_DIM), F32)
    shift0 = jnp.zeros((bp, RWKV_COLS), x_prompt.dtype)
    yp, pk, pv, pw, psh = _hybrid_layer(x_prompt, pos_p, lp, eblk, gf, None, None, wkv0, shift0)
    ys, sk, sv, sw, ssh = _hybrid_layer(x_sample, pos_s, lp, eblk, gf, cache_win_k[0], cache_win_v[0],
                                        state_wkv[0], state_shift[0])
    st = lambda z: z[None]
    return (yp, ys, st(pk), st(pv), st(pw), st(psh), st(sk), st(sv), st(sw), st(ssh))
```

```python
import functools

import jax
import jax.numpy as jnp
from jax import lax
from jax.experimental import pallas as pl
from jax.experimental.pallas import tpu as pltpu

F32, BF16, I32 = jnp.float32, jnp.bfloat16, jnp.int32

D_MODEL = 1024
NORM_EPS = 1e-5
HEAD_DIM = 64
N_Q_HEADS = 8
N_KV_HEADS = 2
Q_PER_KV = N_Q_HEADS // N_KV_HEADS
ATTN_WIDTH = N_Q_HEADS * HEAD_DIM
KV_WIDTH = N_KV_HEADS * HEAD_DIM
WINDOW = 128
ROT_DIM = HEAD_DIM // 4
ROPE_THETA = 500000.0
ATTN_SCALE = HEAD_DIM ** -0.5
RWKV_HEADS = 8
RWKV_HEAD_DIM = 64
RWKV_WIDTH = RWKV_HEADS * RWKV_HEAD_DIM
DECAY_LORA = 32
AAA_LORA = 32
GATE_LORA = 96
GN_EPS = 64e-5
RWKV_COLS = 3 * RWKV_WIDTH + DECAY_LORA + AAA_LORA + GATE_LORA
LORA_PAD = 256
RWKV_PAD = 3 * RWKV_WIDTH + LORA_PAD
OFF_K = ATTN_WIDTH
OFF_V = OFF_K + KV_WIDTH
OFF_B = OFF_V + KV_WIDTH
OFF_GATE = OFF_B + RWKV_COLS
N_KEYS = 128
N_EXPERTS = N_KEYS * N_KEYS
PEER_HEADS = 8
PEER_TOPK = 16
D_KEY_HALF = 128
N_SLOTS = PEER_HEADS * PEER_TOPK
PAST_LEN = 16384
RWKV_CHUNK = 64
LANES = 128
SUBLANES = 8
SOLVE_COLS = 16
TOKEN_UNROLL = 32
DECODE_BATCH_BLOCK = 8
PROMPT_BATCH_BLOCK = 4
DENSE_TOKEN_TILE = 512
EXPERT_TOKEN_TILE = 512
EXPERT_TILE = 1024
EXPERT_ROW_BLOCK = 512
W_PITCH_PAD = 8
NEG_INF = float("-inf")

NN = ((1,), (0,))
NT = ((1,), (1,))
TN = ((0,), (0,))


def _dot(a, b, dims=NN):
    return lax.dot_general(a, b, (dims, ((), ())), preferred_element_type=F32)


def _split2(x):
    hi = x.astype(BF16)
    return hi, (x - hi.astype(F32)).astype(BF16)


def _doth(a, b, dims=NN):
    a_hi, a_lo = _split2(a)
    b_hi, b_lo = _split2(b)
    return _dot(a_hi, b_hi, dims) + _dot(a_hi, b_lo, dims) + _dot(a_lo, b_hi, dims)


def _dotb(a, b, dims=NN):
    return _dot(a.astype(BF16), b.astype(BF16), dims)


def _split3(x):
    hi = x.astype(BF16)
    r1 = x - hi.astype(F32)
    mid = r1.astype(BF16)
    lo = (r1 - mid.astype(F32)).astype(BF16)
    return hi, mid, lo


def _dot_exact_rhs(x, e):
    hi, mid, lo = _split3(x)
    return _dot(hi, e) + _dot(mid, e) + _dot(lo, e)


def _dot_exact_lhs(e, x):
    hi, mid, lo = _split3(x)
    return _dot(e, hi) + _dot(e, mid) + _dot(e, lo)


def _params(sem, vmem_mb):
    return pltpu.CompilerParams(dimension_semantics=sem, vmem_limit_bytes=vmem_mb * 1024 * 1024)


def _full(a):
    nd = a.ndim
    return pl.BlockSpec(a.shape, lambda *_: (0,) * nd)


SEG_QKV = ATTN_WIDTH + 2 * KV_WIDTH
SEG_P = SEG_QKV + RWKV_PAD
SEG_END = SEG_P + 2 * D_MODEL


def _inproj_body(x_ref, g_ref, w_ref, b_ref, rope_ref, q_ref, k_ref, v_ref, p_ref, ga_ref, gb_ref):
    x = x_ref[...]
    h = (x * lax.rsqrt(jnp.mean(x * x, axis=-1, keepdims=True) + NORM_EPS) * g_ref[...]).astype(BF16)

    def seg(lo, hi):
        return _dot(h, w_ref[:, lo:hi]) + b_ref[:, lo:hi]

    cos = rope_ref[:, 0:LANES]
    sin_up = rope_ref[:, LANES:2 * LANES]
    sin_dn = rope_ref[:, 2 * LANES:3 * LANES]

    def rope(z):
        return z * cos + pltpu.roll(z, ROT_DIM // 2, 1) * sin_up + pltpu.roll(z, LANES - ROT_DIM // 2, 1) * sin_dn

    for c in range(ATTN_WIDTH // LANES):
        q_ref[:, c * LANES:(c + 1) * LANES] = rope(seg(c * LANES, (c + 1) * LANES))
    k_ref[...] = rope(seg(OFF_K, OFF_V))
    v_ref[...] = seg(OFF_V, SEG_QKV)
    p_ref[...] = seg(SEG_QKV, SEG_P)
    ga_ref[...] = jax.nn.sigmoid(seg(SEG_P, SEG_P + D_MODEL))
    gb_ref[...] = jax.nn.sigmoid(seg(SEG_P + D_MODEL, SEG_END))


def _in_proj(x2, g1, w_cat, b_cat, rope_tab, tm):
    m = x2.shape[0]
    nrep = rope_tab.shape[0] // tm
    widths = (ATTN_WIDTH, KV_WIDTH, KV_WIDTH, RWKV_PAD, D_MODEL, D_MODEL)
    return pl.pallas_call(
        _inproj_body,
        grid=(m // tm,),
        in_specs=[
            pl.BlockSpec((tm, D_MODEL), lambda i: (i, 0)),
            _full(g1), _full(w_cat), _full(b_cat),
            pl.BlockSpec((tm, 3 * LANES), lambda i: (i % nrep, 0)),
        ],
        out_specs=[pl.BlockSpec((tm, w), lambda i: (i, 0)) for w in widths],
        out_shape=[jax.ShapeDtypeStruct((m, w), F32) for w in widths],
        compiler_params=_params(("parallel",), 56),
        name="in_proj",
    )(x2, g1, w_cat, b_cat, rope_tab)


def _rope_table(pos, rows):
    t = pos.shape[0]
    half = ROT_DIM // 2
    inv_freq = ROPE_THETA ** (-jnp.arange(half, dtype=F32) / half)
    ang = pos.astype(F32)[:, None] * inv_freq[None, :]
    cos, sin = jnp.cos(ang), jnp.sin(ang)
    rest = HEAD_DIM - ROT_DIM
    z8 = jnp.zeros((t, half), F32)
    cos64 = jnp.concatenate([cos, cos, jnp.ones((t, rest), F32)], axis=1)
    up64 = jnp.concatenate([z8, sin, jnp.zeros((t, rest), F32)], axis=1)
    dn64 = jnp.concatenate([-sin, z8, jnp.zeros((t, rest), F32)], axis=1)
    tab = jnp.concatenate([jnp.tile(a, (1, LANES // HEAD_DIM)) for a in (cos64, up64, dn64)], axis=1)
    if rows > t:
        tab = jnp.tile(tab, (rows // t, 1))
    return tab


def _softmax_av(s, sink, vv):
    m = jnp.maximum(jnp.max(s, axis=-1, keepdims=True), sink)
    p = jnp.exp(s - m)
    den = jnp.sum(p, axis=-1, keepdims=True) + jnp.exp(sink - m)
    return _dot(p.astype(BF16), vv) / den


def _attn_prompt_body(sink_ref, q_ref, kc_ref, kp_ref, vc_ref, vp_ref, o_ref):
    n = pl.program_id(1)
    blk = q_ref.shape[0]
    i = lax.broadcasted_iota(I32, (blk, 2 * blk), 0)
    j = lax.broadcasted_iota(I32, (blk, 2 * blk), 1)
    diff = blk + i - j
    j_lo = jnp.where(n > 0, 0, blk)
    mask = (diff >= 0) & (diff < WINDOW) & (j >= j_lo)
    outs = []
    for g in range(N_KV_HEADS):
        sl = slice(g * HEAD_DIM, (g + 1) * HEAD_DIM)
        kk = jnp.concatenate([kp_ref[:, sl], kc_ref[:, sl]], axis=0).astype(BF16)
        vv = jnp.concatenate([vp_ref[:, sl], vc_ref[:, sl]], axis=0).astype(BF16)
        for hq in range(Q_PER_KV):
            h = g * Q_PER_KV + hq
            qh = q_ref[:, h * HEAD_DIM:(h + 1) * HEAD_DIM].astype(BF16)
            s = jnp.where(mask, _dot(qh, kk, NT) * ATTN_SCALE, NEG_INF)
            outs.append(_softmax_av(s, sink_ref[h], vv))
    o_ref[...] = jnp.concatenate(outs, axis=1).astype(o_ref.dtype)


def _attn_prompt(sinks, q3, k3, v3):
    b, t, _ = q3.shape
    blk = WINDOW
    cur = lambda w: pl.BlockSpec((None, blk, w), lambda bi, n: (bi, n, 0))
    prev = lambda w: pl.BlockSpec((None, blk, w), lambda bi, n: (bi, jnp.maximum(n - 1, 0), 0))
    return pl.pallas_call(
        _attn_prompt_body,
        grid=(b, t // blk),
        in_specs=[pl.BlockSpec(memory_space=pltpu.SMEM), cur(ATTN_WIDTH), cur(KV_WIDTH), prev(KV_WIDTH),
                  cur(KV_WIDTH), prev(KV_WIDTH)],
        out_specs=cur(ATTN_WIDTH),
        out_shape=jax.ShapeDtypeStruct((b, t, ATTN_WIDTH), BF16),
        compiler_params=_params(("parallel", "arbitrary"), 32),
        name="attn_prompt",
    )(sinks, q3, k3, k3, v3, v3)


def _attn_sample_body(sink_ref, *refs, t):
    for bi in range(refs[0].shape[0]):
        _attn_sample_one(sink_ref, *[r.at[bi] for r in refs], t=t)


def _attn_sample_one(sink_ref, q_ref, kn_ref, vn_ref, kc_ref, vc_ref, o_ref, *, t):
    rows = Q_PER_KV * t
    n_past = kc_ref.shape[0]
    ri = lax.broadcasted_iota(I32, (rows, 1), 0)
    qi = ri % t
    jc = lax.broadcasted_iota(I32, (rows, n_past), 1)
    mask_c = (n_past + qi - jc) < WINDOW
    for g in range(N_KV_HEADS):
        sl = slice(g * HEAD_DIM, (g + 1) * HEAD_DIM)
        sink = jnp.zeros((rows, 1), F32)
        for hq in range(Q_PER_KV):
            sink = jnp.where(ri // t == hq, sink_ref[g * Q_PER_KV + hq], sink)
        qg = q_ref[g]
        qb = qg.astype(BF16)
        kn = kn_ref[:, sl]
        vn = vn_ref[:, sl]
        s_c = jnp.where(mask_c, _dot(qb, kc_ref[:, sl].astype(BF16), NT) * ATTN_SCALE, NEG_INF)
        s_n = []
        for jn in range(t):
            sj = jnp.sum(qg * kn[jn:jn + 1, :], axis=-1, keepdims=True) * ATTN_SCALE
            s_n.append(jnp.where(qi >= jn, sj, NEG_INF))
        m = jnp.maximum(jnp.max(s_c, axis=-1, keepdims=True), sink)
        for sj in s_n:
            m = jnp.maximum(m, sj)
        p_c = jnp.exp(s_c - m)
        den = jnp.sum(p_c, axis=-1, keepdims=True) + jnp.exp(sink - m)
        o = _dot(p_c.astype(BF16), vc_ref[:, sl].astype(BF16))
        for jn in range(t):
            p_n = jnp.exp(s_n[jn] - m)
            den = den + p_n
            o = o + p_n * vn[jn:jn + 1, :]
        o_ref[g] = (o / den).astype(o_ref.dtype)


def _attn_sample(sinks, qg, kn, vn, cache_k, cache_v, t):
    b = qg.shape[0]
    rows = Q_PER_KV * t
    n_past = cache_k.shape[1]
    bb = DECODE_BATCH_BLOCK if b % DECODE_BATCH_BLOCK == 0 else 1
    qspec = pl.BlockSpec((bb, N_KV_HEADS, rows, HEAD_DIM), lambda bi: (bi, 0, 0, 0))
    nspec = pl.BlockSpec((bb, t, KV_WIDTH), lambda bi: (bi, 0, 0))
    cspec = pl.BlockSpec((bb, n_past, KV_WIDTH), lambda bi: (bi, 0, 0))
    return pl.pallas_call(
        functools.partial(_attn_sample_body, t=t),
        grid=(b // bb,),
        in_specs=[pl.BlockSpec(memory_space=pltpu.SMEM), qspec, nspec, nspec, cspec, cspec],
        out_specs=qspec,
        out_shape=jax.ShapeDtypeStruct((b, N_KV_HEADS, rows, HEAD_DIM), BF16),
        compiler_params=_params(("parallel",), 32),
        name="attn_sample",
    )(sinks, qg, kn, vn, cache_k, cache_v)


N_PREP_CONSTS = 9


def _rwkv_prep_body(*refs, t_real):
    seq, consts, outs = refs[:3], refs[3:3 + N_PREP_CONSTS], refs[3 + N_PREP_CONSTS:]
    for bi in range(seq[0].shape[0]):
        _rwkv_prep_one(*[r.at[bi] for r in seq], *consts, *[r.at[bi] for r in outs], t_real=t_real)


def _rwkv_prep_one(p_ref, pprev_ref, shift_ref, mu_ref, w0_ref, a0_ref, kk_ref, ka_ref, w2_ref, a2_ref, g2_ref,
                   eblk_ref, akb_ref, akk_ref, qrk_ref, qrb_ref, kkt_ref, rt_ref, kdec_ref, bdec_ref, kmod_ref,
                   r_ref, v_ref, g_ref, gamc_ref, *, t_real):
    c = pl.program_id(1)
    ch = p_ref.shape[0]
    pc = p_ref[...]
    row = lax.broadcasted_iota(I32, (ch, 1), 0)
    first = jnp.where(c == 0, shift_ref[...], pprev_ref[SUBLANES - 1:SUBLANES, :])
    prev = jnp.where(row == 0, first, pltpu.roll(pc, 1, 0))
    xm = pc + mu_ref[...] * (prev - pc)
    valid = (c * ch + row) < t_real
    w3 = 3 * RWKV_WIDTH
    r = xm[:, 0:RWKV_WIDTH]
    k = xm[:, RWKV_WIDTH:2 * RWKV_WIDTH]
    v = xm[:, 2 * RWKV_WIDTH:w3]
    lo = xm[:, w3:w3 + LORA_PAD]
    lane = lax.broadcasted_iota(I32, lo.shape, 1)
    act = jnp.where(lane < DECAY_LORA, jnp.tanh(lo),
                    jnp.where(lane < DECAY_LORA + AAA_LORA, lo, jax.nn.sigmoid(lo)))
    z = w0_ref[...] + _doth(act, w2_ref[...])
    w_log = -(jnp.maximum(-z, 0.0) + jnp.log1p(jnp.exp(-jnp.abs(z)))) - 0.5
    lw = jnp.where(valid, -jnp.exp(w_log), 0.0)
    a = jax.nn.sigmoid(a0_ref[...] + _doth(act, a2_ref[...]))
    g = _dot(act.astype(BF16), g2_ref[...].astype(BF16))
    kk = k * kk_ref[...]
    ss = _dot_exact_rhs(kk * kk, eblk_ref[...])
    kk = jnp.where(valid, kk * lax.rsqrt(jnp.maximum(ss, 1e-24)), 0.0)
    kmod = jnp.where(valid, k * (1.0 + (a - 1.0) * ka_ref[...]), 0.0)
    bb = kk * a
    ti = lax.broadcasted_iota(I32, (ch, ch), 0)
    tj = lax.broadcasted_iota(I32, (ch, ch), 1)
    tri = jnp.where(ti >= tj, 1.0, 0.0).astype(BF16)
    cum = _dot_exact_lhs(tri, lw)
    cum_c = cum[ch - 1:ch, :]
    e_neg = jnp.exp(-cum)
    e_rem = jnp.exp(cum_c - cum)
    kkt = kk * jnp.exp(cum - lw)
    rt = r * jnp.exp(cum)
    khat = kmod * e_neg
    bhat = bb * e_neg
    strict = ti > tj
    incl = ti >= tj
    hs = range(RWKV_HEADS)
    sl = [slice(h * RWKV_HEAD_DIM, (h + 1) * RWKV_HEAD_DIM) for h in hs]
    split = lambda z: [z[:, s].astype(BF16) for s in sl]
    kkt_s, rt_s, khat_s, bhat_s = split(kkt), split(rt), split(khat), split(bhat)
    akb = [_dot(kkt_s[h], bhat_s[h], NT) for h in hs]
    akk = [_dot(kkt_s[h], khat_s[h], NT) for h in hs]
    qrk = [_dot(rt_s[h], khat_s[h], NT) for h in hs]
    qrb = [_dot(rt_s[h], bhat_s[h], NT) for h in hs]
    for h in hs:
        akb_ref[h] = jnp.where(strict, akb[h], 0.0)
        akk_ref[h] = jnp.where(strict, akk[h], 0.0)
        qrk_ref[h] = jnp.where(incl, qrk[h], 0.0)
        qrb_ref[h] = jnp.where(incl, qrb[h], 0.0)
    kkt_ref[...] = kkt
    rt_ref[...] = rt
    kdec_ref[...] = kmod * e_rem
    bdec_ref[...] = bb * e_rem
    kmod_ref[...] = kmod
    r_ref[...] = r
    v_ref[...] = v
    g_ref[...] = g
    gamc_ref[...] = jnp.exp(cum_c)


def _rwkv_prep(p3, shift0, lp, eblk, ch, t_real, bb):
    b, tp, _ = p3.shape
    nc = tp // ch
    per8 = ch // SUBLANES
    mats = jax.ShapeDtypeStruct((b, nc, RWKV_HEADS, ch, ch), F32)
    rows = jax.ShapeDtypeStruct((b, tp, RWKV_WIDTH), F32)
    mspec = pl.BlockSpec((bb, None, RWKV_HEADS, ch, ch), lambda bi, ci: (bi, ci, 0, 0, 0))
    rspec = pl.BlockSpec((bb, ch, RWKV_WIDTH), lambda bi, ci: (bi, ci, 0))
    consts = (lp["mu"], lp["w0"], lp["a0"], lp["k_k"], lp["k_a"], lp["w2p"], lp["a2p"], lp["g2p"], eblk)
    assert len(consts) == N_PREP_CONSTS
    return pl.pallas_call(
        functools.partial(_rwkv_prep_body, t_real=t_real),
        grid=(b // bb, nc),
        in_specs=[
            pl.BlockSpec((bb, ch, RWKV_PAD), lambda bi, ci: (bi, ci, 0)),
            pl.BlockSpec((bb, SUBLANES, RWKV_PAD), lambda bi, ci: (bi, jnp.maximum(ci * per8 - 1, 0), 0)),
            pl.BlockSpec((bb, 1, RWKV_PAD), lambda bi, ci: (bi, 0, 0)),
        ] + [_full(a) for a in consts],
        out_specs=[mspec] * 4 + [rspec] * 8 + [pl.BlockSpec((bb, None, 1, RWKV_WIDTH), lambda bi, ci: (bi, ci, 0, 0))],
        out_shape=[mats] * 4 + [rows] * 8 + [jax.ShapeDtypeStruct((b, nc, 1, RWKV_WIDTH), F32)],
        compiler_params=_params(("parallel", "parallel"), 48),
        name="rwkv_prep",
    )(p3, p3, shift0, *consts)


def _tri_solve_body(a_ref, t_ref):
    ch = a_ref.shape[0]
    cols = min(SOLVE_COLS, ch)
    t_ref[...] = jnp.zeros(t_ref.shape, F32)
    ci = lax.broadcasted_iota(I32, (cols, SUBLANES, LANES), 0)

    def row_body(s, carry):
        for kc in range(ch // cols):
            c0 = kc * cols

            @pl.when(s >= c0)
            def _():
                acc0 = jnp.where(ci == s - c0, 1.0, 0.0).astype(F32)

                def r_body(r, acc):
                    return acc - a_ref[s, r][None] * t_ref[r, pl.ds(c0, cols)]

                t_ref[s, pl.ds(c0, cols)] = lax.fori_loop(c0, s, r_body, acc0)
        return carry

    lax.fori_loop(0, ch, row_body, 0)


def _tri_solve(akb):
    b, nc, nh, ch, _ = akb.shape
    nsys = b * nc * nh
    slab = SUBLANES * LANES
    nslab = -(-nsys // slab)
    a2 = akb.reshape(nsys, ch * ch)
    a2 = jnp.pad(a2, ((0, nslab * slab - nsys), (0, 0)))
    a5 = a2.T.reshape(ch, ch, nslab, SUBLANES, LANES)
    spec = pl.BlockSpec((ch, ch, None, SUBLANES, LANES), lambda gi: (0, 0, gi, 0, 0),
                        pipeline_mode=pl.Buffered(1))
    t5 = pl.pallas_call(
        _tri_solve_body,
        grid=(nslab,),
        in_specs=[spec],
        out_specs=spec,
        out_shape=jax.ShapeDtypeStruct(a5.shape, F32),
        compiler_params=_params(("parallel",), 48),
        name="rwkv_tri_solve",
    )(a5)
    t2 = t5.reshape(ch * ch, nslab * slab).T[:nsys]
    return t2.reshape(b, nc, nh, ch, ch)


N_SCAN_SEQ = 14
N_SCAN_CONSTS = 4


def _rwkv_scan_body(*refs):
    seq, consts, rest = refs[:N_SCAN_SEQ], refs[N_SCAN_SEQ:N_SCAN_SEQ + N_SCAN_CONSTS], refs[N_SCAN_SEQ + N_SCAN_CONSTS:]
    for bi in range(seq[0].shape[0]):
        _rwkv_scan_one(*[r.at[bi] for r in seq], *consts, *[r.at[bi] for r in rest])


def _rwkv_scan_one(t_ref, akk_ref, qrk_ref, qrb_ref, kkt_ref, rt_ref, kdec_ref, bdec_ref, kmod_ref, r_ref, v_ref,
                   g_ref, gamc_ref, wkv0_ref, eblk_ref, rk_ref, lnw_ref, lnb_ref, rw_ref, wkv_ref, s_scr):
    c = pl.program_id(1)
    nd = RWKV_HEAD_DIM

    @pl.when(c == 0)
    def _():
        s_scr[...] = wkv0_ref[...]

    hs = range(RWKV_HEADS)
    sl = [slice(h * nd, (h + 1) * nd) for h in hs]
    tmat = [t_ref[h] for h in hs]
    vh = [v_ref[:, sl[h]] for h in hs]
    bdec = [bdec_ref[:, sl[h]] for h in hs]
    qrb = [qrb_ref[h] for h in hs]
    w1 = [_dotb(tmat[h], kkt_ref[:, sl[h]]) for h in hs]
    av = [_dotb(akk_ref[h], vh[h]) for h in hs]
    qv = [_dotb(qrk_ref[h], vh[h]) for h in hs]
    vk = [_dotb(vh[h], kdec_ref[:, sl[h]], TN) for h in hs]
    w2 = [_dotb(tmat[h], av[h]) for h in hs]
    qw1 = [_dotb(qrb[h], w1[h]) for h in hs]
    bw1 = [_dotb(bdec[h], w1[h], TN) for h in hs]
    qw2 = [_dotb(qrb[h], w2[h]) for h in hs]
    wb = [_dotb(w2[h], bdec[h], TN) for h in hs]
    s0 = [s_scr[h] for h in hs]
    ys = [_dotb(rt_ref[:, sl[h]] - qw1[h], s0[h], NT) + (qv[h] - qw2[h]) for h in hs]
    for h in hs:
        s_scr[h] = s0[h] * gamc_ref[:, sl[h]] - _dotb(s0[h], bw1[h], NT) + (vk[h] - wb[h])
    y = jnp.concatenate(ys, axis=1)
    eblk = eblk_ref[...]
    inv_n = 1.0 / nd
    mean = _dot_exact_rhs(y, eblk) * inv_n
    yc = y - mean
    var = _dot_exact_rhs(yc * yc, eblk) * inv_n
    yn = yc * lax.rsqrt(var + GN_EPS) * lnw_ref[...] + lnb_ref[...]
    v = v_ref[...]
    bonus = _dot_exact_rhs(r_ref[...] * kmod_ref[...] * rk_ref[...], eblk) * v
    rw_ref[...] = ((yn + bonus) * g_ref[...]).astype(rw_ref.dtype)

    @pl.when(c == pl.num_programs(1) - 1)
    def _():
        wkv_ref[...] = s_scr[...]


def _rwkv_scan(tmat, prep, wkv0, lp, eblk, bb):
    akk, qrk, qrb, kkt, rt, kdec, bdec, kmod, r, v, g, gamc = prep
    b, nc, nh, ch, _ = tmat.shape
    tp = kkt.shape[1]
    nd = RWKV_HEAD_DIM
    mspec = pl.BlockSpec((bb, None, nh, ch, ch), lambda bi, ci: (bi, ci, 0, 0, 0))
    rspec = pl.BlockSpec((bb, ch, RWKV_WIDTH), lambda bi, ci: (bi, ci, 0))
    sspec = pl.BlockSpec((bb, nh, nd, nd), lambda bi, ci: (bi, 0, 0, 0))
    consts = (eblk, lp["r_k"], lp["ln_w"], lp["ln_b"])
    assert len(consts) == N_SCAN_CONSTS
    return pl.pallas_call(
        _rwkv_scan_body,
        grid=(b // bb, nc),
        in_specs=[mspec] * 4 + [rspec] * 8
        + [pl.BlockSpec((bb, None, 1, RWKV_WIDTH), lambda bi, ci: (bi, ci, 0, 0)), sspec]
        + [_full(a) for a in consts],
        out_specs=[rspec, sspec],
        out_shape=[jax.ShapeDtypeStruct((b, tp, RWKV_WIDTH), BF16), jax.ShapeDtypeStruct((b, nh, nd, nd), F32)],
        scratch_shapes=[pltpu.VMEM((bb, nh, nd, nd), F32)],
        compiler_params=_params(("parallel", "arbitrary"), 48),
        name="rwkv_scan",
    )(tmat, akk, qrk, qrb, kkt, rt, kdec, bdec, kmod, r, v, g, gamc, wkv0, *consts)


def _merge_body(x_ref, att_ref, rw_ref, ga_ref, gb_ref, wa_ref, wb_ref, wo_ref, g2_ref, wq_ref,
                x1_ref, h2_ref, qp_ref):
    merged = ga_ref[...] * _dot(att_ref[...], wa_ref[...]) + gb_ref[...] * _dot(rw_ref[...], wb_ref[...])
    x1 = x_ref[...] + _dot(merged.astype(BF16), wo_ref[...])
    x1_ref[...] = x1
    h2 = (x1 * lax.rsqrt(jnp.mean(x1 * x1, axis=-1, keepdims=True) + NORM_EPS) * g2_ref[...]).astype(BF16)
    h2_ref[...] = h2
    qp_ref[...] = _dot(h2, wq_ref[...]).astype(qp_ref.dtype)


def _merge(x2, att, rw, ga, gb, lp, tm):
    m = x2.shape[0]
    kq = lp["w_query"].shape[1]
    row = lambda w: pl.BlockSpec((tm, w), lambda i: (i, 0))
    consts = (lp["w_up_a"], lp["w_up_b"], lp["w_o"], lp["norm2_g"], lp["w_query"])
    return pl.pallas_call(
        _merge_body,
        grid=(m // tm,),
        in_specs=[row(D_MODEL), row(ATTN_WIDTH), row(RWKV_WIDTH), row(D_MODEL), row(D_MODEL)]
        + [_full(a) for a in consts],
        out_specs=[row(D_MODEL), row(D_MODEL), row(kq)],
        out_shape=[jax.ShapeDtypeStruct((m, D_MODEL), F32), jax.ShapeDtypeStruct((m, D_MODEL), BF16),
                   jax.ShapeDtypeStruct((m, kq), BF16)],
        compiler_params=_params(("parallel",), 56),
        name="merge",
    )(x2, att, rw, ga, gb, *consts)


def _argmax_rows(src_ref):
    nslab = src_ref.shape[0] // SUBLANES
    row0 = lax.broadcasted_iota(I32, (SUBLANES, src_ref.shape[1]), 0).astype(F32)
    vals = [src_ref[i * SUBLANES:(i + 1) * SUBLANES, :] for i in range(nslab)]
    rows = [row0 + float(i * SUBLANES) for i in range(nslab)]
    while len(vals) > 1:
        nv, nr = [], []
        for a in range(0, len(vals) - 1, 2):
            take = vals[a + 1] > vals[a]
            nv.append(jnp.maximum(vals[a], vals[a + 1]))
            nr.append(jnp.where(take, rows[a + 1], rows[a]))
        if len(vals) % 2:
            nv.append(vals[-1])
            nr.append(rows[-1])
        vals, rows = nv, nr
    v, i = vals[0], rows[0]
    for shift in (4, 2, 1):
        v2 = pltpu.roll(v, shift, 0)
        i2 = pltpu.roll(i, shift, 0)
        take = (v2 > v) | ((v2 == v) & (i2 < i))
        v = jnp.where(take, v2, v)
        i = jnp.where(take, i2, i)
    return jnp.max(v, axis=0, keepdims=True), jnp.max(i, axis=0, keepdims=True)


def _extract_top(jobs, n_out):
    def body(r, carry):
        for src_ref, val_ref, idx_ref in jobs:
            rowf = lax.broadcasted_iota(I32, src_ref.shape, 0).astype(F32)
            m, idx = _argmax_rows(src_ref)
            val_ref[pl.ds(r, 1), :] = m
            idx_ref[pl.ds(r, 1), :] = idx
            src_ref[...] = jnp.where(rowf == idx, NEG_INF, src_ref[...])
        return carry

    lax.fori_loop(0, n_out, body, 0)


def _cand_groups(k):
    groups = []
    a = 0
    while a < k and k // (a + 1) > 1:
        nb = k // (a + 1)
        groups.append((a, -(-nb // SUBLANES) * SUBLANES, nb))
        a += 1
    return groups, a


def _n_cand_rows(k):
    groups, a_tail = _cand_groups(k)
    return sum(rows for _, rows, _ in groups) + k - a_tail


def _peer_topk_head(qp_ref, sk_ref, h, i1_scr, i2_scr, gate_scr, s_scr, cand_scr, sv_scr, si_scr, bs_scr, bp_scr):
    k = PEER_TOPK
    for c in range(2):
        col = (2 * h + c) * D_KEY_HALF
        s_scr[c] = _dot(sk_ref[h, c], qp_ref[:, col:col + D_KEY_HALF], NT)
    _extract_top([(s_scr.at[c], sv_scr.at[c], si_scr.at[c]) for c in range(2)], k)
    sv0 = sv_scr[0]
    sv1 = sv_scr[1]
    groups, a_tail = _cand_groups(k)
    off = 0
    for a, rows, nb in groups:
        blk = sv0[a:a + 1, :] + sv1[0:rows, :]
        if nb < rows:
            blk = jnp.where(lax.broadcasted_iota(I32, blk.shape, 0) < nb, blk, NEG_INF)
        cand_scr[off:off + rows, :] = blk
        off += rows
    cand_scr[off:off + k - a_tail, :] = sv0[a_tail:k, :] + sv1[0:1, :]
    _extract_top([(cand_scr, bs_scr, bp_scr)], k)
    best_s = bs_scr[...]
    pos = bp_scr[...]
    a_sel = jnp.zeros_like(pos)
    b_sel = pos
    off = 0
    for a, rows, nb in groups:
        if a > 0:
            a_sel = jnp.where(pos >= off, float(a), a_sel)
            b_sel = jnp.where(pos >= off, pos - off, b_sel)
        off += rows
    a_sel = jnp.where(pos >= off, a_tail + pos - off, a_sel)
    b_sel = jnp.where(pos >= off, 0.0, b_sel)
    si0 = si_scr[0]
    si1 = si_scr[1]
    i1 = jnp.zeros_like(pos)
    i2 = jnp.zeros_like(pos)
    for a in range(k):
        i1 = jnp.where(a_sel == a, si0[a:a + 1, :], i1)
        i2 = jnp.where(b_sel == a, si1[a:a + 1, :], i2)
    e = jnp.exp(best_s - jnp.max(best_s, axis=0, keepdims=True))
    i1_scr[h * k:(h + 1) * k, :] = i1
    i2_scr[h * k:(h + 1) * k, :] = i2
    gate_scr[h * k:(h + 1) * k, :] = e / jnp.sum(e, axis=0, keepdims=True)


def _peer_topk_body(qp_ref, sk_ref, i1_ref, i2_ref, gate_ref, i1_scr, i2_scr, gate_scr, *work):
    for h in range(PEER_HEADS):
        _peer_topk_head(qp_ref, sk_ref, h, i1_scr, i2_scr, gate_scr, *work)
    i1_ref[...] = i1_scr[...].T.astype(I32)
    i2_ref[...] = i2_scr[...].T.astype(I32)
    gate_ref[...] = gate_scr[...].T


def _peer_topk(qp, sub_keys, tm):
    m = qp.shape[0]
    k = PEER_TOPK
    ospec = pl.BlockSpec((tm, N_SLOTS), lambda i: (i, 0))
    slots = pltpu.VMEM((N_SLOTS, tm), F32)
    return pl.pallas_call(
        _peer_topk_body,
        grid=(m // tm,),
        in_specs=[pl.BlockSpec((tm, qp.shape[1]), lambda i: (i, 0)), _full(sub_keys)],
        out_specs=[ospec, ospec, ospec],
        out_shape=[jax.ShapeDtypeStruct((m, N_SLOTS), I32), jax.ShapeDtypeStruct((m, N_SLOTS), I32),
                   jax.ShapeDtypeStruct((m, N_SLOTS), F32)],
        scratch_shapes=[slots, slots, slots,
                        pltpu.VMEM((2, N_KEYS, tm), F32), pltpu.VMEM((_n_cand_rows(k), tm), F32),
                        pltpu.VMEM((2, k, tm), F32), pltpu.VMEM((2, k, tm), F32),
                        pltpu.VMEM((k, tm), F32), pltpu.VMEM((k, tm), F32)],
        compiler_params=_params(("parallel",), 32),
        name="peer_topk",
    )(qp, sub_keys)


def _peer_expert_body(x1_ref, h2_ref, i1_ref, i2_ref, gate_ref, ut_ref, v_ref, gf_ref, y_ref, w_scr):
    j = pl.program_id(1)
    tm = h2_ref.shape[0]
    te = v_ref.shape[0]
    pitch = tm + W_PITCH_PAD

    @pl.when(j == 0)
    def _():
        y_ref[...] = x1_ref[...]
        sub = lax.broadcasted_iota(I32, (N_KEYS, N_SLOTS), 0)

        def tok(t, carry):
            hit1 = sub == i1_ref[pl.ds(t, 1), :]
            hit2 = sub == i2_ref[pl.ds(t, 1), :]
            a = jnp.where(hit1, gate_ref[pl.ds(t, 1), :], 0.0).astype(BF16)
            bm = jnp.where(hit2, 1.0, 0.0).astype(BF16)
            w_scr[pl.ds(t, N_KEYS, stride=pitch), :] = _dot(a, bm, NT)
            return carry

        lax.fori_loop(0, tm, tok, 0, unroll=TOKEN_UNROLL)

    rb = min(tm, EXPERT_ROW_BLOCK)
    pair = 2 * N_KEYS
    for r in range(tm // rb):
        x = h2_ref[r * rb:(r + 1) * rb, :]
        parts = []
        for q in range(te // pair):
            h = _dot(x, ut_ref[:, q * pair:(q + 1) * pair])
            i1 = j * (te // N_KEYS) + 2 * q
            w = jnp.concatenate([w_scr[pl.ds(pl.multiple_of(i1 * pitch + r * rb, SUBLANES), rb), :],
                                 w_scr[pl.ds(pl.multiple_of((i1 + 1) * pitch + r * rb, SUBLANES), rb), :]], axis=1)
            act = 0.5 * h * (1.0 + lax.erf(h * (2.0 ** -0.5)))
            parts.append((w * act).astype(BF16))
        y_ref[r * rb:(r + 1) * rb, :] += _dot(jnp.concatenate(parts, axis=1), v_ref[...])

    @pl.when(j == pl.num_programs(1) - 1)
    def _():
        x2 = y_ref[...]
        y_ref[...] = x2 * lax.rsqrt(jnp.mean(x2 * x2, axis=-1, keepdims=True) + NORM_EPS) * gf_ref[...]


def _peer_experts(x1, h2, i1, i2, gate, eu, ev, gf, tm, te):
    m = x1.shape[0]
    row = lambda w: pl.BlockSpec((tm, w), lambda i, j: (i, 0))
    tab = pl.BlockSpec((te, D_MODEL), lambda i, j: (j, 0))
    tab_t = pl.BlockSpec((D_MODEL, te), lambda i, j: (0, j))
    x1_spec = pl.BlockSpec((tm, D_MODEL), lambda i, j: (i, 0), pipeline_mode=pl.Buffered(1))
    return pl.pallas_call(
        _peer_expert_body,
        grid=(m // tm, N_EXPERTS // te),
        in_specs=[x1_spec, row(D_MODEL), row(N_SLOTS), row(N_SLOTS), row(N_SLOTS), tab_t, tab,
                  pl.BlockSpec((1, D_MODEL), lambda i, j: (0, 0))],
        out_specs=row(D_MODEL),
        out_shape=jax.ShapeDtypeStruct((m, D_MODEL), F32),
        scratch_shapes=[pltpu.VMEM((N_KEYS * (tm + W_PITCH_PAD), N_KEYS), F32)],
        compiler_params=_params(("parallel", "arbitrary"), 60),
        name="peer_experts",
    )(x1, h2, i1, i2, gate, eu, ev, gf)


def _pad_cols(a, width):
    return jnp.pad(a, [(0, 0)] * (a.ndim - 1) + [(0, width - a.shape[-1])])


def _prep_layer_params(l, norm1_g, w_in, b_in, attn_sinks, shift_mu, w0, w2, a0, a2, g2, k_k, k_a, r_k, ln_x_w,
                       ln_x_b, w_up_a, w_up_b, w_o, norm2_g, w_query, sub_keys, expert_u, expert_v):
    wi, bi = w_in[l], b_in[l][None, :]
    cat = lambda z: jnp.concatenate([z[:, :OFF_B], _pad_cols(z[:, OFF_B:OFF_GATE], RWKV_PAD), z[:, OFF_GATE:]], axis=1)
    row = lambda z: z[l].reshape(1, -1)
    lora = lambda z, off: jnp.pad(z[l], ((off, LORA_PAD - off - z.shape[1]), (0, 0)))
    return dict(
        norm1_g=row(norm1_g), w_cat=cat(wi).astype(BF16), b_cat=cat(bi), sinks=attn_sinks[l],
        mu=_pad_cols(row(shift_mu), RWKV_PAD), w0=row(w0), a0=row(a0), k_k=row(k_k), k_a=row(k_a),
        w2p=lora(w2, 0), a2p=lora(a2, DECAY_LORA), g2p=lora(g2, DECAY_LORA + AAA_LORA),
        r_k=row(r_k), ln_w=row(ln_x_w), ln_b=row(ln_x_b),
        w_up_a=w_up_a[l].astype(BF16), w_up_b=w_up_b[l].astype(BF16), w_o=w_o[l].astype(BF16),
        norm2_g=row(norm2_g), w_query=w_query[l].astype(BF16), sub_keys=sub_keys[l].astype(BF16),
        expert_ut=expert_u[l].astype(BF16).T, expert_v=expert_v[l].astype(BF16),
    )


def _row_tile(m, pref):
    return pref if m % pref == 0 else m


def _hybrid_layer(x, pos, lp, eblk, gf, win_k, win_v, wkv0, shift0):
    b, t, _ = x.shape
    m = b * t
    x2 = x.reshape(m, D_MODEL)
    tm = _row_tile(m, 256)
    tmd = _row_tile(m, DENSE_TOKEN_TILE)
    rope_tab = _rope_table(pos, max(t, tmd))
    q, k, v, p, ga, gb = _in_proj(x2, lp["norm1_g"], lp["w_cat"], lp["b_cat"], rope_tab, tmd)

    k3 = k.reshape(b, t, KV_WIDTH)
    v3 = v.reshape(b, t, KV_WIDTH)
    if win_k is None:
        att = _attn_prompt(lp["sinks"], q.reshape(b, t, ATTN_WIDTH), k3, v3).reshape(m, ATTN_WIDTH)
        keep = min(WINDOW, t)
        new_k, new_v = k3[:, t - keep:], v3[:, t - keep:]
    else:
        n_past = win_k.shape[1]
        qg = q.reshape(b, t, N_KV_HEADS, Q_PER_KV, HEAD_DIM).transpose(0, 2, 3, 1, 4)
        qg = qg.reshape(b, N_KV_HEADS, Q_PER_KV * t, HEAD_DIM)
        ck = win_k.reshape(b, n_past, KV_WIDTH)
        cv = win_v.reshape(b, n_past, KV_WIDTH)
        og = _attn_sample(lp["sinks"], qg, k3, v3, ck, cv, t)
        att = og.reshape(b, N_KV_HEADS, Q_PER_KV, t, HEAD_DIM).transpose(0, 3, 1, 2, 4).reshape(m, ATTN_WIDTH)
        new_k = jnp.concatenate([ck, k3], axis=1)[:, t:]
        new_v = jnp.concatenate([cv, v3], axis=1)[:, t:]
    new_k = new_k.reshape(b, -1, N_KV_HEADS, HEAD_DIM)
    new_v = new_v.reshape(b, -1, N_KV_HEADS, HEAD_DIM)

    p3 = p.reshape(b, t, RWKV_PAD)
    if t % RWKV_CHUNK == 0:
        ch, tp = RWKV_CHUNK, t
    else:
        tp = -(-t // SUBLANES) * SUBLANES
        ch = tp
        p3 = jnp.pad(p3, ((0, 0), (0, tp - t), (0, 0)))
    shift_pad = _pad_cols(shift0, RWKV_PAD)[:, None, :]
    want = DECODE_BATCH_BLOCK if tp == ch else PROMPT_BATCH_BLOCK
    bb = want if b % want == 0 else 1
    akb, *prep = _rwkv_prep(p3, shift_pad, lp, eblk, ch, t, bb)
    tmat = _tri_solve(akb)
    rw, wkv_t = _rwkv_scan(tmat, prep, wkv0, lp, eblk, bb)
    rw = rw[:, :t].reshape(m, RWKV_WIDTH)
    shift_t = p3[:, t - 1, :RWKV_COLS]

    x1, h2, qp = _merge(x2, att, rw, ga, gb, lp, tmd)
    i1, i2, gate = _peer_topk(qp, lp["sub_keys"], tm)
    y = _peer_experts(x1, h2, i1, i2, gate, lp["expert_ut"], lp["expert_v"], gf,
                      _row_tile(m, EXPERT_TOKEN_TILE), EXPERT_TILE)
    return y.reshape(b, t, D_MODEL), new_k, new_v, wkv_t, shift_t


def kernel(x_prompt, x_sample, cache_win_k, cache_win_v, state_wkv, state_shift, norm1_g, w_in, b_in, attn_sinks,
           shift_mu, w0, w2, a0, a2, g2, k_k, k_a, r_k, ln_x_w, ln_x_b, w_up_a, w_up_b, w_o, norm2_g, w_query,
           sub_keys, expert_u, expert_v, final_norm_g):
    depth = w_in.shape[0]
    assert depth == 1, "the final RMSNorm is fused into the last layer's expert kernel"
    pos_p = jnp.arange(x_prompt.shape[1])
    pos_s = PAST_LEN + jnp.arange(x_sample.shape[1])
    li = lax.broadcasted_iota(I32, (RWKV_WIDTH, RWKV_WIDTH), 0) // RWKV_HEAD_DIM
    lj = lax.broadcasted_iota(I32, (RWKV_WIDTH, RWKV_WIDTH), 1) // RWKV_HEAD_DIM
    eblk = (li == lj).astype(BF16)
    gf = final_norm_g.reshape(1, D_MODEL)
    lp = _prep_layer_params(0, norm1_g, w_in, b_in, attn_sinks, shift_mu, w0, w2, a0, a2, g2, k_k, k_a, r_k, ln_x_w,
                            ln_x_b, w_up_a, w_up_b, w_o, norm2_g, w_query, sub_keys, expert_u, expert_v)
    bp = x_prompt.shape[0]
    wkv0 = jnp.zeros((bp, RWKV_HEADS, RWKV_HEAD_DIM, RWKV_HEAD_DIM), F32)
    shift0 = jnp.zeros((bp, RWKV_COLS), x_prompt.dtype)
    yp, pk, pv, pw, psh = _hybrid_layer(x_prompt, pos_p, lp, eblk, gf, None, None, wkv0, shift0)
    ys, sk, sv, sw, ssh = _hybrid_layer(x_sample, pos_s, lp, eblk, gf, cache_win_k[0], cache_win_v[0],
                                        state_wkv[0], state_shift[0])
    st = lambda z: z[None]
    return (yp, ys, st(pk), st(pv), st(pw), st(psh), st(sk), st(sv), st(sw), st(ssh))
```

```python
import functools

import jax
import jax.numpy as jnp
from jax import lax
from jax.experimental import pallas as pl
from jax.experimental.pallas import tpu as pltpu

F32, BF16, I32 = jnp.float32, jnp.bfloat16, jnp.int32

D_MODEL = 1024
NORM_EPS = 1e-5
HEAD_DIM = 64
N_Q_HEADS = 8
N_KV_HEADS = 2
Q_PER_KV = N_Q_HEADS // N_KV_HEADS
ATTN_WIDTH = N_Q_HEADS * HEAD_DIM
KV_WIDTH = N_KV_HEADS * HEAD_DIM
WINDOW = 128
ROT_DIM = HEAD_DIM // 4
ROPE_THETA = 500000.0
ATTN_SCALE = HEAD_DIM ** -0.5
RWKV_HEADS = 8
RWKV_HEAD_DIM = 64
RWKV_WIDTH = RWKV_HEADS * RWKV_HEAD_DIM
DECAY_LORA = 32
AAA_LORA = 32
GATE_LORA = 96
GN_EPS = 64e-5
RWKV_COLS = 3 * RWKV_WIDTH + DECAY_LORA + AAA_LORA + GATE_LORA
LORA_PAD = 256
RWKV_PAD = 3 * RWKV_WIDTH + LORA_PAD
OFF_K = ATTN_WIDTH
OFF_V = OFF_K + KV_WIDTH
OFF_B = OFF_V + KV_WIDTH
OFF_GATE = OFF_B + RWKV_COLS
N_KEYS = 128
N_EXPERTS = N_KEYS * N_KEYS
PEER_HEADS = 8
PEER_TOPK = 16
D_KEY_HALF = 128
N_SLOTS = PEER_HEADS * PEER_TOPK
PAST_LEN = 16384
RWKV_CHUNK = 64
LANES = 128
SUBLANES = 8
SOLVE_COLS = 16
TOKEN_UNROLL = 32
DECODE_BATCH_BLOCK = 16
PROMPT_BATCH_BLOCK = 8
DENSE_TOKEN_TILE = 512
EXPERT_TOKEN_TILE = 512
EXPERT_TILE = 1024
EXPERT_ROW_BLOCK = 512
W_PITCH_PAD = 8
NEG_INF = float("-inf")

NN = ((1,), (0,))
NT = ((1,), (1,))
TN = ((0,), (0,))


def _dot(a, b, dims=NN):
    return lax.dot_general(a, b, (dims, ((), ())), preferred_element_type=F32)


def _split2(x):
    hi = x.astype(BF16)
    return hi, (x - hi.astype(F32)).astype(BF16)


def _doth(a, b, dims=NN):
    a_hi, a_lo = _split2(a)
    b_hi, b_lo = _split2(b)
    return _dot(a_hi, b_hi, dims) + _dot(a_hi, b_lo, dims) + _dot(a_lo, b_hi, dims)


def _dotb(a, b, dims=NN):
    return _dot(a.astype(BF16), b.astype(BF16), dims)


def _split3(x):
    hi = x.astype(BF16)
    r1 = x - hi.astype(F32)
    mid = r1.astype(BF16)
    lo = (r1 - mid.astype(F32)).astype(BF16)
    return hi, mid, lo


def _dot_exact_rhs(x, e):
    hi, mid, lo = _split3(x)
    return _dot(hi, e) + _dot(mid, e) + _dot(lo, e)


def _dot_exact_lhs(e, x):
    hi, mid, lo = _split3(x)
    return _dot(e, hi) + _dot(e, mid) + _dot(e, lo)


def _params(sem, vmem_mb):
    return pltpu.CompilerParams(dimension_semantics=sem, vmem_limit_bytes=vmem_mb * 1024 * 1024)


def _full(a):
    nd = a.ndim
    return pl.BlockSpec(a.shape, lambda *_: (0,) * nd)


SEG_QKV = ATTN_WIDTH + 2 * KV_WIDTH
SEG_P = SEG_QKV + RWKV_PAD
SEG_END = SEG_P + 2 * D_MODEL


def _inproj_body(x_ref, g_ref, w_ref, b_ref, rope_ref, q_ref, k_ref, v_ref, p_ref, ga_ref, gb_ref):
    x = x_ref[...]
    h = (x * lax.rsqrt(jnp.mean(x * x, axis=-1, keepdims=True) + NORM_EPS) * g_ref[...]).astype(BF16)

    def seg(lo, hi):
        return _dot(h, w_ref[:, lo:hi]) + b_ref[:, lo:hi]

    cos = rope_ref[:, 0:LANES]
    sin_up = rope_ref[:, LANES:2 * LANES]
    sin_dn = rope_ref[:, 2 * LANES:3 * LANES]

    def rope(z):
        return z * cos + pltpu.roll(z, ROT_DIM // 2, 1) * sin_up + pltpu.roll(z, LANES - ROT_DIM // 2, 1) * sin_dn

    for c in range(ATTN_WIDTH // LANES):
        q_ref[:, c * LANES:(c + 1) * LANES] = rope(seg(c * LANES, (c + 1) * LANES))
    k_ref[...] = rope(seg(OFF_K, OFF_V))
    v_ref[...] = seg(OFF_V, SEG_QKV)
    p_ref[...] = seg(SEG_QKV, SEG_P)
    ga_ref[...] = jax.nn.sigmoid(seg(SEG_P, SEG_P + D_MODEL))
    gb_ref[...] = jax.nn.sigmoid(seg(SEG_P + D_MODEL, SEG_END))


def _in_proj(x2, g1, w_cat, b_cat, rope_tab, tm):
    m = x2.shape[0]
    nrep = rope_tab.shape[0] // tm
    widths = (ATTN_WIDTH, KV_WIDTH, KV_WIDTH, RWKV_PAD, D_MODEL, D_MODEL)
    return pl.pallas_call(
        _inproj_body,
        grid=(m // tm,),
        in_specs=[
            pl.BlockSpec((tm, D_MODEL), lambda i: (i, 0)),
            _full(g1), _full(w_cat), _full(b_cat),
            pl.BlockSpec((tm, 3 * LANES), lambda i: (i % nrep, 0)),
        ],
        out_specs=[pl.BlockSpec((tm, w), lambda i: (i, 0)) for w in widths],
        out_shape=[jax.ShapeDtypeStruct((m, w), F32) for w in widths],
        compiler_params=_params(("parallel",), 56),
        name="in_proj",
    )(x2, g1, w_cat, b_cat, rope_tab)


def _rope_table(pos, rows):
    t = pos.shape[0]
    half = ROT_DIM // 2
    inv_freq = ROPE_THETA ** (-jnp.arange(half, dtype=F32) / half)
    ang = pos.astype(F32)[:, None] * inv_freq[None, :]
    cos, sin = jnp.cos(ang), jnp.sin(ang)
    rest = HEAD_DIM - ROT_DIM
    z8 = jnp.zeros((t, half), F32)
    cos64 = jnp.concatenate([cos, cos, jnp.ones((t, rest), F32)], axis=1)
    up64 = jnp.concatenate([z8, sin, jnp.zeros((t, rest), F32)], axis=1)
    dn64 = jnp.concatenate([-sin, z8, jnp.zeros((t, rest), F32)], axis=1)
    tab = jnp.concatenate([jnp.tile(a, (1, LANES // HEAD_DIM)) for a in (cos64, up64, dn64)], axis=1)
    if rows > t:
        tab = jnp.tile(tab, (rows // t, 1))
    return tab


def _softmax_av(s, sink, vv):
    m = jnp.maximum(jnp.max(s, axis=-1, keepdims=True), sink)
    p = jnp.exp(s - m)
    den = jnp.sum(p, axis=-1, keepdims=True) + jnp.exp(sink - m)
    return _dot(p.astype(BF16), vv) / den


def _attn_prompt_body(sink_ref, q_ref, kc_ref, kp_ref, vc_ref, vp_ref, o_ref):
    n = pl.program_id(1)
    blk = q_ref.shape[0]
    i = lax.broadcasted_iota(I32, (blk, 2 * blk), 0)
    j = lax.broadcasted_iota(I32, (blk, 2 * blk), 1)
    diff = blk + i - j
    j_lo = jnp.where(n > 0, 0, blk)
    mask = (diff >= 0) & (diff < WINDOW) & (j >= j_lo)
    outs = []
    for g in range(N_KV_HEADS):
        sl = slice(g * HEAD_DIM, (g + 1) * HEAD_DIM)
        kk = jnp.concatenate([kp_ref[:, sl], kc_ref[:, sl]], axis=0).astype(BF16)
        vv = jnp.concatenate([vp_ref[:, sl], vc_ref[:, sl]], axis=0).astype(BF16)
        for hq in range(Q_PER_KV):
            h = g * Q_PER_KV + hq
            qh = q_ref[:, h * HEAD_DIM:(h + 1) * HEAD_DIM].astype(BF16)
            s = jnp.where(mask, _dot(qh, kk, NT) * ATTN_SCALE, NEG_INF)
            outs.append(_softmax_av(s, sink_ref[h], vv))
    o_ref[...] = jnp.concatenate(outs, axis=1).astype(o_ref.dtype)


def _attn_prompt(sinks, q3, k3, v3):
    b, t, _ = q3.shape
    blk = WINDOW
    cur = lambda w: pl.BlockSpec((None, blk, w), lambda bi, n: (bi, n, 0))
    prev = lambda w: pl.BlockSpec((None, blk, w), lambda bi, n: (bi, jnp.maximum(n - 1, 0), 0))
    return pl.pallas_call(
        _attn_prompt_body,
        grid=(b, t // blk),
        in_specs=[pl.BlockSpec(memory_space=pltpu.SMEM), cur(ATTN_WIDTH), cur(KV_WIDTH), prev(KV_WIDTH),
                  cur(KV_WIDTH), prev(KV_WIDTH)],
        out_specs=cur(ATTN_WIDTH),
        out_shape=jax.ShapeDtypeStruct((b, t, ATTN_WIDTH), BF16),
        compiler_params=_params(("parallel", "arbitrary"), 32),
        name="attn_prompt",
    )(sinks, q3, k3, k3, v3, v3)


def _attn_sample_body(sink_ref, *refs, t):
    for bi in range(refs[0].shape[0]):
        _attn_sample_one(sink_ref, *[r.at[bi] for r in refs], t=t)


def _attn_sample_one(sink_ref, q_ref, kn_ref, vn_ref, kc_ref, vc_ref, o_ref, *, t):
    rows = Q_PER_KV * t
    n_past = kc_ref.shape[0]
    ri = lax.broadcasted_iota(I32, (rows, 1), 0)
    qi = ri % t
    jc = lax.broadcasted_iota(I32, (rows, n_past), 1)
    mask_c = (n_past + qi - jc) < WINDOW
    for g in range(N_KV_HEADS):
        sl = slice(g * HEAD_DIM, (g + 1) * HEAD_DIM)
        sink = jnp.zeros((rows, 1), F32)
        for hq in range(Q_PER_KV):
            sink = jnp.where(ri // t == hq, sink_ref[g * Q_PER_KV + hq], sink)
        qg = q_ref[g]
        qb = qg.astype(BF16)
        kn = kn_ref[:, sl]
        vn = vn_ref[:, sl]
        s_c = jnp.where(mask_c, _dot(qb, kc_ref[:, sl].astype(BF16), NT) * ATTN_SCALE, NEG_INF)
        s_n = []
        for jn in range(t):
            sj = jnp.sum(qg * kn[jn:jn + 1, :], axis=-1, keepdims=True) * ATTN_SCALE
            s_n.append(jnp.where(qi >= jn, sj, NEG_INF))
        m = jnp.maximum(jnp.max(s_c, axis=-1, keepdims=True), sink)
        for sj in s_n:
            m = jnp.maximum(m, sj)
        p_c = jnp.exp(s_c - m)
        den = jnp.sum(p_c, axis=-1, keepdims=True) + jnp.exp(sink - m)
        o = _dot(p_c.astype(BF16), vc_ref[:, sl].astype(BF16))
        for jn in range(t):
            p_n = jnp.exp(s_n[jn] - m)
            den = den + p_n
            o = o + p_n * vn[jn:jn + 1, :]
        o_ref[g] = (o / den).astype(o_ref.dtype)


def _attn_sample(sinks, qg, kn, vn, cache_k, cache_v, t):
    b = qg.shape[0]
    rows = Q_PER_KV * t
    n_past = cache_k.shape[1]
    bb = DECODE_BATCH_BLOCK if b % DECODE_BATCH_BLOCK == 0 else 1
    qspec = pl.BlockSpec((bb, N_KV_HEADS, rows, HEAD_DIM), lambda bi: (bi, 0, 0, 0))
    nspec = pl.BlockSpec((bb, t, KV_WIDTH), lambda bi: (bi, 0, 0))
    cspec = pl.BlockSpec((bb, n_past, KV_WIDTH), lambda bi: (bi, 0, 0))
    return pl.pallas_call(
        functools.partial(_attn_sample_body, t=t),
        grid=(b // bb,),
        in_specs=[pl.BlockSpec(memory_space=pltpu.SMEM), qspec, nspec, nspec, cspec, cspec],
        out_specs=qspec,
        out_shape=jax.ShapeDtypeStruct((b, N_KV_HEADS, rows, HEAD_DIM), BF16),
        compiler_params=_params(("parallel",), 32),
        name="attn_sample",
    )(sinks, qg, kn, vn, cache_k, cache_v)


N_PREP_CONSTS = 9


def _rwkv_prep_body(*refs, t_real):
    seq, consts, outs = refs[:3], refs[3:3 + N_PREP_CONSTS], refs[3 + N_PREP_CONSTS:]
    for bi in range(seq[0].shape[0]):
        _rwkv_prep_one(*[r.at[bi] for r in seq], *consts, *[r.at[bi] for r in outs], t_real=t_real)


def _rwkv_prep_one(p_ref, pprev_ref, shift_ref, mu_ref, w0_ref, a0_ref, kk_ref, ka_ref, w2_ref, a2_ref, g2_ref,
                   eblk_ref, akb_ref, akk_ref, qrk_ref, qrb_ref, kkt_ref, rt_ref, kdec_ref, bdec_ref, kmod_ref,
                   r_ref, v_ref, g_ref, gamc_ref, *, t_real):
    c = pl.program_id(1)
    ch = p_ref.shape[0]
    pc = p_ref[...]
    row = lax.broadcasted_iota(I32, (ch, 1), 0)
    first = jnp.where(c == 0, shift_ref[...], pprev_ref[SUBLANES - 1:SUBLANES, :])
    prev = jnp.where(row == 0, first, pltpu.roll(pc, 1, 0))
    xm = pc + mu_ref[...] * (prev - pc)
    valid = (c * ch + row) < t_real
    w3 = 3 * RWKV_WIDTH
    r = xm[:, 0:RWKV_WIDTH]
    k = xm[:, RWKV_WIDTH:2 * RWKV_WIDTH]
    v = xm[:, 2 * RWKV_WIDTH:w3]
    lo = xm[:, w3:w3 + LORA_PAD]
    lane = lax.broadcasted_iota(I32, lo.shape, 1)
    act = jnp.where(lane < DECAY_LORA, jnp.tanh(lo),
                    jnp.where(lane < DECAY_LORA + AAA_LORA, lo, jax.nn.sigmoid(lo)))
    z = w0_ref[...] + _doth(act, w2_ref[...])
    w_log = -(jnp.maximum(-z, 0.0) + jnp.log1p(jnp.exp(-jnp.abs(z)))) - 0.5
    lw = jnp.where(valid, -jnp.exp(w_log), 0.0)
    a = jax.nn.sigmoid(a0_ref[...] + _doth(act, a2_ref[...]))
    g = _dot(act.astype(BF16), g2_ref[...].astype(BF16))
    kk = k * kk_ref[...]
    ss = _dot_exact_rhs(kk * kk, eblk_ref[...])
    kk = jnp.where(valid, kk * lax.rsqrt(jnp.maximum(ss, 1e-24)), 0.0)
    kmod = jnp.where(valid, k * (1.0 + (a - 1.0) * ka_ref[...]), 0.0)
    bb = kk * a
    ti = lax.broadcasted_iota(I32, (ch, ch), 0)
    tj = lax.broadcasted_iota(I32, (ch, ch), 1)
    tri = jnp.where(ti >= tj, 1.0, 0.0).astype(BF16)
    cum = _dot_exact_lhs(tri, lw)
    cum_c = cum[ch - 1:ch, :]
    e_neg = jnp.exp(-cum)
    e_rem = jnp.exp(cum_c - cum)
    kkt = kk * jnp.exp(cum - lw)
    rt = r * jnp.exp(cum)
    khat = kmod * e_neg
    bhat = bb * e_neg
    strict = ti > tj
    incl = ti >= tj
    hs = range(RWKV_HEADS)
    sl = [slice(h * RWKV_HEAD_DIM, (h + 1) * RWKV_HEAD_DIM) for h in hs]
    split = lambda z: [z[:, s].astype(BF16) for s in sl]
    kkt_s, rt_s, khat_s, bhat_s = split(kkt), split(rt), split(khat), split(bhat)
    akb = [_dot(kkt_s[h], bhat_s[h], NT) for h in hs]
    akk = [_dot(kkt_s[h], khat_s[h], NT) for h in hs]
    qrk = [_dot(rt_s[h], khat_s[h], NT) for h in hs]
    qrb = [_dot(rt_s[h], bhat_s[h], NT) for h in hs]
    for h in hs:
        akb_ref[h] = jnp.where(strict, akb[h], 0.0)
        akk_ref[h] = jnp.where(strict, akk[h], 0.0)
        qrk_ref[h] = jnp.where(incl, qrk[h], 0.0)
        qrb_ref[h] = jnp.where(incl, qrb[h], 0.0)
    kkt_ref[...] = kkt
    rt_ref[...] = rt
    kdec_ref[...] = kmod * e_rem
    bdec_ref[...] = bb * e_rem
    kmod_ref[...] = kmod
    r_ref[...] = r
    v_ref[...] = v
    g_ref[...] = g
    gamc_ref[...] = jnp.exp(cum_c)


def _rwkv_prep(p3, shift0, lp, eblk, ch, t_real, bb):
    b, tp, _ = p3.shape
    nc = tp // ch
    per8 = ch // SUBLANES
    mats = jax.ShapeDtypeStruct((b, nc, RWKV_HEADS, ch, ch), F32)
    rows = jax.ShapeDtypeStruct((b, tp, RWKV_WIDTH), F32)
    mspec = pl.BlockSpec((bb, None, RWKV_HEADS, ch, ch), lambda bi, ci: (bi, ci, 0, 0, 0))
    rspec = pl.BlockSpec((bb, ch, RWKV_WIDTH), lambda bi, ci: (bi, ci, 0))
    consts = (lp["mu"], lp["w0"], lp["a0"], lp["k_k"], lp["k_a"], lp["w2p"], lp["a2p"], lp["g2p"], eblk)
    assert len(consts) == N_PREP_CONSTS
    return pl.pallas_call(
        functools.partial(_rwkv_prep_body, t_real=t_real),
        grid=(b // bb, nc),
        in_specs=[
            pl.BlockSpec((bb, ch, RWKV_PAD), lambda bi, ci: (bi, ci, 0)),
            pl.BlockSpec((bb, SUBLANES, RWKV_PAD), lambda bi, ci: (bi, jnp.maximum(ci * per8 - 1, 0), 0)),
            pl.BlockSpec((bb, 1, RWKV_PAD), lambda bi, ci: (bi, 0, 0)),
        ] + [_full(a) for a in consts],
        out_specs=[mspec] * 4 + [rspec] * 8 + [pl.BlockSpec((bb, None, 1, RWKV_WIDTH), lambda bi, ci: (bi, ci, 0, 0))],
        out_shape=[mats] * 4 + [rows] * 8 + [jax.ShapeDtypeStruct((b, nc, 1, RWKV_WIDTH), F32)],
        compiler_params=_params(("parallel", "parallel"), 48),
        name="rwkv_prep",
    )(p3, p3, shift0, *consts)


def _tri_solve_body(a_ref, t_ref, w_scr, *, ch):
    k = pl.program_id(1)
    kc_w = a_ref.shape[1]
    nk = (pl.num_programs(1) - 1) // 2
    slab = a_ref.shape[0]
    groups = slab // LANES

    @pl.when(k < nk)
    def _():
        xt = a_ref[...].T
        for g in range(groups):
            w_scr[pl.ds(k * (kc_w * groups) + g, kc_w, stride=groups), :] = xt[:, g * LANES:(g + 1) * LANES]

    @pl.when(k == nk)
    def _():
        cols = min(SOLVE_COLS, ch)
        cid = lax.broadcasted_iota(I32, (cols * groups, LANES), 0) // groups

        def row_body(s, carry):
            for kc in range(ch // cols):
                c0 = kc * cols

                @pl.when(s >= c0)
                def _():
                    acc0 = jnp.where(cid == s - c0, 1.0, 0.0).astype(F32)

                    def r_body(r, acc):
                        a_sr = w_scr[pl.ds(pl.multiple_of((s * ch + r) * groups, groups), groups), :]
                        t_r = w_scr[pl.ds(pl.multiple_of((r * ch + c0) * groups, groups), cols * groups), :]
                        return acc - jnp.tile(a_sr, (cols, 1)) * t_r

                    w_scr[pl.ds(pl.multiple_of((s * ch + c0) * groups, groups), cols * groups), :] = (
                        lax.fori_loop(c0, s, r_body, acc0))
            return carry

        lax.fori_loop(0, ch, row_body, 0)

    @pl.when(k > nk)
    def _():
        kk = k - nk - 1
        parts = [w_scr[pl.ds(kk * (kc_w * groups) + g, kc_w, stride=groups), :] for g in range(groups)]
        t_ref[...] = jnp.concatenate(parts, axis=1).T


def _tri_solve(akb):
    b, nc, nh, ch, _ = akb.shape
    nsys = b * nc * nh
    slab = SUBLANES * LANES
    nslab = -(-nsys // slab)
    ne = ch * ch
    ne_pad = -(-ne // LANES) * LANES
    a2 = jnp.pad(akb.reshape(nsys, ne), ((0, nslab * slab - nsys), (0, ne_pad - ne)))
    nk = ne_pad // LANES
    t2 = pl.pallas_call(
        functools.partial(_tri_solve_body, ch=ch),
        grid=(nslab, 2 * nk + 1),
        in_specs=[pl.BlockSpec((slab, LANES), lambda gi, k: (gi, jnp.minimum(k, nk - 1)))],
        out_specs=pl.BlockSpec((slab, LANES), lambda gi, k: (gi, jnp.clip(k - nk - 1, 0, nk - 1))),
        out_shape=jax.ShapeDtypeStruct(a2.shape, F32),
        scratch_shapes=[pltpu.VMEM((ne_pad * SUBLANES, LANES), F32)],
        compiler_params=_params(("parallel", "arbitrary"), 48),
        name="rwkv_tri_solve",
    )(a2)
    return t2[:nsys, :ne].reshape(b, nc, nh, ch, ch)


N_SCAN_SEQ = 14
N_SCAN_CONSTS = 4


def _rwkv_scan_body(*refs):
    seq, consts, rest = refs[:N_SCAN_SEQ], refs[N_SCAN_SEQ:N_SCAN_SEQ + N_SCAN_CONSTS], refs[N_SCAN_SEQ + N_SCAN_CONSTS:]
    for bi in range(seq[0].shape[0]):
        _rwkv_scan_one(*[r.at[bi] for r in seq], *consts, *[r.at[bi] for r in rest])


def _rwkv_scan_one(t_ref, akk_ref, qrk_ref, qrb_ref, kkt_ref, rt_ref, kdec_ref, bdec_ref, kmod_ref, r_ref, v_ref,
                   g_ref, gamc_ref, wkv0_ref, eblk_ref, rk_ref, lnw_ref, lnb_ref, rw_ref, wkv_ref, s_scr):
    c = pl.program_id(1)
    nd = RWKV_HEAD_DIM

    @pl.when(c == 0)
    def _():
        s_scr[...] = wkv0_ref[...]

    hs = range(RWKV_HEADS)
    sl = [slice(h * nd, (h + 1) * nd) for h in hs]
    tmat = [t_ref[h] for h in hs]
    vh = [v_ref[:, sl[h]] for h in hs]
    bdec = [bdec_ref[:, sl[h]] for h in hs]
    qrb = [qrb_ref[h] for h in hs]
    w1 = [_dotb(tmat[h], kkt_ref[:, sl[h]]) for h in hs]
    av = [_dotb(akk_ref[h], vh[h]) for h in hs]
    qv = [_dotb(qrk_ref[h], vh[h]) for h in hs]
    vk = [_dotb(vh[h], kdec_ref[:, sl[h]], TN) for h in hs]
    w2 = [_dotb(tmat[h], av[h]) for h in hs]
    qw1 = [_dotb(qrb[h], w1[h]) for h in hs]
    bw1 = [_dotb(bdec[h], w1[h], TN) for h in hs]
    qw2 = [_dotb(qrb[h], w2[h]) for h in hs]
    wb = [_dotb(w2[h], bdec[h], TN) for h in hs]
    s0 = [s_scr[h] for h in hs]
    ys = [_dotb(rt_ref[:, sl[h]] - qw1[h], s0[h], NT) + (qv[h] - qw2[h]) for h in hs]
    for h in hs:
        s_scr[h] = s0[h] * gamc_ref[:, sl[h]] - _dotb(s0[h], bw1[h], NT) + (vk[h] - wb[h])
    y = jnp.concatenate(ys, axis=1)
    eblk = eblk_ref[...]
    inv_n = 1.0 / nd
    mean = _dot_exact_rhs(y, eblk) * inv_n
    yc = y - mean
    var = _dot_exact_rhs(yc * yc, eblk) * inv_n
    yn = yc * lax.rsqrt(var + GN_EPS) * lnw_ref[...] + lnb_ref[...]
    v = v_ref[...]
    bonus = _dot_exact_rhs(r_ref[...] * kmod_ref[...] * rk_ref[...], eblk) * v
    rw_ref[...] = ((yn + bonus) * g_ref[...]).astype(rw_ref.dtype)

    @pl.when(c == pl.num_programs(1) - 1)
    def _():
        wkv_ref[...] = s_scr[...]


def _rwkv_scan(tmat, prep, wkv0, lp, eblk, bb):
    akk, qrk, qrb, kkt, rt, kdec, bdec, kmod, r, v, g, gamc = prep
    b, nc, nh, ch, _ = tmat.shape
    tp = kkt.shape[1]
    nd = RWKV_HEAD_DIM
    mspec = pl.BlockSpec((bb, None, nh, ch, ch), lambda bi, ci: (bi, ci, 0, 0, 0))
    rspec = pl.BlockSpec((bb, ch, RWKV_WIDTH), lambda bi, ci: (bi, ci, 0))
    sspec = pl.BlockSpec((bb, nh, nd, nd), lambda bi, ci: (bi, 0, 0, 0))
    consts = (eblk, lp["r_k"], lp["ln_w"], lp["ln_b"])
    assert len(consts) == N_SCAN_CONSTS
    return pl.pallas_call(
        _rwkv_scan_body,
        grid=(b // bb, nc),
        in_specs=[mspec] * 4 + [rspec] * 8
        + [pl.BlockSpec((bb, None, 1, RWKV_WIDTH), lambda bi, ci: (bi, ci, 0, 0)), sspec]
        + [_full(a) for a in consts],
        out_specs=[rspec, sspec],
        out_shape=[jax.ShapeDtypeStruct((b, tp, RWKV_WIDTH), BF16), jax.ShapeDtypeStruct((b, nh, nd, nd), F32)],
        scratch_shapes=[pltpu.VMEM((bb, nh, nd, nd), F32)],
        compiler_params=_params(("parallel", "arbitrary"), 48),
        name="rwkv_scan",
    )(tmat, akk, qrk, qrb, kkt, rt, kdec, bdec, kmod, r, v, g, gamc, wkv0, *consts)


def _merge_body(x_ref, att_ref, rw_ref, ga_ref, gb_ref, wa_ref, wb_ref, wo_ref, g2_ref, wq_ref,
                x1_ref, h2_ref, qp_ref):
    merged = ga_ref[...] * _dot(att_ref[...], wa_ref[...]) + gb_ref[...] * _dot(rw_ref[...], wb_ref[...])
    x1 = x_ref[...] + _dot(merged.astype(BF16), wo_ref[...])
    x1_ref[...] = x1
    h2 = (x1 * lax.rsqrt(jnp.mean(x1 * x1, axis=-1, keepdims=True) + NORM_EPS) * g2_ref[...]).astype(BF16)
    h2_ref[...] = h2
    qp_ref[...] = _dot(h2, wq_ref[...]).astype(qp_ref.dtype)


def _merge(x2, att, rw, ga, gb, lp, tm):
    m = x2.shape[0]
    kq = lp["w_query"].shape[1]
    row = lambda w: pl.BlockSpec((tm, w), lambda i: (i, 0))
    consts = (lp["w_up_a"], lp["w_up_b"], lp["w_o"], lp["norm2_g"], lp["w_query"])
    return pl.pallas_call(
        _merge_body,
        grid=(m // tm,),
        in_specs=[row(D_MODEL), row(ATTN_WIDTH), row(RWKV_WIDTH), row(D_MODEL), row(D_MODEL)]
        + [_full(a) for a in consts],
        out_specs=[row(D_MODEL), row(D_MODEL), row(kq)],
        out_shape=[jax.ShapeDtypeStruct((m, D_MODEL), F32), jax.ShapeDtypeStruct((m, D_MODEL), BF16),
                   jax.ShapeDtypeStruct((m, kq), BF16)],
        compiler_params=_params(("parallel",), 56),
        name="merge",
    )(x2, att, rw, ga, gb, *consts)


def _argmax_rows(src_ref):
    nslab = src_ref.shape[0] // SUBLANES
    row0 = lax.broadcasted_iota(I32, (SUBLANES, src_ref.shape[1]), 0).astype(F32)
    vals = [src_ref[i * SUBLANES:(i + 1) * SUBLANES, :] for i in range(nslab)]
    rows = [row0 + float(i * SUBLANES) for i in range(nslab)]
    while len(vals) > 1:
        nv, nr = [], []
        for a in range(0, len(vals) - 1, 2):
            take = vals[a + 1] > vals[a]
            nv.append(jnp.maximum(vals[a], vals[a + 1]))
            nr.append(jnp.where(take, rows[a + 1], rows[a]))
        if len(vals) % 2:
            nv.append(vals[-1])
            nr.append(rows[-1])
        vals, rows = nv, nr
    v, i = vals[0], rows[0]
    for shift in (4, 2, 1):
        v2 = pltpu.roll(v, shift, 0)
        i2 = pltpu.roll(i, shift, 0)
        take = (v2 > v) | ((v2 == v) & (i2 < i))
        v = jnp.where(take, v2, v)
        i = jnp.where(take, i2, i)
    return jnp.max(v, axis=0, keepdims=True), jnp.max(i, axis=0, keepdims=True)


def _extract_top(jobs, n_out):
    def body(r, carry):
        for src_ref, val_ref, idx_ref in jobs:
            rowf = lax.broadcasted_iota(I32, src_ref.shape, 0).astype(F32)
            m, idx = _argmax_rows(src_ref)
            val_ref[pl.ds(r, 1), :] = m
            idx_ref[pl.ds(r, 1), :] = idx
            src_ref[...] = jnp.where(rowf == idx, NEG_INF, src_ref[...])
        return carry

    lax.fori_loop(0, n_out, body, 0)


def _cand_groups(k):
    groups = []
    a = 0
    while a < k and k // (a + 1) > 1:
        nb = k // (a + 1)
        groups.append((a, -(-nb // SUBLANES) * SUBLANES, nb))
        a += 1
    return groups, a


def _n_cand_rows(k):
    groups, a_tail = _cand_groups(k)
    return sum(rows for _, rows, _ in groups) + k - a_tail


def _peer_topk_head(qp_ref, sk_ref, h, i1_scr, i2_scr, gate_scr, s_scr, cand_scr, sv_scr, si_scr, bs_scr, bp_scr):
    k = PEER_TOPK
    for c in range(2):
        col = (2 * h + c) * D_KEY_HALF
        s_scr[c] = _dot(sk_ref[h, c], qp_ref[:, col:col + D_KEY_HALF], NT)
    _extract_top([(s_scr.at[c], sv_scr.at[c], si_scr.at[c]) for c in range(2)], k)
    sv0 = sv_scr[0]
    sv1 = sv_scr[1]
    groups, a_tail = _cand_groups(k)
    off = 0
    for a, rows, nb in groups:
        blk = sv0[a:a + 1, :] + sv1[0:rows, :]
        if nb < rows:
            blk = jnp.where(lax.broadcasted_iota(I32, blk.shape, 0) < nb, blk, NEG_INF)
        cand_scr[off:off + rows, :] = blk
        off += rows
    cand_scr[off:off + k - a_tail, :] = sv0[a_tail:k, :] + sv1[0:1, :]
    _extract_top([(cand_scr, bs_scr, bp_scr)], k)
    best_s = bs_scr[...]
    pos = bp_scr[...]
    a_sel = jnp.zeros_like(pos)
    b_sel = pos
    off = 0
    for a, rows, nb in groups:
        if a > 0:
            a_sel = jnp.where(pos >= off, float(a), a_sel)
            b_sel = jnp.where(pos >= off, pos - off, b_sel)
        off += rows
    a_sel = jnp.where(pos >= off, a_tail + pos - off, a_sel)
    b_sel = jnp.where(pos >= off, 0.0, b_sel)
    si0 = si_scr[0]
    si1 = si_scr[1]
    i1 = jnp.zeros_like(pos)
    i2 = jnp.zeros_like(pos)
    for a in range(k):
        i1 = jnp.where(a_sel == a, si0[a:a + 1, :], i1)
        i2 = jnp.where(b_sel == a, si1[a:a + 1, :], i2)
    e = jnp.exp(best_s - jnp.max(best_s, axis=0, keepdims=True))
    i1_scr[h * k:(h + 1) * k, :] = i1
    i2_scr[h * k:(h + 1) * k, :] = i2
    gate_scr[h * k:(h + 1) * k, :] = e / jnp.sum(e, axis=0, keepdims=True)


def _peer_topk_body(qp_ref, sk_ref, i1_ref, i2_ref, gate_ref, i1_scr, i2_scr, gate_scr, *work):
    for h in range(PEER_HEADS):
        _peer_topk_head(qp_ref, sk_ref, h, i1_scr, i2_scr, gate_scr, *work)
    i1_ref[...] = i1_scr[...].T.astype(I32)
    i2_ref[...] = i2_scr[...].T.astype(I32)
    gate_ref[...] = gate_scr[...].T


def _peer_topk(qp, sub_keys, tm):
    m = qp.shape[0]
    k = PEER_TOPK
    ospec = pl.BlockSpec((tm, N_SLOTS), lambda i: (i, 0))
    slots = pltpu.VMEM((N_SLOTS, tm), F32)
    return pl.pallas_call(
        _peer_topk_body,
        grid=(m // tm,),
        in_specs=[pl.BlockSpec((tm, qp.shape[1]), lambda i: (i, 0)), _full(sub_keys)],
        out_specs=[ospec, ospec, ospec],
        out_shape=[jax.ShapeDtypeStruct((m, N_SLOTS), I32), jax.ShapeDtypeStruct((m, N_SLOTS), I32),
                   jax.ShapeDtypeStruct((m, N_SLOTS), F32)],
        scratch_shapes=[slots, slots, slots,
                        pltpu.VMEM((2, N_KEYS, tm), F32), pltpu.VMEM((_n_cand_rows(k), tm), F32),
                        pltpu.VMEM((2, k, tm), F32), pltpu.VMEM((2, k, tm), F32),
                        pltpu.VMEM((k, tm), F32), pltpu.VMEM((k, tm), F32)],
        compiler_params=_params(("parallel",), 32),
        name="peer_topk",
    )(qp, sub_keys)


def _peer_expert_body(x1_ref, h2_ref, i1_ref, i2_ref, gate_ref, ut_ref, v_ref, gf_ref, y_ref, w_scr):
    j = pl.program_id(1)
    tm = h2_ref.shape[0]
    te = v_ref.shape[0]
    pitch = tm + W_PITCH_PAD

    @pl.when(j == 0)
    def _():
        y_ref[...] = x1_ref[...]
        sub = lax.broadcasted_iota(I32, (N_KEYS, N_SLOTS), 0)

        def tok(t, carry):
            hit1 = sub == i1_ref[pl.ds(t, 1), :]
            hit2 = sub == i2_ref[pl.ds(t, 1), :]
            a = jnp.where(hit1, gate_ref[pl.ds(t, 1), :], 0.0).astype(BF16)
            bm = jnp.where(hit2, 1.0, 0.0).astype(BF16)
            w_scr[pl.ds(t, N_KEYS, stride=pitch), :] = _dot(a, bm, NT)
            return carry

        lax.fori_loop(0, tm, tok, 0, unroll=TOKEN_UNROLL)

    rb = min(tm, EXPERT_ROW_BLOCK)
    pair = 2 * N_KEYS
    for r in range(tm // rb):
        x = h2_ref[r * rb:(r + 1) * rb, :]
        parts = []
        for q in range(te // pair):
            h = _dot(x, ut_ref[:, q * pair:(q + 1) * pair])
            i1 = j * (te // N_KEYS) + 2 * q
            w = jnp.concatenate([w_scr[pl.ds(pl.multiple_of(i1 * pitch + r * rb, SUBLANES), rb), :],
                                 w_scr[pl.ds(pl.multiple_of((i1 + 1) * pitch + r * rb, SUBLANES), rb), :]], axis=1)
            act = 0.5 * h * (1.0 + lax.erf(h * (2.0 ** -0.5)))
            parts.append((w * act).astype(BF16))
        y_ref[r * rb:(r + 1) * rb, :] += _dot(jnp.concatenate(parts, axis=1), v_ref[...])

    @pl.when(j == pl.num_programs(1) - 1)
    def _():
        x2 = y_ref[...]
        y_ref[...] = x2 * lax.rsqrt(jnp.mean(x2 * x2, axis=-1, keepdims=True) + NORM_EPS) * gf_ref[...]


def _peer_experts(x1, h2, i1, i2, gate, eu, ev, gf, tm, te):
    m = x1.shape[0]
    row = lambda w: pl.BlockSpec((tm, w), lambda i, j: (i, 0))
    tab = pl.BlockSpec((te, D_MODEL), lambda i, j: (j, 0))
    tab_t = pl.BlockSpec((D_MODEL, te), lambda i, j: (0, j))
    x1_spec = pl.BlockSpec((tm, D_MODEL), lambda i, j: (i, 0), pipeline_mode=pl.Buffered(1))
    return pl.pallas_call(
        _peer_expert_body,
        grid=(m // tm, N_EXPERTS // te),
        in_specs=[x1_spec, row(D_MODEL), row(N_SLOTS), row(N_SLOTS), row(N_SLOTS), tab_t, tab,
                  pl.BlockSpec((1, D_MODEL), lambda i, j: (0, 0))],
        out_specs=row(D_MODEL),
        out_shape=jax.ShapeDtypeStruct((m, D_MODEL), F32),
        scratch_shapes=[pltpu.VMEM((N_KEYS * (tm + W_PITCH_PAD), N_KEYS), F32)],
        compiler_params=_params(("parallel", "arbitrary"), 60),
        name="peer_experts",
    )(x1, h2, i1, i2, gate, eu, ev, gf)


def _pad_cols(a, width):
    return jnp.pad(a, [(0, 0)] * (a.ndim - 1) + [(0, width - a.shape[-1])])


def _prep_layer_params(l, norm1_g, w_in, b_in, attn_sinks, shift_mu, w0, w2, a0, a2, g2, k_k, k_a, r_k, ln_x_w,
                       ln_x_b, w_up_a, w_up_b, w_o, norm2_g, w_query, sub_keys, expert_u, expert_v):
    wi, bi = w_in[l], b_in[l][None, :]
    cat = lambda z: jnp.concatenate([z[:, :OFF_B], _pad_cols(z[:, OFF_B:OFF_GATE], RWKV_PAD), z[:, OFF_GATE:]], axis=1)
    row = lambda z: z[l].reshape(1, -1)
    lora = lambda z, off: jnp.pad(z[l], ((off, LORA_PAD - off - z.shape[1]), (0, 0)))
    return dict(
        norm1_g=row(norm1_g), w_cat=cat(wi).astype(BF16), b_cat=cat(bi), sinks=attn_sinks[l],
        mu=_pad_cols(row(shift_mu), RWKV_PAD), w0=row(w0), a0=row(a0), k_k=row(k_k), k_a=row(k_a),
        w2p=lora(w2, 0), a2p=lora(a2, DECAY_LORA), g2p=lora(g2, DECAY_LORA + AAA_LORA),
        r_k=row(r_k), ln_w=row(ln_x_w), ln_b=row(ln_x_b),
        w_up_a=w_up_a[l].astype(BF16), w_up_b=w_up_b[l].astype(BF16), w_o=w_o[l].astype(BF16),
        norm2_g=row(norm2_g), w_query=w_query[l].astype(BF16), sub_keys=sub_keys[l].astype(BF16),
        expert_ut=expert_u[l].astype(BF16).T, expert_v=expert_v[l].astype(BF16),
    )


def _row_tile(m, pref):
    return pref if m % pref == 0 else m


def _hybrid_layer(x, pos, lp, eblk, gf, win_k, win_v, wkv0, shift0):
    b, t, _ = x.shape
    m = b * t
    x2 = x.reshape(m, D_MODEL)
    tm = _row_tile(m, 256)
    tmd = _row_tile(m, DENSE_TOKEN_TILE)
    rope_tab = _rope_table(pos, max(t, tmd))
    q, k, v, p, ga, gb = _in_proj(x2, lp["norm1_g"], lp["w_cat"], lp["b_cat"], rope_tab, tmd)

    k3 = k.reshape(b, t, KV_WIDTH)
    v3 = v.reshape(b, t, KV_WIDTH)
    if win_k is None:
        att = _attn_prompt(lp["sinks"], q.reshape(b, t, ATTN_WIDTH), k3, v3).reshape(m, ATTN_WIDTH)
        keep = min(WINDOW, t)
        new_k, new_v = k3[:, t - keep:], v3[:, t - keep:]
    else:
        n_past = win_k.shape[1]
        qg = q.reshape(b, t, N_KV_HEADS, Q_PER_KV, HEAD_DIM).transpose(0, 2, 3, 1, 4)
        qg = qg.reshape(b, N_KV_HEADS, Q_PER_KV * t, HEAD_DIM)
        ck = win_k.reshape(b, n_past, KV_WIDTH)
        cv = win_v.reshape(b, n_past, KV_WIDTH)
        og = _attn_sample(lp["sinks"], qg, k3, v3, ck, cv, t)
        att = og.reshape(b, N_KV_HEADS, Q_PER_KV, t, HEAD_DIM).transpose(0, 3, 1, 2, 4).reshape(m, ATTN_WIDTH)
        new_k = jnp.concatenate([ck, k3], axis=1)[:, t:]
        new_v = jnp.concatenate([cv, v3], axis=1)[:, t:]
    new_k = new_k.reshape(b, -1, N_KV_HEADS, HEAD_DIM)
    new_v = new_v.reshape(b, -1, N_KV_HEADS, HEAD_DIM)

    p3 = p.reshape(b, t, RWKV_PAD)
    if t % RWKV_CHUNK == 0:
        ch, tp = RWKV_CHUNK, t
    else:
        tp = -(-t // SUBLANES) * SUBLANES
        ch = tp
        p3 = jnp.pad(p3, ((0, 0), (0, tp - t), (0, 0)))
    shift_pad = _pad_cols(shift0, RWKV_PAD)[:, None, :]
    want = DECODE_BATCH_BLOCK if tp == ch else PROMPT_BATCH_BLOCK
    bb = want if b % want == 0 else 1
    akb, *prep = _rwkv_prep(p3, shift_pad, lp, eblk, ch, t, bb)
    tmat = _tri_solve(akb)
    rw, wkv_t = _rwkv_scan(tmat, prep, wkv0, lp, eblk, bb)
    rw = rw[:, :t].reshape(m, RWKV_WIDTH)
    shift_t = p3[:, t - 1, :RWKV_COLS]

    x1, h2, qp = _merge(x2, att, rw, ga, gb, lp, tmd)
    i1, i2, gate = _peer_topk(qp, lp["sub_keys"], tm)
    y = _peer_experts(x1, h2, i1, i2, gate, lp["expert_ut"], lp["expert_v"], gf,
                      _row_tile(m, EXPERT_TOKEN_TILE), EXPERT_TILE)
    return y.reshape(b, t, D_MODEL), new_k, new_v, wkv_t, shift_t


def kernel(x_prompt, x_sample, cache_win_k, cache_win_v, state_wkv, state_shift, norm1_g, w_in, b_in, attn_sinks,
           shift_mu, w0, w2, a0, a2, g2, k_k, k_a, r_k, ln_x_w, ln_x_b, w_up_a, w_up_b, w_o, norm2_g, w_query,
           sub_keys, expert_u, expert_v, final_norm_g):
    depth = w_in.shape[0]
    assert depth == 1, "the final RMSNorm is fused into the last layer's expert kernel"
    pos_p = jnp.arange(x_prompt.shape[1])
    pos_s = PAST_LEN + jnp.arange(x_sample.shape[1])
    li = lax.broadcasted_iota(I32, (RWKV_WIDTH, RWKV_WIDTH), 0) // RWKV_HEAD_DIM
    lj = lax.broadcasted_iota(I32, (RWKV_WIDTH, RWKV_WIDTH), 1) // RWKV_HEAD_DIM
    eblk = (li == lj).astype(BF16)
    gf = final_norm_g.reshape(1, D_MODEL)
    lp = _prep_layer_params(0, norm1_g, w_in, b_in, attn_sinks, shift_mu, w0, w2, a0, a2, g2, k_k, k_a, r_k, ln_x_w,
                            ln_x_b, w_up_a, w_up_b, w_o, norm2_g, w_query, sub_keys, expert_u, expert_v)
    bp = x_prompt.shape[0]
    wkv0 = jnp.zeros((bp, RWKV_HEADS, RWKV_HEAD_DIM, RWKV_HEAD_DIM), F32)
    shift0 = jnp.zeros((bp, RWKV_COLS), x_prompt.dtype)
    yp, pk, pv, pw, psh = _hybrid_layer(x_prompt, pos_p, lp, eblk, gf, None, None, wkv0, shift0)
    ys, sk, sv, sw, ssh = _hybrid_layer(x_sample, pos_s, lp, eblk, gf, cache_win_k[0], cache_win_v[0],
                                        state_wkv[0], state_shift[0])
    st = lambda z: z[None]
    return (yp, ys, st(pk), st(pv), st(pw), st(psh), st(sk), st(sv), st(sw), st(ssh))
```

```python
import functools

import jax
import jax.numpy as jnp
from jax import lax
from jax.experimental import pallas as pl
from jax.experimental.pallas import tpu as pltpu

F32, BF16, I32 = jnp.float32, jnp.bfloat16, jnp.int32

D_MODEL = 1024
NORM_EPS = 1e-5
HEAD_DIM = 64
N_Q_HEADS = 8
N_KV_HEADS = 2
Q_PER_KV = N_Q_HEADS // N_KV_HEADS
ATTN_WIDTH = N_Q_HEADS * HEAD_DIM
KV_WIDTH = N_KV_HEADS * HEAD_DIM
WINDOW = 128
ROT_DIM = HEAD_DIM // 4
ROPE_THETA = 500000.0
ATTN_SCALE = HEAD_DIM ** -0.5
RWKV_HEADS = 8
RWKV_HEAD_DIM = 64
RWKV_WIDTH = RWKV_HEADS * RWKV_HEAD_DIM
DECAY_LORA = 32
AAA_LORA = 32
GATE_LORA = 96
GN_EPS = 64e-5
RWKV_COLS = 3 * RWKV_WIDTH + DECAY_LORA + AAA_LORA + GATE_LORA
LORA_PAD = 256
RWKV_PAD = 3 * RWKV_WIDTH + LORA_PAD
OFF_K = ATTN_WIDTH
OFF_V = OFF_K + KV_WIDTH
OFF_B = OFF_V + KV_WIDTH
OFF_GATE = OFF_B + RWKV_COLS
N_KEYS = 128
N_EXPERTS = N_KEYS * N_KEYS
PEER_HEADS = 8
PEER_TOPK = 16
D_KEY_HALF = 128
N_SLOTS = PEER_HEADS * PEER_TOPK
PAST_LEN = 16384
RWKV_CHUNK = 64
LANES = 128
SUBLANES = 8
SOLVE_COLS = 16
TOKEN_UNROLL = 32
DECODE_BATCH_BLOCK = 16
PROMPT_BATCH_BLOCK = 8
DENSE_TOKEN_TILE = 512
EXPERT_TOKEN_TILE = 512
EXPERT_TILE = 1024
EXPERT_ROW_BLOCK = 512
W_PITCH_PAD = 8
NEG_INF = float("-inf")

NN = ((1,), (0,))
NT = ((1,), (1,))
TN = ((0,), (0,))


def _dot(a, b, dims=NN):
    return lax.dot_general(a, b, (dims, ((), ())), preferred_element_type=F32)


def _split2(x):
    hi = x.astype(BF16)
    return hi, (x - hi.astype(F32)).astype(BF16)


def _doth(a, b, dims=NN):
    a_hi, a_lo = _split2(a)
    b_hi, b_lo = _split2(b)
    return _dot(a_hi, b_hi, dims) + _dot(a_hi, b_lo, dims) + _dot(a_lo, b_hi, dims)


def _dotb(a, b, dims=NN):
    return _dot(a.astype(BF16), b.astype(BF16), dims)


def _split3(x):
    hi = x.astype(BF16)
    r1 = x - hi.astype(F32)
    mid = r1.astype(BF16)
    lo = (r1 - mid.astype(F32)).astype(BF16)
    return hi, mid, lo


def _dot_exact_rhs(x, e):
    hi, mid, lo = _split3(x)
    return _dot(hi, e) + _dot(mid, e) + _dot(lo, e)


def _dot_exact_lhs(e, x):
    hi, mid, lo = _split3(x)
    return _dot(e, hi) + _dot(e, mid) + _dot(e, lo)


def _params(sem, vmem_mb):
    return pltpu.CompilerParams(dimension_semantics=sem, vmem_limit_bytes=vmem_mb * 1024 * 1024)


def _full(a):
    nd = a.ndim
    return pl.BlockSpec(a.shape, lambda *_: (0,) * nd)


SEG_QKV = ATTN_WIDTH + 2 * KV_WIDTH
SEG_P = SEG_QKV + RWKV_PAD
SEG_END = SEG_P + 2 * D_MODEL


def _inproj_body(x_ref, g_ref, w_ref, b_ref, rope_ref, q_ref, k_ref, v_ref, p_ref, ga_ref, gb_ref):
    x = x_ref[...]
    h = (x * lax.rsqrt(jnp.mean(x * x, axis=-1, keepdims=True) + NORM_EPS) * g_ref[...]).astype(BF16)

    def seg(lo, hi):
        return _dot(h, w_ref[:, lo:hi]) + b_ref[:, lo:hi]

    cos = rope_ref[:, 0:LANES]
    sin_up = rope_ref[:, LANES:2 * LANES]
    sin_dn = rope_ref[:, 2 * LANES:3 * LANES]

    def rope(z):
        return z * cos + pltpu.roll(z, ROT_DIM // 2, 1) * sin_up + pltpu.roll(z, LANES - ROT_DIM // 2, 1) * sin_dn

    for c in range(ATTN_WIDTH // LANES):
        q_ref[:, c * LANES:(c + 1) * LANES] = rope(seg(c * LANES, (c + 1) * LANES))
    k_ref[...] = rope(seg(OFF_K, OFF_V))
    v_ref[...] = seg(OFF_V, SEG_QKV)
    p_ref[...] = seg(SEG_QKV, SEG_P)
    ga_ref[...] = jax.nn.sigmoid(seg(SEG_P, SEG_P + D_MODEL))
    gb_ref[...] = jax.nn.sigmoid(seg(SEG_P + D_MODEL, SEG_END))


def _in_proj(x2, g1, w_cat, b_cat, rope_tab, tm):
    m = x2.shape[0]
    nrep = rope_tab.shape[0] // tm
    widths = (ATTN_WIDTH, KV_WIDTH, KV_WIDTH, RWKV_PAD, D_MODEL, D_MODEL)
    return pl.pallas_call(
        _inproj_body,
        grid=(m // tm,),
        in_specs=[
            pl.BlockSpec((tm, D_MODEL), lambda i: (i, 0)),
            _full(g1), _full(w_cat), _full(b_cat),
            pl.BlockSpec((tm, 3 * LANES), lambda i: (i % nrep, 0)),
        ],
        out_specs=[pl.BlockSpec((tm, w), lambda i: (i, 0)) for w in widths],
        out_shape=[jax.ShapeDtypeStruct((m, w), F32) for w in widths],
        compiler_params=_params(("parallel",), 56),
        name="in_proj",
    )(x2, g1, w_cat, b_cat, rope_tab)


def _rope_table(pos, rows):
    t = pos.shape[0]
    half = ROT_DIM // 2
    inv_freq = ROPE_THETA ** (-jnp.arange(half, dtype=F32) / half)
    ang = pos.astype(F32)[:, None] * inv_freq[None, :]
    cos, sin = jnp.cos(ang), jnp.sin(ang)
    rest = HEAD_DIM - ROT_DIM
    z8 = jnp.zeros((t, half), F32)
    cos64 = jnp.concatenate([cos, cos, jnp.ones((t, rest), F32)], axis=1)
    up64 = jnp.concatenate([z8, sin, jnp.zeros((t, rest), F32)], axis=1)
    dn64 = jnp.concatenate([-sin, z8, jnp.zeros((t, rest), F32)], axis=1)
    tab = jnp.concatenate([jnp.tile(a, (1, LANES // HEAD_DIM)) for a in (cos64, up64, dn64)], axis=1)
    if rows > t:
        tab = jnp.tile(tab, (rows // t, 1))
    return tab


def _softmax_av(s, sink, vv):
    m = jnp.maximum(jnp.max(s, axis=-1, keepdims=True), sink)
    p = jnp.exp(s - m)
    den = jnp.sum(p, axis=-1, keepdims=True) + jnp.exp(sink - m)
    return _dot(p.astype(BF16), vv) / den


def _attn_prompt_body(sink_ref, q_ref, kc_ref, kp_ref, vc_ref, vp_ref, o_ref):
    n = pl.program_id(1)
    blk = q_ref.shape[0]
    i = lax.broadcasted_iota(I32, (blk, 2 * blk), 0)
    j = lax.broadcasted_iota(I32, (blk, 2 * blk), 1)
    diff = blk + i - j
    j_lo = jnp.where(n > 0, 0, blk)
    mask = (diff >= 0) & (diff < WINDOW) & (j >= j_lo)
    outs = []
    for g in range(N_KV_HEADS):
        sl = slice(g * HEAD_DIM, (g + 1) * HEAD_DIM)
        kk = jnp.concatenate([kp_ref[:, sl], kc_ref[:, sl]], axis=0).astype(BF16)
        vv = jnp.concatenate([vp_ref[:, sl], vc_ref[:, sl]], axis=0).astype(BF16)
        for hq in range(Q_PER_KV):
            h = g * Q_PER_KV + hq
            qh = q_ref[:, h * HEAD_DIM:(h + 1) * HEAD_DIM].astype(BF16)
            s = jnp.where(mask, _dot(qh, kk, NT) * ATTN_SCALE, NEG_INF)
            outs.append(_softmax_av(s, sink_ref[h], vv))
    o_ref[...] = jnp.concatenate(outs, axis=1).astype(o_ref.dtype)


def _attn_prompt(sinks, q3, k3, v3):
    b, t, _ = q3.shape
    blk = WINDOW
    cur = lambda w: pl.BlockSpec((None, blk, w), lambda bi, n: (bi, n, 0))
    prev = lambda w: pl.BlockSpec((None, blk, w), lambda bi, n: (bi, jnp.maximum(n - 1, 0), 0))
    return pl.pallas_call(
        _attn_prompt_body,
        grid=(b, t // blk),
        in_specs=[pl.BlockSpec(memory_space=pltpu.SMEM), cur(ATTN_WIDTH), cur(KV_WIDTH), prev(KV_WIDTH),
                  cur(KV_WIDTH), prev(KV_WIDTH)],
        out_specs=cur(ATTN_WIDTH),
        out_shape=jax.ShapeDtypeStruct((b, t, ATTN_WIDTH), BF16),
        compiler_params=_params(("parallel", "arbitrary"), 32),
        name="attn_prompt",
    )(sinks, q3, k3, k3, v3, v3)


def _attn_sample_body(sink_ref, *refs, t):
    for bi in range(refs[0].shape[0]):
        _attn_sample_one(sink_ref, *[r.at[bi] for r in refs], t=t)


def _attn_sample_one(sink_ref, q_ref, kn_ref, vn_ref, kc_ref, vc_ref, o_ref, *, t):
    rows = Q_PER_KV * t
    n_past = kc_ref.shape[0]
    ri = lax.broadcasted_iota(I32, (rows, 1), 0)
    qi = ri % t
    jc = lax.broadcasted_iota(I32, (rows, n_past), 1)
    mask_c = (n_past + qi - jc) < WINDOW
    for g in range(N_KV_HEADS):
        sl = slice(g * HEAD_DIM, (g + 1) * HEAD_DIM)
        sink = jnp.zeros((rows, 1), F32)
        for hq in range(Q_PER_KV):
            sink = jnp.where(ri // t == hq, sink_ref[g * Q_PER_KV + hq], sink)
        qg = q_ref[g]
        qb = qg.astype(BF16)
        kn = kn_ref[:, sl]
        vn = vn_ref[:, sl]
        s_c = jnp.where(mask_c, _dot(qb, kc_ref[:, sl].astype(BF16), NT) * ATTN_SCALE, NEG_INF)
        s_n = []
        for jn in range(t):
            sj = jnp.sum(qg * kn[jn:jn + 1, :], axis=-1, keepdims=True) * ATTN_SCALE
            s_n.append(jnp.where(qi >= jn, sj, NEG_INF))
        m = jnp.maximum(jnp.max(s_c, axis=-1, keepdims=True), sink)
        for sj in s_n:
            m = jnp.maximum(m, sj)
        p_c = jnp.exp(s_c - m)
        den = jnp.sum(p_c, axis=-1, keepdims=True) + jnp.exp(sink - m)
        o = _dot(p_c.astype(BF16), vc_ref[:, sl].astype(BF16))
        for jn in range(t):
            p_n = jnp.exp(s_n[jn] - m)
            den = den + p_n
            o = o + p_n * vn[jn:jn + 1, :]
        o_ref[g] = (o / den).astype(o_ref.dtype)


def _attn_sample(sinks, qg, kn, vn, cache_k, cache_v, t):
    b = qg.shape[0]
    rows = Q_PER_KV * t
    n_past = cache_k.shape[1]
    bb = DECODE_BATCH_BLOCK if b % DECODE_BATCH_BLOCK == 0 else 1
    qspec = pl.BlockSpec((bb, N_KV_HEADS, rows, HEAD_DIM), lambda bi: (bi, 0, 0, 0))
    nspec = pl.BlockSpec((bb, t, KV_WIDTH), lambda bi: (bi, 0, 0))
    cspec = pl.BlockSpec((bb, n_past, KV_WIDTH), lambda bi: (bi, 0, 0))
    return pl.pallas_call(
        functools.partial(_attn_sample_body, t=t),
        grid=(b // bb,),
        in_specs=[pl.BlockSpec(memory_space=pltpu.SMEM), qspec, nspec, nspec, cspec, cspec],
        out_specs=qspec,
        out_shape=jax.ShapeDtypeStruct((b, N_KV_HEADS, rows, HEAD_DIM), BF16),
        compiler_params=_params(("parallel",), 32),
        name="attn_sample",
    )(sinks, qg, kn, vn, cache_k, cache_v)


N_PREP_CONSTS = 9


def _head_pack(ch):
    return min(RWKV_HEADS, max(1, LANES // ch))


def _rwkv_prep_body(*refs, t_real):
    seq, consts, outs = refs[:3], refs[3:3 + N_PREP_CONSTS], refs[3 + N_PREP_CONSTS:]
    for bi in range(seq[0].shape[0]):
        _rwkv_prep_one(*[r.at[bi] for r in seq], *consts, *[r.at[bi] for r in outs], t_real=t_real)


def _rwkv_prep_one(p_ref, pprev_ref, shift_ref, mu_ref, w0_ref, a0_ref, kk_ref, ka_ref, w2_ref, a2_ref, g2_ref,
                   eblk_ref, akb_ref, akk_ref, qrk_ref, qrb_ref, kkt_ref, rt_ref, kdec_ref, bdec_ref, kmod_ref,
                   r_ref, v_ref, g_ref, gamc_ref, *, t_real):
    c = pl.program_id(1)
    ch = p_ref.shape[0]
    pc = p_ref[...]
    row = lax.broadcasted_iota(I32, (ch, 1), 0)
    first = jnp.where(c == 0, shift_ref[...], pprev_ref[SUBLANES - 1:SUBLANES, :])
    prev = jnp.where(row == 0, first, pltpu.roll(pc, 1, 0))
    xm = pc + mu_ref[...] * (prev - pc)
    valid = (c * ch + row) < t_real
    w3 = 3 * RWKV_WIDTH
    r = xm[:, 0:RWKV_WIDTH]
    k = xm[:, RWKV_WIDTH:2 * RWKV_WIDTH]
    v = xm[:, 2 * RWKV_WIDTH:w3]
    lo = xm[:, w3:w3 + LORA_PAD]
    lane = lax.broadcasted_iota(I32, lo.shape, 1)
    act = jnp.where(lane < DECAY_LORA, jnp.tanh(lo),
                    jnp.where(lane < DECAY_LORA + AAA_LORA, lo, jax.nn.sigmoid(lo)))
    z = w0_ref[...] + _doth(act, w2_ref[...])
    w_log = -(jnp.maximum(-z, 0.0) + jnp.log1p(jnp.exp(-jnp.abs(z)))) - 0.5
    lw = jnp.where(valid, -jnp.exp(w_log), 0.0)
    a = jax.nn.sigmoid(a0_ref[...] + _doth(act, a2_ref[...]))
    g = _dot(act.astype(BF16), g2_ref[...].astype(BF16))
    kk = k * kk_ref[...]
    ss = _dot_exact_rhs(kk * kk, eblk_ref[...])
    kk = jnp.where(valid, kk * lax.rsqrt(jnp.maximum(ss, 1e-24)), 0.0)
    kmod = jnp.where(valid, k * (1.0 + (a - 1.0) * ka_ref[...]), 0.0)
    bb = kk * a
    ti = lax.broadcasted_iota(I32, (ch, ch), 0)
    tj = lax.broadcasted_iota(I32, (ch, ch), 1)
    tri = jnp.where(ti >= tj, 1.0, 0.0).astype(BF16)
    cum = _dot_exact_lhs(tri, lw)
    cum_c = cum[ch - 1:ch, :]
    e_neg = jnp.exp(-cum)
    e_rem = jnp.exp(cum_c - cum)
    kkt = kk * jnp.exp(cum - lw)
    rt = r * jnp.exp(cum)
    khat = kmod * e_neg
    bhat = bb * e_neg
    strict = ti > tj
    incl = ti >= tj
    hs = range(RWKV_HEADS)
    sl = [slice(h * RWKV_HEAD_DIM, (h + 1) * RWKV_HEAD_DIM) for h in hs]
    split = lambda z: [z[:, s].astype(BF16) for s in sl]
    kkt_s, rt_s, khat_s, bhat_s = split(kkt), split(rt), split(khat), split(bhat)
    akb = [_dot(kkt_s[h], bhat_s[h], NT) for h in hs]
    akk = [_dot(kkt_s[h], khat_s[h], NT) for h in hs]
    qrk = [_dot(rt_s[h], khat_s[h], NT) for h in hs]
    qrb = [_dot(rt_s[h], bhat_s[h], NT) for h in hs]
    pack = _head_pack(ch)
    for gi in range(RWKV_HEADS // pack):
        grp = range(gi * pack, (gi + 1) * pack)
        side = lambda mats, keep: jnp.concatenate([jnp.where(keep, mats[h], 0.0) for h in grp], axis=1)
        akb_ref[gi] = side(akb, strict)
        akk_ref[gi] = side(akk, strict)
        qrk_ref[gi] = side(qrk, incl)
        qrb_ref[gi] = side(qrb, incl)
    kkt_ref[...] = kkt
    rt_ref[...] = rt
    kdec_ref[...] = kmod * e_rem
    bdec_ref[...] = bb * e_rem
    kmod_ref[...] = kmod
    r_ref[...] = r
    v_ref[...] = v
    g_ref[...] = g
    gamc_ref[...] = jnp.exp(cum_c)


def _rwkv_prep(p3, shift0, lp, eblk, ch, t_real, bb):
    b, tp, _ = p3.shape
    nc = tp // ch
    per8 = ch // SUBLANES
    pack = _head_pack(ch)
    ng, gw = RWKV_HEADS // pack, pack * ch
    mats = jax.ShapeDtypeStruct((b, nc, ng, ch, gw), F32)
    rows = jax.ShapeDtypeStruct((b, tp, RWKV_WIDTH), F32)
    mspec = pl.BlockSpec((bb, None, ng, ch, gw), lambda bi, ci: (bi, ci, 0, 0, 0))
    rspec = pl.BlockSpec((bb, ch, RWKV_WIDTH), lambda bi, ci: (bi, ci, 0))
    consts = (lp["mu"], lp["w0"], lp["a0"], lp["k_k"], lp["k_a"], lp["w2p"], lp["a2p"], lp["g2p"], eblk)
    assert len(consts) == N_PREP_CONSTS
    return pl.pallas_call(
        functools.partial(_rwkv_prep_body, t_real=t_real),
        grid=(b // bb, nc),
        in_specs=[
            pl.BlockSpec((bb, ch, RWKV_PAD), lambda bi, ci: (bi, ci, 0)),
            pl.BlockSpec((bb, SUBLANES, RWKV_PAD), lambda bi, ci: (bi, jnp.maximum(ci * per8 - 1, 0), 0)),
            pl.BlockSpec((bb, 1, RWKV_PAD), lambda bi, ci: (bi, 0, 0)),
        ] + [_full(a) for a in consts],
        out_specs=[mspec] * 4 + [rspec] * 8 + [pl.BlockSpec((bb, None, 1, RWKV_WIDTH), lambda bi, ci: (bi, ci, 0, 0))],
        out_shape=[mats] * 4 + [rows] * 8 + [jax.ShapeDtypeStruct((b, nc, 1, RWKV_WIDTH), F32)],
        compiler_params=_params(("parallel", "parallel"), 48),
        name="rwkv_prep",
    )(p3, p3, shift0, *consts)


def _tri_solve_body(a_ref, t_ref, w_scr, *, ch, pack):
    k = pl.program_id(1)
    kc_w = a_ref.shape[1]
    nk = (pl.num_programs(1) - 1) // 2
    groups = a_ref.shape[0] // LANES
    gw = pack * ch

    @pl.when(k < nk)
    def _():
        xt = a_ref[...].T
        for g in range(groups):
            w_scr[pl.ds(k * (kc_w * groups) + g, kc_w, stride=groups), :] = xt[:, g * LANES:(g + 1) * LANES]

    @pl.when(k == nk)
    def _():
        cols = min(SOLVE_COLS, ch)
        cid = lax.broadcasted_iota(I32, (cols * groups, LANES), 0) // groups
        at = lambda e, n: pl.ds(pl.multiple_of(e * groups, groups), n * groups)

        for j in range(pack):
            def row_body(s, carry, j=j):
                for kc in range(ch // cols):
                    c0 = kc * cols

                    @pl.when(s >= c0)
                    def _():
                        acc0 = jnp.where(cid == s - c0, 1.0, 0.0).astype(F32)

                        def r_body(r, acc):
                            a_sr = w_scr[at(s * gw + j * ch + r, 1), :]
                            return acc - jnp.tile(a_sr, (cols, 1)) * w_scr[at(r * gw + j * ch + c0, cols), :]

                        w_scr[at(s * gw + j * ch + c0, cols), :] = lax.fori_loop(c0, s, r_body, acc0)
                return carry

            lax.fori_loop(0, ch, row_body, 0)

    @pl.when(k > nk)
    def _():
        kk = k - nk - 1
        parts = [w_scr[pl.ds(kk * (kc_w * groups) + g, kc_w, stride=groups), :] for g in range(groups)]
        t_ref[...] = jnp.concatenate(parts, axis=1).T


def _tri_solve(akb):
    b, nc, ng, ch, gw = akb.shape
    nrow = b * nc * ng
    slab = SUBLANES * LANES
    ne = ch * gw
    row_pad = -nrow % slab
    ne_pad = -ne % LANES
    a2 = akb.reshape(nrow, ne)
    if row_pad or ne_pad:
        a2 = jnp.pad(a2, ((0, row_pad), (0, ne_pad)))
    nk = (ne + ne_pad) // LANES
    t2 = pl.pallas_call(
        functools.partial(_tri_solve_body, ch=ch, pack=gw // ch),
        grid=((nrow + row_pad) // slab, 2 * nk + 1),
        in_specs=[pl.BlockSpec((slab, LANES), lambda gi, k: (gi, jnp.minimum(k, nk - 1)))],
        out_specs=pl.BlockSpec((slab, LANES), lambda gi, k: (gi, jnp.clip(k - nk - 1, 0, nk - 1))),
        out_shape=jax.ShapeDtypeStruct(a2.shape, F32),
        scratch_shapes=[pltpu.VMEM(((ne + ne_pad) * SUBLANES, LANES), F32)],
        compiler_params=_params(("parallel", "arbitrary"), 48),
        name="rwkv_tri_solve",
    )(a2)
    if row_pad or ne_pad:
        t2 = t2[:nrow, :ne]
    return t2.reshape(b, nc, ng, ch, gw)


N_SCAN_SEQ = 14
N_SCAN_CONSTS = 4


def _rwkv_scan_body(*refs):
    seq, consts, rest = refs[:N_SCAN_SEQ], refs[N_SCAN_SEQ:N_SCAN_SEQ + N_SCAN_CONSTS], refs[N_SCAN_SEQ + N_SCAN_CONSTS:]
    for bi in range(seq[0].shape[0]):
        _rwkv_scan_one(*[r.at[bi] for r in seq], *consts, *[r.at[bi] for r in rest])


def _rwkv_scan_one(t_ref, akk_ref, qrk_ref, qrb_ref, kkt_ref, rt_ref, kdec_ref, bdec_ref, kmod_ref, r_ref, v_ref,
                   g_ref, gamc_ref, wkv0_ref, eblk_ref, rk_ref, lnw_ref, lnb_ref, rw_ref, wkv_ref, s_scr):
    c = pl.program_id(1)
    nd = RWKV_HEAD_DIM

    @pl.when(c == 0)
    def _():
        s_scr[...] = wkv0_ref[...]

    hs = range(RWKV_HEADS)
    sl = [slice(h * nd, (h + 1) * nd) for h in hs]
    ch = kkt_ref.shape[0]
    pack = _head_pack(ch)
    mat = lambda ref, h: ref[h // pack][:, (h % pack) * ch:(h % pack + 1) * ch]
    tmat = [mat(t_ref, h) for h in hs]
    vh = [v_ref[:, sl[h]] for h in hs]
    bdec = [bdec_ref[:, sl[h]] for h in hs]
    qrb = [mat(qrb_ref, h) for h in hs]
    w1 = [_dotb(tmat[h], kkt_ref[:, sl[h]]) for h in hs]
    av = [_dotb(mat(akk_ref, h), vh[h]) for h in hs]
    qv = [_dotb(mat(qrk_ref, h), vh[h]) for h in hs]
    vk = [_dotb(vh[h], kdec_ref[:, sl[h]], TN) for h in hs]
    w2 = [_dotb(tmat[h], av[h]) for h in hs]
    qw1 = [_dotb(qrb[h], w1[h]) for h in hs]
    bw1 = [_dotb(bdec[h], w1[h], TN) for h in hs]
    qw2 = [_dotb(qrb[h], w2[h]) for h in hs]
    wb = [_dotb(w2[h], bdec[h], TN) for h in hs]
    s0 = [s_scr[h] for h in hs]
    ys = [_dotb(rt_ref[:, sl[h]] - qw1[h], s0[h], NT) + (qv[h] - qw2[h]) for h in hs]
    for h in hs:
        s_scr[h] = s0[h] * gamc_ref[:, sl[h]] - _dotb(s0[h], bw1[h], NT) + (vk[h] - wb[h])
    y = jnp.concatenate(ys, axis=1)
    eblk = eblk_ref[...]
    inv_n = 1.0 / nd
    mean = _dot_exact_rhs(y, eblk) * inv_n
    yc = y - mean
    var = _dot_exact_rhs(yc * yc, eblk) * inv_n
    yn = yc * lax.rsqrt(var + GN_EPS) * lnw_ref[...] + lnb_ref[...]
    v = v_ref[...]
    bonus = _dot_exact_rhs(r_ref[...] * kmod_ref[...] * rk_ref[...], eblk) * v
    rw_ref[...] = ((yn + bonus) * g_ref[...]).astype(rw_ref.dtype)

    @pl.when(c == pl.num_programs(1) - 1)
    def _():
        wkv_ref[...] = s_scr[...]


def _rwkv_scan(tmat, prep, wkv0, lp, eblk, bb):
    akk, qrk, qrb, kkt, rt, kdec, bdec, kmod, r, v, g, gamc = prep
    b, nc, ng, ch, gw = tmat.shape
    nh = RWKV_HEADS
    tp = kkt.shape[1]
    nd = RWKV_HEAD_DIM
    mspec = pl.BlockSpec((bb, None, ng, ch, gw), lambda bi, ci: (bi, ci, 0, 0, 0))
    rspec = pl.BlockSpec((bb, ch, RWKV_WIDTH), lambda bi, ci: (bi, ci, 0))
    sspec = pl.BlockSpec((bb, nh, nd, nd), lambda bi, ci: (bi, 0, 0, 0))
    consts = (eblk, lp["r_k"], lp["ln_w"], lp["ln_b"])
    assert len(consts) == N_SCAN_CONSTS
    return pl.pallas_call(
        _rwkv_scan_body,
        grid=(b // bb, nc),
        in_specs=[mspec] * 4 + [rspec] * 8
        + [pl.BlockSpec((bb, None, 1, RWKV_WIDTH), lambda bi, ci: (bi, ci, 0, 0)), sspec]
        + [_full(a) for a in consts],
        out_specs=[rspec, sspec],
        out_shape=[jax.ShapeDtypeStruct((b, tp, RWKV_WIDTH), BF16), jax.ShapeDtypeStruct((b, nh, nd, nd), F32)],
        scratch_shapes=[pltpu.VMEM((bb, nh, nd, nd), F32)],
        compiler_params=_params(("parallel", "arbitrary"), 48),
        name="rwkv_scan",
    )(tmat, akk, qrk, qrb, kkt, rt, kdec, bdec, kmod, r, v, g, gamc, wkv0, *consts)


def _merge_body(x_ref, att_ref, rw_ref, ga_ref, gb_ref, wa_ref, wb_ref, wo_ref, g2_ref, wq_ref,
                x1_ref, h2_ref, qp_ref):
    merged = ga_ref[...] * _dot(att_ref[...], wa_ref[...]) + gb_ref[...] * _dot(rw_ref[...], wb_ref[...])
    x1 = x_ref[...] + _dot(merged.astype(BF16), wo_ref[...])
    x1_ref[...] = x1
    h2 = (x1 * lax.rsqrt(jnp.mean(x1 * x1, axis=-1, keepdims=True) + NORM_EPS) * g2_ref[...]).astype(BF16)
    h2_ref[...] = h2
    qp_ref[...] = _dot(h2, wq_ref[...]).astype(qp_ref.dtype)


def _merge(x2, att, rw, ga, gb, lp, tm):
    m = x2.shape[0]
    kq = lp["w_query"].shape[1]
    row = lambda w: pl.BlockSpec((tm, w), lambda i: (i, 0))
    consts = (lp["w_up_a"], lp["w_up_b"], lp["w_o"], lp["norm2_g"], lp["w_query"])
    return pl.pallas_call(
        _merge_body,
        grid=(m // tm,),
        in_specs=[row(D_MODEL), row(ATTN_WIDTH), row(RWKV_WIDTH), row(D_MODEL), row(D_MODEL)]
        + [_full(a) for a in consts],
        out_specs=[row(D_MODEL), row(D_MODEL), row(kq)],
        out_shape=[jax.ShapeDtypeStruct((m, D_MODEL), F32), jax.ShapeDtypeStruct((m, D_MODEL), BF16),
                   jax.ShapeDtypeStruct((m, kq), BF16)],
        compiler_params=_params(("parallel",), 56),
        name="merge",
    )(x2, att, rw, ga, gb, *consts)


def _argmax_rows(src_ref):
    nslab = src_ref.shape[0] // SUBLANES
    row0 = lax.broadcasted_iota(I32, (SUBLANES, src_ref.shape[1]), 0).astype(F32)
    vals = [src_ref[i * SUBLANES:(i + 1) * SUBLANES, :] for i in range(nslab)]
    rows = [row0 + float(i * SUBLANES) for i in range(nslab)]
    while len(vals) > 1:
        nv, nr = [], []
        for a in range(0, len(vals) - 1, 2):
            take = vals[a + 1] > vals[a]
            nv.append(jnp.maximum(vals[a], vals[a + 1]))
            nr.append(jnp.where(take, rows[a + 1], rows[a]))
        if len(vals) % 2:
            nv.append(vals[-1])
            nr.append(rows[-1])
        vals, rows = nv, nr
    v, i = vals[0], rows[0]
    for shift in (4, 2, 1):
        v2 = pltpu.roll(v, shift, 0)
        i2 = pltpu.roll(i, shift, 0)
        take = (v2 > v) | ((v2 == v) & (i2 < i))
        v = jnp.where(take, v2, v)
        i = jnp.where(take, i2, i)
    return jnp.max(v, axis=0, keepdims=True), jnp.max(i, axis=0, keepdims=True)


def _extract_top(jobs, n_out):
    def body(r, carry):
        for src_ref, val_ref, idx_ref in jobs:
            rowf = lax.broadcasted_iota(I32, src_ref.shape, 0).astype(F32)
            m, idx = _argmax_rows(src_ref)
            val_ref[pl.ds(r, 1), :] = m
            idx_ref[pl.ds(r, 1), :] = idx
            src_ref[...] = jnp.where(rowf == idx, NEG_INF, src_ref[...])
        return carry

    lax.fori_loop(0, n_out, body, 0)


def _cand_groups(k):
    groups = []
    a = 0
    while a < k and k // (a + 1) > 1:
        nb = k // (a + 1)
        groups.append((a, -(-nb // SUBLANES) * SUBLANES, nb))
        a += 1
    return groups, a


def _n_cand_rows(k):
    groups, a_tail = _cand_groups(k)
    return sum(rows for _, rows, _ in groups) + k - a_tail


def _peer_topk_head(qp_ref, sk_ref, h, i1_scr, i2_scr, gate_scr, s_scr, cand_scr, sv_scr, si_scr, bs_scr, bp_scr):
    k = PEER_TOPK
    for c in range(2):
        col = (2 * h + c) * D_KEY_HALF
        s_scr[c] = _dot(sk_ref[h, c], qp_ref[:, col:col + D_KEY_HALF], NT)
    _extract_top([(s_scr.at[c], sv_scr.at[c], si_scr.at[c]) for c in range(2)], k)
    sv0 = sv_scr[0]
    sv1 = sv_scr[1]
    groups, a_tail = _cand_groups(k)
    off = 0
    for a, rows, nb in groups:
        blk = sv0[a:a + 1, :] + sv1[0:rows, :]
        if nb < rows:
            blk = jnp.where(lax.broadcasted_iota(I32, blk.shape, 0) < nb, blk, NEG_INF)
        cand_scr[off:off + rows, :] = blk
        off += rows
    cand_scr[off:off + k - a_tail, :] = sv0[a_tail:k, :] + sv1[0:1, :]
    _extract_top([(cand_scr, bs_scr, bp_scr)], k)
    best_s = bs_scr[...]
    pos = bp_scr[...]
    a_sel = jnp.zeros_like(pos)
    b_sel = pos
    off = 0
    for a, rows, nb in groups:
        if a > 0:
            a_sel = jnp.where(pos >= off, float(a), a_sel)
            b_sel = jnp.where(pos >= off, pos - off, b_sel)
        off += rows
    a_sel = jnp.where(pos >= off, a_tail + pos - off, a_sel)
    b_sel = jnp.where(pos >= off, 0.0, b_sel)
    si0 = si_scr[0]
    si1 = si_scr[1]
    i1 = jnp.zeros_like(pos)
    i2 = jnp.zeros_like(pos)
    for a in range(k):
        i1 = jnp.where(a_sel == a, si0[a:a + 1, :], i1)
        i2 = jnp.where(b_sel == a, si1[a:a + 1, :], i2)
    e = jnp.exp(best_s - jnp.max(best_s, axis=0, keepdims=True))
    i1_scr[h * k:(h + 1) * k, :] = i1
    i2_scr[h * k:(h + 1) * k, :] = i2
    gate_scr[h * k:(h + 1) * k, :] = e / jnp.sum(e, axis=0, keepdims=True)


def _peer_topk_body(qp_ref, sk_ref, i1_ref, i2_ref, gate_ref, i1_scr, i2_scr, gate_scr, *work):
    for h in range(PEER_HEADS):
        _peer_topk_head(qp_ref, sk_ref, h, i1_scr, i2_scr, gate_scr, *work)
    i1_ref[...] = i1_scr[...].T.astype(I32)
    i2_ref[...] = i2_scr[...].T.astype(I32)
    gate_ref[...] = gate_scr[...].T


def _peer_topk(qp, sub_keys, tm):
    m = qp.shape[0]
    k = PEER_TOPK
    ospec = pl.BlockSpec((tm, N_SLOTS), lambda i: (i, 0))
    slots = pltpu.VMEM((N_SLOTS, tm), F32)
    return pl.pallas_call(
        _peer_topk_body,
        grid=(m // tm,),
        in_specs=[pl.BlockSpec((tm, qp.shape[1]), lambda i: (i, 0)), _full(sub_keys)],
        out_specs=[ospec, ospec, ospec],
        out_shape=[jax.ShapeDtypeStruct((m, N_SLOTS), I32), jax.ShapeDtypeStruct((m, N_SLOTS), I32),
                   jax.ShapeDtypeStruct((m, N_SLOTS), F32)],
        scratch_shapes=[slots, slots, slots,
                        pltpu.VMEM((2, N_KEYS, tm), F32), pltpu.VMEM((_n_cand_rows(k), tm), F32),
                        pltpu.VMEM((2, k, tm), F32), pltpu.VMEM((2, k, tm), F32),
                        pltpu.VMEM((k, tm), F32), pltpu.VMEM((k, tm), F32)],
        compiler_params=_params(("parallel",), 32),
        name="peer_topk",
    )(qp, sub_keys)


def _peer_expert_body(x1_ref, h2_ref, i1_ref, i2_ref, gate_ref, ut_ref, v_ref, gf_ref, y_ref, w_scr):
    j = pl.program_id(1)
    tm = h2_ref.shape[0]
    te = v_ref.shape[0]
    pitch = tm + W_PITCH_PAD

    @pl.when(j == 0)
    def _():
        y_ref[...] = x1_ref[...]
        sub = lax.broadcasted_iota(I32, (N_KEYS, N_SLOTS), 0)

        def tok(t, carry):
            hit1 = sub == i1_ref[pl.ds(t, 1), :]
            hit2 = sub == i2_ref[pl.ds(t, 1), :]
            a = jnp.where(hit1, gate_ref[pl.ds(t, 1), :], 0.0).astype(BF16)
            bm = jnp.where(hit2, 1.0, 0.0).astype(BF16)
            w_scr[pl.ds(t, N_KEYS, stride=pitch), :] = _dot(a, bm, NT)
            return carry

        lax.fori_loop(0, tm, tok, 0, unroll=TOKEN_UNROLL)

    rb = min(tm, EXPERT_ROW_BLOCK)
    pair = 2 * N_KEYS
    for r in range(tm // rb):
        x = h2_ref[r * rb:(r + 1) * rb, :]
        parts = []
        for q in range(te // pair):
            h = _dot(x, ut_ref[:, q * pair:(q + 1) * pair])
            i1 = j * (te // N_KEYS) + 2 * q
            w = jnp.concatenate([w_scr[pl.ds(pl.multiple_of(i1 * pitch + r * rb, SUBLANES), rb), :],
                                 w_scr[pl.ds(pl.multiple_of((i1 + 1) * pitch + r * rb, SUBLANES), rb), :]], axis=1)
            act = 0.5 * h * (1.0 + lax.erf(h * (2.0 ** -0.5)))
            parts.append((w * act).astype(BF16))
        y_ref[r * rb:(r + 1) * rb, :] += _dot(jnp.concatenate(parts, axis=1), v_ref[...])

    @pl.when(j == pl.num_programs(1) - 1)
    def _():
        x2 = y_ref[...]
        y_ref[...] = x2 * lax.rsqrt(jnp.mean(x2 * x2, axis=-1, keepdims=True) + NORM_EPS) * gf_ref[...]


def _peer_experts(x1, h2, i1, i2, gate, eu, ev, gf, tm, te):
    m = x1.shape[0]
    row = lambda w: pl.BlockSpec((tm, w), lambda i, j: (i, 0))
    tab = pl.BlockSpec((te, D_MODEL), lambda i, j: (j, 0))
    tab_t = pl.BlockSpec((D_MODEL, te), lambda i, j: (0, j))
    x1_spec = pl.BlockSpec((tm, D_MODEL), lambda i, j: (i, 0), pipeline_mode=pl.Buffered(1))
    return pl.pallas_call(
        _peer_expert_body,
        grid=(m // tm, N_EXPERTS // te),
        in_specs=[x1_spec, row(D_MODEL), row(N_SLOTS), row(N_SLOTS), row(N_SLOTS), tab_t, tab,
                  pl.BlockSpec((1, D_MODEL), lambda i, j: (0, 0))],
        out_specs=row(D_MODEL),
        out_shape=jax.ShapeDtypeStruct((m, D_MODEL), F32),
        scratch_shapes=[pltpu.VMEM((N_KEYS * (tm + W_PITCH_PAD), N_KEYS), F32)],
        compiler_params=_params(("parallel", "arbitrary"), 60),
        name="peer_experts",
    )(x1, h2, i1, i2, gate, eu, ev, gf)


def _pad_cols(a, width):
    return jnp.pad(a, [(0, 0)] * (a.ndim - 1) + [(0, width - a.shape[-1])])


def _prep_layer_params(l, norm1_g, w_in, b_in, attn_sinks, shift_mu, w0, w2, a0, a2, g2, k_k, k_a, r_k, ln_x_w,
                       ln_x_b, w_up_a, w_up_b, w_o, norm2_g, w_query, sub_keys, expert_u, expert_v):
    wi, bi = w_in[l], b_in[l][None, :]
    cat = lambda z: jnp.concatenate([z[:, :OFF_B], _pad_cols(z[:, OFF_B:OFF_GATE], RWKV_PAD), z[:, OFF_GATE:]], axis=1)
    row = lambda z: z[l].reshape(1, -1)
    lora = lambda z, off: jnp.pad(z[l], ((off, LORA_PAD - off - z.shape[1]), (0, 0)))
    return dict(
        norm1_g=row(norm1_g), w_cat=cat(wi).astype(BF16), b_cat=cat(bi), sinks=attn_sinks[l],
        mu=_pad_cols(row(shift_mu), RWKV_PAD), w0=row(w0), a0=row(a0), k_k=row(k_k), k_a=row(k_a),
        w2p=lora(w2, 0), a2p=lora(a2, DECAY_LORA), g2p=lora(g2, DECAY_LORA + AAA_LORA),
        r_k=row(r_k), ln_w=row(ln_x_w), ln_b=row(ln_x_b),
        w_up_a=w_up_a[l].astype(BF16), w_up_b=w_up_b[l].astype(BF16), w_o=w_o[l].astype(BF16),
        norm2_g=row(norm2_g), w_query=w_query[l].astype(BF16), sub_keys=sub_keys[l].astype(BF16),
        expert_ut=expert_u[l].astype(BF16).T, expert_v=expert_v[l].astype(BF16),
    )


def _row_tile(m, pref):
    return pref if m % pref == 0 else m


def _hybrid_layer(x, pos, lp, eblk, gf, win_k, win_v, wkv0, shift0):
    b, t, _ = x.shape
    m = b * t
    x2 = x.reshape(m, D_MODEL)
    tm = _row_tile(m, 256)
    tmd = _row_tile(m, DENSE_TOKEN_TILE)
    rope_tab = _rope_table(pos, max(t, tmd))
    q, k, v, p, ga, gb = _in_proj(x2, lp["norm1_g"], lp["w_cat"], lp["b_cat"], rope_tab, tmd)

    k3 = k.reshape(b, t, KV_WIDTH)
    v3 = v.reshape(b, t, KV_WIDTH)
    if win_k is None:
        att = _attn_prompt(lp["sinks"], q.reshape(b, t, ATTN_WIDTH), k3, v3).reshape(m, ATTN_WIDTH)
        keep = min(WINDOW, t)
        new_k, new_v = k3[:, t - keep:], v3[:, t - keep:]
    else:
        n_past = win_k.shape[1]
        qg = q.reshape(b, t, N_KV_HEADS, Q_PER_KV, HEAD_DIM).transpose(0, 2, 3, 1, 4)
        qg = qg.reshape(b, N_KV_HEADS, Q_PER_KV * t, HEAD_DIM)
        ck = win_k.reshape(b, n_past, KV_WIDTH)
        cv = win_v.reshape(b, n_past, KV_WIDTH)
        og = _attn_sample(lp["sinks"], qg, k3, v3, ck, cv, t)
        att = og.reshape(b, N_KV_HEADS, Q_PER_KV, t, HEAD_DIM).transpose(0, 3, 1, 2, 4).reshape(m, ATTN_WIDTH)
        new_k = jnp.concatenate([ck, k3], axis=1)[:, t:]
        new_v = jnp.concatenate([cv, v3], axis=1)[:, t:]
    new_k = new_k.reshape(b, -1, N_KV_HEADS, HEAD_DIM)
    new_v = new_v.reshape(b, -1, N_KV_HEADS, HEAD_DIM)

    p3 = p.reshape(b, t, RWKV_PAD)
    if t % RWKV_CHUNK == 0:
        ch, tp = RWKV_CHUNK, t
    else:
        tp = -(-t // SUBLANES) * SUBLANES
        ch = tp
        p3 = jnp.pad(p3, ((0, 0), (0, tp - t), (0, 0)))
    shift_pad = _pad_cols(shift0, RWKV_PAD)[:, None, :]
    want = DECODE_BATCH_BLOCK if tp == ch else PROMPT_BATCH_BLOCK
    bb = want if b % want == 0 else 1
    akb, *prep = _rwkv_prep(p3, shift_pad, lp, eblk, ch, t, bb)
    tmat = _tri_solve(akb)
    rw, wkv_t = _rwkv_scan(tmat, prep, wkv0, lp, eblk, bb)
    rw = rw[:, :t].reshape(m, RWKV_WIDTH)
    shift_t = p3[:, t - 1, :RWKV_COLS]

    x1, h2, qp = _merge(x2, att, rw, ga, gb, lp, tmd)
    i1, i2, gate = _peer_topk(qp, lp["sub_keys"], tm)
    y = _peer_experts(x1, h2, i1, i2, gate, lp["expert_ut"], lp["expert_v"], gf,
                      _row_tile(m, EXPERT_TOKEN_TILE), EXPERT_TILE)
    return y.reshape(b, t, D_MODEL), new_k, new_v, wkv_t, shift_t


def kernel(x_prompt, x_sample, cache_win_k, cache_win_v, state_wkv, state_shift, norm1_g, w_in, b_in, attn_sinks,
           shift_mu, w0, w2, a0, a2, g2, k_k, k_a, r_k, ln_x_w, ln_x_b, w_up_a, w_up_b, w_o, norm2_g, w_query,
           sub_keys, expert_u, expert_v, final_norm_g):
    depth = w_in.shape[0]
    assert depth == 1, "the final RMSNorm is fused into the last layer's expert kernel"
    pos_p = jnp.arange(x_prompt.shape[1])
    pos_s = PAST_LEN + jnp.arange(x_sample.shape[1])
    li = lax.broadcasted_iota(I32, (RWKV_WIDTH, RWKV_WIDTH), 0) // RWKV_HEAD_DIM
    lj = lax.broadcasted_iota(I32, (RWKV_WIDTH, RWKV_WIDTH), 1) // RWKV_HEAD_DIM
    eblk = (li == lj).astype(BF16)
    gf = final_norm_g.reshape(1, D_MODEL)
    lp = _prep_layer_params(0, norm1_g, w_in, b_in, attn_sinks, shift_mu, w0, w2, a0, a2, g2, k_k, k_a, r_k, ln_x_w,
                            ln_x_b, w_up_a, w_up_b, w_o, norm2_g, w_query, sub_keys, expert_u, expert_v)
    bp = x_prompt.shape[0]
    wkv0 = jnp.zeros((bp, RWKV_HEADS, RWKV_HEAD_DIM, RWKV_HEAD_DIM), F32)
    shift0 = jnp.zeros((bp, RWKV_COLS), x_prompt.dtype)
    yp, pk, pv, pw, psh = _hybrid_layer(x_prompt, pos_p, lp, eblk, gf, None, None, wkv0, shift0)
    ys, sk, sv, sw, ssh = _hybrid_layer(x_sample, pos_s, lp, eblk, gf, cache_win_k[0], cache_win_v[0],
                                        state_wkv[0], state_shift[0])
    st = lambda z: z[None]
    return (yp, ys, st(pk), st(pv), st(pw), st(psh), st(sk), st(sv), st(sw), st(ssh))
```

```python
import functools

import jax
import jax.numpy as jnp
from jax import lax
from jax.experimental import pallas as pl
from jax.experimental.pallas import tpu as pltpu

F32, BF16, I32 = jnp.float32, jnp.bfloat16, jnp.int32

D_MODEL = 1024
NORM_EPS = 1e-5
HEAD_DIM = 64
N_Q_HEADS = 8
N_KV_HEADS = 2
Q_PER_KV = N_Q_HEADS // N_KV_HEADS
ATTN_WIDTH = N_Q_HEADS * HEAD_DIM
KV_WIDTH = N_KV_HEADS * HEAD_DIM
WINDOW = 128
ROT_DIM = HEAD_DIM // 4
ROPE_THETA = 500000.0
ATTN_SCALE = HEAD_DIM ** -0.5
RWKV_HEADS = 8
RWKV_HEAD_DIM = 64
RWKV_WIDTH = RWKV_HEADS * RWKV_HEAD_DIM
DECAY_LORA = 32
AAA_LORA = 32
GATE_LORA = 96
GN_EPS = 64e-5
RWKV_COLS = 3 * RWKV_WIDTH + DECAY_LORA + AAA_LORA + GATE_LORA
LORA_PAD = 256
RWKV_PAD = 3 * RWKV_WIDTH + LORA_PAD
OFF_K = ATTN_WIDTH
OFF_V = OFF_K + KV_WIDTH
OFF_B = OFF_V + KV_WIDTH
OFF_GATE = OFF_B + RWKV_COLS
N_KEYS = 128
N_EXPERTS = N_KEYS * N_KEYS
PEER_HEADS = 8
PEER_TOPK = 16
D_KEY_HALF = 128
N_SLOTS = PEER_HEADS * PEER_TOPK
PAST_LEN = 16384
RWKV_CHUNK = 64
LANES = 128
SUBLANES = 8
SOLVE_COLS = 16
TOKEN_UNROLL = 32
DECODE_BATCH_BLOCK = 16
PROMPT_BATCH_BLOCK = 8
DENSE_TOKEN_TILE = 512
EXPERT_TOKEN_TILE = 512
EXPERT_TILE = 1024
EXPERT_ROW_BLOCK = 512
W_PITCH_PAD = 8
NEG_INF = float("-inf")

NN = ((1,), (0,))
NT = ((1,), (1,))
TN = ((0,), (0,))


def _dot(a, b, dims=NN):
    return lax.dot_general(a, b, (dims, ((), ())), preferred_element_type=F32)


def _split2(x):
    hi = x.astype(BF16)
    return hi, (x - hi.astype(F32)).astype(BF16)


def _doth(a, b, dims=NN):
    a_hi, a_lo = _split2(a)
    b_hi, b_lo = _split2(b)
    return _dot(a_hi, b_hi, dims) + _dot(a_hi, b_lo, dims) + _dot(a_lo, b_hi, dims)


def _dotb(a, b, dims=NN):
    return _dot(a.astype(BF16), b.astype(BF16), dims)


def _split3(x):
    hi = x.astype(BF16)
    r1 = x - hi.astype(F32)
    mid = r1.astype(BF16)
    lo = (r1 - mid.astype(F32)).astype(BF16)
    return hi, mid, lo


def _dot_exact_rhs(x, e):
    hi, mid, lo = _split3(x)
    return _dot(hi, e) + _dot(mid, e) + _dot(lo, e)


def _dot_exact_lhs(e, x):
    hi, mid, lo = _split3(x)
    return _dot(e, hi) + _dot(e, mid) + _dot(e, lo)


def _params(sem, vmem_mb):
    return pltpu.CompilerParams(dimension_semantics=sem, vmem_limit_bytes=vmem_mb * 1024 * 1024)


def _full(a):
    nd = a.ndim
    return pl.BlockSpec(a.shape, lambda *_: (0,) * nd)


SEG_QKV = ATTN_WIDTH + 2 * KV_WIDTH
SEG_P = SEG_QKV + RWKV_PAD
SEG_END = SEG_P + 2 * D_MODEL


def _inproj_body(x_ref, g_ref, w_ref, b_ref, rope_ref, q_ref, k_ref, v_ref, p_ref, ga_ref, gb_ref):
    x = x_ref[...]
    h = (x * lax.rsqrt(jnp.mean(x * x, axis=-1, keepdims=True) + NORM_EPS) * g_ref[...]).astype(BF16)

    def seg(lo, hi):
        return _dot(h, w_ref[:, lo:hi]) + b_ref[:, lo:hi]

    cos = rope_ref[:, 0:LANES]
    sin_up = rope_ref[:, LANES:2 * LANES]
    sin_dn = rope_ref[:, 2 * LANES:3 * LANES]

    def rope(z):
        return z * cos + pltpu.roll(z, ROT_DIM // 2, 1) * sin_up + pltpu.roll(z, LANES - ROT_DIM // 2, 1) * sin_dn

    for c in range(ATTN_WIDTH // LANES):
        q_ref[:, c * LANES:(c + 1) * LANES] = rope(seg(c * LANES, (c + 1) * LANES))
    k_ref[...] = rope(seg(OFF_K, OFF_V))
    v_ref[...] = seg(OFF_V, SEG_QKV)
    p_ref[...] = seg(SEG_QKV, SEG_P)
    ga_ref[...] = jax.nn.sigmoid(seg(SEG_P, SEG_P + D_MODEL))
    gb_ref[...] = jax.nn.sigmoid(seg(SEG_P + D_MODEL, SEG_END))


def _in_proj(x2, g1, w_cat, b_cat, rope_tab, tm):
    m = x2.shape[0]
    nrep = rope_tab.shape[0] // tm
    widths = (ATTN_WIDTH, KV_WIDTH, KV_WIDTH, RWKV_PAD, D_MODEL, D_MODEL)
    return pl.pallas_call(
        _inproj_body,
        grid=(m // tm,),
        in_specs=[
            pl.BlockSpec((tm, D_MODEL), lambda i: (i, 0)),
            _full(g1), _full(w_cat), _full(b_cat),
            pl.BlockSpec((tm, 3 * LANES), lambda i: (i % nrep, 0)),
        ],
        out_specs=[pl.BlockSpec((tm, w), lambda i: (i, 0)) for w in widths],
        out_shape=[jax.ShapeDtypeStruct((m, w), F32) for w in widths],
        compiler_params=_params(("parallel",), 56),
        name="in_proj",
    )(x2, g1, w_cat, b_cat, rope_tab)


def _rope_table(pos, rows):
    t = pos.shape[0]
    half = ROT_DIM // 2
    inv_freq = ROPE_THETA ** (-jnp.arange(half, dtype=F32) / half)
    ang = pos.astype(F32)[:, None] * inv_freq[None, :]
    cos, sin = jnp.cos(ang), jnp.sin(ang)
    rest = HEAD_DIM - ROT_DIM
    z8 = jnp.zeros((t, half), F32)
    cos64 = jnp.concatenate([cos, cos, jnp.ones((t, rest), F32)], axis=1)
    up64 = jnp.concatenate([z8, sin, jnp.zeros((t, rest), F32)], axis=1)
    dn64 = jnp.concatenate([-sin, z8, jnp.zeros((t, rest), F32)], axis=1)
    tab = jnp.concatenate([jnp.tile(a, (1, LANES // HEAD_DIM)) for a in (cos64, up64, dn64)], axis=1)
    if rows > t:
        tab = jnp.tile(tab, (rows // t, 1))
    return tab


def _softmax_av(s, sink, vv):
    m = jnp.maximum(jnp.max(s, axis=-1, keepdims=True), sink)
    p = jnp.exp(s - m)
    den = jnp.sum(p, axis=-1, keepdims=True) + jnp.exp(sink - m)
    return _dot(p.astype(BF16), vv) / den


def _attn_prompt_body(sink_ref, q_ref, kc_ref, kp_ref, vc_ref, vp_ref, o_ref):
    n = pl.program_id(1)
    blk = q_ref.shape[0]
    i = lax.broadcasted_iota(I32, (blk, 2 * blk), 0)
    j = lax.broadcasted_iota(I32, (blk, 2 * blk), 1)
    diff = blk + i - j
    j_lo = jnp.where(n > 0, 0, blk)
    mask = (diff >= 0) & (diff < WINDOW) & (j >= j_lo)
    outs = []
    for g in range(N_KV_HEADS):
        sl = slice(g * HEAD_DIM, (g + 1) * HEAD_DIM)
        kk = jnp.concatenate([kp_ref[:, sl], kc_ref[:, sl]], axis=0).astype(BF16)
        vv = jnp.concatenate([vp_ref[:, sl], vc_ref[:, sl]], axis=0).astype(BF16)
        for hq in range(Q_PER_KV):
            h = g * Q_PER_KV + hq
            qh = q_ref[:, h * HEAD_DIM:(h + 1) * HEAD_DIM].astype(BF16)
            s = jnp.where(mask, _dot(qh, kk, NT) * ATTN_SCALE, NEG_INF)
            outs.append(_softmax_av(s, sink_ref[h], vv))
    o_ref[...] = jnp.concatenate(outs, axis=1).astype(o_ref.dtype)


def _attn_prompt(sinks, q3, k3, v3):
    b, t, _ = q3.shape
    blk = WINDOW
    cur = lambda w: pl.BlockSpec((None, blk, w), lambda bi, n: (bi, n, 0))
    prev = lambda w: pl.BlockSpec((None, blk, w), lambda bi, n: (bi, jnp.maximum(n - 1, 0), 0))
    return pl.pallas_call(
        _attn_prompt_body,
        grid=(b, t // blk),
        in_specs=[pl.BlockSpec(memory_space=pltpu.SMEM), cur(ATTN_WIDTH), cur(KV_WIDTH), prev(KV_WIDTH),
                  cur(KV_WIDTH), prev(KV_WIDTH)],
        out_specs=cur(ATTN_WIDTH),
        out_shape=jax.ShapeDtypeStruct((b, t, ATTN_WIDTH), BF16),
        compiler_params=_params(("parallel", "arbitrary"), 32),
        name="attn_prompt",
    )(sinks, q3, k3, k3, v3, v3)


def _attn_sample_body(sink_ref, *refs, t):
    for bi in range(refs[0].shape[0]):
        _attn_sample_one(sink_ref, *[r.at[bi] for r in refs], t=t)


def _attn_sample_one(sink_ref, q_ref, kn_ref, vn_ref, kc_ref, vc_ref, o_ref, *, t):
    rows = Q_PER_KV * t
    n_past = kc_ref.shape[0]
    ri = lax.broadcasted_iota(I32, (rows, 1), 0)
    qi = ri % t
    jc = lax.broadcasted_iota(I32, (rows, n_past), 1)
    mask_c = (n_past + qi - jc) < WINDOW
    for g in range(N_KV_HEADS):
        sl = slice(g * HEAD_DIM, (g + 1) * HEAD_DIM)
        sink = jnp.zeros((rows, 1), F32)
        for hq in range(Q_PER_KV):
            sink = jnp.where(ri // t == hq, sink_ref[g * Q_PER_KV + hq], sink)
        qg = q_ref[g]
        qb = qg.astype(BF16)
        kn = kn_ref[:, sl]
        vn = vn_ref[:, sl]
        s_c = jnp.where(mask_c, _dot(qb, kc_ref[:, sl].astype(BF16), NT) * ATTN_SCALE, NEG_INF)
        s_n = []
        for jn in range(t):
            sj = jnp.sum(qg * kn[jn:jn + 1, :], axis=-1, keepdims=True) * ATTN_SCALE
            s_n.append(jnp.where(qi >= jn, sj, NEG_INF))
        m = jnp.maximum(jnp.max(s_c, axis=-1, keepdims=True), sink)
        for sj in s_n:
            m = jnp.maximum(m, sj)
        p_c = jnp.exp(s_c - m)
        den = jnp.sum(p_c, axis=-1, keepdims=True) + jnp.exp(sink - m)
        o = _dot(p_c.astype(BF16), vc_ref[:, sl].astype(BF16))
        for jn in range(t):
            p_n = jnp.exp(s_n[jn] - m)
            den = den + p_n
            o = o + p_n * vn[jn:jn + 1, :]
        o_ref[g] = (o / den).astype(o_ref.dtype)


def _attn_sample(sinks, qg, kn, vn, cache_k, cache_v, t):
    b = qg.shape[0]
    rows = Q_PER_KV * t
    n_past = cache_k.shape[1]
    bb = DECODE_BATCH_BLOCK if b % DECODE_BATCH_BLOCK == 0 else 1
    qspec = pl.BlockSpec((bb, N_KV_HEADS, rows, HEAD_DIM), lambda bi: (bi, 0, 0, 0))
    nspec = pl.BlockSpec((bb, t, KV_WIDTH), lambda bi: (bi, 0, 0))
    cspec = pl.BlockSpec((bb, n_past, KV_WIDTH), lambda bi: (bi, 0, 0))
    return pl.pallas_call(
        functools.partial(_attn_sample_body, t=t),
        grid=(b // bb,),
        in_specs=[pl.BlockSpec(memory_space=pltpu.SMEM), qspec, nspec, nspec, cspec, cspec],
        out_specs=qspec,
        out_shape=jax.ShapeDtypeStruct((b, N_KV_HEADS, rows, HEAD_DIM), BF16),
        compiler_params=_params(("parallel",), 32),
        name="attn_sample",
    )(sinks, qg, kn, vn, cache_k, cache_v)


def _stack_heads(z):
    ch, width = z.shape
    rows = lax.broadcasted_iota(I32, (RWKV_HEADS * ch, width), 0) // ch
    lanes = lax.broadcasted_iota(I32, (RWKV_HEADS * ch, width), 1) // RWKV_HEAD_DIM
    return jnp.where(rows == lanes, jnp.concatenate([z] * RWKV_HEADS, axis=0), 0.0)


def _head_pack(ch):
    return LANES // ch if ch >= RWKV_HEAD_DIM else 1


def _rwkv_prep_body(p_ref, pprev_ref, shift_ref, mu_ref, w0_ref, a0_ref, kk_ref, ka_ref, w2_ref, a2_ref, g2_ref,
                    eblk_ref, *outs, t_real):
    c = pl.program_id(1)
    nb, ch = p_ref.shape[0], p_ref.shape[1]
    row = lax.broadcasted_iota(I32, (ch, 1), 0)
    xms = []
    for bi in range(nb):
        pc = p_ref[bi]
        first = jnp.where(c == 0, shift_ref[bi], pprev_ref[bi, SUBLANES - 1:SUBLANES, :])
        prev = jnp.where(row == 0, first, pltpu.roll(pc, 1, 0))
        xms.append(pc + mu_ref[...] * (prev - pc))
    xm = jnp.concatenate(xms, axis=0)
    valid = (c * ch + lax.broadcasted_iota(I32, (nb * ch, 1), 0) % ch) < t_real
    w3 = 3 * RWKV_WIDTH
    r = xm[:, 0:RWKV_WIDTH]
    k = xm[:, RWKV_WIDTH:2 * RWKV_WIDTH]
    v = xm[:, 2 * RWKV_WIDTH:w3]
    lo = xm[:, w3:w3 + LORA_PAD]
    lane = lax.broadcasted_iota(I32, lo.shape, 1)
    act = jnp.where(lane < DECAY_LORA, jnp.tanh(lo),
                    jnp.where(lane < DECAY_LORA + AAA_LORA, lo, jax.nn.sigmoid(lo)))
    z = w0_ref[...] + _doth(act, w2_ref[...])
    w_log = -(jnp.maximum(-z, 0.0) + jnp.log1p(jnp.exp(-jnp.abs(z)))) - 0.5
    lw = jnp.where(valid, -jnp.exp(w_log), 0.0)
    a = jax.nn.sigmoid(a0_ref[...] + _doth(act, a2_ref[...]))
    g = _dot(act.astype(BF16), g2_ref[...].astype(BF16))
    kk = k * kk_ref[...]
    ss = _dot_exact_rhs(kk * kk, eblk_ref[...])
    kk = jnp.where(valid, kk * lax.rsqrt(jnp.maximum(ss, 1e-24)), 0.0)
    kmod = jnp.where(valid, k * (1.0 + (a - 1.0) * ka_ref[...]), 0.0)
    bb = kk * a
    for bi in range(nb):
        rows = slice(bi * ch, (bi + 1) * ch)
        _rwkv_prep_chunk(lw[rows], kk[rows], kmod[rows], bb[rows], r[rows], v[rows], g[rows],
                         *[o.at[bi] for o in outs])


def _rwkv_prep_chunk(lw, kk, kmod, bb, r, v, g, akb_ref, akk_ref, qrk_ref, qrb_ref, kkt_ref, rt_ref, kdec_ref,
                     bdec_ref, kmod_ref, r_ref, v_ref, g_ref, gamc_ref):
    ch = lw.shape[0]
    ti = lax.broadcasted_iota(I32, (ch, ch), 0)
    tj = lax.broadcasted_iota(I32, (ch, ch), 1)
    tri = jnp.where(ti >= tj, 1.0, 0.0).astype(BF16)
    cum = _dot_exact_lhs(tri, lw)
    cum_c = cum[ch - 1:ch, :]
    e_neg = jnp.exp(-cum)
    e_rem = jnp.exp(cum_c - cum)
    kkt = kk * jnp.exp(cum - lw)
    rt = r * jnp.exp(cum)
    khat = kmod * e_neg
    bhat = bb * e_neg
    strict = ti > tj
    incl = ti >= tj
    hs = range(RWKV_HEADS)
    sl = [slice(h * RWKV_HEAD_DIM, (h + 1) * RWKV_HEAD_DIM) for h in hs]
    split = lambda z: [z[:, s].astype(BF16) for s in sl]
    kkt_s, rt_s, khat_s, bhat_s = split(kkt), split(rt), split(khat), split(bhat)
    if ch < RWKV_HEAD_DIM:
        kkt_bd, rt_bd = _stack_heads(kkt).astype(BF16), _stack_heads(rt).astype(BF16)
        blocks = lambda m: [m[h * ch:(h + 1) * ch, :] for h in hs]
        akb = blocks(_dot(kkt_bd, bhat.astype(BF16), NT))
        akk = blocks(_dot(kkt_bd, khat.astype(BF16), NT))
        qrk = blocks(_dot(rt_bd, khat.astype(BF16), NT))
        qrb = blocks(_dot(rt_bd, bhat.astype(BF16), NT))
    else:
        akb = [_dot(kkt_s[h], bhat_s[h], NT) for h in hs]
        akk = [_dot(kkt_s[h], khat_s[h], NT) for h in hs]
        qrk = [_dot(rt_s[h], khat_s[h], NT) for h in hs]
        qrb = [_dot(rt_s[h], bhat_s[h], NT) for h in hs]
    pack = _head_pack(ch)
    for gi in range(RWKV_HEADS // pack):
        grp = range(gi * pack, (gi + 1) * pack)
        side = lambda mats, keep: jnp.concatenate([jnp.where(keep, mats[h], 0.0) for h in grp], axis=1)
        akb_ref[gi] = side(akb, strict)
        akk_ref[gi] = side(akk, strict)
        qrk_ref[gi] = side(qrk, incl)
        qrb_ref[gi] = side(qrb, incl)
    kkt_ref[...] = kkt
    rt_ref[...] = rt
    kdec_ref[...] = kmod * e_rem
    bdec_ref[...] = bb * e_rem
    kmod_ref[...] = kmod
    r_ref[...] = r
    v_ref[...] = v
    g_ref[...] = g
    gamc_ref[...] = jnp.exp(cum_c)


def _rwkv_prep(p3, shift0, lp, eblk, ch, t_real, bb):
    b, tp, _ = p3.shape
    nc = tp // ch
    per8 = ch // SUBLANES
    pack = _head_pack(ch)
    ng, gw = RWKV_HEADS // pack, pack * ch
    mats = jax.ShapeDtypeStruct((b, nc, ng, ch, gw), F32)
    rows = jax.ShapeDtypeStruct((b, tp, RWKV_WIDTH), F32)
    mspec = pl.BlockSpec((bb, None, ng, ch, gw), lambda bi, ci: (bi, ci, 0, 0, 0))
    rspec = pl.BlockSpec((bb, ch, RWKV_WIDTH), lambda bi, ci: (bi, ci, 0))
    consts = (lp["mu"], lp["w0"], lp["a0"], lp["k_k"], lp["k_a"], lp["w2p"], lp["a2p"], lp["g2p"], eblk)
    return pl.pallas_call(
        functools.partial(_rwkv_prep_body, t_real=t_real),
        grid=(b // bb, nc),
        in_specs=[
            pl.BlockSpec((bb, ch, RWKV_PAD), lambda bi, ci: (bi, ci, 0)),
            pl.BlockSpec((bb, SUBLANES, RWKV_PAD), lambda bi, ci: (bi, jnp.maximum(ci * per8 - 1, 0), 0)),
            pl.BlockSpec((bb, 1, RWKV_PAD), lambda bi, ci: (bi, 0, 0)),
        ] + [_full(a) for a in consts],
        out_specs=[mspec] * 4 + [rspec] * 8 + [pl.BlockSpec((bb, None, 1, RWKV_WIDTH), lambda bi, ci: (bi, ci, 0, 0))],
        out_shape=[mats] * 4 + [rows] * 8 + [jax.ShapeDtypeStruct((b, nc, 1, RWKV_WIDTH), F32)],
        compiler_params=_params(("parallel", "parallel"), 48),
        name="rwkv_prep",
    )(p3, p3, shift0, *consts)


def _tri_solve_body(a_ref, t_ref, w_scr, *, ch, pack):
    k = pl.program_id(1)
    kc_w = a_ref.shape[1]
    nk = (pl.num_programs(1) - 1) // 2
    groups = a_ref.shape[0] // LANES
    gw = pack * ch

    @pl.when(k < nk)
    def _():
        xt = a_ref[...].T
        for g in range(groups):
            w_scr[pl.ds(k * (kc_w * groups) + g, kc_w, stride=groups), :] = xt[:, g * LANES:(g + 1) * LANES]

    @pl.when(k == nk)
    def _():
        cols = min(SOLVE_COLS, ch)
        cid = lax.broadcasted_iota(I32, (cols * groups, LANES), 0) // groups
        at = lambda e, n: pl.ds(pl.multiple_of(e * groups, groups), n * groups)

        for j in range(pack):
            def row_body(s, carry, j=j):
                for kc in range(ch // cols):
                    c0 = kc * cols

                    @pl.when(s >= c0)
                    def _():
                        acc0 = jnp.where(cid == s - c0, 1.0, 0.0).astype(F32)

                        def r_body(r, acc):
                            a_sr = w_scr[at(s * gw + j * ch + r, 1), :]
                            return acc - jnp.tile(a_sr, (cols, 1)) * w_scr[at(r * gw + j * ch + c0, cols), :]

                        w_scr[at(s * gw + j * ch + c0, cols), :] = lax.fori_loop(c0, s, r_body, acc0)
                return carry

            lax.fori_loop(0, ch, row_body, 0)

    @pl.when(k > nk)
    def _():
        kk = k - nk - 1
        parts = [w_scr[pl.ds(kk * (kc_w * groups) + g, kc_w, stride=groups), :] for g in range(groups)]
        t_ref[...] = jnp.concatenate(parts, axis=1).T


def _tri_solve(akb):
    b, nc, ng, ch, gw = akb.shape
    nrow = b * nc * ng
    slab = SUBLANES * LANES
    ne = ch * gw
    row_pad = -nrow % slab
    ne_pad = -ne % LANES
    a2 = akb.reshape(nrow, ne)
    if row_pad or ne_pad:
        a2 = jnp.pad(a2, ((0, row_pad), (0, ne_pad)))
    nk = (ne + ne_pad) // LANES
    t2 = pl.pallas_call(
        functools.partial(_tri_solve_body, ch=ch, pack=gw // ch),
        grid=((nrow + row_pad) // slab, 2 * nk + 1),
        in_specs=[pl.BlockSpec((slab, LANES), lambda gi, k: (gi, jnp.minimum(k, nk - 1)))],
        out_specs=pl.BlockSpec((slab, LANES), lambda gi, k: (gi, jnp.clip(k - nk - 1, 0, nk - 1))),
        out_shape=jax.ShapeDtypeStruct(a2.shape, F32),
        scratch_shapes=[pltpu.VMEM(((ne + ne_pad) * SUBLANES, LANES), F32)],
        compiler_params=_params(("parallel", "arbitrary"), 48),
        name="rwkv_tri_solve",
    )(a2)
    if row_pad or ne_pad:
        t2 = t2[:nrow, :ne]
    return t2.reshape(b, nc, ng, ch, gw)


def _rwkv_scan_body(t_ref, akk_ref, qrk_ref, qrb_ref, kkt_ref, rt_ref, kdec_ref, bdec_ref, kmod_ref, r_ref, v_ref,
                    g_ref, gamc_ref, wkv0_ref, eblk_ref, rk_ref, lnw_ref, lnb_ref, rw_ref, wkv_ref, s_scr):
    nb, ch = kkt_ref.shape[0], kkt_ref.shape[1]
    nd = RWKV_HEAD_DIM
    per_row = (t_ref, akk_ref, qrk_ref, qrb_ref, kkt_ref, rt_ref, kdec_ref, bdec_ref, v_ref, gamc_ref, wkv0_ref,
               wkv_ref, s_scr)
    y = jnp.concatenate([_rwkv_scan_chunk(*[r.at[bi] for r in per_row]) for bi in range(nb)], axis=0)
    rows = lambda ref: ref[...].reshape(nb * ch, RWKV_WIDTH)
    eblk = eblk_ref[...]
    inv_n = 1.0 / nd
    mean = _dot_exact_rhs(y, eblk) * inv_n
    yc = y - mean
    var = _dot_exact_rhs(yc * yc, eblk) * inv_n
    yn = yc * lax.rsqrt(var + GN_EPS) * lnw_ref[...] + lnb_ref[...]
    bonus = _dot_exact_rhs(rows(r_ref) * rows(kmod_ref) * rk_ref[...], eblk) * rows(v_ref)
    rw_ref[...] = ((yn + bonus) * rows(g_ref)).reshape(nb, ch, RWKV_WIDTH).astype(rw_ref.dtype)


def _rwkv_scan_chunk(t_ref, akk_ref, qrk_ref, qrb_ref, kkt_ref, rt_ref, kdec_ref, bdec_ref, v_ref, gamc_ref,
                     wkv0_ref, wkv_ref, s_scr):
    c = pl.program_id(1)
    nd = RWKV_HEAD_DIM

    @pl.when(c == 0)
    def _():
        s_scr[...] = wkv0_ref[...]

    hs = range(RWKV_HEADS)
    sl = [slice(h * nd, (h + 1) * nd) for h in hs]
    ch = kkt_ref.shape[0]
    pack = _head_pack(ch)
    mat = lambda ref, h: ref[h // pack][:, (h % pack) * ch:(h % pack + 1) * ch]
    tmat = [mat(t_ref, h) for h in hs]
    vh = [v_ref[:, sl[h]] for h in hs]
    bdec = [bdec_ref[:, sl[h]] for h in hs]
    qrb = [mat(qrb_ref, h) for h in hs]
    w1 = [_dotb(tmat[h], kkt_ref[:, sl[h]]) for h in hs]
    av = [_dotb(mat(akk_ref, h), vh[h]) for h in hs]
    qv = [_dotb(mat(qrk_ref, h), vh[h]) for h in hs]
    vk = [_dotb(vh[h], kdec_ref[:, sl[h]], TN) for h in hs]
    w2 = [_dotb(tmat[h], av[h]) for h in hs]
    qw1 = [_dotb(qrb[h], w1[h]) for h in hs]
    bw1 = [_dotb(bdec[h], w1[h], TN) for h in hs]
    qw2 = [_dotb(qrb[h], w2[h]) for h in hs]
    wb = [_dotb(w2[h], bdec[h], TN) for h in hs]
    s0 = [s_scr[h] for h in hs]
    ys = [_dotb(rt_ref[:, sl[h]] - qw1[h], s0[h], NT) + (qv[h] - qw2[h]) for h in hs]
    for h in hs:
        s_scr[h] = s0[h] * gamc_ref[:, sl[h]] - _dotb(s0[h], bw1[h], NT) + (vk[h] - wb[h])
    @pl.when(c == pl.num_programs(1) - 1)
    def _():
        wkv_ref[...] = s_scr[...]

    return jnp.concatenate(ys, axis=1)


def _rwkv_scan(tmat, prep, wkv0, lp, eblk, bb):
    akk, qrk, qrb, kkt, rt, kdec, bdec, kmod, r, v, g, gamc = prep
    b, nc, ng, ch, gw = tmat.shape
    nh = RWKV_HEADS
    tp = kkt.shape[1]
    nd = RWKV_HEAD_DIM
    mspec = pl.BlockSpec((bb, None, ng, ch, gw), lambda bi, ci: (bi, ci, 0, 0, 0))
    rspec = pl.BlockSpec((bb, ch, RWKV_WIDTH), lambda bi, ci: (bi, ci, 0))
    sspec = pl.BlockSpec((bb, nh, nd, nd), lambda bi, ci: (bi, 0, 0, 0))
    consts = (eblk, lp["r_k"], lp["ln_w"], lp["ln_b"])
    return pl.pallas_call(
        _rwkv_scan_body,
        grid=(b // bb, nc),
        in_specs=[mspec] * 4 + [rspec] * 8
        + [pl.BlockSpec((bb, None, 1, RWKV_WIDTH), lambda bi, ci: (bi, ci, 0, 0)), sspec]
        + [_full(a) for a in consts],
        out_specs=[rspec, sspec],
        out_shape=[jax.ShapeDtypeStruct((b, tp, RWKV_WIDTH), BF16), jax.ShapeDtypeStruct((b, nh, nd, nd), F32)],
        scratch_shapes=[pltpu.VMEM((bb, nh, nd, nd), F32)],
        compiler_params=_params(("parallel", "arbitrary"), 48),
        name="rwkv_scan",
    )(tmat, akk, qrk, qrb, kkt, rt, kdec, bdec, kmod, r, v, g, gamc, wkv0, *consts)


def _merge_body(x_ref, att_ref, rw_ref, ga_ref, gb_ref, wa_ref, wb_ref, wo_ref, g2_ref, wq_ref,
                x1_ref, h2_ref, qp_ref):
    merged = ga_ref[...] * _dot(att_ref[...], wa_ref[...]) + gb_ref[...] * _dot(rw_ref[...], wb_ref[...])
    x1 = x_ref[...] + _dot(merged.astype(BF16), wo_ref[...])
    x1_ref[...] = x1
    h2 = (x1 * lax.rsqrt(jnp.mean(x1 * x1, axis=-1, keepdims=True) + NORM_EPS) * g2_ref[...]).astype(BF16)
    h2_ref[...] = h2
    qp_ref[...] = _dot(h2, wq_ref[...]).astype(qp_ref.dtype)


def _merge(x2, att, rw, ga, gb, lp, tm):
    m = x2.shape[0]
    kq = lp["w_query"].shape[1]
    row = lambda w: pl.BlockSpec((tm, w), lambda i: (i, 0))
    consts = (lp["w_up_a"], lp["w_up_b"], lp["w_o"], lp["norm2_g"], lp["w_query"])
    return pl.pallas_call(
        _merge_body,
        grid=(m // tm,),
        in_specs=[row(D_MODEL), row(ATTN_WIDTH), row(RWKV_WIDTH), row(D_MODEL), row(D_MODEL)]
        + [_full(a) for a in consts],
        out_specs=[row(D_MODEL), row(D_MODEL), row(kq)],
        out_shape=[jax.ShapeDtypeStruct((m, D_MODEL), F32), jax.ShapeDtypeStruct((m, D_MODEL), BF16),
                   jax.ShapeDtypeStruct((m, kq), BF16)],
        compiler_params=_params(("parallel",), 56),
        name="merge",
    )(x2, att, rw, ga, gb, *consts)


def _argmax_rows(src_ref):
    nslab = src_ref.shape[0] // SUBLANES
    row0 = lax.broadcasted_iota(I32, (SUBLANES, src_ref.shape[1]), 0).astype(F32)
    vals = [src_ref[i * SUBLANES:(i + 1) * SUBLANES, :] for i in range(nslab)]
    rows = [row0 + float(i * SUBLANES) for i in range(nslab)]
    while len(vals) > 1:
        nv, nr = [], []
        for a in range(0, len(vals) - 1, 2):
            take = vals[a + 1] > vals[a]
            nv.append(jnp.maximum(vals[a], vals[a + 1]))
            nr.append(jnp.where(take, rows[a + 1], rows[a]))
        if len(vals) % 2:
            nv.append(vals[-1])
            nr.append(rows[-1])
        vals, rows = nv, nr
    v, i = vals[0], rows[0]
    for shift in (4, 2, 1):
        v2 = pltpu.roll(v, shift, 0)
        i2 = pltpu.roll(i, shift, 0)
        take = (v2 > v) | ((v2 == v) & (i2 < i))
        v = jnp.where(take, v2, v)
        i = jnp.where(take, i2, i)
    return jnp.max(v, axis=0, keepdims=True), jnp.max(i, axis=0, keepdims=True)


def _extract_top(jobs, n_out):
    def body(r, carry):
        for src_ref, val_ref, idx_ref in jobs:
            rowf = lax.broadcasted_iota(I32, src_ref.shape, 0).astype(F32)
            m, idx = _argmax_rows(src_ref)
            val_ref[pl.ds(r, 1), :] = m
            idx_ref[pl.ds(r, 1), :] = idx
            src_ref[...] = jnp.where(rowf == idx, NEG_INF, src_ref[...])
        return carry

    lax.fori_loop(0, n_out, body, 0)


def _cand_groups(k):
    groups = []
    a = 0
    while a < k and k // (a + 1) > 1:
        nb = k // (a + 1)
        groups.append((a, -(-nb // SUBLANES) * SUBLANES, nb))
        a += 1
    return groups, a


def _n_cand_rows(k):
    groups, a_tail = _cand_groups(k)
    return sum(rows for _, rows, _ in groups) + k - a_tail


def _peer_topk_head(qp_ref, sk_ref, h, i1_scr, i2_scr, gate_scr, s_scr, cand_scr, sv_scr, si_scr, bs_scr, bp_scr):
    k = PEER_TOPK
    for c in range(2):
        col = (2 * h + c) * D_KEY_HALF
        s_scr[c] = _dot(sk_ref[h, c], qp_ref[:, col:col + D_KEY_HALF], NT)
    _extract_top([(s_scr.at[c], sv_scr.at[c], si_scr.at[c]) for c in range(2)], k)
    sv0 = sv_scr[0]
    sv1 = sv_scr[1]
    groups, a_tail = _cand_groups(k)
    off = 0
    for a, rows, nb in groups:
        blk = sv0[a:a + 1, :] + sv1[0:rows, :]
        if nb < rows:
            blk = jnp.where(lax.broadcasted_iota(I32, blk.shape, 0) < nb, blk, NEG_INF)
        cand_scr[off:off + rows, :] = blk
        off += rows
    cand_scr[off:off + k - a_tail, :] = sv0[a_tail:k, :] + sv1[0:1, :]
    _extract_top([(cand_scr, bs_scr, bp_scr)], k)
    best_s = bs_scr[...]
    pos = bp_scr[...]
    a_sel = jnp.zeros_like(pos)
    b_sel = pos
    off = 0
    for a, rows, nb in groups:
        if a > 0:
            a_sel = jnp.where(pos >= off, float(a), a_sel)
            b_sel = jnp.where(pos >= off, pos - off, b_sel)
        off += rows
    a_sel = jnp.where(pos >= off, a_tail + pos - off, a_sel)
    b_sel = jnp.where(pos >= off, 0.0, b_sel)
    si0 = si_scr[0]
    si1 = si_scr[1]
    i1 = jnp.zeros_like(pos)
    i2 = jnp.zeros_like(pos)
    for a in range(k):
        i1 = jnp.where(a_sel == a, si0[a:a + 1, :], i1)
        i2 = jnp.where(b_sel == a, si1[a:a + 1, :], i2)
    e = jnp.exp(best_s - jnp.max(best_s, axis=0, keepdims=True))
    i1_scr[h * k:(h + 1) * k, :] = i1
    i2_scr[h * k:(h + 1) * k, :] = i2
    gate_scr[h * k:(h + 1) * k, :] = e / jnp.sum(e, axis=0, keepdims=True)


def _peer_topk_body(qp_ref, sk_ref, i1_ref, i2_ref, gate_ref, i1_scr, i2_scr, gate_scr, *work):
    for h in range(PEER_HEADS):
        _peer_topk_head(qp_ref, sk_ref, h, i1_scr, i2_scr, gate_scr, *work)
    i1_ref[...] = i1_scr[...].T.astype(I32)
    i2_ref[...] = i2_scr[...].T.astype(I32)
    gate_ref[...] = gate_scr[...].T


def _peer_topk(qp, sub_keys, tm):
    m = qp.shape[0]
    k = PEER_TOPK
    ospec = pl.BlockSpec((tm, N_SLOTS), lambda i: (i, 0))
    slots = pltpu.VMEM((N_SLOTS, tm), F32)
    return pl.pallas_call(
        _peer_topk_body,
        grid=(m // tm,),
        in_specs=[pl.BlockSpec((tm, qp.shape[1]), lambda i: (i, 0)), _full(sub_keys)],
        out_specs=[ospec, ospec, ospec],
        out_shape=[jax.ShapeDtypeStruct((m, N_SLOTS), I32), jax.ShapeDtypeStruct((m, N_SLOTS), I32),
                   jax.ShapeDtypeStruct((m, N_SLOTS), F32)],
        scratch_shapes=[slots, slots, slots,
                        pltpu.VMEM((2, N_KEYS, tm), F32), pltpu.VMEM((_n_cand_rows(k), tm), F32),
                        pltpu.VMEM((2, k, tm), F32), pltpu.VMEM((2, k, tm), F32),
                        pltpu.VMEM((k, tm), F32), pltpu.VMEM((k, tm), F32)],
        compiler_params=_params(("parallel",), 32),
        name="peer_topk",
    )(qp, sub_keys)


def _peer_expert_body(x1_ref, h2_ref, i1_ref, i2_ref, gate_ref, ut_ref, v_ref, gf_ref, y_ref, w_scr):
    j = pl.program_id(1)
    tm = h2_ref.shape[0]
    te = v_ref.shape[0]
    pitch = tm + W_PITCH_PAD

    @pl.when(j == 0)
    def _():
        y_ref[...] = x1_ref[...]
        sub = lax.broadcasted_iota(I32, (N_KEYS, N_SLOTS), 0)

        def tok(t, carry):
            hit1 = sub == i1_ref[pl.ds(t, 1), :]
            hit2 = sub == i2_ref[pl.ds(t, 1), :]
            a = jnp.where(hit1, gate_ref[pl.ds(t, 1), :], 0.0).astype(BF16)
            bm = jnp.where(hit2, 1.0, 0.0).astype(BF16)
            w_scr[pl.ds(t, N_KEYS, stride=pitch), :] = _dot(a, bm, NT)
            return carry

        lax.fori_loop(0, tm, tok, 0, unroll=TOKEN_UNROLL)

    rb = min(tm, EXPERT_ROW_BLOCK)
    pair = 2 * N_KEYS
    for r in range(tm // rb):
        x = h2_ref[r * rb:(r + 1) * rb, :]
        parts = []
        for q in range(te // pair):
            h = _dot(x, ut_ref[:, q * pair:(q + 1) * pair])
            i1 = j * (te // N_KEYS) + 2 * q
            w = jnp.concatenate([w_scr[pl.ds(pl.multiple_of(i1 * pitch + r * rb, SUBLANES), rb), :],
                                 w_scr[pl.ds(pl.multiple_of((i1 + 1) * pitch + r * rb, SUBLANES), rb), :]], axis=1)
            act = 0.5 * h * (1.0 + lax.erf(h * (2.0 ** -0.5)))
            parts.append((w * act).astype(BF16))
        y_ref[r * rb:(r + 1) * rb, :] += _dot(jnp.concatenate(parts, axis=1), v_ref[...])

    @pl.when(j == pl.num_programs(1) - 1)
    def _():
        x2 = y_ref[...]
        y_ref[...] = x2 * lax.rsqrt(jnp.mean(x2 * x2, axis=-1, keepdims=True) + NORM_EPS) * gf_ref[...]


def _peer_experts(x1, h2, i1, i2, gate, eu, ev, gf, tm, te):
    m = x1.shape[0]
    row = lambda w: pl.BlockSpec((tm, w), lambda i, j: (i, 0))
    tab = pl.BlockSpec((te, D_MODEL), lambda i, j: (j, 0))
    tab_t = pl.BlockSpec((D_MODEL, te), lambda i, j: (0, j))
    x1_spec = pl.BlockSpec((tm, D_MODEL), lambda i, j: (i, 0), pipeline_mode=pl.Buffered(1))
    return pl.pallas_call(
        _peer_expert_body,
        grid=(m // tm, N_EXPERTS // te),
        in_specs=[x1_spec, row(D_MODEL), row(N_SLOTS), row(N_SLOTS), row(N_SLOTS), tab_t, tab,
                  pl.BlockSpec((1, D_MODEL), lambda i, j: (0, 0))],
        out_specs=row(D_MODEL),
        out_shape=jax.ShapeDtypeStruct((m, D_MODEL), F32),
        scratch_shapes=[pltpu.VMEM((N_KEYS * (tm + W_PITCH_PAD), N_KEYS), F32)],
        compiler_params=_params(("parallel", "arbitrary"), 60),
        name="peer_experts",
    )(x1, h2, i1, i2, gate, eu, ev, gf)


def _pad_cols(a, width):
    return jnp.pad(a, [(0, 0)] * (a.ndim - 1) + [(0, width - a.shape[-1])])


def _prep_layer_params(l, norm1_g, w_in, b_in, attn_sinks, shift_mu, w0, w2, a0, a2, g2, k_k, k_a, r_k, ln_x_w,
                       ln_x_b, w_up_a, w_up_b, w_o, norm2_g, w_query, sub_keys, expert_u, expert_v):
    wi, bi = w_in[l], b_in[l][None, :]
    cat = lambda z: jnp.concatenate([z[:, :OFF_B], _pad_cols(z[:, OFF_B:OFF_GATE], RWKV_PAD), z[:, OFF_GATE:]], axis=1)
    row = lambda z: z[l].reshape(1, -1)
    lora = lambda z, off: jnp.pad(z[l], ((off, LORA_PAD - off - z.shape[1]), (0, 0)))
    return dict(
        norm1_g=row(norm1_g), w_cat=cat(wi).astype(BF16), b_cat=cat(bi), sinks=attn_sinks[l],
        mu=_pad_cols(row(shift_mu), RWKV_PAD), w0=row(w0), a0=row(a0), k_k=row(k_k), k_a=row(k_a),
        w2p=lora(w2, 0), a2p=lora(a2, DECAY_LORA), g2p=lora(g2, DECAY_LORA + AAA_LORA),
        r_k=row(r_k), ln_w=row(ln_x_w), ln_b=row(ln_x_b),
        w_up_a=w_up_a[l].astype(BF16), w_up_b=w_up_b[l].astype(BF16), w_o=w_o[l].astype(BF16),
        norm2_g=row(norm2_g), w_query=w_query[l].astype(BF16), sub_keys=sub_keys[l].astype(BF16),
        expert_ut=expert_u[l].astype(BF16).T, expert_v=expert_v[l].astype(BF16),
    )


def _row_tile(m, pref):
    return pref if m % pref == 0 else m


def _hybrid_layer(x, pos, lp, eblk, gf, win_k, win_v, wkv0, shift0):
    b, t, _ = x.shape
    m = b * t
    x2 = x.reshape(m, D_MODEL)
    tm = _row_tile(m, 256)
    tmd = _row_tile(m, DENSE_TOKEN_TILE)
    rope_tab = _rope_table(pos, max(t, tmd))
    q, k, v, p, ga, gb = _in_proj(x2, lp["norm1_g"], lp["w_cat"], lp["b_cat"], rope_tab, tmd)

    k3 = k.reshape(b, t, KV_WIDTH)
    v3 = v.reshape(b, t, KV_WIDTH)
    if win_k is None:
        att = _attn_prompt(lp["sinks"], q.reshape(b, t, ATTN_WIDTH), k3, v3).reshape(m, ATTN_WIDTH)
        keep = min(WINDOW, t)
        new_k, new_v = k3[:, t - keep:], v3[:, t - keep:]
    else:
        n_past = win_k.shape[1]
        qg = q.reshape(b, t, N_KV_HEADS, Q_PER_KV, HEAD_DIM).transpose(0, 2, 3, 1, 4)
        qg = qg.reshape(b, N_KV_HEADS, Q_PER_KV * t, HEAD_DIM)
        ck = win_k.reshape(b, n_past, KV_WIDTH)
        cv = win_v.reshape(b, n_past, KV_WIDTH)
        og = _attn_sample(lp["sinks"], qg, k3, v3, ck, cv, t)
        att = og.reshape(b, N_KV_HEADS, Q_PER_KV, t, HEAD_DIM).transpose(0, 3, 1, 2, 4).reshape(m, ATTN_WIDTH)
        new_k = jnp.concatenate([ck, k3], axis=1)[:, t:]
        new_v = jnp.concatenate([cv, v3], axis=1)[:, t:]
    new_k = new_k.reshape(b, -1, N_KV_HEADS, HEAD_DIM)
    new_v = new_v.reshape(b, -1, N_KV_HEADS, HEAD_DIM)

    p3 = p.reshape(b, t, RWKV_PAD)
    if t % RWKV_CHUNK == 0:
        ch, tp = RWKV_CHUNK, t
    else:
        tp = -(-t // SUBLANES) * SUBLANES
        ch = tp
        p3 = jnp.pad(p3, ((0, 0), (0, tp - t), (0, 0)))
    shift_pad = _pad_cols(shift0, RWKV_PAD)[:, None, :]
    want = DECODE_BATCH_BLOCK if tp == ch else PROMPT_BATCH_BLOCK
    bb = want if b % want == 0 else 1
    akb, *prep = _rwkv_prep(p3, shift_pad, lp, eblk, ch, t, bb)
    tmat = _tri_solve(akb)
    rw, wkv_t = _rwkv_scan(tmat, prep, wkv0, lp, eblk, bb)
    rw = rw[:, :t].reshape(m, RWKV_WIDTH)
    shift_t = p3[:, t - 1, :RWKV_COLS]

    x1, h2, qp = _merge(x2, att, rw, ga, gb, lp, tmd)
    i1, i2, gate = _peer_topk(qp, lp["sub_keys"], tm)
    y = _peer_experts(x1, h2, i1, i2, gate, lp["expert_ut"], lp["expert_v"], gf,
                      _row_tile(m, EXPERT_TOKEN_TILE), EXPERT_TILE)
    return y.reshape(b, t, D_MODEL), new_k, new_v, wkv_t, shift_t


def kernel(x_prompt, x_sample, cache_win_k, cache_win_v, state_wkv, state_shift, norm1_g, w_in, b_in, attn_sinks,
           shift_mu, w0, w2, a0, a2, g2, k_k, k_a, r_k, ln_x_w, ln_x_b, w_up_a, w_up_b, w_o, norm2_g, w_query,
           sub_keys, expert_u, expert_v, final_norm_g):
    depth = w_in.shape[0]
    assert depth == 1, "the final RMSNorm is fused into the last layer's expert kernel"
    pos_p = jnp.arange(x_prompt.shape[1])
    pos_s = PAST_LEN + jnp.arange(x_sample.shape[1])
    li = lax.broadcasted_iota(I32, (RWKV_WIDTH, RWKV_WIDTH), 0) // RWKV_HEAD_DIM
    lj = lax.broadcasted_iota(I32, (RWKV_WIDTH, RWKV_WIDTH), 1) // RWKV_HEAD_DIM
    eblk = (li == lj).astype(BF16)
    gf = final_norm_g.reshape(1, D_MODEL)
    lp = _prep_layer_params(0, norm1_g, w_in, b_in, attn_sinks, shift_mu, w0, w2, a0, a2, g2, k_k, k_a, r_k, ln_x_w,
                            ln_x_b, w_up_a, w_up_b, w_o, norm2_g, w_query, sub_keys, expert_u, expert_v)
    bp = x_prompt.shape[0]
    wkv0 = jnp.zeros((bp, RWKV_HEADS, RWKV_HEAD_DIM, RWKV_HEAD_DIM), F32)
    shift0 = jnp.zeros((bp, RWKV_COLS), x_prompt.dtype)
    yp, pk, pv, pw, psh = _hybrid_layer(x_prompt, pos_p, lp, eblk, gf, None, None, wkv0, shift0)
    ys, sk, sv, sw, ssh = _hybrid_layer(x_sample, pos_s, lp, eblk, gf, cache_win_k[0], cache_win_v[0],
                                        state_wkv[0], state_shift[0])
    st = lambda z: z[None]
    return (yp, ys, st(pk), st(pv), st(pw), st(psh), st(sk), st(sv), st(sw), st(ssh))
```

```python
import functools

import jax
import jax.numpy as jnp
from jax import lax
from jax.experimental import pallas as pl
from jax.experimental.pallas import tpu as pltpu

F32, BF16, I32 = jnp.float32, jnp.bfloat16, jnp.int32

D_MODEL = 1024
NORM_EPS = 1e-5
HEAD_DIM = 64
N_Q_HEADS = 8
N_KV_HEADS = 2
Q_PER_KV = N_Q_HEADS // N_KV_HEADS
ATTN_WIDTH = N_Q_HEADS * HEAD_DIM
KV_WIDTH = N_KV_HEADS * HEAD_DIM
WINDOW = 128
ROT_DIM = HEAD_DIM // 4
ROPE_THETA = 500000.0
ATTN_SCALE = HEAD_DIM ** -0.5
RWKV_HEADS = 8
RWKV_HEAD_DIM = 64
RWKV_WIDTH = RWKV_HEADS * RWKV_HEAD_DIM
DECAY_LORA = 32
AAA_LORA = 32
GATE_LORA = 96
GN_EPS = 64e-5
RWKV_COLS = 3 * RWKV_WIDTH + DECAY_LORA + AAA_LORA + GATE_LORA
LORA_PAD = 256
RWKV_PAD = 3 * RWKV_WIDTH + LORA_PAD
OFF_K = ATTN_WIDTH
OFF_V = OFF_K + KV_WIDTH
OFF_B = OFF_V + KV_WIDTH
OFF_GATE = OFF_B + RWKV_COLS
N_KEYS = 128
N_EXPERTS = N_KEYS * N_KEYS
PEER_HEADS = 8
PEER_TOPK = 16
D_KEY_HALF = 128
N_SLOTS = PEER_HEADS * PEER_TOPK
PAST_LEN = 16384
RWKV_CHUNK = 64
LANES = 128
SUBLANES = 8
SOLVE_COLS = 16
TOKEN_UNROLL = 32
DECODE_BATCH_BLOCK = 16
PROMPT_BATCH_BLOCK = 8
DENSE_TOKEN_TILE = 512
EXPERT_TOKEN_TILE = 512
EXPERT_TILE = 1024
EXPERT_ROW_BLOCK = 512
W_PITCH_PAD = 8
NEG_INF = float("-inf")

NN = ((1,), (0,))
NT = ((1,), (1,))
TN = ((0,), (0,))


def _dot(a, b, dims=NN):
    return lax.dot_general(a, b, (dims, ((), ())), preferred_element_type=F32)


def _split2(x):
    hi = x.astype(BF16)
    return hi, (x - hi.astype(F32)).astype(BF16)


def _doth(a, b, dims=NN):
    a_hi, a_lo = _split2(a)
    b_hi, b_lo = _split2(b)
    return _dot(a_hi, b_hi, dims) + _dot(a_hi, b_lo, dims) + _dot(a_lo, b_hi, dims)


def _dotb(a, b, dims=NN):
    return _dot(a.astype(BF16), b.astype(BF16), dims)


def _split3(x):
    hi = x.astype(BF16)
    r1 = x - hi.astype(F32)
    mid = r1.astype(BF16)
    lo = (r1 - mid.astype(F32)).astype(BF16)
    return hi, mid, lo


def _dot_exact_rhs(x, e):
    hi, mid, lo = _split3(x)
    return _dot(hi, e) + _dot(mid, e) + _dot(lo, e)


def _dot_exact_lhs(e, x):
    hi, mid, lo = _split3(x)
    return _dot(e, hi) + _dot(e, mid) + _dot(e, lo)


def _params(sem, vmem_mb):
    return pltpu.CompilerParams(dimension_semantics=sem, vmem_limit_bytes=vmem_mb * 1024 * 1024)


def _full(a):
    nd = a.ndim
    return pl.BlockSpec(a.shape, lambda *_: (0,) * nd)


SEG_QKV = ATTN_WIDTH + 2 * KV_WIDTH
SEG_P = SEG_QKV + RWKV_PAD
SEG_END = SEG_P + 2 * D_MODEL


def _inproj_body(x_ref, g_ref, w_ref, b_ref, rope_ref, q_ref, k_ref, v_ref, p_ref, ga_ref, gb_ref):
    x = x_ref[...]
    h = (x * lax.rsqrt(jnp.mean(x * x, axis=-1, keepdims=True) + NORM_EPS) * g_ref[...]).astype(BF16)

    def seg(lo, hi):
        return _dot(h, w_ref[:, lo:hi]) + b_ref[:, lo:hi]

    cos = rope_ref[:, 0:LANES]
    sin_up = rope_ref[:, LANES:2 * LANES]
    sin_dn = rope_ref[:, 2 * LANES:3 * LANES]

    def rope(z):
        return z * cos + pltpu.roll(z, ROT_DIM // 2, 1) * sin_up + pltpu.roll(z, LANES - ROT_DIM // 2, 1) * sin_dn

    qkv = seg(0, SEG_QKV)
    for c in range(ATTN_WIDTH // LANES):
        q_ref[:, c * LANES:(c + 1) * LANES] = rope(qkv[:, c * LANES:(c + 1) * LANES])
    k_ref[...] = rope(qkv[:, OFF_K:OFF_V])
    v_ref[...] = qkv[:, OFF_V:SEG_QKV]
    p_ref[...] = seg(SEG_QKV, SEG_P)
    ga_ref[...] = jax.nn.sigmoid(seg(SEG_P, SEG_P + D_MODEL))
    gb_ref[...] = jax.nn.sigmoid(seg(SEG_P + D_MODEL, SEG_END))


def _in_proj(x2, g1, w_cat, b_cat, rope_tab, tm):
    m = x2.shape[0]
    nrep = rope_tab.shape[0] // tm
    widths = (ATTN_WIDTH, KV_WIDTH, KV_WIDTH, RWKV_PAD, D_MODEL, D_MODEL)
    return pl.pallas_call(
        _inproj_body,
        grid=(m // tm,),
        in_specs=[
            pl.BlockSpec((tm, D_MODEL), lambda i: (i, 0)),
            _full(g1), _full(w_cat), _full(b_cat),
            pl.BlockSpec((tm, 3 * LANES), lambda i: (i % nrep, 0)),
        ],
        out_specs=[pl.BlockSpec((tm, w), lambda i: (i, 0)) for w in widths],
        out_shape=[jax.ShapeDtypeStruct((m, w), F32) for w in widths],
        compiler_params=_params(("parallel",), 56),
        name="in_proj",
    )(x2, g1, w_cat, b_cat, rope_tab)


def _rope_table(pos, rows):
    t = pos.shape[0]
    half = ROT_DIM // 2
    inv_freq = ROPE_THETA ** (-jnp.arange(half, dtype=F32) / half)
    ang = pos.astype(F32)[:, None] * inv_freq[None, :]
    cos, sin = jnp.cos(ang), jnp.sin(ang)
    rest = HEAD_DIM - ROT_DIM
    z8 = jnp.zeros((t, half), F32)
    cos64 = jnp.concatenate([cos, cos, jnp.ones((t, rest), F32)], axis=1)
    up64 = jnp.concatenate([z8, sin, jnp.zeros((t, rest), F32)], axis=1)
    dn64 = jnp.concatenate([-sin, z8, jnp.zeros((t, rest), F32)], axis=1)
    tab = jnp.concatenate([jnp.tile(a, (1, LANES // HEAD_DIM)) for a in (cos64, up64, dn64)], axis=1)
    if rows > t:
        tab = jnp.tile(tab, (rows // t, 1))
    return tab


def _softmax_av(s, sink, vv):
    m = jnp.maximum(jnp.max(s, axis=-1, keepdims=True), sink)
    p = jnp.exp(s - m)
    den = jnp.sum(p, axis=-1, keepdims=True) + jnp.exp(sink - m)
    return _dot(p.astype(BF16), vv) / den


def _attn_prompt_body(sink_ref, q_ref, kc_ref, kp_ref, vc_ref, vp_ref, o_ref):
    n = pl.program_id(1)
    blk = q_ref.shape[0]
    i = lax.broadcasted_iota(I32, (blk, 2 * blk), 0)
    j = lax.broadcasted_iota(I32, (blk, 2 * blk), 1)
    diff = blk + i - j
    j_lo = jnp.where(n > 0, 0, blk)
    mask = (diff >= 0) & (diff < WINDOW) & (j >= j_lo)
    outs = []
    for g in range(N_KV_HEADS):
        sl = slice(g * HEAD_DIM, (g + 1) * HEAD_DIM)
        kk = jnp.concatenate([kp_ref[:, sl], kc_ref[:, sl]], axis=0).astype(BF16)
        vv = jnp.concatenate([vp_ref[:, sl], vc_ref[:, sl]], axis=0).astype(BF16)
        for hq in range(Q_PER_KV):
            h = g * Q_PER_KV + hq
            qh = q_ref[:, h * HEAD_DIM:(h + 1) * HEAD_DIM].astype(BF16)
            s = jnp.where(mask, _dot(qh, kk, NT) * ATTN_SCALE, NEG_INF)
            outs.append(_softmax_av(s, sink_ref[h], vv))
    o_ref[...] = jnp.concatenate(outs, axis=1).astype(o_ref.dtype)


def _attn_prompt(sinks, q3, k3, v3):
    b, t, _ = q3.shape
    blk = WINDOW
    cur = lambda w: pl.BlockSpec((None, blk, w), lambda bi, n: (bi, n, 0))
    prev = lambda w: pl.BlockSpec((None, blk, w), lambda bi, n: (bi, jnp.maximum(n - 1, 0), 0))
    return pl.pallas_call(
        _attn_prompt_body,
        grid=(b, t // blk),
        in_specs=[pl.BlockSpec(memory_space=pltpu.SMEM), cur(ATTN_WIDTH), cur(KV_WIDTH), prev(KV_WIDTH),
                  cur(KV_WIDTH), prev(KV_WIDTH)],
        out_specs=cur(ATTN_WIDTH),
        out_shape=jax.ShapeDtypeStruct((b, t, ATTN_WIDTH), BF16),
        compiler_params=_params(("parallel", "arbitrary"), 32),
        name="attn_prompt",
    )(sinks, q3, k3, k3, v3, v3)


def _attn_sample_body(sink_ref, *refs, t):
    for bi in range(refs[0].shape[0]):
        _attn_sample_one(sink_ref, *[r.at[bi] for r in refs], t=t)


def _attn_sample_one(sink_ref, q_ref, kn_ref, vn_ref, kc_ref, vc_ref, o_ref, *, t):
    rows = Q_PER_KV * t
    n_past = kc_ref.shape[0]
    ri = lax.broadcasted_iota(I32, (rows, 1), 0)
    qi = ri % t
    jc = lax.broadcasted_iota(I32, (rows, n_past), 1)
    mask_c = (n_past + qi - jc) < WINDOW
    for g in range(N_KV_HEADS):
        sl = slice(g * HEAD_DIM, (g + 1) * HEAD_DIM)
        sink = jnp.zeros((rows, 1), F32)
        for hq in range(Q_PER_KV):
            sink = jnp.where(ri // t == hq, sink_ref[g * Q_PER_KV + hq], sink)
        qg = q_ref[g]
        qb = qg.astype(BF16)
        kn = kn_ref[:, sl]
        vn = vn_ref[:, sl]
        s_c = jnp.where(mask_c, _dot(qb, kc_ref[:, sl].astype(BF16), NT) * ATTN_SCALE, NEG_INF)
        s_n = []
        for jn in range(t):
            sj = jnp.sum(qg * kn[jn:jn + 1, :], axis=-1, keepdims=True) * ATTN_SCALE
            s_n.append(jnp.where(qi >= jn, sj, NEG_INF))
        m = jnp.maximum(jnp.max(s_c, axis=-1, keepdims=True), sink)
        for sj in s_n:
            m = jnp.maximum(m, sj)
        p_c = jnp.exp(s_c - m)
        den = jnp.sum(p_c, axis=-1, keepdims=True) + jnp.exp(sink - m)
        o = _dot(p_c.astype(BF16), vc_ref[:, sl].astype(BF16))
        for jn in range(t):
            p_n = jnp.exp(s_n[jn] - m)
            den = den + p_n
            o = o + p_n * vn[jn:jn + 1, :]
        o_ref[g] = (o / den).astype(o_ref.dtype)


def _attn_sample(sinks, qg, kn, vn, cache_k, cache_v, t):
    b = qg.shape[0]
    rows = Q_PER_KV * t
    n_past = cache_k.shape[1]
    bb = DECODE_BATCH_BLOCK if b % DECODE_BATCH_BLOCK == 0 else 1
    qspec = pl.BlockSpec((bb, N_KV_HEADS, rows, HEAD_DIM), lambda bi: (bi, 0, 0, 0))
    nspec = pl.BlockSpec((bb, t, KV_WIDTH), lambda bi: (bi, 0, 0))
    cspec = pl.BlockSpec((bb, n_past, KV_WIDTH), lambda bi: (bi, 0, 0))
    return pl.pallas_call(
        functools.partial(_attn_sample_body, t=t),
        grid=(b // bb,),
        in_specs=[pl.BlockSpec(memory_space=pltpu.SMEM), qspec, nspec, nspec, cspec, cspec],
        out_specs=qspec,
        out_shape=jax.ShapeDtypeStruct((b, N_KV_HEADS, rows, HEAD_DIM), BF16),
        compiler_params=_params(("parallel",), 32),
        name="attn_sample",
    )(sinks, qg, kn, vn, cache_k, cache_v)


def _stack_heads(z):
    ch, width = z.shape
    rows = lax.broadcasted_iota(I32, (RWKV_HEADS * ch, width), 0) // ch
    lanes = lax.broadcasted_iota(I32, (RWKV_HEADS * ch, width), 1) // RWKV_HEAD_DIM
    return jnp.where(rows == lanes, jnp.concatenate([z] * RWKV_HEADS, axis=0), 0.0)


def _head_pack(ch):
    return LANES // ch if ch >= RWKV_HEAD_DIM else 1


def _rwkv_prep_body(p_ref, pprev_ref, shift_ref, mu_ref, w0_ref, a0_ref, kk_ref, ka_ref, w2_ref, a2_ref, g2_ref,
                    eblk_ref, *outs, t_real):
    c = pl.program_id(1)
    nb, ch = p_ref.shape[0], p_ref.shape[1]
    row = lax.broadcasted_iota(I32, (ch, 1), 0)
    xms = []
    for bi in range(nb):
        pc = p_ref[bi]
        first = jnp.where(c == 0, shift_ref[bi], pprev_ref[bi, SUBLANES - 1:SUBLANES, :])
        prev = jnp.where(row == 0, first, pltpu.roll(pc, 1, 0))
        xms.append(pc + mu_ref[...] * (prev - pc))
    xm = jnp.concatenate(xms, axis=0)
    valid = (c * ch + lax.broadcasted_iota(I32, (nb * ch, 1), 0) % ch) < t_real
    w3 = 3 * RWKV_WIDTH
    r = xm[:, 0:RWKV_WIDTH]
    k = xm[:, RWKV_WIDTH:2 * RWKV_WIDTH]
    v = xm[:, 2 * RWKV_WIDTH:w3]
    lo = xm[:, w3:w3 + LORA_PAD]
    lane = lax.broadcasted_iota(I32, lo.shape, 1)
    act = jnp.where(lane < DECAY_LORA, jnp.tanh(lo),
                    jnp.where(lane < DECAY_LORA + AAA_LORA, lo, jax.nn.sigmoid(lo)))
    z = w0_ref[...] + _doth(act, w2_ref[...])
    w_log = -(jnp.maximum(-z, 0.0) + jnp.log1p(jnp.exp(-jnp.abs(z)))) - 0.5
    lw = jnp.where(valid, -jnp.exp(w_log), 0.0)
    a = jax.nn.sigmoid(a0_ref[...] + _doth(act, a2_ref[...]))
    g = _dot(act.astype(BF16), g2_ref[...].astype(BF16))
    kk = k * kk_ref[...]
    ss = _dot_exact_rhs(kk * kk, eblk_ref[...])
    kk = jnp.where(valid, kk * lax.rsqrt(jnp.maximum(ss, 1e-24)), 0.0)
    kmod = jnp.where(valid, k * (1.0 + (a - 1.0) * ka_ref[...]), 0.0)
    bb = kk * a
    for bi in range(nb):
        rows = slice(bi * ch, (bi + 1) * ch)
        _rwkv_prep_chunk(lw[rows], kk[rows], kmod[rows], bb[rows], r[rows], v[rows], g[rows],
                         *[o.at[bi] for o in outs])


def _rwkv_prep_chunk(lw, kk, kmod, bb, r, v, g, akb_ref, akk_ref, qrk_ref, qrb_ref, kkt_ref, rt_ref, kdec_ref,
                     bdec_ref, kmod_ref, r_ref, v_ref, g_ref, gamc_ref):
    ch = lw.shape[0]
    ti = lax.broadcasted_iota(I32, (ch, ch), 0)
    tj = lax.broadcasted_iota(I32, (ch, ch), 1)
    tri = jnp.where(ti >= tj, 1.0, 0.0).astype(BF16)
    cum = _dot_exact_lhs(tri, lw)
    cum_c = cum[ch - 1:ch, :]
    e_neg = jnp.exp(-cum)
    e_rem = jnp.exp(cum_c - cum)
    kkt = kk * jnp.exp(cum - lw)
    rt = r * jnp.exp(cum)
    khat = kmod * e_neg
    bhat = bb * e_neg
    strict = ti > tj
    incl = ti >= tj
    hs = range(RWKV_HEADS)
    sl = [slice(h * RWKV_HEAD_DIM, (h + 1) * RWKV_HEAD_DIM) for h in hs]
    split = lambda z: [z[:, s].astype(BF16) for s in sl]
    kkt_s, rt_s, khat_s, bhat_s = split(kkt), split(rt), split(khat), split(bhat)
    if ch < RWKV_HEAD_DIM:
        kkt_bd, rt_bd = _stack_heads(kkt).astype(BF16), _stack_heads(rt).astype(BF16)
        blocks = lambda m: [m[h * ch:(h + 1) * ch, :] for h in hs]
        akb = blocks(_dot(kkt_bd, bhat.astype(BF16), NT))
        akk = blocks(_dot(kkt_bd, khat.astype(BF16), NT))
        qrk = blocks(_dot(rt_bd, khat.astype(BF16), NT))
        qrb = blocks(_dot(rt_bd, bhat.astype(BF16), NT))
    else:
        akb = [_dot(kkt_s[h], bhat_s[h], NT) for h in hs]
        akk = [_dot(kkt_s[h], khat_s[h], NT) for h in hs]
        qrk = [_dot(rt_s[h], khat_s[h], NT) for h in hs]
        qrb = [_dot(rt_s[h], bhat_s[h], NT) for h in hs]
    pack = _head_pack(ch)
    for gi in range(RWKV_HEADS // pack):
        grp = range(gi * pack, (gi + 1) * pack)
        side = lambda mats, keep: jnp.concatenate([jnp.where(keep, mats[h], 0.0) for h in grp], axis=1)
        akb_ref[gi] = side(akb, strict)
        akk_ref[gi] = side(akk, strict)
        qrk_ref[gi] = side(qrk, incl)
        qrb_ref[gi] = side(qrb, incl)
    kkt_ref[...] = kkt
    rt_ref[...] = rt
    kdec_ref[...] = kmod * e_rem
    bdec_ref[...] = bb * e_rem
    kmod_ref[...] = kmod
    r_ref[...] = r
    v_ref[...] = v
    g_ref[...] = g
    gamc_ref[...] = jnp.exp(cum_c)


def _rwkv_prep(p3, shift0, lp, eblk, ch, t_real, bb):
    b, tp, _ = p3.shape
    nc = tp // ch
    per8 = ch // SUBLANES
    pack = _head_pack(ch)
    ng, gw = RWKV_HEADS // pack, pack * ch
    mats = jax.ShapeDtypeStruct((b, nc, ng, ch, gw), F32)
    rows = jax.ShapeDtypeStruct((b, tp, RWKV_WIDTH), F32)
    mspec = pl.BlockSpec((bb, None, ng, ch, gw), lambda bi, ci: (bi, ci, 0, 0, 0))
    rspec = pl.BlockSpec((bb, ch, RWKV_WIDTH), lambda bi, ci: (bi, ci, 0))
    consts = (lp["mu"], lp["w0"], lp["a0"], lp["k_k"], lp["k_a"], lp["w2p"], lp["a2p"], lp["g2p"], eblk)
    return pl.pallas_call(
        functools.partial(_rwkv_prep_body, t_real=t_real),
        grid=(b // bb, nc),
        in_specs=[
            pl.BlockSpec((bb, ch, RWKV_PAD), lambda bi, ci: (bi, ci, 0)),
            pl.BlockSpec((bb, SUBLANES, RWKV_PAD), lambda bi, ci: (bi, jnp.maximum(ci * per8 - 1, 0), 0)),
            pl.BlockSpec((bb, 1, RWKV_PAD), lambda bi, ci: (bi, 0, 0)),
        ] + [_full(a) for a in consts],
        out_specs=[mspec] * 4 + [rspec] * 8 + [pl.BlockSpec((bb, None, 1, RWKV_WIDTH), lambda bi, ci: (bi, ci, 0, 0))],
        out_shape=[mats] * 4 + [rows] * 8 + [jax.ShapeDtypeStruct((b, nc, 1, RWKV_WIDTH), F32)],
        compiler_params=_params(("parallel", "parallel"), 48),
        name="rwkv_prep",
    )(p3, p3, shift0, *consts)


def _tri_solve_body(a_ref, t_ref, w_scr, *, ch, pack):
    k = pl.program_id(1)
    kc_w = a_ref.shape[1]
    nk = (pl.num_programs(1) - 1) // 2
    groups = a_ref.shape[0] // LANES
    gw = pack * ch

    @pl.when(k < nk)
    def _():
        xt = a_ref[...].T
        for g in range(groups):
            w_scr[pl.ds(k * (kc_w * groups) + g, kc_w, stride=groups), :] = xt[:, g * LANES:(g + 1) * LANES]

    @pl.when(k == nk)
    def _():
        cols = min(SOLVE_COLS, ch)
        cid = lax.broadcasted_iota(I32, (cols * groups, LANES), 0) // groups
        at = lambda e, n: pl.ds(pl.multiple_of(e * groups, groups), n * groups)

        for j in range(pack):
            def row_body(s, carry, j=j):
                for kc in range(ch // cols):
                    c0 = kc * cols

                    @pl.when(s >= c0)
                    def _():
                        acc0 = jnp.where(cid == s - c0, 1.0, 0.0).astype(F32)

                        def r_body(r, acc):
                            a_sr = w_scr[at(s * gw + j * ch + r, 1), :]
                            return acc - jnp.tile(a_sr, (cols, 1)) * w_scr[at(r * gw + j * ch + c0, cols), :]

                        w_scr[at(s * gw + j * ch + c0, cols), :] = lax.fori_loop(c0, s, r_body, acc0)
                return carry

            lax.fori_loop(0, ch, row_body, 0)

    @pl.when(k > nk)
    def _():
        kk = k - nk - 1
        parts = [w_scr[pl.ds(kk * (kc_w * groups) + g, kc_w, stride=groups), :] for g in range(groups)]
        t_ref[...] = jnp.concatenate(parts, axis=1).T


def _tri_solve(akb):
    b, nc, ng, ch, gw = akb.shape
    nrow = b * nc * ng
    slab = SUBLANES * LANES
    ne = ch * gw
    row_pad = -nrow % slab
    ne_pad = -ne % LANES
    a2 = akb.reshape(nrow, ne)
    if row_pad or ne_pad:
        a2 = jnp.pad(a2, ((0, row_pad), (0, ne_pad)))
    nk = (ne + ne_pad) // LANES
    t2 = pl.pallas_call(
        functools.partial(_tri_solve_body, ch=ch, pack=gw // ch),
        grid=((nrow + row_pad) // slab, 2 * nk + 1),
        in_specs=[pl.BlockSpec((slab, LANES), lambda gi, k: (gi, jnp.minimum(k, nk - 1)))],
        out_specs=pl.BlockSpec((slab, LANES), lambda gi, k: (gi, jnp.clip(k - nk - 1, 0, nk - 1))),
        out_shape=jax.ShapeDtypeStruct(a2.shape, F32),
        scratch_shapes=[pltpu.VMEM(((ne + ne_pad) * SUBLANES, LANES), F32)],
        compiler_params=_params(("parallel", "arbitrary"), 48),
        name="rwkv_tri_solve",
    )(a2)
    if row_pad or ne_pad:
        t2 = t2[:nrow, :ne]
    return t2.reshape(b, nc, ng, ch, gw)


def _rwkv_scan_body(t_ref, akk_ref, qrk_ref, qrb_ref, kkt_ref, rt_ref, kdec_ref, bdec_ref, kmod_ref, r_ref, v_ref,
                    g_ref, gamc_ref, wkv0_ref, eblk_ref, rk_ref, lnw_ref, lnb_ref, rw_ref, wkv_ref, s_scr):
    nb, ch = kkt_ref.shape[0], kkt_ref.shape[1]
    nd = RWKV_HEAD_DIM
    per_row = (t_ref, akk_ref, qrk_ref, qrb_ref, kkt_ref, rt_ref, kdec_ref, bdec_ref, v_ref, gamc_ref, wkv0_ref,
               wkv_ref, s_scr)
    y = jnp.concatenate([_rwkv_scan_chunk(*[r.at[bi] for r in per_row]) for bi in range(nb)], axis=0)
    rows = lambda ref: ref[...].reshape(nb * ch, RWKV_WIDTH)
    eblk = eblk_ref[...]
    inv_n = 1.0 / nd
    mean = _dot_exact_rhs(y, eblk) * inv_n
    yc = y - mean
    var = _dot_exact_rhs(yc * yc, eblk) * inv_n
    yn = yc * lax.rsqrt(var + GN_EPS) * lnw_ref[...] + lnb_ref[...]
    bonus = _dot_exact_rhs(rows(r_ref) * rows(kmod_ref) * rk_ref[...], eblk) * rows(v_ref)
    rw_ref[...] = ((yn + bonus) * rows(g_ref)).reshape(nb, ch, RWKV_WIDTH).astype(rw_ref.dtype)


def _rwkv_scan_chunk(t_ref, akk_ref, qrk_ref, qrb_ref, kkt_ref, rt_ref, kdec_ref, bdec_ref, v_ref, gamc_ref,
                     wkv0_ref, wkv_ref, s_scr):
    c = pl.program_id(1)
    nd = RWKV_HEAD_DIM

    @pl.when(c == 0)
    def _():
        s_scr[...] = wkv0_ref[...]

    hs = range(RWKV_HEADS)
    sl = [slice(h * nd, (h + 1) * nd) for h in hs]
    ch = kkt_ref.shape[0]
    pack = _head_pack(ch)
    mat = lambda ref, h: ref[h // pack][:, (h % pack) * ch:(h % pack + 1) * ch]
    tmat = [mat(t_ref, h) for h in hs]
    vh = [v_ref[:, sl[h]] for h in hs]
    bdec = [bdec_ref[:, sl[h]] for h in hs]
    qrb = [mat(qrb_ref, h) for h in hs]
    w1 = [_dotb(tmat[h], kkt_ref[:, sl[h]]) for h in hs]
    av = [_dotb(mat(akk_ref, h), vh[h]) for h in hs]
    qv = [_dotb(mat(qrk_ref, h), vh[h]) for h in hs]
    vk = [_dotb(vh[h], kdec_ref[:, sl[h]], TN) for h in hs]
    w2 = [_dotb(tmat[h], av[h]) for h in hs]
    qw1 = [_dotb(qrb[h], w1[h]) for h in hs]
    bw1 = [_dotb(bdec[h], w1[h], TN) for h in hs]
    qw2 = [_dotb(qrb[h], w2[h]) for h in hs]
    wb = [_dotb(w2[h], bdec[h], TN) for h in hs]
    s0 = [s_scr[h] for h in hs]
    ys = [_dotb(rt_ref[:, sl[h]] - qw1[h], s0[h], NT) + (qv[h] - qw2[h]) for h in hs]
    for h in hs:
        s_scr[h] = s0[h] * gamc_ref[:, sl[h]] - _dotb(s0[h], bw1[h], NT) + (vk[h] - wb[h])
    @pl.when(c == pl.num_programs(1) - 1)
    def _():
        wkv_ref[...] = s_scr[...]

    return jnp.concatenate(ys, axis=1)


def _rwkv_scan(tmat, prep, wkv0, lp, eblk, bb):
    akk, qrk, qrb, kkt, rt, kdec, bdec, kmod, r, v, g, gamc = prep
    b, nc, ng, ch, gw = tmat.shape
    nh = RWKV_HEADS
    tp = kkt.shape[1]
    nd = RWKV_HEAD_DIM
    mspec = pl.BlockSpec((bb, None, ng, ch, gw), lambda bi, ci: (bi, ci, 0, 0, 0))
    rspec = pl.BlockSpec((bb, ch, RWKV_WIDTH), lambda bi, ci: (bi, ci, 0))
    sspec = pl.BlockSpec((bb, nh, nd, nd), lambda bi, ci: (bi, 0, 0, 0))
    consts = (eblk, lp["r_k"], lp["ln_w"], lp["ln_b"])
    return pl.pallas_call(
        _rwkv_scan_body,
        grid=(b // bb, nc),
        in_specs=[mspec] * 4 + [rspec] * 8
        + [pl.BlockSpec((bb, None, 1, RWKV_WIDTH), lambda bi, ci: (bi, ci, 0, 0)), sspec]
        + [_full(a) for a in consts],
        out_specs=[rspec, sspec],
        out_shape=[jax.ShapeDtypeStruct((b, tp, RWKV_WIDTH), BF16), jax.ShapeDtypeStruct((b, nh, nd, nd), F32)],
        scratch_shapes=[pltpu.VMEM((bb, nh, nd, nd), F32)],
        compiler_params=_params(("parallel", "arbitrary"), 48),
        name="rwkv_scan",
    )(tmat, akk, qrk, qrb, kkt, rt, kdec, bdec, kmod, r, v, g, gamc, wkv0, *consts)


def _merge_body(x_ref, att_ref, rw_ref, ga_ref, gb_ref, wa_ref, wb_ref, wo_ref, g2_ref, wq_ref,
                x1_ref, h2_ref, qp_ref):
    merged = ga_ref[...] * _dot(att_ref[...], wa_ref[...]) + gb_ref[...] * _dot(rw_ref[...], wb_ref[...])
    x1 = x_ref[...] + _dot(merged.astype(BF16), wo_ref[...])
    x1_ref[...] = x1
    h2 = (x1 * lax.rsqrt(jnp.mean(x1 * x1, axis=-1, keepdims=True) + NORM_EPS) * g2_ref[...]).astype(BF16)
    h2_ref[...] = h2
    qp_ref[...] = _dot(h2, wq_ref[...]).astype(qp_ref.dtype)


def _merge(x2, att, rw, ga, gb, lp, tm):
    m = x2.shape[0]
    kq = lp["w_query"].shape[1]
    row = lambda w: pl.BlockSpec((tm, w), lambda i: (i, 0))
    consts = (lp["w_up_a"], lp["w_up_b"], lp["w_o"], lp["norm2_g"], lp["w_query"])
    return pl.pallas_call(
        _merge_body,
        grid=(m // tm,),
        in_specs=[row(D_MODEL), row(ATTN_WIDTH), row(RWKV_WIDTH), row(D_MODEL), row(D_MODEL)]
        + [_full(a) for a in consts],
        out_specs=[row(D_MODEL), row(D_MODEL), row(kq)],
        out_shape=[jax.ShapeDtypeStruct((m, D_MODEL), F32), jax.ShapeDtypeStruct((m, D_MODEL), BF16),
                   jax.ShapeDtypeStruct((m, kq), BF16)],
        compiler_params=_params(("parallel",), 56),
        name="merge",
    )(x2, att, rw, ga, gb, *consts)


def _argmax_rows(src_ref):
    nslab = src_ref.shape[0] // SUBLANES
    row0 = lax.broadcasted_iota(I32, (SUBLANES, src_ref.shape[1]), 0).astype(F32)
    vals = [src_ref[i * SUBLANES:(i + 1) * SUBLANES, :] for i in range(nslab)]
    rows = [row0 + float(i * SUBLANES) for i in range(nslab)]
    while len(vals) > 1:
        nv, nr = [], []
        for a in range(0, len(vals) - 1, 2):
            take = vals[a + 1] > vals[a]
            nv.append(jnp.maximum(vals[a], vals[a + 1]))
            nr.append(jnp.where(take, rows[a + 1], rows[a]))
        if len(vals) % 2:
            nv.append(vals[-1])
            nr.append(rows[-1])
        vals, rows = nv, nr
    v, i = vals[0], rows[0]
    for shift in (4, 2, 1):
        v2 = pltpu.roll(v, shift, 0)
        i2 = pltpu.roll(i, shift, 0)
        take = (v2 > v) | ((v2 == v) & (i2 < i))
        v = jnp.where(take, v2, v)
        i = jnp.where(take, i2, i)
    return jnp.max(v, axis=0, keepdims=True), jnp.max(i, axis=0, keepdims=True)


def _extract_top(jobs, n_out):
    def body(r, carry):
        for src_ref, val_ref, idx_ref in jobs:
            rowf = lax.broadcasted_iota(I32, src_ref.shape, 0).astype(F32)
            m, idx = _argmax_rows(src_ref)
            val_ref[pl.ds(r, 1), :] = m
            idx_ref[pl.ds(r, 1), :] = idx
            src_ref[...] = jnp.where(rowf == idx, NEG_INF, src_ref[...])
        return carry

    lax.fori_loop(0, n_out, body, 0)


def _cand_groups(k):
    groups = []
    a = 0
    while a < k and k // (a + 1) > 1:
        nb = k // (a + 1)
        groups.append((a, -(-nb // SUBLANES) * SUBLANES, nb))
        a += 1
    return groups, a


def _n_cand_rows(k):
    groups, a_tail = _cand_groups(k)
    return sum(rows for _, rows, _ in groups) + k - a_tail


def _peer_topk_head(qp_ref, sk_ref, h, i1_scr, i2_scr, gate_scr, s_scr, cand_scr, sv_scr, si_scr, bs_scr, bp_scr):
    k = PEER_TOPK
    for c in range(2):
        col = (2 * h + c) * D_KEY_HALF
        s_scr[c] = _dot(sk_ref[h, c], qp_ref[:, col:col + D_KEY_HALF], NT)
    _extract_top([(s_scr.at[c], sv_scr.at[c], si_scr.at[c]) for c in range(2)], k)
    sv0 = sv_scr[0]
    sv1 = sv_scr[1]
    groups, a_tail = _cand_groups(k)
    off = 0
    for a, rows, nb in groups:
        blk = sv0[a:a + 1, :] + sv1[0:rows, :]
        if nb < rows:
            blk = jnp.where(lax.broadcasted_iota(I32, blk.shape, 0) < nb, blk, NEG_INF)
        cand_scr[off:off + rows, :] = blk
        off += rows
    cand_scr[off:off + k - a_tail, :] = sv0[a_tail:k, :] + sv1[0:1, :]
    _extract_top([(cand_scr, bs_scr, bp_scr)], k)
    best_s = bs_scr[...]
    pos = bp_scr[...]
    a_sel = jnp.zeros_like(pos)
    b_sel = pos
    off = 0
    for a, rows, nb in groups:
        if a > 0:
            a_sel = jnp.where(pos >= off, float(a), a_sel)
            b_sel = jnp.where(pos >= off, pos - off, b_sel)
        off += rows
    a_sel = jnp.where(pos >= off, a_tail + pos - off, a_sel)
    b_sel = jnp.where(pos >= off, 0.0, b_sel)
    si0 = si_scr[0]
    si1 = si_scr[1]
    i1 = jnp.zeros_like(pos)
    i2 = jnp.zeros_like(pos)
    for a in range(k):
        i1 = jnp.where(a_sel == a, si0[a:a + 1, :], i1)
        i2 = jnp.where(b_sel == a, si1[a:a + 1, :], i2)
    e = jnp.exp(best_s - jnp.max(best_s, axis=0, keepdims=True))
    i1_scr[h * k:(h + 1) * k, :] = i1
    i2_scr[h * k:(h + 1) * k, :] = i2
    gate_scr[h * k:(h + 1) * k, :] = e / jnp.sum(e, axis=0, keepdims=True)


def _peer_topk_body(qp_ref, sk_ref, i1_ref, i2_ref, gate_ref, i1_scr, i2_scr, gate_scr, *work):
    for h in range(PEER_HEADS):
        _peer_topk_head(qp_ref, sk_ref, h, i1_scr, i2_scr, gate_scr, *work)
    i1_ref[...] = i1_scr[...].T.astype(I32)
    i2_ref[...] = i2_scr[...].T.astype(I32)
    gate_ref[...] = gate_scr[...].T


def _peer_topk(qp, sub_keys, tm):
    m = qp.shape[0]
    k = PEER_TOPK
    ospec = pl.BlockSpec((tm, N_SLOTS), lambda i: (i, 0))
    slots = pltpu.VMEM((N_SLOTS, tm), F32)
    return pl.pallas_call(
        _peer_topk_body,
        grid=(m // tm,),
        in_specs=[pl.BlockSpec((tm, qp.shape[1]), lambda i: (i, 0)), _full(sub_keys)],
        out_specs=[ospec, ospec, ospec],
        out_shape=[jax.ShapeDtypeStruct((m, N_SLOTS), I32), jax.ShapeDtypeStruct((m, N_SLOTS), I32),
                   jax.ShapeDtypeStruct((m, N_SLOTS), F32)],
        scratch_shapes=[slots, slots, slots,
                        pltpu.VMEM((2, N_KEYS, tm), F32), pltpu.VMEM((_n_cand_rows(k), tm), F32),
                        pltpu.VMEM((2, k, tm), F32), pltpu.VMEM((2, k, tm), F32),
                        pltpu.VMEM((k, tm), F32), pltpu.VMEM((k, tm), F32)],
        compiler_params=_params(("parallel",), 32),
        name="peer_topk",
    )(qp, sub_keys)


def _peer_expert_body(x1_ref, h2_ref, i1_ref, i2_ref, gate_ref, ut_ref, v_ref, gf_ref, y_ref, w_scr):
    j = pl.program_id(1)
    tm = h2_ref.shape[0]
    te = v_ref.shape[0]
    pitch = tm + W_PITCH_PAD

    @pl.when(j == 0)
    def _():
        y_ref[...] = x1_ref[...]
        sub = lax.broadcasted_iota(I32, (N_KEYS, N_SLOTS), 0)

        def tok(t, carry):
            hit1 = sub == i1_ref[pl.ds(t, 1), :]
            hit2 = sub == i2_ref[pl.ds(t, 1), :]
            a = jnp.where(hit1, gate_ref[pl.ds(t, 1), :], 0.0).astype(BF16)
            bm_t = jnp.where(hit2, 1.0, 0.0).T.astype(BF16)
            w_scr[pl.ds(t, N_KEYS, stride=pitch), :] = _dot(a, bm_t)
            return carry

        lax.fori_loop(0, tm, tok, 0, unroll=TOKEN_UNROLL)

    rb = min(tm, EXPERT_ROW_BLOCK)
    pair = 2 * N_KEYS
    for r in range(tm // rb):
        x = h2_ref[r * rb:(r + 1) * rb, :]
        parts = []
        for q in range(te // pair):
            h = _dot(x, ut_ref[:, q * pair:(q + 1) * pair])
            i1 = j * (te // N_KEYS) + 2 * q
            w = jnp.concatenate([w_scr[pl.ds(pl.multiple_of(i1 * pitch + r * rb, SUBLANES), rb), :],
                                 w_scr[pl.ds(pl.multiple_of((i1 + 1) * pitch + r * rb, SUBLANES), rb), :]], axis=1)
            act = 0.5 * h * (1.0 + lax.erf(h * (2.0 ** -0.5)))
            parts.append((w * act).astype(BF16))
        y_ref[r * rb:(r + 1) * rb, :] += _dot(jnp.concatenate(parts, axis=1), v_ref[...])

    @pl.when(j == pl.num_programs(1) - 1)
    def _():
        x2 = y_ref[...]
        y_ref[...] = x2 * lax.rsqrt(jnp.mean(x2 * x2, axis=-1, keepdims=True) + NORM_EPS) * gf_ref[...]


def _peer_experts(x1, h2, i1, i2, gate, eu, ev, gf, tm, te):
    m = x1.shape[0]
    row = lambda w: pl.BlockSpec((tm, w), lambda i, j: (i, 0))
    tab = pl.BlockSpec((te, D_MODEL), lambda i, j: (j, 0))
    tab_t = pl.BlockSpec((D_MODEL, te), lambda i, j: (0, j))
    x1_spec = pl.BlockSpec((tm, D_MODEL), lambda i, j: (i, 0), pipeline_mode=pl.Buffered(1))
    return pl.pallas_call(
        _peer_expert_body,
        grid=(m // tm, N_EXPERTS // te),
        in_specs=[x1_spec, row(D_MODEL), row(N_SLOTS), row(N_SLOTS), row(N_SLOTS), tab_t, tab,
                  pl.BlockSpec((1, D_MODEL), lambda i, j: (0, 0))],
        out_specs=row(D_MODEL),
        out_shape=jax.ShapeDtypeStruct((m, D_MODEL), F32),
        scratch_shapes=[pltpu.VMEM((N_KEYS * (tm + W_PITCH_PAD), N_KEYS), F32)],
        compiler_params=_params(("parallel", "arbitrary"), 60),
        name="peer_experts",
    )(x1, h2, i1, i2, gate, eu, ev, gf)


def _pad_cols(a, width):
    return jnp.pad(a, [(0, 0)] * (a.ndim - 1) + [(0, width - a.shape[-1])])


def _prep_layer_params(l, norm1_g, w_in, b_in, attn_sinks, shift_mu, w0, w2, a0, a2, g2, k_k, k_a, r_k, ln_x_w,
                       ln_x_b, w_up_a, w_up_b, w_o, norm2_g, w_query, sub_keys, expert_u, expert_v):
    wi, bi = w_in[l], b_in[l][None, :]
    cat = lambda z: jnp.concatenate([z[:, :OFF_B], _pad_cols(z[:, OFF_B:OFF_GATE], RWKV_PAD), z[:, OFF_GATE:]], axis=1)
    row = lambda z: z[l].reshape(1, -1)
    lora = lambda z, off: jnp.pad(z[l], ((off, LORA_PAD - off - z.shape[1]), (0, 0)))
    return dict(
        norm1_g=row(norm1_g), w_cat=cat(wi).astype(BF16), b_cat=cat(bi), sinks=attn_sinks[l],
        mu=_pad_cols(row(shift_mu), RWKV_PAD), w0=row(w0), a0=row(a0), k_k=row(k_k), k_a=row(k_a),
        w2p=lora(w2, 0), a2p=lora(a2, DECAY_LORA), g2p=lora(g2, DECAY_LORA + AAA_LORA),
        r_k=row(r_k), ln_w=row(ln_x_w), ln_b=row(ln_x_b),
        w_up_a=w_up_a[l].astype(BF16), w_up_b=w_up_b[l].astype(BF16), w_o=w_o[l].astype(BF16),
        norm2_g=row(norm2_g), w_query=w_query[l].astype(BF16), sub_keys=sub_keys[l].astype(BF16),
        expert_ut=expert_u[l].astype(BF16).T, expert_v=expert_v[l].astype(BF16),
    )


def _row_tile(m, pref):
    return pref if m % pref == 0 else m


def _hybrid_layer(x, pos, lp, eblk, gf, win_k, win_v, wkv0, shift0):
    b, t, _ = x.shape
    m = b * t
    x2 = x.reshape(m, D_MODEL)
    tm = _row_tile(m, 256)
    tmd = _row_tile(m, DENSE_TOKEN_TILE)
    rope_tab = _rope_table(pos, max(t, tmd))
    q, k, v, p, ga, gb = _in_proj(x2, lp["norm1_g"], lp["w_cat"], lp["b_cat"], rope_tab, tmd)

    k3 = k.reshape(b, t, KV_WIDTH)
    v3 = v.reshape(b, t, KV_WIDTH)
    if win_k is None:
        att = _attn_prompt(lp["sinks"], q.reshape(b, t, ATTN_WIDTH), k3, v3).reshape(m, ATTN_WIDTH)
        keep = min(WINDOW, t)
        new_k, new_v = k3[:, t - keep:], v3[:, t - keep:]
    else:
        n_past = win_k.shape[1]
        qg = q.reshape(b, t, N_KV_HEADS, Q_PER_KV, HEAD_DIM).transpose(0, 2, 3, 1, 4)
        qg = qg.reshape(b, N_KV_HEADS, Q_PER_KV * t, HEAD_DIM)
        ck = win_k.reshape(b, n_past, KV_WIDTH)
        cv = win_v.reshape(b, n_past, KV_WIDTH)
        og = _attn_sample(lp["sinks"], qg, k3, v3, ck, cv, t)
        att = og.reshape(b, N_KV_HEADS, Q_PER_KV, t, HEAD_DIM).transpose(0, 3, 1, 2, 4).reshape(m, ATTN_WIDTH)
        new_k = jnp.concatenate([ck, k3], axis=1)[:, t:]
        new_v = jnp.concatenate([cv, v3], axis=1)[:, t:]
    new_k = new_k.reshape(b, -1, N_KV_HEADS, HEAD_DIM)
    new_v = new_v.reshape(b, -1, N_KV_HEADS, HEAD_DIM)

    p3 = p.reshape(b, t, RWKV_PAD)
    if t % RWKV_CHUNK == 0:
        ch, tp = RWKV_CHUNK, t
    else:
        tp = -(-t // SUBLANES) * SUBLANES
        ch = tp
        p3 = jnp.pad(p3, ((0, 0), (0, tp - t), (0, 0)))
    shift_pad = _pad_cols(shift0, RWKV_PAD)[:, None, :]
    want = DECODE_BATCH_BLOCK if tp == ch else PROMPT_BATCH_BLOCK
    bb = want if b % want == 0 else 1
    akb, *prep = _rwkv_prep(p3, shift_pad, lp, eblk, ch, t, bb)
    tmat = _tri_solve(akb)
    rw, wkv_t = _rwkv_scan(tmat, prep, wkv0, lp, eblk, bb)
    rw = rw[:, :t].reshape(m, RWKV_WIDTH)
    shift_t = p3[:, t - 1, :RWKV_COLS]

    x1, h2, qp = _merge(x2, att, rw, ga, gb, lp, tmd)
    i1, i2, gate = _peer_topk(qp, lp["sub_keys"], tm)
    y = _peer_experts(x1, h2, i1, i2, gate, lp["expert_ut"], lp["expert_v"], gf,
                      _row_tile(m, EXPERT_TOKEN_TILE), EXPERT_TILE)
    return y.reshape(b, t, D_MODEL), new_k, new_v, wkv_t, shift_t


def kernel(x_prompt, x_sample, cache_win_k, cache_win_v, state_wkv, state_shift, norm1_g, w_in, b_in, attn_sinks,
           shift_mu, w0, w2, a0, a2, g2, k_k, k_a, r_k, ln_x_w, ln_x_b, w_up_a, w_up_b, w_o, norm2_g, w_query,
           sub_keys, expert_u, expert_v, final_norm_g):
    depth = w_in.shape[0]
    assert depth == 1, "the final RMSNorm is fused into the last layer's expert kernel"
    pos_p = jnp.arange(x_prompt.shape[1])
    pos_s = PAST_LEN + jnp.arange(x_sample.shape[1])
    li = lax.broadcasted_iota(I32, (RWKV_WIDTH, RWKV_WIDTH), 0) // RWKV_HEAD_DIM
    lj = lax.broadcasted_iota(I32, (RWKV_WIDTH, RWKV_WIDTH), 1) // RWKV_HEAD_DIM
    eblk = (li == lj).astype(BF16)
    gf = final_norm_g.reshape(1, D_MODEL)
    lp = _prep_layer_params(0, norm1_g, w_in, b_in, attn_sinks, shift_mu, w0, w2, a0, a2, g2, k_k, k_a, r_k, ln_x_w,
                            ln_x_b, w_up_a, w_up_b, w_o, norm2_g, w_query, sub_keys, expert_u, expert_v)
    bp = x_prompt.shape[0]
    wkv0 = jnp.zeros((bp, RWKV_HEADS, RWKV_HEAD_DIM, RWKV_HEAD_DIM), F32)
    shift0 = jnp.zeros((bp, RWKV_COLS), x_prompt.dtype)
    yp, pk, pv, pw, psh = _hybrid_layer(x_prompt, pos_p, lp, eblk, gf, None, None, wkv0, shift0)
    ys, sk, sv, sw, ssh = _hybrid_layer(x_sample, pos_s, lp, eblk, gf, cache_win_k[0], cache_win_v[0],
                                        state_wkv[0], state_shift[0])
    st = lambda z: z[None]
    return (yp, ys, st(pk), st(pv), st(pw), st(psh), st(sk), st(sv), st(sw), st(ssh))
```

```python
import functools

import jax
import jax.numpy as jnp
from jax import lax
from jax.experimental import pallas as pl
from jax.experimental.pallas import tpu as pltpu

F32, BF16, I32 = jnp.float32, jnp.bfloat16, jnp.int32

D_MODEL = 1024
NORM_EPS = 1e-5
HEAD_DIM = 64
N_Q_HEADS = 8
N_KV_HEADS = 2
Q_PER_KV = N_Q_HEADS // N_KV_HEADS
ATTN_WIDTH = N_Q_HEADS * HEAD_DIM
KV_WIDTH = N_KV_HEADS * HEAD_DIM
WINDOW = 128
ROT_DIM = HEAD_DIM // 4
ROPE_THETA = 500000.0
ATTN_SCALE = HEAD_DIM ** -0.5
RWKV_HEADS = 8
RWKV_HEAD_DIM = 64
RWKV_WIDTH = RWKV_HEADS * RWKV_HEAD_DIM
DECAY_LORA = 32
AAA_LORA = 32
GATE_LORA = 96
GN_EPS = 64e-5
RWKV_COLS = 3 * RWKV_WIDTH + DECAY_LORA + AAA_LORA + GATE_LORA
LORA_PAD = 256
RWKV_PAD = 3 * RWKV_WIDTH + LORA_PAD
OFF_K = ATTN_WIDTH
OFF_V = OFF_K + KV_WIDTH
OFF_B = OFF_V + KV_WIDTH
OFF_GATE = OFF_B + RWKV_COLS
N_KEYS = 128
N_EXPERTS = N_KEYS * N_KEYS
PEER_HEADS = 8
PEER_TOPK = 16
D_KEY_HALF = 128
N_SLOTS = PEER_HEADS * PEER_TOPK
PAST_LEN = 16384
RWKV_CHUNK = 64
LANES = 128
SUBLANES = 8
SOLVE_COLS = 16
TOKEN_UNROLL = 64
DECODE_BATCH_BLOCK = 16
PROMPT_BATCH_BLOCK = 8
DENSE_TOKEN_TILE = 512
EXPERT_TOKEN_TILE = 512
EXPERT_TILE = 1024
EXPERT_ROW_BLOCK = 512
W_PITCH_PAD = 8
NEG_INF = float("-inf")

NN = ((1,), (0,))
NT = ((1,), (1,))
TN = ((0,), (0,))


def _dot(a, b, dims=NN):
    return lax.dot_general(a, b, (dims, ((), ())), preferred_element_type=F32)


def _split2(x):
    hi = x.astype(BF16)
    return hi, (x - hi.astype(F32)).astype(BF16)


def _doth(a, b, dims=NN):
    a_hi, a_lo = _split2(a)
    b_hi, b_lo = _split2(b)
    return _dot(a_hi, b_hi, dims) + _dot(a_hi, b_lo, dims) + _dot(a_lo, b_hi, dims)


def _dotb(a, b, dims=NN):
    return _dot(a.astype(BF16), b.astype(BF16), dims)


def _split3(x):
    hi = x.astype(BF16)
    r1 = x - hi.astype(F32)
    mid = r1.astype(BF16)
    lo = (r1 - mid.astype(F32)).astype(BF16)
    return hi, mid, lo


def _dot_exact_rhs(x, e):
    hi, mid, lo = _split3(x)
    return _dot(hi, e) + _dot(mid, e) + _dot(lo, e)


def _dot_exact_lhs(e, x):
    hi, mid, lo = _split3(x)
    return _dot(e, hi) + _dot(e, mid) + _dot(e, lo)


def _params(sem, vmem_mb):
    return pltpu.CompilerParams(dimension_semantics=sem, vmem_limit_bytes=vmem_mb * 1024 * 1024)


def _full(a):
    nd = a.ndim
    return pl.BlockSpec(a.shape, lambda *_: (0,) * nd)


SEG_QKV = ATTN_WIDTH + 2 * KV_WIDTH
SEG_P = SEG_QKV + RWKV_PAD
SEG_END = SEG_P + 2 * D_MODEL


def _inproj_body(x_ref, g_ref, w_ref, b_ref, rope_ref, q_ref, k_ref, v_ref, p_ref, ga_ref, gb_ref):
    x = x_ref[...]
    h = (x * lax.rsqrt(jnp.mean(x * x, axis=-1, keepdims=True) + NORM_EPS) * g_ref[...]).astype(BF16)

    def seg(lo, hi):
        return _dot(h, w_ref[:, lo:hi]) + b_ref[:, lo:hi]

    cos = rope_ref[:, 0:LANES]
    sin_up = rope_ref[:, LANES:2 * LANES]
    sin_dn = rope_ref[:, 2 * LANES:3 * LANES]

    def rope(z):
        return z * cos + pltpu.roll(z, ROT_DIM // 2, 1) * sin_up + pltpu.roll(z, LANES - ROT_DIM // 2, 1) * sin_dn

    qkv = seg(0, SEG_QKV)
    for c in range(ATTN_WIDTH // LANES):
        q_ref[:, c * LANES:(c + 1) * LANES] = rope(qkv[:, c * LANES:(c + 1) * LANES])
    k_ref[...] = rope(qkv[:, OFF_K:OFF_V])
    v_ref[...] = qkv[:, OFF_V:SEG_QKV]
    p_ref[...] = seg(SEG_QKV, SEG_P)
    ga_ref[...] = jax.nn.sigmoid(seg(SEG_P, SEG_P + D_MODEL))
    gb_ref[...] = jax.nn.sigmoid(seg(SEG_P + D_MODEL, SEG_END))


def _in_proj(x2, g1, w_cat, b_cat, rope_tab, tm):
    m = x2.shape[0]
    nrep = rope_tab.shape[0] // tm
    widths = (ATTN_WIDTH, KV_WIDTH, KV_WIDTH, RWKV_PAD, D_MODEL, D_MODEL)
    return pl.pallas_call(
        _inproj_body,
        grid=(m // tm,),
        in_specs=[
            pl.BlockSpec((tm, D_MODEL), lambda i: (i, 0)),
            _full(g1), _full(w_cat), _full(b_cat),
            pl.BlockSpec((tm, 3 * LANES), lambda i: (i % nrep, 0)),
        ],
        out_specs=[pl.BlockSpec((tm, w), lambda i: (i, 0)) for w in widths],
        out_shape=[jax.ShapeDtypeStruct((m, w), F32) for w in widths],
        compiler_params=_params(("parallel",), 56),
        name="in_proj",
    )(x2, g1, w_cat, b_cat, rope_tab)


def _rope_table(pos, rows):
    t = pos.shape[0]
    half = ROT_DIM // 2
    inv_freq = ROPE_THETA ** (-jnp.arange(half, dtype=F32) / half)
    ang = pos.astype(F32)[:, None] * inv_freq[None, :]
    cos, sin = jnp.cos(ang), jnp.sin(ang)
    rest = HEAD_DIM - ROT_DIM
    z8 = jnp.zeros((t, half), F32)
    cos64 = jnp.concatenate([cos, cos, jnp.ones((t, rest), F32)], axis=1)
    up64 = jnp.concatenate([z8, sin, jnp.zeros((t, rest), F32)], axis=1)
    dn64 = jnp.concatenate([-sin, z8, jnp.zeros((t, rest), F32)], axis=1)
    tab = jnp.concatenate([jnp.tile(a, (1, LANES // HEAD_DIM)) for a in (cos64, up64, dn64)], axis=1)
    if rows > t:
        tab = jnp.tile(tab, (rows // t, 1))
    return tab


def _softmax_av(s, sink, vv):
    m = jnp.maximum(jnp.max(s, axis=-1, keepdims=True), sink)
    p = jnp.exp(s - m)
    den = jnp.sum(p, axis=-1, keepdims=True) + jnp.exp(sink - m)
    return _dot(p.astype(BF16), vv) / den


def _attn_prompt_body(sink_ref, q_ref, kc_ref, kp_ref, vc_ref, vp_ref, o_ref):
    n = pl.program_id(1)
    blk = q_ref.shape[0]
    i = lax.broadcasted_iota(I32, (blk, 2 * blk), 0)
    j = lax.broadcasted_iota(I32, (blk, 2 * blk), 1)
    diff = blk + i - j
    j_lo = jnp.where(n > 0, 0, blk)
    mask = (diff >= 0) & (diff < WINDOW) & (j >= j_lo)
    outs = []
    for g in range(N_KV_HEADS):
        sl = slice(g * HEAD_DIM, (g + 1) * HEAD_DIM)
        kk = jnp.concatenate([kp_ref[:, sl], kc_ref[:, sl]], axis=0).astype(BF16)
        vv = jnp.concatenate([vp_ref[:, sl], vc_ref[:, sl]], axis=0).astype(BF16)
        for hq in range(Q_PER_KV):
            h = g * Q_PER_KV + hq
            qh = q_ref[:, h * HEAD_DIM:(h + 1) * HEAD_DIM].astype(BF16)
            s = jnp.where(mask, _dot(qh, kk, NT) * ATTN_SCALE, NEG_INF)
            outs.append(_softmax_av(s, sink_ref[h], vv))
    o_ref[...] = jnp.concatenate(outs, axis=1).astype(o_ref.dtype)


def _attn_prompt(sinks, q3, k3, v3):
    b, t, _ = q3.shape
    blk = WINDOW
    cur = lambda w: pl.BlockSpec((None, blk, w), lambda bi, n: (bi, n, 0))
    prev = lambda w: pl.BlockSpec((None, blk, w), lambda bi, n: (bi, jnp.maximum(n - 1, 0), 0))
    return pl.pallas_call(
        _attn_prompt_body,
        grid=(b, t // blk),
        in_specs=[pl.BlockSpec(memory_space=pltpu.SMEM), cur(ATTN_WIDTH), cur(KV_WIDTH), prev(KV_WIDTH),
                  cur(KV_WIDTH), prev(KV_WIDTH)],
        out_specs=cur(ATTN_WIDTH),
        out_shape=jax.ShapeDtypeStruct((b, t, ATTN_WIDTH), BF16),
        compiler_params=_params(("parallel", "arbitrary"), 32),
        name="attn_prompt",
    )(sinks, q3, k3, k3, v3, v3)


def _attn_sample_body(sink_ref, *refs, t):
    for bi in range(refs[0].shape[0]):
        _attn_sample_one(sink_ref, *[r.at[bi] for r in refs], t=t)


def _attn_sample_one(sink_ref, q_ref, kn_ref, vn_ref, kc_ref, vc_ref, o_ref, *, t):
    rows = Q_PER_KV * t
    n_past = kc_ref.shape[0]
    ri = lax.broadcasted_iota(I32, (rows, 1), 0)
    qi = ri % t
    jc = lax.broadcasted_iota(I32, (rows, n_past), 1)
    mask_c = (n_past + qi - jc) < WINDOW
    for g in range(N_KV_HEADS):
        sl = slice(g * HEAD_DIM, (g + 1) * HEAD_DIM)
        sink = jnp.zeros((rows, 1), F32)
        for hq in range(Q_PER_KV):
            sink = jnp.where(ri // t == hq, sink_ref[g * Q_PER_KV + hq], sink)
        qg = q_ref[g]
        qb = qg.astype(BF16)
        kn = kn_ref[:, sl]
        vn = vn_ref[:, sl]
        s_c = jnp.where(mask_c, _dot(qb, kc_ref[:, sl].astype(BF16), NT) * ATTN_SCALE, NEG_INF)
        s_n = []
        for jn in range(t):
            sj = jnp.sum(qg * kn[jn:jn + 1, :], axis=-1, keepdims=True) * ATTN_SCALE
            s_n.append(jnp.where(qi >= jn, sj, NEG_INF))
        m = jnp.maximum(jnp.max(s_c, axis=-1, keepdims=True), sink)
        for sj in s_n:
            m = jnp.maximum(m, sj)
        p_c = jnp.exp(s_c - m)
        den = jnp.sum(p_c, axis=-1, keepdims=True) + jnp.exp(sink - m)
        o = _dot(p_c.astype(BF16), vc_ref[:, sl].astype(BF16))
        for jn in range(t):
            p_n = jnp.exp(s_n[jn] - m)
            den = den + p_n
            o = o + p_n * vn[jn:jn + 1, :]
        o_ref[g] = (o / den).astype(o_ref.dtype)


def _attn_sample(sinks, qg, kn, vn, cache_k, cache_v, t):
    b = qg.shape[0]
    rows = Q_PER_KV * t
    n_past = cache_k.shape[1]
    bb = DECODE_BATCH_BLOCK if b % DECODE_BATCH_BLOCK == 0 else 1
    qspec = pl.BlockSpec((bb, N_KV_HEADS, rows, HEAD_DIM), lambda bi: (bi, 0, 0, 0))
    nspec = pl.BlockSpec((bb, t, KV_WIDTH), lambda bi: (bi, 0, 0))
    cspec = pl.BlockSpec((bb, n_past, KV_WIDTH), lambda bi: (bi, 0, 0))
    return pl.pallas_call(
        functools.partial(_attn_sample_body, t=t),
        grid=(b // bb,),
        in_specs=[pl.BlockSpec(memory_space=pltpu.SMEM), qspec, nspec, nspec, cspec, cspec],
        out_specs=qspec,
        out_shape=jax.ShapeDtypeStruct((b, N_KV_HEADS, rows, HEAD_DIM), BF16),
        compiler_params=_params(("parallel",), 32),
        name="attn_sample",
    )(sinks, qg, kn, vn, cache_k, cache_v)


def _stack_heads(z):
    ch, width = z.shape
    rows = lax.broadcasted_iota(I32, (RWKV_HEADS * ch, width), 0) // ch
    lanes = lax.broadcasted_iota(I32, (RWKV_HEADS * ch, width), 1) // RWKV_HEAD_DIM
    return jnp.where(rows == lanes, jnp.concatenate([z] * RWKV_HEADS, axis=0), 0.0)


def _head_pack(ch):
    return LANES // ch if ch >= RWKV_HEAD_DIM else 1


def _rwkv_prep_body(p_ref, pprev_ref, shift_ref, mu_ref, w0_ref, a0_ref, kk_ref, ka_ref, w2_ref, a2_ref, g2_ref,
                    eblk_ref, *outs, t_real):
    c = pl.program_id(1)
    nb, ch = p_ref.shape[0], p_ref.shape[1]
    row = lax.broadcasted_iota(I32, (ch, 1), 0)
    xms = []
    for bi in range(nb):
        pc = p_ref[bi]
        first = jnp.where(c == 0, shift_ref[bi], pprev_ref[bi, SUBLANES - 1:SUBLANES, :])
        prev = jnp.where(row == 0, first, pltpu.roll(pc, 1, 0))
        xms.append(pc + mu_ref[...] * (prev - pc))
    xm = jnp.concatenate(xms, axis=0)
    valid = (c * ch + lax.broadcasted_iota(I32, (nb * ch, 1), 0) % ch) < t_real
    w3 = 3 * RWKV_WIDTH
    r = xm[:, 0:RWKV_WIDTH]
    k = xm[:, RWKV_WIDTH:2 * RWKV_WIDTH]
    v = xm[:, 2 * RWKV_WIDTH:w3]
    lo = xm[:, w3:w3 + LORA_PAD]
    lane = lax.broadcasted_iota(I32, lo.shape, 1)
    act = jnp.where(lane < DECAY_LORA, jnp.tanh(lo),
                    jnp.where(lane < DECAY_LORA + AAA_LORA, lo, jax.nn.sigmoid(lo)))
    z = w0_ref[...] + _doth(act, w2_ref[...])
    w_log = -(jnp.maximum(-z, 0.0) + jnp.log1p(jnp.exp(-jnp.abs(z)))) - 0.5
    lw = jnp.where(valid, -jnp.exp(w_log), 0.0)
    a = jax.nn.sigmoid(a0_ref[...] + _doth(act, a2_ref[...]))
    g = _dot(act.astype(BF16), g2_ref[...].astype(BF16))
    kk = k * kk_ref[...]
    ss = _dot_exact_rhs(kk * kk, eblk_ref[...])
    kk = jnp.where(valid, kk * lax.rsqrt(jnp.maximum(ss, 1e-24)), 0.0)
    kmod = jnp.where(valid, k * (1.0 + (a - 1.0) * ka_ref[...]), 0.0)
    bb = kk * a
    for bi in range(nb):
        rows = slice(bi * ch, (bi + 1) * ch)
        _rwkv_prep_chunk(lw[rows], kk[rows], kmod[rows], bb[rows], r[rows], v[rows], g[rows],
                         *[o.at[bi] for o in outs])


def _rwkv_prep_chunk(lw, kk, kmod, bb, r, v, g, akb_ref, akk_ref, qrk_ref, qrb_ref, kkt_ref, rt_ref, kdec_ref,
                     bdec_ref, kmod_ref, r_ref, v_ref, g_ref, gamc_ref):
    ch = lw.shape[0]
    ti = lax.broadcasted_iota(I32, (ch, ch), 0)
    tj = lax.broadcasted_iota(I32, (ch, ch), 1)
    tri = jnp.where(ti >= tj, 1.0, 0.0).astype(BF16)
    cum = _dot_exact_lhs(tri, lw)
    cum_c = cum[ch - 1:ch, :]
    e_neg = jnp.exp(-cum)
    e_rem = jnp.exp(cum_c - cum)
    kkt = kk * jnp.exp(cum - lw)
    rt = r * jnp.exp(cum)
    khat = kmod * e_neg
    bhat = bb * e_neg
    strict = ti > tj
    incl = ti >= tj
    hs = range(RWKV_HEADS)
    sl = [slice(h * RWKV_HEAD_DIM, (h + 1) * RWKV_HEAD_DIM) for h in hs]
    split = lambda z: [z[:, s].astype(BF16) for s in sl]
    kkt_s, rt_s, khat_s, bhat_s = split(kkt), split(rt), split(khat), split(bhat)
    if ch < RWKV_HEAD_DIM:
        kkt_bd, rt_bd = _stack_heads(kkt).astype(BF16), _stack_heads(rt).astype(BF16)
        blocks = lambda m: [m[h * ch:(h + 1) * ch, :] for h in hs]
        akb = blocks(_dot(kkt_bd, bhat.astype(BF16), NT))
        akk = blocks(_dot(kkt_bd, khat.astype(BF16), NT))
        qrk = blocks(_dot(rt_bd, khat.astype(BF16), NT))
        qrb = blocks(_dot(rt_bd, bhat.astype(BF16), NT))
    else:
        akb = [_dot(kkt_s[h], bhat_s[h], NT) for h in hs]
        akk = [_dot(kkt_s[h], khat_s[h], NT) for h in hs]
        qrk = [_dot(rt_s[h], khat_s[h], NT) for h in hs]
        qrb = [_dot(rt_s[h], bhat_s[h], NT) for h in hs]
    pack = _head_pack(ch)
    for gi in range(RWKV_HEADS // pack):
        grp = range(gi * pack, (gi + 1) * pack)
        side = lambda mats, keep: jnp.concatenate([jnp.where(keep, mats[h], 0.0) for h in grp], axis=1)
        akb_ref[gi] = side(akb, strict)
        akk_ref[gi] = side(akk, strict)
        qrk_ref[gi] = side(qrk, incl)
        qrb_ref[gi] = side(qrb, incl)
    kkt_ref[...] = kkt
    rt_ref[...] = rt
    kdec_ref[...] = kmod * e_rem
    bdec_ref[...] = bb * e_rem
    kmod_ref[...] = kmod
    r_ref[...] = r
    v_ref[...] = v
    g_ref[...] = g
    gamc_ref[...] = jnp.exp(cum_c)


def _rwkv_prep(p3, shift0, lp, eblk, ch, t_real, bb):
    b, tp, _ = p3.shape
    nc = tp // ch
    per8 = ch // SUBLANES
    pack = _head_pack(ch)
    ng, gw = RWKV_HEADS // pack, pack * ch
    mats = jax.ShapeDtypeStruct((b, nc, ng, ch, gw), F32)
    rows = jax.ShapeDtypeStruct((b, tp, RWKV_WIDTH), F32)
    mspec = pl.BlockSpec((bb, None, ng, ch, gw), lambda bi, ci: (bi, ci, 0, 0, 0))
    rspec = pl.BlockSpec((bb, ch, RWKV_WIDTH), lambda bi, ci: (bi, ci, 0))
    consts = (lp["mu"], lp["w0"], lp["a0"], lp["k_k"], lp["k_a"], lp["w2p"], lp["a2p"], lp["g2p"], eblk)
    return pl.pallas_call(
        functools.partial(_rwkv_prep_body, t_real=t_real),
        grid=(b // bb, nc),
        in_specs=[
            pl.BlockSpec((bb, ch, RWKV_PAD), lambda bi, ci: (bi, ci, 0)),
            pl.BlockSpec((bb, SUBLANES, RWKV_PAD), lambda bi, ci: (bi, jnp.maximum(ci * per8 - 1, 0), 0)),
            pl.BlockSpec((bb, 1, RWKV_PAD), lambda bi, ci: (bi, 0, 0)),
        ] + [_full(a) for a in consts],
        out_specs=[mspec] * 4 + [rspec] * 8 + [pl.BlockSpec((bb, None, 1, RWKV_WIDTH), lambda bi, ci: (bi, ci, 0, 0))],
        out_shape=[mats] * 4 + [rows] * 8 + [jax.ShapeDtypeStruct((b, nc, 1, RWKV_WIDTH), F32)],
        compiler_params=_params(("parallel", "parallel"), 48),
        name="rwkv_prep",
    )(p3, p3, shift0, *consts)


def _tri_solve_body(a_ref, t_ref, w_scr, *, ch, pack):
    k = pl.program_id(1)
    kc_w = a_ref.shape[1]
    nk = (pl.num_programs(1) - 1) // 2
    groups = a_ref.shape[0] // LANES
    gw = pack * ch

    @pl.when(k < nk)
    def _():
        xt = a_ref[...].T
        for g in range(groups):
            w_scr[pl.ds(k * (kc_w * groups) + g, kc_w, stride=groups), :] = xt[:, g * LANES:(g + 1) * LANES]

    @pl.when(k == nk)
    def _():
        cols = min(SOLVE_COLS, ch)
        cid = lax.broadcasted_iota(I32, (cols * groups, LANES), 0) // groups
        at = lambda e, n: pl.ds(pl.multiple_of(e * groups, groups), n * groups)

        for j in range(pack):
            def row_body(s, carry, j=j):
                for kc in range(ch // cols):
                    c0 = kc * cols

                    @pl.when(s >= c0)
                    def _():
                        acc0 = jnp.where(cid == s - c0, 1.0, 0.0).astype(F32)

                        def r_body(r, acc):
                            a_sr = w_scr[at(s * gw + j * ch + r, 1), :]
                            return acc - jnp.tile(a_sr, (cols, 1)) * w_scr[at(r * gw + j * ch + c0, cols), :]

                        w_scr[at(s * gw + j * ch + c0, cols), :] = lax.fori_loop(c0, s, r_body, acc0)
                return carry

            lax.fori_loop(0, ch, row_body, 0)

    @pl.when(k > nk)
    def _():
        kk = k - nk - 1
        parts = [w_scr[pl.ds(kk * (kc_w * groups) + g, kc_w, stride=groups), :] for g in range(groups)]
        t_ref[...] = jnp.concatenate(parts, axis=1).T


def _tri_solve(akb):
    b, nc, ng, ch, gw = akb.shape
    nrow = b * nc * ng
    slab = SUBLANES * LANES
    ne = ch * gw
    row_pad = -nrow % slab
    ne_pad = -ne % LANES
    a2 = akb.reshape(nrow, ne)
    if row_pad or ne_pad:
        a2 = jnp.pad(a2, ((0, row_pad), (0, ne_pad)))
    nk = (ne + ne_pad) // LANES
    t2 = pl.pallas_call(
        functools.partial(_tri_solve_body, ch=ch, pack=gw // ch),
        grid=((nrow + row_pad) // slab, 2 * nk + 1),
        in_specs=[pl.BlockSpec((slab, LANES), lambda gi, k: (gi, jnp.minimum(k, nk - 1)))],
        out_specs=pl.BlockSpec((slab, LANES), lambda gi, k: (gi, jnp.clip(k - nk - 1, 0, nk - 1))),
        out_shape=jax.ShapeDtypeStruct(a2.shape, F32),
        scratch_shapes=[pltpu.VMEM(((ne + ne_pad) * SUBLANES, LANES), F32)],
        compiler_params=_params(("parallel", "arbitrary"), 48),
        name="rwkv_tri_solve",
    )(a2)
    if row_pad or ne_pad:
        t2 = t2[:nrow, :ne]
    return t2.reshape(b, nc, ng, ch, gw)


def _rwkv_scan_body(t_ref, akk_ref, qrk_ref, qrb_ref, kkt_ref, rt_ref, kdec_ref, bdec_ref, kmod_ref, r_ref, v_ref,
                    g_ref, gamc_ref, wkv0_ref, eblk_ref, rk_ref, lnw_ref, lnb_ref, rw_ref, wkv_ref, s_scr):
    nb, ch = kkt_ref.shape[0], kkt_ref.shape[1]
    nd = RWKV_HEAD_DIM
    per_row = (t_ref, akk_ref, qrk_ref, qrb_ref, kkt_ref, rt_ref, kdec_ref, bdec_ref, v_ref, gamc_ref, wkv0_ref,
               wkv_ref, s_scr)
    y = jnp.concatenate([_rwkv_scan_chunk(*[r.at[bi] for r in per_row]) for bi in range(nb)], axis=0)
    rows = lambda ref: ref[...].reshape(nb * ch, RWKV_WIDTH)
    eblk = eblk_ref[...]
    inv_n = 1.0 / nd
    mean = _dot_exact_rhs(y, eblk) * inv_n
    yc = y - mean
    var = _dot_exact_rhs(yc * yc, eblk) * inv_n
    yn = yc * lax.rsqrt(var + GN_EPS) * lnw_ref[...] + lnb_ref[...]
    bonus = _dot_exact_rhs(rows(r_ref) * rows(kmod_ref) * rk_ref[...], eblk) * rows(v_ref)
    rw_ref[...] = ((yn + bonus) * rows(g_ref)).reshape(nb, ch, RWKV_WIDTH).astype(rw_ref.dtype)


def _rwkv_scan_chunk(t_ref, akk_ref, qrk_ref, qrb_ref, kkt_ref, rt_ref, kdec_ref, bdec_ref, v_ref, gamc_ref,
                     wkv0_ref, wkv_ref, s_scr):
    c = pl.program_id(1)
    nd = RWKV_HEAD_DIM

    @pl.when(c == 0)
    def _():
        s_scr[...] = wkv0_ref[...]

    hs = range(RWKV_HEADS)
    sl = [slice(h * nd, (h + 1) * nd) for h in hs]
    ch = kkt_ref.shape[0]
    pack = _head_pack(ch)
    mat = lambda ref, h: ref[h // pack][:, (h % pack) * ch:(h % pack + 1) * ch]
    tmat = [mat(t_ref, h) for h in hs]
    vh = [v_ref[:, sl[h]] for h in hs]
    bdec = [bdec_ref[:, sl[h]] for h in hs]
    qrb = [mat(qrb_ref, h) for h in hs]
    w1 = [_dotb(tmat[h], kkt_ref[:, sl[h]]) for h in hs]
    av = [_dotb(mat(akk_ref, h), vh[h]) for h in hs]
    qv = [_dotb(mat(qrk_ref, h), vh[h]) for h in hs]
    vk = [_dotb(vh[h], kdec_ref[:, sl[h]], TN) for h in hs]
    w2 = [_dotb(tmat[h], av[h]) for h in hs]
    qw1 = [_dotb(qrb[h], w1[h]) for h in hs]
    bw1 = [_dotb(bdec[h], w1[h], TN) for h in hs]
    qw2 = [_dotb(qrb[h], w2[h]) for h in hs]
    wb = [_dotb(w2[h], bdec[h], TN) for h in hs]
    s0 = [s_scr[h] for h in hs]
    ys = [_dotb(rt_ref[:, sl[h]] - qw1[h], s0[h], NT) + (qv[h] - qw2[h]) for h in hs]
    for h in hs:
        s_scr[h] = s0[h] * gamc_ref[:, sl[h]] - _dotb(s0[h], bw1[h], NT) + (vk[h] - wb[h])
    @pl.when(c == pl.num_programs(1) - 1)
    def _():
        wkv_ref[...] = s_scr[...]

    return jnp.concatenate(ys, axis=1)


def _rwkv_scan(tmat, prep, wkv0, lp, eblk, bb):
    akk, qrk, qrb, kkt, rt, kdec, bdec, kmod, r, v, g, gamc = prep
    b, nc, ng, ch, gw = tmat.shape
    nh = RWKV_HEADS
    tp = kkt.shape[1]
    nd = RWKV_HEAD_DIM
    mspec = pl.BlockSpec((bb, None, ng, ch, gw), lambda bi, ci: (bi, ci, 0, 0, 0))
    rspec = pl.BlockSpec((bb, ch, RWKV_WIDTH), lambda bi, ci: (bi, ci, 0))
    sspec = pl.BlockSpec((bb, nh, nd, nd), lambda bi, ci: (bi, 0, 0, 0))
    consts = (eblk, lp["r_k"], lp["ln_w"], lp["ln_b"])
    return pl.pallas_call(
        _rwkv_scan_body,
        grid=(b // bb, nc),
        in_specs=[mspec] * 4 + [rspec] * 8
        + [pl.BlockSpec((bb, None, 1, RWKV_WIDTH), lambda bi, ci: (bi, ci, 0, 0)), sspec]
        + [_full(a) for a in consts],
        out_specs=[rspec, sspec],
        out_shape=[jax.ShapeDtypeStruct((b, tp, RWKV_WIDTH), BF16), jax.ShapeDtypeStruct((b, nh, nd, nd), F32)],
        scratch_shapes=[pltpu.VMEM((bb, nh, nd, nd), F32)],
        compiler_params=_params(("parallel", "arbitrary"), 48),
        name="rwkv_scan",
    )(tmat, akk, qrk, qrb, kkt, rt, kdec, bdec, kmod, r, v, g, gamc, wkv0, *consts)


def _merge_body(x_ref, att_ref, rw_ref, ga_ref, gb_ref, wa_ref, wb_ref, wo_ref, g2_ref, wq_ref,
                x1_ref, h2_ref, qp_ref):
    merged = ga_ref[...] * _dot(att_ref[...], wa_ref[...]) + gb_ref[...] * _dot(rw_ref[...], wb_ref[...])
    x1 = x_ref[...] + _dot(merged.astype(BF16), wo_ref[...])
    x1_ref[...] = x1
    h2 = (x1 * lax.rsqrt(jnp.mean(x1 * x1, axis=-1, keepdims=True) + NORM_EPS) * g2_ref[...]).astype(BF16)
    h2_ref[...] = h2
    qp_ref[...] = _dot(h2, wq_ref[...]).astype(qp_ref.dtype)


def _merge(x2, att, rw, ga, gb, lp, tm):
    m = x2.shape[0]
    kq = lp["w_query"].shape[1]
    row = lambda w: pl.BlockSpec((tm, w), lambda i: (i, 0))
    consts = (lp["w_up_a"], lp["w_up_b"], lp["w_o"], lp["norm2_g"], lp["w_query"])
    return pl.pallas_call(
        _merge_body,
        grid=(m // tm,),
        in_specs=[row(D_MODEL), row(ATTN_WIDTH), row(RWKV_WIDTH), row(D_MODEL), row(D_MODEL)]
        + [_full(a) for a in consts],
        out_specs=[row(D_MODEL), row(D_MODEL), row(kq)],
        out_shape=[jax.ShapeDtypeStruct((m, D_MODEL), F32), jax.ShapeDtypeStruct((m, D_MODEL), BF16),
                   jax.ShapeDtypeStruct((m, kq), BF16)],
        compiler_params=_params(("parallel",), 56),
        name="merge",
    )(x2, att, rw, ga, gb, *consts)


def _argmax_rows(src_ref):
    nslab = src_ref.shape[0] // SUBLANES
    row0 = lax.broadcasted_iota(I32, (SUBLANES, src_ref.shape[1]), 0).astype(F32)
    vals = [src_ref[i * SUBLANES:(i + 1) * SUBLANES, :] for i in range(nslab)]
    rows = [row0 + float(i * SUBLANES) for i in range(nslab)]
    while len(vals) > 1:
        nv, nr = [], []
        for a in range(0, len(vals) - 1, 2):
            take = vals[a + 1] > vals[a]
            nv.append(jnp.maximum(vals[a], vals[a + 1]))
            nr.append(jnp.where(take, rows[a + 1], rows[a]))
        if len(vals) % 2:
            nv.append(vals[-1])
            nr.append(rows[-1])
        vals, rows = nv, nr
    v, i = vals[0], rows[0]
    for shift in (4, 2, 1):
        v2 = pltpu.roll(v, shift, 0)
        i2 = pltpu.roll(i, shift, 0)
        take = (v2 > v) | ((v2 == v) & (i2 < i))
        v = jnp.where(take, v2, v)
        i = jnp.where(take, i2, i)
    return jnp.max(v, axis=0, keepdims=True), jnp.max(i, axis=0, keepdims=True)


def _extract_top(jobs, n_out):
    def body(r, carry):
        for src_ref, val_ref, idx_ref in jobs:
            rowf = lax.broadcasted_iota(I32, src_ref.shape, 0).astype(F32)
            m, idx = _argmax_rows(src_ref)
            val_ref[pl.ds(r, 1), :] = m
            idx_ref[pl.ds(r, 1), :] = idx
            src_ref[...] = jnp.where(rowf == idx, NEG_INF, src_ref[...])
        return carry

    lax.fori_loop(0, n_out, body, 0)


def _cand_groups(k):
    groups = []
    a = 0
    while a < k and k // (a + 1) > 1:
        nb = k // (a + 1)
        groups.append((a, -(-nb // SUBLANES) * SUBLANES, nb))
        a += 1
    return groups, a


def _n_cand_rows(k):
    groups, a_tail = _cand_groups(k)
    return sum(rows for _, rows, _ in groups) + k - a_tail


def _peer_topk_head(qp_ref, sk_ref, h, i1_scr, i2_scr, gate_scr, s_scr, cand_scr, sv_scr, si_scr, bs_scr, bp_scr):
    k = PEER_TOPK
    for c in range(2):
        col = (2 * h + c) * D_KEY_HALF
        s_scr[c] = _dot(sk_ref[h, c], qp_ref[:, col:col + D_KEY_HALF], NT)
    _extract_top([(s_scr.at[c], sv_scr.at[c], si_scr.at[c]) for c in range(2)], k)
    sv0 = sv_scr[0]
    sv1 = sv_scr[1]
    groups, a_tail = _cand_groups(k)
    off = 0
    for a, rows, nb in groups:
        blk = sv0[a:a + 1, :] + sv1[0:rows, :]
        if nb < rows:
            blk = jnp.where(lax.broadcasted_iota(I32, blk.shape, 0) < nb, blk, NEG_INF)
        cand_scr[off:off + rows, :] = blk
        off += rows
    cand_scr[off:off + k - a_tail, :] = sv0[a_tail:k, :] + sv1[0:1, :]
    _extract_top([(cand_scr, bs_scr, bp_scr)], k)
    best_s = bs_scr[...]
    pos = bp_scr[...]
    a_sel = jnp.zeros_like(pos)
    b_sel = pos
    off = 0
    for a, rows, nb in groups:
        if a > 0:
            a_sel = jnp.where(pos >= off, float(a), a_sel)
            b_sel = jnp.where(pos >= off, pos - off, b_sel)
        off += rows
    a_sel = jnp.where(pos >= off, a_tail + pos - off, a_sel)
    b_sel = jnp.where(pos >= off, 0.0, b_sel)
    si0 = si_scr[0]
    si1 = si_scr[1]
    i1 = jnp.zeros_like(pos)
    i2 = jnp.zeros_like(pos)
    for a in range(k):
        i1 = jnp.where(a_sel == a, si0[a:a + 1, :], i1)
        i2 = jnp.where(b_sel == a, si1[a:a + 1, :], i2)
    e = jnp.exp(best_s - jnp.max(best_s, axis=0, keepdims=True))
    i1_scr[h * k:(h + 1) * k, :] = i1
    i2_scr[h * k:(h + 1) * k, :] = i2
    gate_scr[h * k:(h + 1) * k, :] = e / jnp.sum(e, axis=0, keepdims=True)


def _peer_topk_body(qp_ref, sk_ref, i1_ref, i2_ref, gate_ref, i1_scr, i2_scr, gate_scr, *work):
    for h in range(PEER_HEADS):
        _peer_topk_head(qp_ref, sk_ref, h, i1_scr, i2_scr, gate_scr, *work)
    i1_ref[...] = i1_scr[...].T.astype(I32)
    i2_ref[...] = i2_scr[...].T.astype(I32)
    gate_ref[...] = gate_scr[...].T


def _peer_topk(qp, sub_keys, tm):
    m = qp.shape[0]
    k = PEER_TOPK
    ospec = pl.BlockSpec((tm, N_SLOTS), lambda i: (i, 0))
    slots = pltpu.VMEM((N_SLOTS, tm), F32)
    return pl.pallas_call(
        _peer_topk_body,
        grid=(m // tm,),
        in_specs=[pl.BlockSpec((tm, qp.shape[1]), lambda i: (i, 0)), _full(sub_keys)],
        out_specs=[ospec, ospec, ospec],
        out_shape=[jax.ShapeDtypeStruct((m, N_SLOTS), I32), jax.ShapeDtypeStruct((m, N_SLOTS), I32),
                   jax.ShapeDtypeStruct((m, N_SLOTS), F32)],
        scratch_shapes=[slots, slots, slots,
                        pltpu.VMEM((2, N_KEYS, tm), F32), pltpu.VMEM((_n_cand_rows(k), tm), F32),
                        pltpu.VMEM((2, k, tm), F32), pltpu.VMEM((2, k, tm), F32),
                        pltpu.VMEM((k, tm), F32), pltpu.VMEM((k, tm), F32)],
        compiler_params=_params(("parallel",), 32),
        name="peer_topk",
    )(qp, sub_keys)


def _peer_expert_body(x1_ref, h2_ref, i1_ref, i2_ref, gate_ref, ut_ref, v_ref, gf_ref, y_ref, w_scr):
    j = pl.program_id(1)
    tm = h2_ref.shape[0]
    te = v_ref.shape[0]
    pitch = tm + W_PITCH_PAD

    @pl.when(j == 0)
    def _():
        y_ref[...] = x1_ref[...]
        sub = lax.broadcasted_iota(I32, (N_KEYS, N_SLOTS), 0)

        def tok(t, carry):
            hit1 = sub == i1_ref[pl.ds(t, 1), :]
            hit2 = sub == i2_ref[pl.ds(t, 1), :]
            a = jnp.where(hit1, gate_ref[pl.ds(t, 1), :], 0.0).astype(BF16)
            bm_t = jnp.where(hit2, 1.0, 0.0).T.astype(BF16)
            w_scr[pl.ds(t, N_KEYS, stride=pitch), :] = _dot(a, bm_t)
            return carry

        lax.fori_loop(0, tm, tok, 0, unroll=TOKEN_UNROLL)

    rb = min(tm, EXPERT_ROW_BLOCK)
    pair = 2 * N_KEYS
    for r in range(tm // rb):
        x = h2_ref[r * rb:(r + 1) * rb, :]
        parts = []
        for q in range(te // pair):
            h = _dot(x, ut_ref[:, q * pair:(q + 1) * pair])
            i1 = j * (te // N_KEYS) + 2 * q
            w = jnp.concatenate([w_scr[pl.ds(pl.multiple_of(i1 * pitch + r * rb, SUBLANES), rb), :],
                                 w_scr[pl.ds(pl.multiple_of((i1 + 1) * pitch + r * rb, SUBLANES), rb), :]], axis=1)
            act = 0.5 * h * (1.0 + lax.erf(h * (2.0 ** -0.5)))
            parts.append((w * act).astype(BF16))
        y_ref[r * rb:(r + 1) * rb, :] += _dot(jnp.concatenate(parts, axis=1), v_ref[...])

    @pl.when(j == pl.num_programs(1) - 1)
    def _():
        x2 = y_ref[...]
        y_ref[...] = x2 * lax.rsqrt(jnp.mean(x2 * x2, axis=-1, keepdims=True) + NORM_EPS) * gf_ref[...]


def _peer_experts(x1, h2, i1, i2, gate, eu, ev, gf, tm, te):
    m = x1.shape[0]
    row = lambda w: pl.BlockSpec((tm, w), lambda i, j: (i, 0))
    tab = pl.BlockSpec((te, D_MODEL), lambda i, j: (j, 0))
    tab_t = pl.BlockSpec((D_MODEL, te), lambda i, j: (0, j))
    x1_spec = pl.BlockSpec((tm, D_MODEL), lambda i, j: (i, 0), pipeline_mode=pl.Buffered(1))
    return pl.pallas_call(
        _peer_expert_body,
        grid=(m // tm, N_EXPERTS // te),
        in_specs=[x1_spec, row(D_MODEL), row(N_SLOTS), row(N_SLOTS), row(N_SLOTS), tab_t, tab,
                  pl.BlockSpec((1, D_MODEL), lambda i, j: (0, 0))],
        out_specs=row(D_MODEL),
        out_shape=jax.ShapeDtypeStruct((m, D_MODEL), F32),
        scratch_shapes=[pltpu.VMEM((N_KEYS * (tm + W_PITCH_PAD), N_KEYS), F32)],
        compiler_params=_params(("parallel", "arbitrary"), 60),
        name="peer_experts",
    )(x1, h2, i1, i2, gate, eu, ev, gf)


def _pad_cols(a, width):
    return jnp.pad(a, [(0, 0)] * (a.ndim - 1) + [(0, width - a.shape[-1])])


def _prep_layer_params(l, norm1_g, w_in, b_in, attn_sinks, shift_mu, w0, w2, a0, a2, g2, k_k, k_a, r_k, ln_x_w,
                       ln_x_b, w_up_a, w_up_b, w_o, norm2_g, w_query, sub_keys, expert_u, expert_v):
    wi, bi = w_in[l], b_in[l][None, :]
    cat = lambda z: jnp.concatenate([z[:, :OFF_B], _pad_cols(z[:, OFF_B:OFF_GATE], RWKV_PAD), z[:, OFF_GATE:]], axis=1)
    row = lambda z: z[l].reshape(1, -1)
    lora = lambda z, off: jnp.pad(z[l], ((off, LORA_PAD - off - z.shape[1]), (0, 0)))
    return dict(
        norm1_g=row(norm1_g), w_cat=cat(wi).astype(BF16), b_cat=cat(bi), sinks=attn_sinks[l],
        mu=_pad_cols(row(shift_mu), RWKV_PAD), w0=row(w0), a0=row(a0), k_k=row(k_k), k_a=row(k_a),
        w2p=lora(w2, 0), a2p=lora(a2, DECAY_LORA), g2p=lora(g2, DECAY_LORA + AAA_LORA),
        r_k=row(r_k), ln_w=row(ln_x_w), ln_b=row(ln_x_b),
        w_up_a=w_up_a[l].astype(BF16), w_up_b=w_up_b[l].astype(BF16), w_o=w_o[l].astype(BF16),
        norm2_g=row(norm2_g), w_query=w_query[l].astype(BF16), sub_keys=sub_keys[l].astype(BF16),
        expert_ut=expert_u[l].astype(BF16).T, expert_v=expert_v[l].astype(BF16),
    )


def _row_tile(m, pref):
    return pref if m % pref == 0 else m


def _hybrid_layer(x, pos, lp, eblk, gf, win_k, win_v, wkv0, shift0):
    b, t, _ = x.shape
    m = b * t
    x2 = x.reshape(m, D_MODEL)
    tm = _row_tile(m, 256)
    tmd = _row_tile(m, DENSE_TOKEN_TILE)
    rope_tab = _rope_table(pos, max(t, tmd))
    q, k, v, p, ga, gb = _in_proj(x2, lp["norm1_g"], lp["w_cat"], lp["b_cat"], rope_tab, tmd)

    k3 = k.reshape(b, t, KV_WIDTH)
    v3 = v.reshape(b, t, KV_WIDTH)
    if win_k is None:
        att = _attn_prompt(lp["sinks"], q.reshape(b, t, ATTN_WIDTH), k3, v3).reshape(m, ATTN_WIDTH)
        keep = min(WINDOW, t)
        new_k, new_v = k3[:, t - keep:], v3[:, t - keep:]
    else:
        n_past = win_k.shape[1]
        qg = q.reshape(b, t, N_KV_HEADS, Q_PER_KV, HEAD_DIM).transpose(0, 2, 3, 1, 4)
        qg = qg.reshape(b, N_KV_HEADS, Q_PER_KV * t, HEAD_DIM)
        ck = win_k.reshape(b, n_past, KV_WIDTH)
        cv = win_v.reshape(b, n_past, KV_WIDTH)
        og = _attn_sample(lp["sinks"], qg, k3, v3, ck, cv, t)
        att = og.reshape(b, N_KV_HEADS, Q_PER_KV, t, HEAD_DIM).transpose(0, 3, 1, 2, 4).reshape(m, ATTN_WIDTH)
        new_k = jnp.concatenate([ck, k3], axis=1)[:, t:]
        new_v = jnp.concatenate([cv, v3], axis=1)[:, t:]
    new_k = new_k.reshape(b, -1, N_KV_HEADS, HEAD_DIM)
    new_v = new_v.reshape(b, -1, N_KV_HEADS, HEAD_DIM)

    p3 = p.reshape(b, t, RWKV_PAD)
    if t % RWKV_CHUNK == 0:
        ch, tp = RWKV_CHUNK, t
    else:
        tp = -(-t // SUBLANES) * SUBLANES
        ch = tp
        p3 = jnp.pad(p3, ((0, 0), (0, tp - t), (0, 0)))
    shift_pad = _pad_cols(shift0, RWKV_PAD)[:, None, :]
    want = DECODE_BATCH_BLOCK if tp == ch else PROMPT_BATCH_BLOCK
    bb = want if b % want == 0 else 1
    akb, *prep = _rwkv_prep(p3, shift_pad, lp, eblk, ch, t, bb)
    tmat = _tri_solve(akb)
    rw, wkv_t = _rwkv_scan(tmat, prep, wkv0, lp, eblk, bb)
    rw = rw[:, :t].reshape(m, RWKV_WIDTH)
    shift_t = p3[:, t - 1, :RWKV_COLS]

    x1, h2, qp = _merge(x2, att, rw, ga, gb, lp, tmd)
    i1, i2, gate = _peer_topk(qp, lp["sub_keys"], tm)
    y = _peer_experts(x1, h2, i1, i2, gate, lp["expert_ut"], lp["expert_v"], gf,
                      _row_tile(m, EXPERT_TOKEN_TILE), EXPERT_TILE)
    return y.reshape(b, t, D_MODEL), new_k, new_v, wkv_t, shift_t


def kernel(x_prompt, x_sample, cache_win_k, cache_win_v, state_wkv, state_shift, norm1_g, w_in, b_in, attn_sinks,
           shift_mu, w0, w2, a0, a2, g2, k_k, k_a, r_k, ln_x_w, ln_x_b, w_up_a, w_up_b, w_o, norm2_g, w_query,
           sub_keys, expert_u, expert_v, final_norm_g):
    depth = w_in.shape[0]
    assert depth == 1, "the final RMSNorm is fused into the last layer's expert kernel"
    pos_p = jnp.arange(x_prompt.shape[1])
    pos_s = PAST_LEN + jnp.arange(x_sample.shape[1])
    li = lax.broadcasted_iota(I32, (RWKV_WIDTH, RWKV_WIDTH), 0) // RWKV_HEAD_DIM
    lj = lax.broadcasted_iota(I32, (RWKV_WIDTH, RWKV_WIDTH), 1) // RWKV_HEAD_DIM
    eblk = (li == lj).astype(BF16)
    gf = final_norm_g.reshape(1, D_MODEL)
    lp = _prep_layer_params(0, norm1_g, w_in, b_in, attn_sinks, shift_mu, w0, w2, a0, a2, g2, k_k, k_a, r_k, ln_x_w,
                            ln_x_b, w_up_a, w_up_b, w_o, norm2_g, w_query, sub_keys, expert_u, expert_v)
    bp = x_prompt.shape[0]
    wkv0 = jnp.zeros((bp, RWKV_HEADS, RWKV_HEAD_DIM, RWKV_HEAD_DIM), F32)
    shift0 = jnp.zeros((bp, RWKV_COLS), x_prompt.dtype)
    yp, pk, pv, pw, psh = _hybrid_layer(x_prompt, pos_p, lp, eblk, gf, None, None, wkv0, shift0)
    ys, sk, sv, sw, ssh = _hybrid_layer(x_sample, pos_s, lp, eblk, gf, cache_win_k[0], cache_win_v[0],
                                        state_wkv[0], state_shift[0])
    st = lambda z: z[None]
    return (yp, ys, st(pk), st(pv), st(pw), st(psh), st(sk), st(sv), st(sw), st(ssh))
```

```python
import functools

import jax
import jax.numpy as jnp
from jax import lax
from jax.experimental import pallas as pl
from jax.experimental.pallas import tpu as pltpu

F32, BF16, I32 = jnp.float32, jnp.bfloat16, jnp.int32

D_MODEL = 1024
NORM_EPS = 1e-5
HEAD_DIM = 64
N_Q_HEADS = 8
N_KV_HEADS = 2
Q_PER_KV = N_Q_HEADS // N_KV_HEADS
ATTN_WIDTH = N_Q_HEADS * HEAD_DIM
KV_WIDTH = N_KV_HEADS * HEAD_DIM
WINDOW = 128
ROT_DIM = HEAD_DIM // 4
ROPE_THETA = 500000.0
ATTN_SCALE = HEAD_DIM ** -0.5
RWKV_HEADS = 8
RWKV_HEAD_DIM = 64
RWKV_WIDTH = RWKV_HEADS * RWKV_HEAD_DIM
DECAY_LORA = 32
AAA_LORA = 32
GATE_LORA = 96
GN_EPS = 64e-5
RWKV_COLS = 3 * RWKV_WIDTH + DECAY_LORA + AAA_LORA + GATE_LORA
LORA_PAD = 256
RWKV_PAD = 3 * RWKV_WIDTH + LORA_PAD
OFF_K = ATTN_WIDTH
OFF_V = OFF_K + KV_WIDTH
OFF_B = OFF_V + KV_WIDTH
OFF_GATE = OFF_B + RWKV_COLS
N_KEYS = 128
N_EXPERTS = N_KEYS * N_KEYS
PEER_HEADS = 8
PEER_TOPK = 16
D_KEY_HALF = 128
N_SLOTS = PEER_HEADS * PEER_TOPK
PAST_LEN = 16384
RWKV_CHUNK = 64
LANES = 128
SUBLANES = 8
SOLVE_COLS = 16
TOKEN_UNROLL = 128
DECODE_BATCH_BLOCK = 16
PROMPT_BATCH_BLOCK = 8
DENSE_TOKEN_TILE = 512
EXPERT_TOKEN_TILE = 512
EXPERT_TILE = 1024
EXPERT_ROW_BLOCK = 512
W_PITCH_PAD = 8
NEG_INF = float("-inf")

NN = ((1,), (0,))
NT = ((1,), (1,))
TN = ((0,), (0,))


def _dot(a, b, dims=NN):
    return lax.dot_general(a, b, (dims, ((), ())), preferred_element_type=F32)


def _split2(x):
    hi = x.astype(BF16)
    return hi, (x - hi.astype(F32)).astype(BF16)


def _doth(a, b, dims=NN):
    a_hi, a_lo = _split2(a)
    b_hi, b_lo = _split2(b)
    return _dot(a_hi, b_hi, dims) + _dot(a_hi, b_lo, dims) + _dot(a_lo, b_hi, dims)


def _dotb(a, b, dims=NN):
    return _dot(a.astype(BF16), b.astype(BF16), dims)


def _split3(x):
    hi = x.astype(BF16)
    r1 = x - hi.astype(F32)
    mid = r1.astype(BF16)
    lo = (r1 - mid.astype(F32)).astype(BF16)
    return hi, mid, lo


def _dot_exact_rhs(x, e):
    hi, mid, lo = _split3(x)
    return _dot(hi, e) + _dot(mid, e) + _dot(lo, e)


def _dot_exact_lhs(e, x):
    hi, mid, lo = _split3(x)
    return _dot(e, hi) + _dot(e, mid) + _dot(e, lo)


def _params(sem, vmem_mb):
    return pltpu.CompilerParams(dimension_semantics=sem, vmem_limit_bytes=vmem_mb * 1024 * 1024)


def _full(a):
    nd = a.ndim
    return pl.BlockSpec(a.shape, lambda *_: (0,) * nd)


SEG_QKV = ATTN_WIDTH + 2 * KV_WIDTH
SEG_P = SEG_QKV + RWKV_PAD
SEG_END = SEG_P + 2 * D_MODEL


def _inproj_body(x_ref, g_ref, w_ref, b_ref, rope_ref, q_ref, k_ref, v_ref, p_ref, ga_ref, gb_ref):
    x = x_ref[...]
    h = (x * lax.rsqrt(jnp.mean(x * x, axis=-1, keepdims=True) + NORM_EPS) * g_ref[...]).astype(BF16)

    def seg(lo, hi):
        return _dot(h, w_ref[:, lo:hi]) + b_ref[:, lo:hi]

    cos = rope_ref[:, 0:LANES]
    sin_up = rope_ref[:, LANES:2 * LANES]
    sin_dn = rope_ref[:, 2 * LANES:3 * LANES]

    def rope(z):
        return z * cos + pltpu.roll(z, ROT_DIM // 2, 1) * sin_up + pltpu.roll(z, LANES - ROT_DIM // 2, 1) * sin_dn

    qkv = seg(0, SEG_QKV)
    for c in range(ATTN_WIDTH // LANES):
        q_ref[:, c * LANES:(c + 1) * LANES] = rope(qkv[:, c * LANES:(c + 1) * LANES])
    k_ref[...] = rope(qkv[:, OFF_K:OFF_V])
    v_ref[...] = qkv[:, OFF_V:SEG_QKV]
    p_ref[...] = seg(SEG_QKV, SEG_P)
    ga_ref[...] = jax.nn.sigmoid(seg(SEG_P, SEG_P + D_MODEL))
    gb_ref[...] = jax.nn.sigmoid(seg(SEG_P + D_MODEL, SEG_END))


def _in_proj(x2, g1, w_cat, b_cat, rope_tab, tm):
    m = x2.shape[0]
    nrep = rope_tab.shape[0] // tm
    widths = (ATTN_WIDTH, KV_WIDTH, KV_WIDTH, RWKV_PAD, D_MODEL, D_MODEL)
    return pl.pallas_call(
        _inproj_body,
        grid=(m // tm,),
        in_specs=[
            pl.BlockSpec((tm, D_MODEL), lambda i: (i, 0)),
            _full(g1), _full(w_cat), _full(b_cat),
            pl.BlockSpec((tm, 3 * LANES), lambda i: (i % nrep, 0)),
        ],
        out_specs=[pl.BlockSpec((tm, w), lambda i: (i, 0)) for w in widths],
        out_shape=[jax.ShapeDtypeStruct((m, w), F32) for w in widths],
        compiler_params=_params(("parallel",), 56),
        name="in_proj",
    )(x2, g1, w_cat, b_cat, rope_tab)


def _rope_table(pos, rows):
    t = pos.shape[0]
    half = ROT_DIM // 2
    inv_freq = ROPE_THETA ** (-jnp.arange(half, dtype=F32) / half)
    ang = pos.astype(F32)[:, None] * inv_freq[None, :]
    cos, sin = jnp.cos(ang), jnp.sin(ang)
    rest = HEAD_DIM - ROT_DIM
    z8 = jnp.zeros((t, half), F32)
    cos64 = jnp.concatenate([cos, cos, jnp.ones((t, rest), F32)], axis=1)
    up64 = jnp.concatenate([z8, sin, jnp.zeros((t, rest), F32)], axis=1)
    dn64 = jnp.concatenate([-sin, z8, jnp.zeros((t, rest), F32)], axis=1)
    tab = jnp.concatenate([jnp.tile(a, (1, LANES // HEAD_DIM)) for a in (cos64, up64, dn64)], axis=1)
    if rows > t:
        tab = jnp.tile(tab, (rows // t, 1))
    return tab


def _softmax_av(s, sink, vv):
    m = jnp.maximum(jnp.max(s, axis=-1, keepdims=True), sink)
    p = jnp.exp(s - m)
    den = jnp.sum(p, axis=-1, keepdims=True) + jnp.exp(sink - m)
    return _dot(p.astype(BF16), vv) / den


def _attn_prompt_body(sink_ref, q_ref, kc_ref, kp_ref, vc_ref, vp_ref, o_ref):
    n = pl.program_id(1)
    blk = q_ref.shape[0]
    i = lax.broadcasted_iota(I32, (blk, 2 * blk), 0)
    j = lax.broadcasted_iota(I32, (blk, 2 * blk), 1)
    diff = blk + i - j
    j_lo = jnp.where(n > 0, 0, blk)
    mask = (diff >= 0) & (diff < WINDOW) & (j >= j_lo)
    outs = []
    for g in range(N_KV_HEADS):
        sl = slice(g * HEAD_DIM, (g + 1) * HEAD_DIM)
        kk = jnp.concatenate([kp_ref[:, sl], kc_ref[:, sl]], axis=0).astype(BF16)
        vv = jnp.concatenate([vp_ref[:, sl], vc_ref[:, sl]], axis=0).astype(BF16)
        for hq in range(Q_PER_KV):
            h = g * Q_PER_KV + hq
            qh = q_ref[:, h * HEAD_DIM:(h + 1) * HEAD_DIM].astype(BF16)
            s = jnp.where(mask, _dot(qh, kk, NT) * ATTN_SCALE, NEG_INF)
            outs.append(_softmax_av(s, sink_ref[h], vv))
    o_ref[...] = jnp.concatenate(outs, axis=1).astype(o_ref.dtype)


def _attn_prompt(sinks, q3, k3, v3):
    b, t, _ = q3.shape
    blk = WINDOW
    cur = lambda w: pl.BlockSpec((None, blk, w), lambda bi, n: (bi, n, 0))
    prev = lambda w: pl.BlockSpec((None, blk, w), lambda bi, n: (bi, jnp.maximum(n - 1, 0), 0))
    return pl.pallas_call(
        _attn_prompt_body,
        grid=(b, t // blk),
        in_specs=[pl.BlockSpec(memory_space=pltpu.SMEM), cur(ATTN_WIDTH), cur(KV_WIDTH), prev(KV_WIDTH),
                  cur(KV_WIDTH), prev(KV_WIDTH)],
        out_specs=cur(ATTN_WIDTH),
        out_shape=jax.ShapeDtypeStruct((b, t, ATTN_WIDTH), BF16),
        compiler_params=_params(("parallel", "arbitrary"), 32),
        name="attn_prompt",
    )(sinks, q3, k3, k3, v3, v3)


def _attn_sample_body(sink_ref, *refs, t):
    for bi in range(refs[0].shape[0]):
        _attn_sample_one(sink_ref, *[r.at[bi] for r in refs], t=t)


def _attn_sample_one(sink_ref, q_ref, kn_ref, vn_ref, kc_ref, vc_ref, o_ref, *, t):
    rows = Q_PER_KV * t
    n_past = kc_ref.shape[0]
    ri = lax.broadcasted_iota(I32, (rows, 1), 0)
    qi = ri % t
    jc = lax.broadcasted_iota(I32, (rows, n_past), 1)
    mask_c = (n_past + qi - jc) < WINDOW
    for g in range(N_KV_HEADS):
        sl = slice(g * HEAD_DIM, (g + 1) * HEAD_DIM)
        sink = jnp.zeros((rows, 1), F32)
        for hq in range(Q_PER_KV):
            sink = jnp.where(ri // t == hq, sink_ref[g * Q_PER_KV + hq], sink)
        qg = q_ref[g]
        qb = qg.astype(BF16)
        kn = kn_ref[:, sl]
        vn = vn_ref[:, sl]
        s_c = jnp.where(mask_c, _dot(qb, kc_ref[:, sl].astype(BF16), NT) * ATTN_SCALE, NEG_INF)
        s_n = []
        for jn in range(t):
            sj = jnp.sum(qg * kn[jn:jn + 1, :], axis=-1, keepdims=True) * ATTN_SCALE
            s_n.append(jnp.where(qi >= jn, sj, NEG_INF))
        m = jnp.maximum(jnp.max(s_c, axis=-1, keepdims=True), sink)
        for sj in s_n:
            m = jnp.maximum(m, sj)
        p_c = jnp.exp(s_c - m)
        den = jnp.sum(p_c, axis=-1, keepdims=True) + jnp.exp(sink - m)
        o = _dot(p_c.astype(BF16), vc_ref[:, sl].astype(BF16))
        for jn in range(t):
            p_n = jnp.exp(s_n[jn] - m)
            den = den + p_n
            o = o + p_n * vn[jn:jn + 1, :]
        o_ref[g] = (o / den).astype(o_ref.dtype)


def _attn_sample(sinks, qg, kn, vn, cache_k, cache_v, t):
    b = qg.shape[0]
    rows = Q_PER_KV * t
    n_past = cache_k.shape[1]
    bb = DECODE_BATCH_BLOCK if b % DECODE_BATCH_BLOCK == 0 else 1
    qspec = pl.BlockSpec((bb, N_KV_HEADS, rows, HEAD_DIM), lambda bi: (bi, 0, 0, 0))
    nspec = pl.BlockSpec((bb, t, KV_WIDTH), lambda bi: (bi, 0, 0))
    cspec = pl.BlockSpec((bb, n_past, KV_WIDTH), lambda bi: (bi, 0, 0))
    return pl.pallas_call(
        functools.partial(_attn_sample_body, t=t),
        grid=(b // bb,),
        in_specs=[pl.BlockSpec(memory_space=pltpu.SMEM), qspec, nspec, nspec, cspec, cspec],
        out_specs=qspec,
        out_shape=jax.ShapeDtypeStruct((b, N_KV_HEADS, rows, HEAD_DIM), BF16),
        compiler_params=_params(("parallel",), 32),
        name="attn_sample",
    )(sinks, qg, kn, vn, cache_k, cache_v)


def _stack_heads(z):
    ch, width = z.shape
    rows = lax.broadcasted_iota(I32, (RWKV_HEADS * ch, width), 0) // ch
    lanes = lax.broadcasted_iota(I32, (RWKV_HEADS * ch, width), 1) // RWKV_HEAD_DIM
    return jnp.where(rows == lanes, jnp.concatenate([z] * RWKV_HEADS, axis=0), 0.0)


def _head_pack(ch):
    return LANES // ch if ch >= RWKV_HEAD_DIM else 1


def _rwkv_prep_body(p_ref, pprev_ref, shift_ref, mu_ref, w0_ref, a0_ref, kk_ref, ka_ref, w2_ref, a2_ref, g2_ref,
                    eblk_ref, *outs, t_real):
    c = pl.program_id(1)
    nb, ch = p_ref.shape[0], p_ref.shape[1]
    row = lax.broadcasted_iota(I32, (ch, 1), 0)
    xms = []
    for bi in range(nb):
        pc = p_ref[bi]
        first = jnp.where(c == 0, shift_ref[bi], pprev_ref[bi, SUBLANES - 1:SUBLANES, :])
        prev = jnp.where(row == 0, first, pltpu.roll(pc, 1, 0))
        xms.append(pc + mu_ref[...] * (prev - pc))
    xm = jnp.concatenate(xms, axis=0)
    valid = (c * ch + lax.broadcasted_iota(I32, (nb * ch, 1), 0) % ch) < t_real
    w3 = 3 * RWKV_WIDTH
    r = xm[:, 0:RWKV_WIDTH]
    k = xm[:, RWKV_WIDTH:2 * RWKV_WIDTH]
    v = xm[:, 2 * RWKV_WIDTH:w3]
    lo = xm[:, w3:w3 + LORA_PAD]
    lane = lax.broadcasted_iota(I32, lo.shape, 1)
    act = jnp.where(lane < DECAY_LORA, jnp.tanh(lo),
                    jnp.where(lane < DECAY_LORA + AAA_LORA, lo, jax.nn.sigmoid(lo)))
    z = w0_ref[...] + _doth(act, w2_ref[...])
    w_log = -(jnp.maximum(-z, 0.0) + jnp.log1p(jnp.exp(-jnp.abs(z)))) - 0.5
    lw = jnp.where(valid, -jnp.exp(w_log), 0.0)
    a = jax.nn.sigmoid(a0_ref[...] + _doth(act, a2_ref[...]))
    g = _dot(act.astype(BF16), g2_ref[...].astype(BF16))
    kk = k * kk_ref[...]
    ss = _dot_exact_rhs(kk * kk, eblk_ref[...])
    kk = jnp.where(valid, kk * lax.rsqrt(jnp.maximum(ss, 1e-24)), 0.0)
    kmod = jnp.where(valid, k * (1.0 + (a - 1.0) * ka_ref[...]), 0.0)
    bb = kk * a
    for bi in range(nb):
        rows = slice(bi * ch, (bi + 1) * ch)
        _rwkv_prep_chunk(lw[rows], kk[rows], kmod[rows], bb[rows], r[rows], v[rows], g[rows],
                         *[o.at[bi] for o in outs])


def _rwkv_prep_chunk(lw, kk, kmod, bb, r, v, g, akb_ref, akk_ref, qrk_ref, qrb_ref, kkt_ref, rt_ref, kdec_ref,
                     bdec_ref, kmod_ref, r_ref, v_ref, g_ref, gamc_ref):
    ch = lw.shape[0]
    ti = lax.broadcasted_iota(I32, (ch, ch), 0)
    tj = lax.broadcasted_iota(I32, (ch, ch), 1)
    tri = jnp.where(ti >= tj, 1.0, 0.0).astype(BF16)
    cum = _dot_exact_lhs(tri, lw)
    cum_c = cum[ch - 1:ch, :]
    e_neg = jnp.exp(-cum)
    e_rem = jnp.exp(cum_c - cum)
    kkt = kk * jnp.exp(cum - lw)
    rt = r * jnp.exp(cum)
    khat = kmod * e_neg
    bhat = bb * e_neg
    strict = ti > tj
    incl = ti >= tj
    hs = range(RWKV_HEADS)
    sl = [slice(h * RWKV_HEAD_DIM, (h + 1) * RWKV_HEAD_DIM) for h in hs]
    split = lambda z: [z[:, s].astype(BF16) for s in sl]
    kkt_s, rt_s, khat_s, bhat_s = split(kkt), split(rt), split(khat), split(bhat)
    if ch < RWKV_HEAD_DIM:
        kkt_bd, rt_bd = _stack_heads(kkt).astype(BF16), _stack_heads(rt).astype(BF16)
        blocks = lambda m: [m[h * ch:(h + 1) * ch, :] for h in hs]
        akb = blocks(_dot(kkt_bd, bhat.astype(BF16), NT))
        akk = blocks(_dot(kkt_bd, khat.astype(BF16), NT))
        qrk = blocks(_dot(rt_bd, khat.astype(BF16), NT))
        qrb = blocks(_dot(rt_bd, bhat.astype(BF16), NT))
    else:
        akb = [_dot(kkt_s[h], bhat_s[h], NT) for h in hs]
        akk = [_dot(kkt_s[h], khat_s[h], NT) for h in hs]
        qrk = [_dot(rt_s[h], khat_s[h], NT) for h in hs]
        qrb = [_dot(rt_s[h], bhat_s[h], NT) for h in hs]
    pack = _head_pack(ch)
    for gi in range(RWKV_HEADS // pack):
        grp = range(gi * pack, (gi + 1) * pack)
        side = lambda mats, keep: jnp.concatenate([jnp.where(keep, mats[h], 0.0) for h in grp], axis=1)
        akb_ref[gi] = side(akb, strict)
        akk_ref[gi] = side(akk, strict)
        qrk_ref[gi] = side(qrk, incl)
        qrb_ref[gi] = side(qrb, incl)
    kkt_ref[...] = kkt
    rt_ref[...] = rt
    kdec_ref[...] = kmod * e_rem
    bdec_ref[...] = bb * e_rem
    kmod_ref[...] = kmod
    r_ref[...] = r
    v_ref[...] = v
    g_ref[...] = g
    gamc_ref[...] = jnp.exp(cum_c)


def _rwkv_prep(p3, shift0, lp, eblk, ch, t_real, bb):
    b, tp, _ = p3.shape
    nc = tp // ch
    per8 = ch // SUBLANES
    pack = _head_pack(ch)
    ng, gw = RWKV_HEADS // pack, pack * ch
    mats = jax.ShapeDtypeStruct((b, nc, ng, ch, gw), F32)
    rows = jax.ShapeDtypeStruct((b, tp, RWKV_WIDTH), F32)
    mspec = pl.BlockSpec((bb, None, ng, ch, gw), lambda bi, ci: (bi, ci, 0, 0, 0))
    rspec = pl.BlockSpec((bb, ch, RWKV_WIDTH), lambda bi, ci: (bi, ci, 0))
    consts = (lp["mu"], lp["w0"], lp["a0"], lp["k_k"], lp["k_a"], lp["w2p"], lp["a2p"], lp["g2p"], eblk)
    return pl.pallas_call(
        functools.partial(_rwkv_prep_body, t_real=t_real),
        grid=(b // bb, nc),
        in_specs=[
            pl.BlockSpec((bb, ch, RWKV_PAD), lambda bi, ci: (bi, ci, 0)),
            pl.BlockSpec((bb, SUBLANES, RWKV_PAD), lambda bi, ci: (bi, jnp.maximum(ci * per8 - 1, 0), 0)),
            pl.BlockSpec((bb, 1, RWKV_PAD), lambda bi, ci: (bi, 0, 0)),
        ] + [_full(a) for a in consts],
        out_specs=[mspec] * 4 + [rspec] * 8 + [pl.BlockSpec((bb, None, 1, RWKV_WIDTH), lambda bi, ci: (bi, ci, 0, 0))],
        out_shape=[mats] * 4 + [rows] * 8 + [jax.ShapeDtypeStruct((b, nc, 1, RWKV_WIDTH), F32)],
        compiler_params=_params(("parallel", "parallel"), 48),
        name="rwkv_prep",
    )(p3, p3, shift0, *consts)


def _tri_solve_body(a_ref, t_ref, w_scr, *, ch, pack):
    k = pl.program_id(1)
    kc_w = a_ref.shape[1]
    nk = (pl.num_programs(1) - 1) // 2
    groups = a_ref.shape[0] // LANES
    gw = pack * ch

    @pl.when(k < nk)
    def _():
        xt = a_ref[...].T
        for g in range(groups):
            w_scr[pl.ds(k * (kc_w * groups) + g, kc_w, stride=groups), :] = xt[:, g * LANES:(g + 1) * LANES]

    @pl.when(k == nk)
    def _():
        cols = min(SOLVE_COLS, ch)
        cid = lax.broadcasted_iota(I32, (cols * groups, LANES), 0) // groups
        at = lambda e, n: pl.ds(pl.multiple_of(e * groups, groups), n * groups)

        for j in range(pack):
            def row_body(s, carry, j=j):
                for kc in range(ch // cols):
                    c0 = kc * cols

                    @pl.when(s >= c0)
                    def _():
                        acc0 = jnp.where(cid == s - c0, 1.0, 0.0).astype(F32)

                        def r_body(r, acc):
                            a_sr = w_scr[at(s * gw + j * ch + r, 1), :]
                            return acc - jnp.tile(a_sr, (cols, 1)) * w_scr[at(r * gw + j * ch + c0, cols), :]

                        w_scr[at(s * gw + j * ch + c0, cols), :] = lax.fori_loop(c0, s, r_body, acc0)
                return carry

            lax.fori_loop(0, ch, row_body, 0)

    @pl.when(k > nk)
    def _():
        kk = k - nk - 1
        parts = [w_scr[pl.ds(kk * (kc_w * groups) + g, kc_w, stride=groups), :] for g in range(groups)]
        t_ref[...] = jnp.concatenate(parts, axis=1).T


def _tri_solve(akb):
    b, nc, ng, ch, gw = akb.shape
    nrow = b * nc * ng
    slab = SUBLANES * LANES
    ne = ch * gw
    row_pad = -nrow % slab
    ne_pad = -ne % LANES
    a2 = akb.reshape(nrow, ne)
    if row_pad or ne_pad:
        a2 = jnp.pad(a2, ((0, row_pad), (0, ne_pad)))
    nk = (ne + ne_pad) // LANES
    t2 = pl.pallas_call(
        functools.partial(_tri_solve_body, ch=ch, pack=gw // ch),
        grid=((nrow + row_pad) // slab, 2 * nk + 1),
        in_specs=[pl.BlockSpec((slab, LANES), lambda gi, k: (gi, jnp.minimum(k, nk - 1)))],
        out_specs=pl.BlockSpec((slab, LANES), lambda gi, k: (gi, jnp.clip(k - nk - 1, 0, nk - 1))),
        out_shape=jax.ShapeDtypeStruct(a2.shape, F32),
        scratch_shapes=[pltpu.VMEM(((ne + ne_pad) * SUBLANES, LANES), F32)],
        compiler_params=_params(("parallel", "arbitrary"), 48),
        name="rwkv_tri_solve",
    )(a2)
    if row_pad or ne_pad:
        t2 = t2[:nrow, :ne]
    return t2.reshape(b, nc, ng, ch, gw)


def _rwkv_scan_body(t_ref, akk_ref, qrk_ref, qrb_ref, kkt_ref, rt_ref, kdec_ref, bdec_ref, kmod_ref, r_ref, v_ref,
                    g_ref, gamc_ref, wkv0_ref, eblk_ref, rk_ref, lnw_ref, lnb_ref, rw_ref, wkv_ref, s_scr):
    nb, ch = kkt_ref.shape[0], kkt_ref.shape[1]
    nd = RWKV_HEAD_DIM
    per_row = (t_ref, akk_ref, qrk_ref, qrb_ref, kkt_ref, rt_ref, kdec_ref, bdec_ref, v_ref, gamc_ref, wkv0_ref,
               wkv_ref, s_scr)
    y = jnp.concatenate([_rwkv_scan_chunk(*[r.at[bi] for r in per_row]) for bi in range(nb)], axis=0)
    rows = lambda ref: ref[...].reshape(nb * ch, RWKV_WIDTH)
    eblk = eblk_ref[...]
    inv_n = 1.0 / nd
    mean = _dot_exact_rhs(y, eblk) * inv_n
    yc = y - mean
    var = _dot_exact_rhs(yc * yc, eblk) * inv_n
    yn = yc * lax.rsqrt(var + GN_EPS) * lnw_ref[...] + lnb_ref[...]
    bonus = _dot_exact_rhs(rows(r_ref) * rows(kmod_ref) * rk_ref[...], eblk) * rows(v_ref)
    rw_ref[...] = ((yn + bonus) * rows(g_ref)).reshape(nb, ch, RWKV_WIDTH).astype(rw_ref.dtype)


def _rwkv_scan_chunk(t_ref, akk_ref, qrk_ref, qrb_ref, kkt_ref, rt_ref, kdec_ref, bdec_ref, v_ref, gamc_ref,
                     wkv0_ref, wkv_ref, s_scr):
    c = pl.program_id(1)
    nd = RWKV_HEAD_DIM

    @pl.when(c == 0)
    def _():
        s_scr[...] = wkv0_ref[...]

    hs = range(RWKV_HEADS)
    sl = [slice(h * nd, (h + 1) * nd) for h in hs]
    ch = kkt_ref.shape[0]
    pack = _head_pack(ch)
    mat = lambda ref, h: ref[h // pack][:, (h % pack) * ch:(h % pack + 1) * ch]
    tmat = [mat(t_ref, h) for h in hs]
    vh = [v_ref[:, sl[h]] for h in hs]
    bdec = [bdec_ref[:, sl[h]] for h in hs]
    qrb = [mat(qrb_ref, h) for h in hs]
    w1 = [_dotb(tmat[h], kkt_ref[:, sl[h]]) for h in hs]
    av = [_dotb(mat(akk_ref, h), vh[h]) for h in hs]
    qv = [_dotb(mat(qrk_ref, h), vh[h]) for h in hs]
    vk = [_dotb(vh[h], kdec_ref[:, sl[h]], TN) for h in hs]
    w2 = [_dotb(tmat[h], av[h]) for h in hs]
    qw1 = [_dotb(qrb[h], w1[h]) for h in hs]
    bw1 = [_dotb(bdec[h], w1[h], TN) for h in hs]
    qw2 = [_dotb(qrb[h], w2[h]) for h in hs]
    wb = [_dotb(w2[h], bdec[h], TN) for h in hs]
    s0 = [s_scr[h] for h in hs]
    ys = [_dotb(rt_ref[:, sl[h]] - qw1[h], s0[h], NT) + (qv[h] - qw2[h]) for h in hs]
    for h in hs:
        s_scr[h] = s0[h] * gamc_ref[:, sl[h]] - _dotb(s0[h], bw1[h], NT) + (vk[h] - wb[h])
    @pl.when(c == pl.num_programs(1) - 1)
    def _():
        wkv_ref[...] = s_scr[...]

    return jnp.concatenate(ys, axis=1)


def _rwkv_scan(tmat, prep, wkv0, lp, eblk, bb):
    akk, qrk, qrb, kkt, rt, kdec, bdec, kmod, r, v, g, gamc = prep
    b, nc, ng, ch, gw = tmat.shape
    nh = RWKV_HEADS
    tp = kkt.shape[1]
    nd = RWKV_HEAD_DIM
    mspec = pl.BlockSpec((bb, None, ng, ch, gw), lambda bi, ci: (bi, ci, 0, 0, 0))
    rspec = pl.BlockSpec((bb, ch, RWKV_WIDTH), lambda bi, ci: (bi, ci, 0))
    sspec = pl.BlockSpec((bb, nh, nd, nd), lambda bi, ci: (bi, 0, 0, 0))
    consts = (eblk, lp["r_k"], lp["ln_w"], lp["ln_b"])
    return pl.pallas_call(
        _rwkv_scan_body,
        grid=(b // bb, nc),
        in_specs=[mspec] * 4 + [rspec] * 8
        + [pl.BlockSpec((bb, None, 1, RWKV_WIDTH), lambda bi, ci: (bi, ci, 0, 0)), sspec]
        + [_full(a) for a in consts],
        out_specs=[rspec, sspec],
        out_shape=[jax.ShapeDtypeStruct((b, tp, RWKV_WIDTH), BF16), jax.ShapeDtypeStruct((b, nh, nd, nd), F32)],
        scratch_shapes=[pltpu.VMEM((bb, nh, nd, nd), F32)],
        compiler_params=_params(("parallel", "arbitrary"), 48),
        name="rwkv_scan",
    )(tmat, akk, qrk, qrb, kkt, rt, kdec, bdec, kmod, r, v, g, gamc, wkv0, *consts)


def _merge_body(x_ref, att_ref, rw_ref, ga_ref, gb_ref, wa_ref, wb_ref, wo_ref, g2_ref, wq_ref,
                x1_ref, h2_ref, qp_ref):
    merged = ga_ref[...] * _dot(att_ref[...], wa_ref[...]) + gb_ref[...] * _dot(rw_ref[...], wb_ref[...])
    x1 = x_ref[...] + _dot(merged.astype(BF16), wo_ref[...])
    x1_ref[...] = x1
    h2 = (x1 * lax.rsqrt(jnp.mean(x1 * x1, axis=-1, keepdims=True) + NORM_EPS) * g2_ref[...]).astype(BF16)
    h2_ref[...] = h2
    qp_ref[...] = _dot(h2, wq_ref[...]).astype(qp_ref.dtype)


def _merge(x2, att, rw, ga, gb, lp, tm):
    m = x2.shape[0]
    kq = lp["w_query"].shape[1]
    row = lambda w: pl.BlockSpec((tm, w), lambda i: (i, 0))
    consts = (lp["w_up_a"], lp["w_up_b"], lp["w_o"], lp["norm2_g"], lp["w_query"])
    return pl.pallas_call(
        _merge_body,
        grid=(m // tm,),
        in_specs=[row(D_MODEL), row(ATTN_WIDTH), row(RWKV_WIDTH), row(D_MODEL), row(D_MODEL)]
        + [_full(a) for a in consts],
        out_specs=[row(D_MODEL), row(D_MODEL), row(kq)],
        out_shape=[jax.ShapeDtypeStruct((m, D_MODEL), F32), jax.ShapeDtypeStruct((m, D_MODEL), BF16),
                   jax.ShapeDtypeStruct((m, kq), BF16)],
        compiler_params=_params(("parallel",), 56),
        name="merge",
    )(x2, att, rw, ga, gb, *consts)


def _argmax_rows(src_ref):
    nslab = src_ref.shape[0] // SUBLANES
    row0 = lax.broadcasted_iota(I32, (SUBLANES, src_ref.shape[1]), 0).astype(F32)
    vals = [src_ref[i * SUBLANES:(i + 1) * SUBLANES, :] for i in range(nslab)]
    rows = [row0 + float(i * SUBLANES) for i in range(nslab)]
    while len(vals) > 1:
        nv, nr = [], []
        for a in range(0, len(vals) - 1, 2):
            take = vals[a + 1] > vals[a]
            nv.append(jnp.maximum(vals[a], vals[a + 1]))
            nr.append(jnp.where(take, rows[a + 1], rows[a]))
        if len(vals) % 2:
            nv.append(vals[-1])
            nr.append(rows[-1])
        vals, rows = nv, nr
    v, i = vals[0], rows[0]
    for shift in (4, 2, 1):
        v2 = pltpu.roll(v, shift, 0)
        i2 = pltpu.roll(i, shift, 0)
        take = (v2 > v) | ((v2 == v) & (i2 < i))
        v = jnp.where(take, v2, v)
        i = jnp.where(take, i2, i)
    return jnp.max(v, axis=0, keepdims=True), jnp.max(i, axis=0, keepdims=True)


def _extract_top(jobs, n_out):
    def body(r, carry):
        for src_ref, val_ref, idx_ref in jobs:
            rowf = lax.broadcasted_iota(I32, src_ref.shape, 0).astype(F32)
            m, idx = _argmax_rows(src_ref)
            val_ref[pl.ds(r, 1), :] = m
            idx_ref[pl.ds(r, 1), :] = idx
            src_ref[...] = jnp.where(rowf == idx, NEG_INF, src_ref[...])
        return carry

    lax.fori_loop(0, n_out, body, 0)


def _cand_groups(k):
    groups = []
    a = 0
    while a < k and k // (a + 1) > 1:
        nb = k // (a + 1)
        groups.append((a, -(-nb // SUBLANES) * SUBLANES, nb))
        a += 1
    return groups, a


def _n_cand_rows(k):
    groups, a_tail = _cand_groups(k)
    return sum(rows for _, rows, _ in groups) + k - a_tail


def _peer_topk_head(qp_ref, sk_ref, h, i1_scr, i2_scr, gate_scr, s_scr, cand_scr, sv_scr, si_scr, bs_scr, bp_scr):
    k = PEER_TOPK
    for c in range(2):
        col = (2 * h + c) * D_KEY_HALF
        s_scr[c] = _dot(sk_ref[h, c], qp_ref[:, col:col + D_KEY_HALF], NT)
    _extract_top([(s_scr.at[c], sv_scr.at[c], si_scr.at[c]) for c in range(2)], k)
    sv0 = sv_scr[0]
    sv1 = sv_scr[1]
    groups, a_tail = _cand_groups(k)
    off = 0
    for a, rows, nb in groups:
        blk = sv0[a:a + 1, :] + sv1[0:rows, :]
        if nb < rows:
            blk = jnp.where(lax.broadcasted_iota(I32, blk.shape, 0) < nb, blk, NEG_INF)
        cand_scr[off:off + rows, :] = blk
        off += rows
    cand_scr[off:off + k - a_tail, :] = sv0[a_tail:k, :] + sv1[0:1, :]
    _extract_top([(cand_scr, bs_scr, bp_scr)], k)
    best_s = bs_scr[...]
    pos = bp_scr[...]
    a_sel = jnp.zeros_like(pos)
    b_sel = pos
    off = 0
    for a, rows, nb in groups:
        if a > 0:
            a_sel = jnp.where(pos >= off, float(a), a_sel)
            b_sel = jnp.where(pos >= off, pos - off, b_sel)
        off += rows
    a_sel = jnp.where(pos >= off, a_tail + pos - off, a_sel)
    b_sel = jnp.where(pos >= off, 0.0, b_sel)
    si0 = si_scr[0]
    si1 = si_scr[1]
    i1 = jnp.zeros_like(pos)
    i2 = jnp.zeros_like(pos)
    for a in range(k):
        i1 = jnp.where(a_sel == a, si0[a:a + 1, :], i1)
        i2 = jnp.where(b_sel == a, si1[a:a + 1, :], i2)
    e = jnp.exp(best_s - jnp.max(best_s, axis=0, keepdims=True))
    i1_scr[h * k:(h + 1) * k, :] = i1
    i2_scr[h * k:(h + 1) * k, :] = i2
    gate_scr[h * k:(h + 1) * k, :] = e / jnp.sum(e, axis=0, keepdims=True)


def _peer_topk_body(qp_ref, sk_ref, i1_ref, i2_ref, gate_ref, i1_scr, i2_scr, gate_scr, *work):
    for h in range(PEER_HEADS):
        _peer_topk_head(qp_ref, sk_ref, h, i1_scr, i2_scr, gate_scr, *work)
    i1_ref[...] = i1_scr[...].T.astype(I32)
    i2_ref[...] = i2_scr[...].T.astype(I32)
    gate_ref[...] = gate_scr[...].T


def _peer_topk(qp, sub_keys, tm):
    m = qp.shape[0]
    k = PEER_TOPK
    ospec = pl.BlockSpec((tm, N_SLOTS), lambda i: (i, 0))
    slots = pltpu.VMEM((N_SLOTS, tm), F32)
    return pl.pallas_call(
        _peer_topk_body,
        grid=(m // tm,),
        in_specs=[pl.BlockSpec((tm, qp.shape[1]), lambda i: (i, 0)), _full(sub_keys)],
        out_specs=[ospec, ospec, ospec],
        out_shape=[jax.ShapeDtypeStruct((m, N_SLOTS), I32), jax.ShapeDtypeStruct((m, N_SLOTS), I32),
                   jax.ShapeDtypeStruct((m, N_SLOTS), F32)],
        scratch_shapes=[slots, slots, slots,
                        pltpu.VMEM((2, N_KEYS, tm), F32), pltpu.VMEM((_n_cand_rows(k), tm), F32),
                        pltpu.VMEM((2, k, tm), F32), pltpu.VMEM((2, k, tm), F32),
                        pltpu.VMEM((k, tm), F32), pltpu.VMEM((k, tm), F32)],
        compiler_params=_params(("parallel",), 32),
        name="peer_topk",
    )(qp, sub_keys)


def _peer_expert_body(x1_ref, h2_ref, i1_ref, i2_ref, gate_ref, ut_ref, v_ref, gf_ref, y_ref, w_scr):
    j = pl.program_id(1)
    tm = h2_ref.shape[0]
    te = v_ref.shape[0]
    pitch = tm + W_PITCH_PAD

    @pl.when(j == 0)
    def _():
        y_ref[...] = x1_ref[...]
        sub = lax.broadcasted_iota(I32, (N_KEYS, N_SLOTS), 0)

        def tok(t, carry):
            hit1 = sub == i1_ref[pl.ds(t, 1), :]
            hit2 = sub == i2_ref[pl.ds(t, 1), :]
            a = jnp.where(hit1, gate_ref[pl.ds(t, 1), :], 0.0).astype(BF16)
            bm_t = jnp.where(hit2, 1.0, 0.0).T.astype(BF16)
            w_scr[pl.ds(t, N_KEYS, stride=pitch), :] = _dot(a, bm_t)
            return carry

        lax.fori_loop(0, tm, tok, 0, unroll=TOKEN_UNROLL)

    rb = min(tm, EXPERT_ROW_BLOCK)
    pair = 2 * N_KEYS
    for r in range(tm // rb):
        x = h2_ref[r * rb:(r + 1) * rb, :]
        parts = []
        for q in range(te // pair):
            h = _dot(x, ut_ref[:, q * pair:(q + 1) * pair])
            i1 = j * (te // N_KEYS) + 2 * q
            w = jnp.concatenate([w_scr[pl.ds(pl.multiple_of(i1 * pitch + r * rb, SUBLANES), rb), :],
                                 w_scr[pl.ds(pl.multiple_of((i1 + 1) * pitch + r * rb, SUBLANES), rb), :]], axis=1)
            act = 0.5 * h * (1.0 + lax.erf(h * (2.0 ** -0.5)))
            parts.append((w * act).astype(BF16))
        y_ref[r * rb:(r + 1) * rb, :] += _dot(jnp.concatenate(parts, axis=1), v_ref[...])

    @pl.when(j == pl.num_programs(1) - 1)
    def _():
        x2 = y_ref[...]
        y_ref[...] = x2 * lax.rsqrt(jnp.mean(x2 * x2, axis=-1, keepdims=True) + NORM_EPS) * gf_ref[...]


def _peer_experts(x1, h2, i1, i2, gate, eu, ev, gf, tm, te):
    m = x1.shape[0]
    row = lambda w: pl.BlockSpec((tm, w), lambda i, j: (i, 0))
    tab = pl.BlockSpec((te, D_MODEL), lambda i, j: (j, 0))
    tab_t = pl.BlockSpec((D_MODEL, te), lambda i, j: (0, j))
    x1_spec = pl.BlockSpec((tm, D_MODEL), lambda i, j: (i, 0), pipeline_mode=pl.Buffered(1))
    return pl.pallas_call(
        _peer_expert_body,
        grid=(m // tm, N_EXPERTS // te),
        in_specs=[x1_spec, row(D_MODEL), row(N_SLOTS), row(N_SLOTS), row(N_SLOTS), tab_t, tab,
                  pl.BlockSpec((1, D_MODEL), lambda i, j: (0, 0))],
        out_specs=row(D_MODEL),
        out_shape=jax.ShapeDtypeStruct((m, D_MODEL), F32),
        scratch_shapes=[pltpu.VMEM((N_KEYS * (tm + W_PITCH_PAD), N_KEYS), F32)],
        compiler_params=_params(("parallel", "arbitrary"), 60),
        name="peer_experts",
    )(x1, h2, i1, i2, gate, eu, ev, gf)


def _pad_cols(a, width):
    return jnp.pad(a, [(0, 0)] * (a.ndim - 1) + [(0, width - a.shape[-1])])


def _prep_layer_params(l, norm1_g, w_in, b_in, attn_sinks, shift_mu, w0, w2, a0, a2, g2, k_k, k_a, r_k, ln_x_w,
                       ln_x_b, w_up_a, w_up_b, w_o, norm2_g, w_query, sub_keys, expert_u, expert_v):
    wi, bi = w_in[l], b_in[l][None, :]
    cat = lambda z: jnp.concatenate([z[:, :OFF_B], _pad_cols(z[:, OFF_B:OFF_GATE], RWKV_PAD), z[:, OFF_GATE:]], axis=1)
    row = lambda z: z[l].reshape(1, -1)
    lora = lambda z, off: jnp.pad(z[l], ((off, LORA_PAD - off - z.shape[1]), (0, 0)))
    return dict(
        norm1_g=row(norm1_g), w_cat=cat(wi).astype(BF16), b_cat=cat(bi), sinks=attn_sinks[l],
        mu=_pad_cols(row(shift_mu), RWKV_PAD), w0=row(w0), a0=row(a0), k_k=row(k_k), k_a=row(k_a),
        w2p=lora(w2, 0), a2p=lora(a2, DECAY_LORA), g2p=lora(g2, DECAY_LORA + AAA_LORA),
        r_k=row(r_k), ln_w=row(ln_x_w), ln_b=row(ln_x_b),
        w_up_a=w_up_a[l].astype(BF16), w_up_b=w_up_b[l].astype(BF16), w_o=w_o[l].astype(BF16),
        norm2_g=row(norm2_g), w_query=w_query[l].astype(BF16), sub_keys=sub_keys[l].astype(BF16),
        expert_ut=expert_u[l].astype(BF16).T, expert_v=expert_v[l].astype(BF16),
    )


def _row_tile(m, pref):
    return pref if m % pref == 0 else m


def _hybrid_layer(x, pos, lp, eblk, gf, win_k, win_v, wkv0, shift0):
    b, t, _ = x.shape
    m = b * t
    x2 = x.reshape(m, D_MODEL)
    tm = _row_tile(m, 256)
    tmd = _row_tile(m, DENSE_TOKEN_TILE)
    rope_tab = _rope_table(pos, max(t, tmd))
    q, k, v, p, ga, gb = _in_proj(x2, lp["norm1_g"], lp["w_cat"], lp["b_cat"], rope_tab, tmd)

    k3 = k.reshape(b, t, KV_WIDTH)
    v3 = v.reshape(b, t, KV_WIDTH)
    if win_k is None:
        att = _attn_prompt(lp["sinks"], q.reshape(b, t, ATTN_WIDTH), k3, v3).reshape(m, ATTN_WIDTH)
        keep = min(WINDOW, t)
        new_k, new_v = k3[:, t - keep:], v3[:, t - keep:]
    else:
        n_past = win_k.shape[1]
        qg = q.reshape(b, t, N_KV_HEADS, Q_PER_KV, HEAD_DIM).transpose(0, 2, 3, 1, 4)
        qg = qg.reshape(b, N_KV_HEADS, Q_PER_KV * t, HEAD_DIM)
        ck = win_k.reshape(b, n_past, KV_WIDTH)
        cv = win_v.reshape(b, n_past, KV_WIDTH)
        og = _attn_sample(lp["sinks"], qg, k3, v3, ck, cv, t)
        att = og.reshape(b, N_KV_HEADS, Q_PER_KV, t, HEAD_DIM).transpose(0, 3, 1, 2, 4).reshape(m, ATTN_WIDTH)
        new_k = jnp.concatenate([ck, k3], axis=1)[:, t:]
        new_v = jnp.concatenate([cv, v3], axis=1)[:, t:]
    new_k = new_k.reshape(b, -1, N_KV_HEADS, HEAD_DIM)
    new_v = new_v.reshape(b, -1, N_KV_HEADS, HEAD_DIM)

    p3 = p.reshape(b, t, RWKV_PAD)
    if t % RWKV_CHUNK == 0:
        ch, tp = RWKV_CHUNK, t
    else:
        tp = -(-t // SUBLANES) * SUBLANES
        ch = tp
        p3 = jnp.pad(p3, ((0, 0), (0, tp - t), (0, 0)))
    shift_pad = _pad_cols(shift0, RWKV_PAD)[:, None, :]
    want = DECODE_BATCH_BLOCK if tp == ch else PROMPT_BATCH_BLOCK
    bb = want if b % want == 0 else 1
    akb, *prep = _rwkv_prep(p3, shift_pad, lp, eblk, ch, t, bb)
    tmat = _tri_solve(akb)
    rw, wkv_t = _rwkv_scan(tmat, prep, wkv0, lp, eblk, bb)
    rw = rw[:, :t].reshape(m, RWKV_WIDTH)
    shift_t = p3[:, t - 1, :RWKV_COLS]

    x1, h2, qp = _merge(x2, att, rw, ga, gb, lp, tmd)
    i1, i2, gate = _peer_topk(qp, lp["sub_keys"], tm)
    y = _peer_experts(x1, h2, i1, i2, gate, lp["expert_ut"], lp["expert_v"], gf,
                      _row_tile(m, EXPERT_TOKEN_TILE), EXPERT_TILE)
    return y.reshape(b, t, D_MODEL), new_k, new_v, wkv_t, shift_t


def kernel(x_prompt, x_sample, cache_win_k, cache_win_v, state_wkv, state_shift, norm1_g, w_in, b_in, attn_sinks,
           shift_mu, w0, w2, a0, a2, g2, k_k, k_a, r_k, ln_x_w, ln_x_b, w_up_a, w_up_b, w_o, norm2_g, w_query,
           sub_keys, expert_u, expert_v, final_norm_g):
    depth = w_in.shape[0]
    assert depth == 1, "the final RMSNorm is fused into the last layer's expert kernel"
    pos_p = jnp.arange(x_prompt.shape[1])
    pos_s = PAST_LEN + jnp.arange(x_sample.shape[1])
    li = lax.broadcasted_iota(I32, (RWKV_WIDTH, RWKV_WIDTH), 0) // RWKV_HEAD_DIM
    lj = lax.broadcasted_iota(I32, (RWKV_WIDTH, RWKV_WIDTH), 1) // RWKV_HEAD_DIM
    eblk = (li == lj).astype(BF16)
    gf = final_norm_g.reshape(1, D_MODEL)
    lp = _prep_layer_params(0, norm1_g, w_in, b_in, attn_sinks, shift_mu, w0, w2, a0, a2, g2, k_k, k_a, r_k, ln_x_w,
                            ln_x_b, w_up_a, w_up_b, w_o, norm2_g, w_query, sub_keys, expert_u, expert_v)
    bp = x_prompt.shape[0]
    wkv0 = jnp.zeros((bp, RWKV_HEADS, RWKV_HEAD_DIM, RWKV_HEAD_DIM), F32)
    shift0 = jnp.zeros((bp, RWKV_COLS), x_prompt.dtype)
    yp, pk, pv, pw, psh = _hybrid_layer(x_prompt, pos_p, lp, eblk, gf, None, None, wkv0, shift0)
    ys, sk, sv, sw, ssh = _hybrid_layer(x_sample, pos_s, lp, eblk, gf, cache_win_k[0], cache_win_v[0],
                                        state_wkv[0], state_shift[0])
    st = lambda z: z[None]
    return (yp, ys, st(pk), st(pv), st(pw), st(psh), st(sk), st(sv), st(sw), st(ssh))
```

```python
import functools

import jax
import jax.numpy as jnp
from jax import lax
from jax.experimental import pallas as pl
from jax.experimental.pallas import tpu as pltpu

F32, BF16, I32 = jnp.float32, jnp.bfloat16, jnp.int32

D_MODEL = 1024
NORM_EPS = 1e-5
HEAD_DIM = 64
N_Q_HEADS = 8
N_KV_HEADS = 2
Q_PER_KV = N_Q_HEADS // N_KV_HEADS
ATTN_WIDTH = N_Q_HEADS * HEAD_DIM
KV_WIDTH = N_KV_HEADS * HEAD_DIM
WINDOW = 128
ROT_DIM = HEAD_DIM // 4
ROPE_THETA = 500000.0
ATTN_SCALE = HEAD_DIM ** -0.5
RWKV_HEADS = 8
RWKV_HEAD_DIM = 64
RWKV_WIDTH = RWKV_HEADS * RWKV_HEAD_DIM
DECAY_LORA = 32
AAA_LORA = 32
GATE_LORA = 96
GN_EPS = 64e-5
RWKV_COLS = 3 * RWKV_WIDTH + DECAY_LORA + AAA_LORA + GATE_LORA
LORA_PAD = 256
RWKV_PAD = 3 * RWKV_WIDTH + LORA_PAD
OFF_K = ATTN_WIDTH
OFF_V = OFF_K + KV_WIDTH
OFF_B = OFF_V + KV_WIDTH
OFF_GATE = OFF_B + RWKV_COLS
N_KEYS = 128
N_EXPERTS = N_KEYS * N_KEYS
PEER_HEADS = 8
PEER_TOPK = 16
D_KEY_HALF = 128
N_SLOTS = PEER_HEADS * PEER_TOPK
PAST_LEN = 16384
RWKV_CHUNK = 64
LANES = 128
SUBLANES = 8
SOLVE_COLS = 16
TOKEN_UNROLL = 128
DECODE_BATCH_BLOCK = 16
PROMPT_BATCH_BLOCK = 8
DENSE_TOKEN_TILE = 512
EXPERT_TOKEN_TILE = 512
EXPERT_TILE = 1024
EXPERT_ROW_BLOCK = 512
W_PITCH_PAD = 8
NEG_INF = float("-inf")

NN = ((1,), (0,))
NT = ((1,), (1,))
TN = ((0,), (0,))


def _dot(a, b, dims=NN):
    return lax.dot_general(a, b, (dims, ((), ())), preferred_element_type=F32)


def _split2(x):
    hi = x.astype(BF16)
    return hi, (x - hi.astype(F32)).astype(BF16)


def _doth(a, b, dims=NN):
    a_hi, a_lo = _split2(a)
    b_hi, b_lo = _split2(b)
    return _dot(a_hi, b_hi, dims) + _dot(a_hi, b_lo, dims) + _dot(a_lo, b_hi, dims)


def _dotb(a, b, dims=NN):
    return _dot(a.astype(BF16), b.astype(BF16), dims)


def _split3(x):
    hi = x.astype(BF16)
    r1 = x - hi.astype(F32)
    mid = r1.astype(BF16)
    lo = (r1 - mid.astype(F32)).astype(BF16)
    return hi, mid, lo


def _dot_exact_rhs(x, e):
    hi, mid, lo = _split3(x)
    return _dot(hi, e) + _dot(mid, e) + _dot(lo, e)


def _dot_exact_lhs(e, x):
    hi, mid, lo = _split3(x)
    return _dot(e, hi) + _dot(e, mid) + _dot(e, lo)


def _params(sem, vmem_mb):
    return pltpu.CompilerParams(dimension_semantics=sem, vmem_limit_bytes=vmem_mb * 1024 * 1024)


def _full(a):
    nd = a.ndim
    return pl.BlockSpec(a.shape, lambda *_: (0,) * nd)


SEG_QKV = ATTN_WIDTH + 2 * KV_WIDTH
SEG_P = SEG_QKV + RWKV_PAD
SEG_END = SEG_P + 2 * D_MODEL


def _inproj_body(x_ref, g_ref, w_ref, b_ref, rope_ref, q_ref, k_ref, v_ref, p_ref, ga_ref, gb_ref):
    x = x_ref[...]
    h = (x * lax.rsqrt(jnp.mean(x * x, axis=-1, keepdims=True) + NORM_EPS) * g_ref[...]).astype(BF16)

    def seg(lo, hi):
        return _dot(h, w_ref[:, lo:hi]) + b_ref[:, lo:hi]

    cos = rope_ref[:, 0:LANES]
    sin_up = rope_ref[:, LANES:2 * LANES]
    sin_dn = rope_ref[:, 2 * LANES:3 * LANES]

    def rope(z):
        return z * cos + pltpu.roll(z, ROT_DIM // 2, 1) * sin_up + pltpu.roll(z, LANES - ROT_DIM // 2, 1) * sin_dn

    qkv = seg(0, SEG_QKV)
    for c in range(ATTN_WIDTH // LANES):
        q_ref[:, c * LANES:(c + 1) * LANES] = rope(qkv[:, c * LANES:(c + 1) * LANES])
    k_ref[...] = rope(qkv[:, OFF_K:OFF_V])
    v_ref[...] = qkv[:, OFF_V:SEG_QKV]
    p_ref[...] = seg(SEG_QKV, SEG_P)
    ga_ref[...] = jax.nn.sigmoid(seg(SEG_P, SEG_P + D_MODEL))
    gb_ref[...] = jax.nn.sigmoid(seg(SEG_P + D_MODEL, SEG_END))


def _in_proj(x2, g1, w_cat, b_cat, rope_tab, tm):
    m = x2.shape[0]
    nrep = rope_tab.shape[0] // tm
    widths = (ATTN_WIDTH, KV_WIDTH, KV_WIDTH, RWKV_PAD, D_MODEL, D_MODEL)
    return pl.pallas_call(
        _inproj_body,
        grid=(m // tm,),
        in_specs=[
            pl.BlockSpec((tm, D_MODEL), lambda i: (i, 0)),
            _full(g1), _full(w_cat), _full(b_cat),
            pl.BlockSpec((tm, 3 * LANES), lambda i: (i % nrep, 0)),
        ],
        out_specs=[pl.BlockSpec((tm, w), lambda i: (i, 0)) for w in widths],
        out_shape=[jax.ShapeDtypeStruct((m, w), F32) for w in widths],
        compiler_params=_params(("parallel",), 56),
        name="in_proj",
    )(x2, g1, w_cat, b_cat, rope_tab)


def _rope_table(pos, rows):
    t = pos.shape[0]
    half = ROT_DIM // 2
    inv_freq = ROPE_THETA ** (-jnp.arange(half, dtype=F32) / half)
    ang = pos.astype(F32)[:, None] * inv_freq[None, :]
    cos, sin = jnp.cos(ang), jnp.sin(ang)
    rest = HEAD_DIM - ROT_DIM
    z8 = jnp.zeros((t, half), F32)
    cos64 = jnp.concatenate([cos, cos, jnp.ones((t, rest), F32)], axis=1)
    up64 = jnp.concatenate([z8, sin, jnp.zeros((t, rest), F32)], axis=1)
    dn64 = jnp.concatenate([-sin, z8, jnp.zeros((t, rest), F32)], axis=1)
    tab = jnp.concatenate([jnp.tile(a, (1, LANES // HEAD_DIM)) for a in (cos64, up64, dn64)], axis=1)
    if rows > t:
        tab = jnp.tile(tab, (rows // t, 1))
    return tab


def _softmax_av(s, sink, vv):
    m = jnp.maximum(jnp.max(s, axis=-1, keepdims=True), sink)
    p = jnp.exp(s - m)
    den = jnp.sum(p, axis=-1, keepdims=True) + jnp.exp(sink - m)
    return _dot(p.astype(BF16), vv) / den


def _attn_prompt_body(sink_ref, q_ref, kc_ref, kp_ref, vc_ref, vp_ref, o_ref):
    n = pl.program_id(1)
    blk = q_ref.shape[0]
    i = lax.broadcasted_iota(I32, (blk, 2 * blk), 0)
    j = lax.broadcasted_iota(I32, (blk, 2 * blk), 1)
    diff = blk + i - j
    j_lo = jnp.where(n > 0, 0, blk)
    mask = (diff >= 0) & (diff < WINDOW) & (j >= j_lo)
    outs = []
    for g in range(N_KV_HEADS):
        sl = slice(g * HEAD_DIM, (g + 1) * HEAD_DIM)
        kk = jnp.concatenate([kp_ref[:, sl], kc_ref[:, sl]], axis=0).astype(BF16)
        vv = jnp.concatenate([vp_ref[:, sl], vc_ref[:, sl]], axis=0).astype(BF16)
        for hq in range(Q_PER_KV):
            h = g * Q_PER_KV + hq
            qh = q_ref[:, h * HEAD_DIM:(h + 1) * HEAD_DIM].astype(BF16)
            s = jnp.where(mask, _dot(qh, kk, NT) * ATTN_SCALE, NEG_INF)
            outs.append(_softmax_av(s, sink_ref[h], vv))
    o_ref[...] = jnp.concatenate(outs, axis=1).astype(o_ref.dtype)


def _attn_prompt(sinks, q3, k3, v3):
    b, t, _ = q3.shape
    blk = WINDOW
    cur = lambda w: pl.BlockSpec((None, blk, w), lambda bi, n: (bi, n, 0))
    prev = lambda w: pl.BlockSpec((None, blk, w), lambda bi, n: (bi, jnp.maximum(n - 1, 0), 0))
    return pl.pallas_call(
        _attn_prompt_body,
        grid=(b, t // blk),
        in_specs=[pl.BlockSpec(memory_space=pltpu.SMEM), cur(ATTN_WIDTH), cur(KV_WIDTH), prev(KV_WIDTH),
                  cur(KV_WIDTH), prev(KV_WIDTH)],
        out_specs=cur(ATTN_WIDTH),
        out_shape=jax.ShapeDtypeStruct((b, t, ATTN_WIDTH), BF16),
        compiler_params=_params(("parallel", "arbitrary"), 32),
        name="attn_prompt",
    )(sinks, q3, k3, k3, v3, v3)


def _attn_sample_body(sink_ref, *refs, t):
    for bi in range(refs[0].shape[0]):
        _attn_sample_one(sink_ref, *[r.at[bi] for r in refs], t=t)


def _attn_sample_one(sink_ref, q_ref, kn_ref, vn_ref, kc_ref, vc_ref, o_ref, *, t):
    rows = Q_PER_KV * t
    n_past = kc_ref.shape[0]
    ri = lax.broadcasted_iota(I32, (rows, 1), 0)
    qi = ri % t
    jc = lax.broadcasted_iota(I32, (rows, n_past), 1)
    mask_c = (n_past + qi - jc) < WINDOW
    for g in range(N_KV_HEADS):
        sl = slice(g * HEAD_DIM, (g + 1) * HEAD_DIM)
        sink = jnp.zeros((rows, 1), F32)
        for hq in range(Q_PER_KV):
            sink = jnp.where(ri // t == hq, sink_ref[g * Q_PER_KV + hq], sink)
        qg = q_ref[g]
        qb = qg.astype(BF16)
        kn = kn_ref[:, sl]
        vn = vn_ref[:, sl]
        s_c = jnp.where(mask_c, _dot(qb, kc_ref[:, sl].astype(BF16), NT) * ATTN_SCALE, NEG_INF)
        s_n = []
        for jn in range(t):
            sj = jnp.sum(qg * kn[jn:jn + 1, :], axis=-1, keepdims=True) * ATTN_SCALE
            s_n.append(jnp.where(qi >= jn, sj, NEG_INF))
        m = jnp.maximum(jnp.max(s_c, axis=-1, keepdims=True), sink)
        for sj in s_n:
            m = jnp.maximum(m, sj)
        p_c = jnp.exp(s_c - m)
        den = jnp.sum(p_c, axis=-1, keepdims=True) + jnp.exp(sink - m)
        o = _dot(p_c.astype(BF16), vc_ref[:, sl].astype(BF16))
        for jn in range(t):
            p_n = jnp.exp(s_n[jn] - m)
            den = den + p_n
            o = o + p_n * vn[jn:jn + 1, :]
        o_ref[g] = (o / den).astype(o_ref.dtype)


def _attn_sample(sinks, qg, kn, vn, cache_k, cache_v, t):
    b = qg.shape[0]
    rows = Q_PER_KV * t
    n_past = cache_k.shape[1]
    bb = DECODE_BATCH_BLOCK if b % DECODE_BATCH_BLOCK == 0 else 1
    qspec = pl.BlockSpec((bb, N_KV_HEADS, rows, HEAD_DIM), lambda bi: (bi, 0, 0, 0))
    nspec = pl.BlockSpec((bb, t, KV_WIDTH), lambda bi: (bi, 0, 0))
    cspec = pl.BlockSpec((bb, n_past, KV_WIDTH), lambda bi: (bi, 0, 0))
    return pl.pallas_call(
        functools.partial(_attn_sample_body, t=t),
        grid=(b // bb,),
        in_specs=[pl.BlockSpec(memory_space=pltpu.SMEM), qspec, nspec, nspec, cspec, cspec],
        out_specs=qspec,
        out_shape=jax.ShapeDtypeStruct((b, N_KV_HEADS, rows, HEAD_DIM), BF16),
        compiler_params=_params(("parallel",), 32),
        name="attn_sample",
    )(sinks, qg, kn, vn, cache_k, cache_v)


def _stack_heads(z):
    ch, width = z.shape
    rows = lax.broadcasted_iota(I32, (RWKV_HEADS * ch, width), 0) // ch
    lanes = lax.broadcasted_iota(I32, (RWKV_HEADS * ch, width), 1) // RWKV_HEAD_DIM
    return jnp.where(rows == lanes, jnp.concatenate([z] * RWKV_HEADS, axis=0), 0.0)


def _head_pack(ch):
    return LANES // ch if ch >= RWKV_HEAD_DIM else 1


def _rwkv_prep_body(p_ref, pprev_ref, shift_ref, mu_ref, w0_ref, a0_ref, kk_ref, ka_ref, w2_ref, a2_ref, g2_ref,
                    eblk_ref, *outs, t_real):
    c = pl.program_id(1)
    nb, ch = p_ref.shape[0], p_ref.shape[1]
    row = lax.broadcasted_iota(I32, (ch, 1), 0)
    xms = []
    for bi in range(nb):
        pc = p_ref[bi]
        first = jnp.where(c == 0, shift_ref[bi], pprev_ref[bi, SUBLANES - 1:SUBLANES, :])
        prev = jnp.where(row == 0, first, pltpu.roll(pc, 1, 0))
        xms.append(pc + mu_ref[...] * (prev - pc))
    xm = jnp.concatenate(xms, axis=0)
    valid = (c * ch + lax.broadcasted_iota(I32, (nb * ch, 1), 0) % ch) < t_real
    w3 = 3 * RWKV_WIDTH
    r = xm[:, 0:RWKV_WIDTH]
    k = xm[:, RWKV_WIDTH:2 * RWKV_WIDTH]
    v = xm[:, 2 * RWKV_WIDTH:w3]
    lo = xm[:, w3:w3 + LORA_PAD]
    lane = lax.broadcasted_iota(I32, lo.shape, 1)
    act = jnp.where(lane < DECAY_LORA, jnp.tanh(lo),
                    jnp.where(lane < DECAY_LORA + AAA_LORA, lo, jax.nn.sigmoid(lo)))
    z = w0_ref[...] + _doth(act, w2_ref[...])
    w_log = -(jnp.maximum(-z, 0.0) + jnp.log1p(jnp.exp(-jnp.abs(z)))) - 0.5
    lw = jnp.where(valid, -jnp.exp(w_log), 0.0)
    a = jax.nn.sigmoid(a0_ref[...] + _doth(act, a2_ref[...]))
    g = _dot(act.astype(BF16), g2_ref[...].astype(BF16))
    kk = k * kk_ref[...]
    ss = _dot_exact_rhs(kk * kk, eblk_ref[...])
    kk = jnp.where(valid, kk * lax.rsqrt(jnp.maximum(ss, 1e-24)), 0.0)
    kmod = jnp.where(valid, k * (1.0 + (a - 1.0) * ka_ref[...]), 0.0)
    bb = kk * a
    for bi in range(nb):
        rows = slice(bi * ch, (bi + 1) * ch)
        _rwkv_prep_chunk(lw[rows], kk[rows], kmod[rows], bb[rows], r[rows], v[rows], g[rows],
                         *[o.at[bi] for o in outs])


def _rwkv_prep_chunk(lw, kk, kmod, bb, r, v, g, akb_ref, akk_ref, qrk_ref, qrb_ref, kkt_ref, rt_ref, kdec_ref,
                     bdec_ref, kmod_ref, r_ref, v_ref, g_ref, gamc_ref):
    ch = lw.shape[0]
    ti = lax.broadcasted_iota(I32, (ch, ch), 0)
    tj = lax.broadcasted_iota(I32, (ch, ch), 1)
    tri = jnp.where(ti >= tj, 1.0, 0.0).astype(BF16)
    cum = _dot_exact_lhs(tri, lw)
    cum_c = cum[ch - 1:ch, :]
    e_neg = jnp.exp(-cum)
    e_rem = jnp.exp(cum_c - cum)
    kkt = kk * jnp.exp(cum - lw)
    rt = r * jnp.exp(cum)
    khat = kmod * e_neg
    bhat = bb * e_neg
    strict = ti > tj
    incl = ti >= tj
    hs = range(RWKV_HEADS)
    sl = [slice(h * RWKV_HEAD_DIM, (h + 1) * RWKV_HEAD_DIM) for h in hs]
    split = lambda z: [z[:, s].astype(BF16) for s in sl]
    kkt_s, rt_s, khat_s, bhat_s = split(kkt), split(rt), split(khat), split(bhat)
    if ch < RWKV_HEAD_DIM:
        kkt_bd, rt_bd = _stack_heads(kkt).astype(BF16), _stack_heads(rt).astype(BF16)
        blocks = lambda m: [m[h * ch:(h + 1) * ch, :] for h in hs]
        akb = blocks(_dot(kkt_bd, bhat.astype(BF16), NT))
        akk = blocks(_dot(kkt_bd, khat.astype(BF16), NT))
        qrk = blocks(_dot(rt_bd, khat.astype(BF16), NT))
        qrb = blocks(_dot(rt_bd, bhat.astype(BF16), NT))
    else:
        akb = [_dot(kkt_s[h], bhat_s[h], NT) for h in hs]
        akk = [_dot(kkt_s[h], khat_s[h], NT) for h in hs]
        qrk = [_dot(rt_s[h], khat_s[h], NT) for h in hs]
        qrb = [_dot(rt_s[h], bhat_s[h], NT) for h in hs]
    pack = _head_pack(ch)
    for gi in range(RWKV_HEADS // pack):
        grp = range(gi * pack, (gi + 1) * pack)
        side = lambda mats, keep: jnp.concatenate([jnp.where(keep, mats[h], 0.0) for h in grp], axis=1)
        akb_ref[gi] = side(akb, strict)
        akk_ref[gi] = side(akk, strict)
        qrk_ref[gi] = side(qrk, incl)
        qrb_ref[gi] = side(qrb, incl)
    kkt_ref[...] = kkt
    rt_ref[...] = rt
    kdec_ref[...] = kmod * e_rem
    bdec_ref[...] = bb * e_rem
    kmod_ref[...] = kmod
    r_ref[...] = r
    v_ref[...] = v
    g_ref[...] = g
    gamc_ref[...] = jnp.exp(cum_c)


def _rwkv_prep(p3, shift0, lp, eblk, ch, t_real, bb):
    b, tp, _ = p3.shape
    nc = tp // ch
    per8 = ch // SUBLANES
    pack = _head_pack(ch)
    ng, gw = RWKV_HEADS // pack, pack * ch
    mats = jax.ShapeDtypeStruct((b, nc, ng, ch, gw), F32)
    rows = jax.ShapeDtypeStruct((b, tp, RWKV_WIDTH), F32)
    mspec = pl.BlockSpec((bb, None, ng, ch, gw), lambda bi, ci: (bi, ci, 0, 0, 0))
    rspec = pl.BlockSpec((bb, ch, RWKV_WIDTH), lambda bi, ci: (bi, ci, 0))
    consts = (lp["mu"], lp["w0"], lp["a0"], lp["k_k"], lp["k_a"], lp["w2p"], lp["a2p"], lp["g2p"], eblk)
    return pl.pallas_call(
        functools.partial(_rwkv_prep_body, t_real=t_real),
        grid=(b // bb, nc),
        in_specs=[
            pl.BlockSpec((bb, ch, RWKV_PAD), lambda bi, ci: (bi, ci, 0)),
            pl.BlockSpec((bb, SUBLANES, RWKV_PAD), lambda bi, ci: (bi, jnp.maximum(ci * per8 - 1, 0), 0)),
            pl.BlockSpec((bb, 1, RWKV_PAD), lambda bi, ci: (bi, 0, 0)),
        ] + [_full(a) for a in consts],
        out_specs=[mspec] * 4 + [rspec] * 8 + [pl.BlockSpec((bb, None, 1, RWKV_WIDTH), lambda bi, ci: (bi, ci, 0, 0))],
        out_shape=[mats] * 4 + [rows] * 8 + [jax.ShapeDtypeStruct((b, nc, 1, RWKV_WIDTH), F32)],
        compiler_params=_params(("parallel", "parallel"), 48),
        name="rwkv_prep",
    )(p3, p3, shift0, *consts)


def _tri_solve_body(a_ref, t_ref, w_scr, *, ch, pack):
    k = pl.program_id(1)
    kc_w = a_ref.shape[1]
    nk = (pl.num_programs(1) - 1) // 2
    groups = a_ref.shape[0] // LANES
    gw = pack * ch

    @pl.when(k < nk)
    def _():
        xt = a_ref[...].T
        for g in range(groups):
            w_scr[pl.ds(k * (kc_w * groups) + g, kc_w, stride=groups), :] = xt[:, g * LANES:(g + 1) * LANES]

    @pl.when(k == nk)
    def _():
        cols = min(SOLVE_COLS, ch)
        cid = lax.broadcasted_iota(I32, (cols * groups, LANES), 0) // groups
        at = lambda e, n: pl.ds(pl.multiple_of(e * groups, groups), n * groups)

        for j in range(pack):
            def row_body(s, carry, j=j):
                for kc in range(ch // cols):
                    c0 = kc * cols

                    @pl.when(s >= c0)
                    def _():
                        acc0 = jnp.where(cid == s - c0, 1.0, 0.0).astype(F32)

                        def r_body(r, acc):
                            a_sr = w_scr[at(s * gw + j * ch + r, 1), :]
                            return acc - jnp.tile(a_sr, (cols, 1)) * w_scr[at(r * gw + j * ch + c0, cols), :]

                        w_scr[at(s * gw + j * ch + c0, cols), :] = lax.fori_loop(c0, s, r_body, acc0)
                return carry

            lax.fori_loop(0, ch, row_body, 0)

    @pl.when(k > nk)
    def _():
        kk = k - nk - 1
        parts = [w_scr[pl.ds(kk * (kc_w * groups) + g, kc_w, stride=groups), :] for g in range(groups)]
        t_ref[...] = jnp.concatenate(parts, axis=1).T


def _tri_solve(akb):
    b, nc, ng, ch, gw = akb.shape
    nrow = b * nc * ng
    slab = SUBLANES * LANES
    ne = ch * gw
    row_pad = -nrow % slab
    ne_pad = -ne % LANES
    a2 = akb.reshape(nrow, ne)
    if row_pad or ne_pad:
        a2 = jnp.pad(a2, ((0, row_pad), (0, ne_pad)))
    nk = (ne + ne_pad) // LANES
    t2 = pl.pallas_call(
        functools.partial(_tri_solve_body, ch=ch, pack=gw // ch),
        grid=((nrow + row_pad) // slab, 2 * nk + 1),
        in_specs=[pl.BlockSpec((slab, LANES), lambda gi, k: (gi, jnp.minimum(k, nk - 1)))],
        out_specs=pl.BlockSpec((slab, LANES), lambda gi, k: (gi, jnp.clip(k - nk - 1, 0, nk - 1))),
        out_shape=jax.ShapeDtypeStruct(a2.shape, F32),
        scratch_shapes=[pltpu.VMEM(((ne + ne_pad) * SUBLANES, LANES), F32)],
        compiler_params=_params(("parallel", "arbitrary"), 48),
        name="rwkv_tri_solve",
    )(a2)
    if row_pad or ne_pad:
        t2 = t2[:nrow, :ne]
    return t2.reshape(b, nc, ng, ch, gw)


def _rwkv_scan_body(t_ref, akk_ref, qrk_ref, qrb_ref, kkt_ref, rt_ref, kdec_ref, bdec_ref, kmod_ref, r_ref, v_ref,
                    g_ref, gamc_ref, wkv0_ref, eblk_ref, rk_ref, lnw_ref, lnb_ref, rw_ref, wkv_ref, s_scr):
    nb, ch = kkt_ref.shape[0], kkt_ref.shape[1]
    nd = RWKV_HEAD_DIM
    per_row = (t_ref, akk_ref, qrk_ref, qrb_ref, kkt_ref, rt_ref, kdec_ref, bdec_ref, v_ref, gamc_ref, wkv0_ref,
               wkv_ref, s_scr)
    y = jnp.concatenate([_rwkv_scan_chunk(*[r.at[bi] for r in per_row]) for bi in range(nb)], axis=0)
    rows = lambda ref: ref[...].reshape(nb * ch, RWKV_WIDTH)
    eblk = eblk_ref[...]
    inv_n = 1.0 / nd
    mean = _dot_exact_rhs(y, eblk) * inv_n
    yc = y - mean
    var = _dot_exact_rhs(yc * yc, eblk) * inv_n
    yn = yc * lax.rsqrt(var + GN_EPS) * lnw_ref[...] + lnb_ref[...]
    bonus = _dot_exact_rhs(rows(r_ref) * rows(kmod_ref) * rk_ref[...], eblk) * rows(v_ref)
    rw_ref[...] = ((yn + bonus) * rows(g_ref)).reshape(nb, ch, RWKV_WIDTH).astype(rw_ref.dtype)


def _rwkv_scan_chunk(t_ref, akk_ref, qrk_ref, qrb_ref, kkt_ref, rt_ref, kdec_ref, bdec_ref, v_ref, gamc_ref,
                     wkv0_ref, wkv_ref, s_scr):
    c = pl.program_id(1)
    nd = RWKV_HEAD_DIM

    @pl.when(c == 0)
    def _():
        s_scr[...] = wkv0_ref[...]

    hs = range(RWKV_HEADS)
    sl = [slice(h * nd, (h + 1) * nd) for h in hs]
    ch = kkt_ref.shape[0]
    pack = _head_pack(ch)
    mat = lambda ref, h: ref[h // pack][:, (h % pack) * ch:(h % pack + 1) * ch]
    tmat = [mat(t_ref, h) for h in hs]
    vh = [v_ref[:, sl[h]] for h in hs]
    bdec = [bdec_ref[:, sl[h]] for h in hs]
    qrb = [mat(qrb_ref, h) for h in hs]
    w1 = [_dotb(tmat[h], kkt_ref[:, sl[h]]) for h in hs]
    av = [_dotb(mat(akk_ref, h), vh[h]) for h in hs]
    qv = [_dotb(mat(qrk_ref, h), vh[h]) for h in hs]
    vk = [_dotb(vh[h], kdec_ref[:, sl[h]], TN) for h in hs]
    w2 = [_dotb(tmat[h], av[h]) for h in hs]
    qw1 = [_dotb(qrb[h], w1[h]) for h in hs]
    bw1 = [_dotb(bdec[h], w1[h], TN) for h in hs]
    qw2 = [_dotb(qrb[h], w2[h]) for h in hs]
    wb = [_dotb(w2[h], bdec[h], TN) for h in hs]
    s0 = [s_scr[h] for h in hs]
    ys = [_dotb(rt_ref[:, sl[h]] - qw1[h], s0[h], NT) + (qv[h] - qw2[h]) for h in hs]
    for h in hs:
        s_scr[h] = s0[h] * gamc_ref[:, sl[h]] - _dotb(s0[h], bw1[h], NT) + (vk[h] - wb[h])
    @pl.when(c == pl.num_programs(1) - 1)
    def _():
        wkv_ref[...] = s_scr[...]

    return jnp.concatenate(ys, axis=1)


def _rwkv_scan(tmat, prep, wkv0, lp, eblk, bb):
    akk, qrk, qrb, kkt, rt, kdec, bdec, kmod, r, v, g, gamc = prep
    b, nc, ng, ch, gw = tmat.shape
    nh = RWKV_HEADS
    tp = kkt.shape[1]
    nd = RWKV_HEAD_DIM
    mspec = pl.BlockSpec((bb, None, ng, ch, gw), lambda bi, ci: (bi, ci, 0, 0, 0))
    rspec = pl.BlockSpec((bb, ch, RWKV_WIDTH), lambda bi, ci: (bi, ci, 0))
    sspec = pl.BlockSpec((bb, nh, nd, nd), lambda bi, ci: (bi, 0, 0, 0))
    consts = (eblk, lp["r_k"], lp["ln_w"], lp["ln_b"])
    return pl.pallas_call(
        _rwkv_scan_body,
        grid=(b // bb, nc),
        in_specs=[mspec] * 4 + [rspec] * 8
        + [pl.BlockSpec((bb, None, 1, RWKV_WIDTH), lambda bi, ci: (bi, ci, 0, 0)), sspec]
        + [_full(a) for a in consts],
        out_specs=[rspec, sspec],
        out_shape=[jax.ShapeDtypeStruct((b, tp, RWKV_WIDTH), BF16), jax.ShapeDtypeStruct((b, nh, nd, nd), F32)],
        scratch_shapes=[pltpu.VMEM((bb, nh, nd, nd), F32)],
        compiler_params=_params(("parallel", "arbitrary"), 48),
        name="rwkv_scan",
    )(tmat, akk, qrk, qrb, kkt, rt, kdec, bdec, kmod, r, v, g, gamc, wkv0, *consts)


def _merge_body(x_ref, att_ref, rw_ref, ga_ref, gb_ref, wa_ref, wb_ref, wo_ref, g2_ref, wq_ref,
                x1_ref, h2_ref, qp_ref):
    merged = ga_ref[...] * _dot(att_ref[...], wa_ref[...]) + gb_ref[...] * _dot(rw_ref[...], wb_ref[...])
    x1 = x_ref[...] + _dot(merged.astype(BF16), wo_ref[...])
    x1_ref[...] = x1
    h2 = (x1 * lax.rsqrt(jnp.mean(x1 * x1, axis=-1, keepdims=True) + NORM_EPS) * g2_ref[...]).astype(BF16)
    h2_ref[...] = h2
    qp_ref[...] = _dot(h2, wq_ref[...]).astype(qp_ref.dtype)


def _merge(x2, att, rw, ga, gb, lp, tm):
    m = x2.shape[0]
    kq = lp["w_query"].shape[1]
    row = lambda w: pl.BlockSpec((tm, w), lambda i: (i, 0))
    consts = (lp["w_up_a"], lp["w_up_b"], lp["w_o"], lp["norm2_g"], lp["w_query"])
    return pl.pallas_call(
        _merge_body,
        grid=(m // tm,),
        in_specs=[row(D_MODEL), row(ATTN_WIDTH), row(RWKV_WIDTH), row(D_MODEL), row(D_MODEL)]
        + [_full(a) for a in consts],
        out_specs=[row(D_MODEL), row(D_MODEL), row(kq)],
        out_shape=[jax.ShapeDtypeStruct((m, D_MODEL), F32), jax.ShapeDtypeStruct((m, D_MODEL), BF16),
                   jax.ShapeDtypeStruct((m, kq), BF16)],
        compiler_params=_params(("parallel",), 56),
        name="merge",
    )(x2, att, rw, ga, gb, *consts)


def _argmax_rows(src_ref):
    nslab = src_ref.shape[0] // SUBLANES
    row0 = lax.broadcasted_iota(I32, (SUBLANES, src_ref.shape[1]), 0).astype(F32)
    vals = [src_ref[i * SUBLANES:(i + 1) * SUBLANES, :] for i in range(nslab)]
    rows = [row0 + float(i * SUBLANES) for i in range(nslab)]
    while len(vals) > 1:
        nv, nr = [], []
        for a in range(0, len(vals) - 1, 2):
            take = vals[a + 1] > vals[a]
            nv.append(jnp.maximum(vals[a], vals[a + 1]))
            nr.append(jnp.where(take, rows[a + 1], rows[a]))
        if len(vals) % 2:
            nv.append(vals[-1])
            nr.append(rows[-1])
        vals, rows = nv, nr
    v, i = vals[0], rows[0]
    for shift in (4, 2, 1):
        v2 = pltpu.roll(v, shift, 0)
        i2 = pltpu.roll(i, shift, 0)
        take = (v2 > v) | ((v2 == v) & (i2 < i))
        v = jnp.where(take, v2, v)
        i = jnp.where(take, i2, i)
    return jnp.max(v, axis=0, keepdims=True), jnp.max(i, axis=0, keepdims=True)


def _extract_top(jobs, n_out):
    def body(r, carry):
        for src_ref, val_ref, idx_ref in jobs:
            rowf = lax.broadcasted_iota(I32, src_ref.shape, 0).astype(F32)
            m, idx = _argmax_rows(src_ref)
            val_ref[pl.ds(r, 1), :] = m
            idx_ref[pl.ds(r, 1), :] = idx
            src_ref[...] = jnp.where(rowf == idx, NEG_INF, src_ref[...])
        return carry

    lax.fori_loop(0, n_out, body, 0)


def _cand_groups(k):
    groups = []
    a = 0
    while a < k and k // (a + 1) > 1:
        nb = k // (a + 1)
        groups.append((a, -(-nb // SUBLANES) * SUBLANES, nb))
        a += 1
    return groups, a


def _n_cand_rows(k):
    groups, a_tail = _cand_groups(k)
    return sum(rows for _, rows, _ in groups) + k - a_tail


def _peer_topk_head(qp_ref, sk_ref, h, i1_scr, i2_scr, gate_scr, s_scr, cand_scr, sv_scr, si_scr, bs_scr, bp_scr):
    k = PEER_TOPK
    both = _dot(sk_ref[h], qp_ref[:, 2 * h * D_KEY_HALF:2 * (h + 1) * D_KEY_HALF], NT)
    for c in range(2):
        s_scr[c] = both[c * N_KEYS:(c + 1) * N_KEYS, :]
    _extract_top([(s_scr.at[c], sv_scr.at[c], si_scr.at[c]) for c in range(2)], k)
    sv0 = sv_scr[0]
    sv1 = sv_scr[1]
    groups, a_tail = _cand_groups(k)
    off = 0
    for a, rows, nb in groups:
        blk = sv0[a:a + 1, :] + sv1[0:rows, :]
        if nb < rows:
            blk = jnp.where(lax.broadcasted_iota(I32, blk.shape, 0) < nb, blk, NEG_INF)
        cand_scr[off:off + rows, :] = blk
        off += rows
    cand_scr[off:off + k - a_tail, :] = sv0[a_tail:k, :] + sv1[0:1, :]
    _extract_top([(cand_scr, bs_scr, bp_scr)], k)
    best_s = bs_scr[...]
    pos = bp_scr[...]
    a_sel = jnp.zeros_like(pos)
    b_sel = pos
    off = 0
    for a, rows, nb in groups:
        if a > 0:
            a_sel = jnp.where(pos >= off, float(a), a_sel)
            b_sel = jnp.where(pos >= off, pos - off, b_sel)
        off += rows
    a_sel = jnp.where(pos >= off, a_tail + pos - off, a_sel)
    b_sel = jnp.where(pos >= off, 0.0, b_sel)
    si0 = si_scr[0]
    si1 = si_scr[1]
    i1 = jnp.zeros_like(pos)
    i2 = jnp.zeros_like(pos)
    for a in range(k):
        i1 = jnp.where(a_sel == a, si0[a:a + 1, :], i1)
        i2 = jnp.where(b_sel == a, si1[a:a + 1, :], i2)
    e = jnp.exp(best_s - jnp.max(best_s, axis=0, keepdims=True))
    i1_scr[h * k:(h + 1) * k, :] = i1
    i2_scr[h * k:(h + 1) * k, :] = i2
    gate_scr[h * k:(h + 1) * k, :] = e / jnp.sum(e, axis=0, keepdims=True)


def _peer_topk_body(qp_ref, sk_ref, i1_ref, i2_ref, gate_ref, i1_scr, i2_scr, gate_scr, *work):
    for h in range(PEER_HEADS):
        _peer_topk_head(qp_ref, sk_ref, h, i1_scr, i2_scr, gate_scr, *work)
    i1_ref[...] = i1_scr[...].T.astype(I32)
    i2_ref[...] = i2_scr[...].T.astype(I32)
    gate_ref[...] = gate_scr[...].T


def _peer_topk(qp, sub_keys, tm):
    m = qp.shape[0]
    k = PEER_TOPK
    ospec = pl.BlockSpec((tm, N_SLOTS), lambda i: (i, 0))
    slots = pltpu.VMEM((N_SLOTS, tm), F32)
    return pl.pallas_call(
        _peer_topk_body,
        grid=(m // tm,),
        in_specs=[pl.BlockSpec((tm, qp.shape[1]), lambda i: (i, 0)), _full(sub_keys)],
        out_specs=[ospec, ospec, ospec],
        out_shape=[jax.ShapeDtypeStruct((m, N_SLOTS), I32), jax.ShapeDtypeStruct((m, N_SLOTS), I32),
                   jax.ShapeDtypeStruct((m, N_SLOTS), F32)],
        scratch_shapes=[slots, slots, slots,
                        pltpu.VMEM((2, N_KEYS, tm), F32), pltpu.VMEM((_n_cand_rows(k), tm), F32),
                        pltpu.VMEM((2, k, tm), F32), pltpu.VMEM((2, k, tm), F32),
                        pltpu.VMEM((k, tm), F32), pltpu.VMEM((k, tm), F32)],
        compiler_params=_params(("parallel",), 32),
        name="peer_topk",
    )(qp, sub_keys)


def _peer_expert_body(x1_ref, h2_ref, i1_ref, i2_ref, gate_ref, ut_ref, v_ref, gf_ref, y_ref, w_scr):
    j = pl.program_id(1)
    tm = h2_ref.shape[0]
    te = v_ref.shape[0]
    pitch = tm + W_PITCH_PAD

    @pl.when(j == 0)
    def _():
        y_ref[...] = x1_ref[...]
        sub = lax.broadcasted_iota(I32, (N_KEYS, N_SLOTS), 0)

        def tok(t, carry):
            hit1 = sub == i1_ref[pl.ds(t, 1), :]
            hit2 = sub == i2_ref[pl.ds(t, 1), :]
            a = jnp.where(hit1, gate_ref[pl.ds(t, 1), :], 0.0).astype(BF16)
            bm_t = jnp.where(hit2, 1.0, 0.0).T.astype(BF16)
            w_scr[pl.ds(t, N_KEYS, stride=pitch), :] = _dot(a, bm_t)
            return carry

        lax.fori_loop(0, tm, tok, 0, unroll=TOKEN_UNROLL)

    rb = min(tm, EXPERT_ROW_BLOCK)
    pair = 2 * N_KEYS
    for r in range(tm // rb):
        x = h2_ref[r * rb:(r + 1) * rb, :]
        parts = []
        for q in range(te // pair):
            h = _dot(x, ut_ref[:, q * pair:(q + 1) * pair])
            i1 = j * (te // N_KEYS) + 2 * q
            w = jnp.concatenate([w_scr[pl.ds(pl.multiple_of(i1 * pitch + r * rb, SUBLANES), rb), :],
                                 w_scr[pl.ds(pl.multiple_of((i1 + 1) * pitch + r * rb, SUBLANES), rb), :]], axis=1)
            act = 0.5 * h * (1.0 + lax.erf(h * (2.0 ** -0.5)))
            parts.append((w * act).astype(BF16))
        y_ref[r * rb:(r + 1) * rb, :] += _dot(jnp.concatenate(parts, axis=1), v_ref[...])

    @pl.when(j == pl.num_programs(1) - 1)
    def _():
        x2 = y_ref[...]
        y_ref[...] = x2 * lax.rsqrt(jnp.mean(x2 * x2, axis=-1, keepdims=True) + NORM_EPS) * gf_ref[...]


def _peer_experts(x1, h2, i1, i2, gate, eu, ev, gf, tm, te):
    m = x1.shape[0]
    row = lambda w: pl.BlockSpec((tm, w), lambda i, j: (i, 0))
    tab = pl.BlockSpec((te, D_MODEL), lambda i, j: (j, 0))
    tab_t = pl.BlockSpec((D_MODEL, te), lambda i, j: (0, j))
    x1_spec = pl.BlockSpec((tm, D_MODEL), lambda i, j: (i, 0), pipeline_mode=pl.Buffered(1))
    return pl.pallas_call(
        _peer_expert_body,
        grid=(m // tm, N_EXPERTS // te),
        in_specs=[x1_spec, row(D_MODEL), row(N_SLOTS), row(N_SLOTS), row(N_SLOTS), tab_t, tab,
                  pl.BlockSpec((1, D_MODEL), lambda i, j: (0, 0))],
        out_specs=row(D_MODEL),
        out_shape=jax.ShapeDtypeStruct((m, D_MODEL), F32),
        scratch_shapes=[pltpu.VMEM((N_KEYS * (tm + W_PITCH_PAD), N_KEYS), F32)],
        compiler_params=_params(("parallel", "arbitrary"), 60),
        name="peer_experts",
    )(x1, h2, i1, i2, gate, eu, ev, gf)


def _block_diag_keys(sk):
    z = jnp.zeros_like(sk[:, 0])
    top = jnp.concatenate([sk[:, 0], z], axis=2)
    bot = jnp.concatenate([z, sk[:, 1]], axis=2)
    return jnp.concatenate([top, bot], axis=1)


def _pad_cols(a, width):
    return jnp.pad(a, [(0, 0)] * (a.ndim - 1) + [(0, width - a.shape[-1])])


def _prep_layer_params(l, norm1_g, w_in, b_in, attn_sinks, shift_mu, w0, w2, a0, a2, g2, k_k, k_a, r_k, ln_x_w,
                       ln_x_b, w_up_a, w_up_b, w_o, norm2_g, w_query, sub_keys, expert_u, expert_v):
    wi, bi = w_in[l], b_in[l][None, :]
    cat = lambda z: jnp.concatenate([z[:, :OFF_B], _pad_cols(z[:, OFF_B:OFF_GATE], RWKV_PAD), z[:, OFF_GATE:]], axis=1)
    row = lambda z: z[l].reshape(1, -1)
    lora = lambda z, off: jnp.pad(z[l], ((off, LORA_PAD - off - z.shape[1]), (0, 0)))
    return dict(
        norm1_g=row(norm1_g), w_cat=cat(wi).astype(BF16), b_cat=cat(bi), sinks=attn_sinks[l],
        mu=_pad_cols(row(shift_mu), RWKV_PAD), w0=row(w0), a0=row(a0), k_k=row(k_k), k_a=row(k_a),
        w2p=lora(w2, 0), a2p=lora(a2, DECAY_LORA), g2p=lora(g2, DECAY_LORA + AAA_LORA),
        r_k=row(r_k), ln_w=row(ln_x_w), ln_b=row(ln_x_b),
        w_up_a=w_up_a[l].astype(BF16), w_up_b=w_up_b[l].astype(BF16), w_o=w_o[l].astype(BF16),
        norm2_g=row(norm2_g), w_query=w_query[l].astype(BF16), sub_keys=_block_diag_keys(sub_keys[l].astype(BF16)),
        expert_ut=expert_u[l].astype(BF16).T, expert_v=expert_v[l].astype(BF16),
    )


def _row_tile(m, pref):
    return pref if m % pref == 0 else m


def _hybrid_layer(x, pos, lp, eblk, gf, win_k, win_v, wkv0, shift0):
    b, t, _ = x.shape
    m = b * t
    x2 = x.reshape(m, D_MODEL)
    tm = _row_tile(m, 256)
    tmd = _row_tile(m, DENSE_TOKEN_TILE)
    rope_tab = _rope_table(pos, max(t, tmd))
    q, k, v, p, ga, gb = _in_proj(x2, lp["norm1_g"], lp["w_cat"], lp["b_cat"], rope_tab, tmd)

    k3 = k.reshape(b, t, KV_WIDTH)
    v3 = v.reshape(b, t, KV_WIDTH)
    if win_k is None:
        att = _attn_prompt(lp["sinks"], q.reshape(b, t, ATTN_WIDTH), k3, v3).reshape(m, ATTN_WIDTH)
        keep = min(WINDOW, t)
        new_k, new_v = k3[:, t - keep:], v3[:, t - keep:]
    else:
        n_past = win_k.shape[1]
        qg = q.reshape(b, t, N_KV_HEADS, Q_PER_KV, HEAD_DIM).transpose(0, 2, 3, 1, 4)
        qg = qg.reshape(b, N_KV_HEADS, Q_PER_KV * t, HEAD_DIM)
        ck = win_k.reshape(b, n_past, KV_WIDTH)
        cv = win_v.reshape(b, n_past, KV_WIDTH)
        og = _attn_sample(lp["sinks"], qg, k3, v3, ck, cv, t)
        att = og.reshape(b, N_KV_HEADS, Q_PER_KV, t, HEAD_DIM).transpose(0, 3, 1, 2, 4).reshape(m, ATTN_WIDTH)
        new_k = jnp.concatenate([ck, k3], axis=1)[:, t:]
        new_v = jnp.concatenate([cv, v3], axis=1)[:, t:]
    new_k = new_k.reshape(b, -1, N_KV_HEADS, HEAD_DIM)
    new_v = new_v.reshape(b, -1, N_KV_HEADS, HEAD_DIM)

    p3 = p.reshape(b, t, RWKV_PAD)
    if t % RWKV_CHUNK == 0:
        ch, tp = RWKV_CHUNK, t
    else:
        tp = -(-t // SUBLANES) * SUBLANES
        ch = tp
        p3 = jnp.pad(p3, ((0, 0), (0, tp - t), (0, 0)))
    shift_pad = _pad_cols(shift0, RWKV_PAD)[:, None, :]
    want = DECODE_BATCH_BLOCK if tp == ch else PROMPT_BATCH_BLOCK
    bb = want if b % want == 0 else 1
    akb, *prep = _rwkv_prep(p3, shift_pad, lp, eblk, ch, t, bb)
    tmat = _tri_solve(akb)
    rw, wkv_t = _rwkv_scan(tmat, prep, wkv0, lp, eblk, bb)
    rw = rw[:, :t].reshape(m, RWKV_WIDTH)
    shift_t = p3[:, t - 1, :RWKV_COLS]

    x1, h2, qp = _merge(x2, att, rw, ga, gb, lp, tmd)
    i1, i2, gate = _peer_topk(qp, lp["sub_keys"], tm)
    y = _peer_experts(x1, h2, i1, i2, gate, lp["expert_ut"], lp["expert_v"], gf,
                      _row_tile(m, EXPERT_TOKEN_TILE), EXPERT_TILE)
    return y.reshape(b, t, D_MODEL), new_k, new_v, wkv_t, shift_t


def kernel(x_prompt, x_sample, cache_win_k, cache_win_v, state_wkv, state_shift, norm1_g, w_in, b_in, attn_sinks,
           shift_mu, w0, w2, a0, a2, g2, k_k, k_a, r_k, ln_x_w, ln_x_b, w_up_a, w_up_b, w_o, norm2_g, w_query,
           sub_keys, expert_u, expert_v, final_norm_g):
    depth = w_in.shape[0]
    assert depth == 1, "the final RMSNorm is fused into the last layer's expert kernel"
    pos_p = jnp.arange(x_prompt.shape[1])
    pos_s = PAST_LEN + jnp.arange(x_sample.shape[1])
    li = lax.broadcasted_iota(I32, (RWKV_WIDTH, RWKV_WIDTH), 0) // RWKV_HEAD_DIM
    lj = lax.broadcasted_iota(I32, (RWKV_WIDTH, RWKV_WIDTH), 1) // RWKV_HEAD_DIM
    eblk = (li == lj).astype(BF16)
    gf = final_norm_g.reshape(1, D_MODEL)
    lp = _prep_layer_params(0, norm1_g, w_in, b_in, attn_sinks, shift_mu, w0, w2, a0, a2, g2, k_k, k_a, r_k, ln_x_w,
                            ln_x_b, w_up_a, w_up_b, w_o, norm2_g, w_query, sub_keys, expert_u, expert_v)
    bp = x_prompt.shape[0]
    wkv0 = jnp.zeros((bp, RWKV_HEADS, RWKV_HEAD_DIM, RWKV_HEAD_DIM), F32)
    shift0 = jnp.zeros((bp, RWKV_COLS), x_prompt.dtype)
    yp, pk, pv, pw, psh = _hybrid_layer(x_prompt, pos_p, lp, eblk, gf, None, None, wkv0, shift0)
    ys, sk, sv, sw, ssh = _hybrid_layer(x_sample, pos_s, lp, eblk, gf, cache_win_k[0], cache_win_v[0],
                                        state_wkv[0], state_shift[0])
    st = lambda z: z[None]
    return (yp, ys, st(pk), st(pv), st(pw), st(psh), st(sk), st(sv), st(sw), st(ssh))
```
